```python
import math, functools
import jax, jax.numpy as jnp
from jax import lax
import numpy as np

D_MODEL = 1024
BATCH = 2
SEQ = 16384
DEPTH = 2
DEC_BATCH = 32
DEC_SEQ = 64
PAST_LEN = 2048

CHUNK = 64
N_HEADS = 16
HEAD_DIM = 64
ATTN_WIDTH = N_HEADS * HEAD_DIM
KV_HEADS_A = 4
IDX_HEADS = 8
IDX_DIM = 64
TOPK_MAX = 256
QBLOCK = 128
KV_HEADS_B = 2
WINDOW = 128
WIN_CHUNKS = WINDOW // CHUNK
NUM_BUCKETS = 32
MAX_DISTANCE = 1024
D_FF = 3584
N_EXPERTS = 8
TOP_K = 2
D_FF_EXPERT = 3584
N_A_LAYERS = (DEPTH + 1) // 2
N_B_LAYERS = DEPTH // 2
EPS = 1e-6

A_SPLIT = [ATTN_WIDTH,
           ATTN_WIDTH + KV_HEADS_A * HEAD_DIM,
           ATTN_WIDTH + 2 * KV_HEADS_A * HEAD_DIM,
           ATTN_WIDTH + 2 * KV_HEADS_A * HEAD_DIM + IDX_HEADS * IDX_DIM,
           ATTN_WIDTH + 2 * KV_HEADS_A * HEAD_DIM + IDX_HEADS * IDX_DIM + IDX_DIM]
A_IN = A_SPLIT[-1] + IDX_HEADS
B_SPLIT = [ATTN_WIDTH, ATTN_WIDTH + KV_HEADS_B * HEAD_DIM]
B_IN = ATTN_WIDTH + 2 * KV_HEADS_B * HEAD_DIM

kernel_name = 'hybrid_dsa_swa_sink_streaming_step'

F32 = jnp.float32


def rms_norm(x, g):
    xf = x.astype(F32)
    y = xf * lax.rsqrt(jnp.mean(xf * xf, axis=-1, keepdims=True) + EPS)
    return (y * g.astype(F32)).astype(x.dtype)


def rel_bucket(rel):
    half = NUM_BUCKETS // 2
    max_exact = half // 2
    base = jnp.where(rel > 0, half, 0)
    n = jnp.abs(rel)
    nf = jnp.maximum(n, 1).astype(F32)
    large = max_exact + (jnp.log(nf / max_exact) / math.log(MAX_DISTANCE / max_exact)
                         * (half - max_exact)).astype(jnp.int32)
    large = jnp.minimum(large, half - 1)
    return base + jnp.where(n < max_exact, n, large)


def adaln(c, w, b):
    mod = jax.nn.silu(c) @ w + b
    return jnp.split(mod[:, None, :], 6, axis=-1)


def modulate(x, g, shift, scale):
    return rms_norm(x, g) * (1.0 + scale) + shift


def project_a(h, w_in, g_q, g_k, g_kidx):
    b, t, _ = h.shape
    q, k, v, qi, ki, wi = jnp.split(h @ w_in, A_SPLIT, axis=-1)
    q = rms_norm(q.reshape(b, t, N_HEADS, HEAD_DIM), g_q)
    k = rms_norm(k.reshape(b, t, KV_HEADS_A, HEAD_DIM), g_k)
    v = v.reshape(b, t, KV_HEADS_A, HEAD_DIM)
    qi = qi.reshape(b, t, IDX_HEADS, IDX_DIM)
    ki = rms_norm(ki, g_kidx)
    wi = wi * IDX_HEADS ** -0.5
    return q, k, v, qi, ki, wi


def dsa_attend(q, qi, wi, q_pos, k, v, ki, n_sel, rel_table):
    bsz, tq = q.shape[0], q.shape[1]
    length = k.shape[1]
    g = N_HEADS // KV_HEADS_A
    s = jnp.einsum('bthe,bse->bths', qi.astype(F32), ki.astype(F32)) * IDX_DIM ** -0.5
    score = jnp.einsum('bth,bths->bts', wi.astype(F32), jax.nn.relu(s))
    q_chunk = q_pos // CHUNK
    visible = (jnp.arange(length) // CHUNK)[None, :] <= q_chunk[:, None]
    score = jnp.where(visible[None], score, -jnp.inf)
    _, idx = lax.top_k(score, n_sel)
    valid = (idx // CHUNK) <= q_chunk[None, :, None]
    gather = jax.vmap(lambda a, i: a[i])
    kg = gather(k, idx)
    vg = gather(v, idx)
    qg = q.reshape(bsz, tq, KV_HEADS_A, g, HEAD_DIM)
    logits = jnp.einsum('btkgd,btnkd->btkgn', qg, kg).astype(F32) * HEAD_DIM ** -0.5
    bias = rel_table[rel_bucket(idx - q_pos[None, :, None])]
    bias = bias.reshape(bsz, tq, n_sel, KV_HEADS_A, g).transpose(0, 1, 3, 4, 2)
    logits = jnp.where(valid[:, :, None, None, :], logits + bias.astype(F32), -jnp.inf)
    p = jax.nn.softmax(logits, axis=-1).astype(v.dtype)
    out = jnp.einsum('btkgn,btnkd->btkgd', p, vg)
    return out.reshape(bsz, tq, ATTN_WIDTH)


def mixer_a_prompt(h, w_in, w_out, g_q, g_k, g_kidx, rel_table):
    q, k, v, qi, ki, wi = project_a(h, w_in, g_q, g_k, g_kidx)
    bsz, seq = h.shape[0], h.shape[1]
    n_sel = min(TOPK_MAX, seq // 4)
    nb = seq // QBLOCK

    def blocks(a):
        return a.reshape(bsz, nb, QBLOCK, *a.shape[2:]).swapaxes(0, 1)

    starts = jnp.arange(nb, dtype=jnp.int32) * QBLOCK

    def body(xs):
        qb, qib, wib, st = xs
        return dsa_attend(qb, qib, wib, st + jnp.arange(QBLOCK, dtype=jnp.int32),
                          k, v, ki, n_sel, rel_table)

    o = lax.map(body, (blocks(q), blocks(qi), blocks(wi), starts))
    o = o.swapaxes(0, 1).reshape(bsz, seq, ATTN_WIDTH)
    return o @ w_out, (k, v, ki)


def mixer_a_sample(h, ck, cv, cki, w_in, w_out, g_q, g_k, g_kidx, rel_table):
    q, k, v, qi, ki, wi = project_a(h, w_in, g_q, g_k, g_kidx)
    past, n = ck.shape[1], h.shape[1]
    n_sel = min(TOPK_MAX, (past + n) // 4)
    kf = jnp.concatenate([ck.astype(k.dtype), k], axis=1)
    vf = jnp.concatenate([cv.astype(v.dtype), v], axis=1)
    kif = jnp.concatenate([cki.astype(ki.dtype), ki], axis=1)
    q_pos = past + jnp.arange(n, dtype=jnp.int32)
    o = dsa_attend(q, qi, wi, q_pos, kf, vf, kif, n_sel, rel_table)
    return o @ w_out, (k, v, ki)


def project_b(h, w_in, g_q, g_k):
    b, t, _ = h.shape
    q, k, v = jnp.split(h @ w_in, B_SPLIT, axis=-1)
    q = rms_norm(q.reshape(b, t, N_HEADS, HEAD_DIM), g_q)
    k = rms_norm(k.reshape(b, t, KV_HEADS_B, HEAD_DIM), g_k)
    v = v.reshape(b, t, KV_HEADS_B, HEAD_DIM)
    return q, k, v


def band_attend(q, k, v, key_valid, rel, sinks, rel_table):
    bsz, nblk, tq = q.shape[0], q.shape[1], q.shape[2]
    tk = k.shape[2]
    g = N_HEADS // KV_HEADS_B
    qg = q.reshape(bsz, nblk, tq, KV_HEADS_B, g, HEAD_DIM)
    logits = jnp.einsum('bnqkgd,bnskd->bnkgqs', qg, k).astype(F32) * HEAD_DIM ** -0.5
    bias = rel_table[rel_bucket(rel)].reshape(tq, tk, KV_HEADS_B, g).transpose(2, 3, 0, 1)
    logits = jnp.where(key_valid[None, :, None, None, None, :], logits + bias.astype(F32), -jnp.inf)
    sink = sinks.astype(F32).reshape(KV_HEADS_B, g)[:, :, None, None]
    m = jnp.maximum(jnp.max(logits, axis=-1, keepdims=True), sink)
    e = jnp.exp(logits - m)
    p = e / (jnp.sum(e, axis=-1, keepdims=True) + jnp.exp(sink - m))
    out = jnp.einsum('bnkgqs,bnskd->bnqkgd', p.astype(v.dtype), v)
    return out.reshape(bsz, nblk, tq, ATTN_WIDTH)


def mixer_b_prompt(h, w_in, w_out, g_q, g_k, sinks, rel_table):
    q, k, v = project_b(h, w_in, g_q, g_k)
    bsz, seq = h.shape[0], h.shape[1]
    nc = seq // CHUNK
    tk = (WIN_CHUNKS + 1) * CHUNK

    def band(a):
        ac = a.reshape(bsz, nc, CHUNK, *a.shape[2:])
        ap = jnp.pad(ac, ((0, 0), (WIN_CHUNKS, 0), (0, 0), (0, 0), (0, 0)))
        return jnp.concatenate([ap[:, j:j + nc] for j in range(WIN_CHUNKS + 1)], axis=2)

    qc = q.reshape(bsz, nc, CHUNK, N_HEADS, HEAD_DIM)
    band_chunk = (jnp.arange(nc)[:, None] - WIN_CHUNKS
                  + (jnp.arange(tk) // CHUNK)[None, :])
    key_valid = band_chunk >= 0
    rel = jnp.arange(tk)[None, :] - WINDOW - jnp.arange(CHUNK)[:, None]
    o = band_attend(qc, band(k), band(v), key_valid, rel, sinks, rel_table)
    o = o.reshape(bsz, seq, ATTN_WIDTH)
    return o @ w_out, (k[:, -WINDOW:], v[:, -WINDOW:])


def mixer_b_sample(h, ck, cv, w_in, w_out, g_q, g_k, sinks, rel_table):
    q, k, v = project_b(h, w_in, g_q, g_k)
    n = h.shape[1]
    kf = jnp.concatenate([ck.astype(k.dtype), k], axis=1)
    vf = jnp.concatenate([cv.astype(v.dtype), v], axis=1)
    tk = WINDOW + n
    rel = jnp.arange(tk)[None, :] - WINDOW - jnp.arange(n)[:, None]
    key_valid = jnp.ones((1, tk), dtype=bool)
    o = band_attend(q[:, None], kf[:, None], vf[:, None], key_valid, rel, sinks, rel_table)[:, 0]
    return o @ w_out, (kf[:, -WINDOW:], vf[:, -WINDOW:])


def swiglu(h, w_gate, w_up, w_down):
    return (jax.nn.silu(h @ w_gate) * (h @ w_up)) @ w_down


def moe_swiglu(h, w_router, w_gate, w_up, w_down):
    logits = (h @ w_router).astype(F32)
    top_v, top_i = lax.top_k(logits, TOP_K)
    gates = jax.nn.softmax(top_v, axis=-1)
    out = jnp.zeros_like(h)
    for e in range(N_EXPERTS):
        ge = jnp.sum(jnp.where(top_i == e, gates, 0.0), axis=-1)[..., None].astype(h.dtype)
        out = out + ge * swiglu(h, w_gate[e], w_up[e], w_down[e])
    return out


def sublayers(x, c, mix_fn, ffn_fn, g, w, b):
    sh1, sc1, gt1, sh2, sc2, gt2 = adaln(c, w, b)
    m, st = mix_fn(modulate(x, g[0], sh1, sc1))
    x = x + gt1 * m
    x = x + gt2 * ffn_fn(modulate(x, g[1], sh2, sc2))
    return x, st


def setup_inputs(seed: int = 0) -> dict:
    key = jax.random.key(seed)
    ks = iter(jax.random.split(key, 40))
    D = D_MODEL

    def nrm(shape, s):
        return jax.random.normal(next(ks), shape, F32) * s

    return {
        'x_prompt': nrm((BATCH, SEQ, D), 1.0),
        'x_sample': nrm((DEC_BATCH, DEC_SEQ, D), 1.0),
        'c_prompt': nrm((BATCH, D), 1.0),
        'c_sample': nrm((DEC_BATCH, D), 1.0),
        'cache_a_k': nrm((N_A_LAYERS, DEC_BATCH, PAST_LEN, KV_HEADS_A, HEAD_DIM), 1.0),
        'cache_a_v': nrm((N_A_LAYERS, DEC_BATCH, PAST_LEN, KV_HEADS_A, HEAD_DIM), 1.0),
        'cache_a_kidx': nrm((N_A_LAYERS, DEC_BATCH, PAST_LEN, IDX_DIM), 1.0),
        'cache_b_k': nrm((N_B_LAYERS, DEC_BATCH, WINDOW, KV_HEADS_B, HEAD_DIM), 1.0),
        'cache_b_v': nrm((N_B_LAYERS, DEC_BATCH, WINDOW, KV_HEADS_B, HEAD_DIM), 1.0),
        'rel_bias': nrm((NUM_BUCKETS, N_HEADS), 0.5),
        'norm_g': 1.0 + nrm((DEPTH, 2, D), 0.05),
        'w_ada': nrm((DEPTH, D, 6 * D), 0.5 * D ** -0.5),
        'b_ada': nrm((DEPTH, 6 * D), 0.02),
        'a_w_in': nrm((N_A_LAYERS, D, A_IN), D ** -0.5),
        'a_w_out': nrm((N_A_LAYERS, ATTN_WIDTH, D), ATTN_WIDTH ** -0.5),
        'a_g_q': 1.0 + nrm((N_A_LAYERS, HEAD_DIM), 0.05),
        'a_g_k': 1.0 + nrm((N_A_LAYERS, HEAD_DIM), 0.05),
        'a_g_kidx': 1.0 + nrm((N_A_LAYERS, IDX_DIM), 0.05),
        'b_w_in': nrm((N_B_LAYERS, D, B_IN), D ** -0.5),
        'b_w_out': nrm((N_B_LAYERS, ATTN_WIDTH, D), ATTN_WIDTH ** -0.5),
        'b_g_q': 1.0 + nrm((N_B_LAYERS, HEAD_DIM), 0.05),
        'b_g_k': 1.0 + nrm((N_B_LAYERS, HEAD_DIM), 0.05),
        'b_sinks': nrm((N_B_LAYERS, N_HEADS), 0.5),
        'ffn_w_gate': nrm((N_A_LAYERS, D, D_FF), D ** -0.5),
        'ffn_w_up': nrm((N_A_LAYERS, D, D_FF), D ** -0.5),
        'ffn_w_down': nrm((N_A_LAYERS, D_FF, D), D_FF ** -0.5),
        'moe_w_router': nrm((N_B_LAYERS, D, N_EXPERTS), D ** -0.5),
        'moe_w_gate': nrm((N_B_LAYERS, N_EXPERTS, D, D_FF_EXPERT), D ** -0.5),
        'moe_w_up': nrm((N_B_LAYERS, N_EXPERTS, D, D_FF_EXPERT), D ** -0.5),
        'moe_w_down': nrm((N_B_LAYERS, N_EXPERTS, D_FF_EXPERT, D), D_FF_EXPERT ** -0.5),
    }


def reference(x_prompt, x_sample, c_prompt, c_sample, cache_a_k, cache_a_v, cache_a_kidx,
              cache_b_k, cache_b_v, rel_bias, norm_g, w_ada, b_ada,
              a_w_in, a_w_out, a_g_q, a_g_k, a_g_kidx,
              b_w_in, b_w_out, b_g_q, b_g_k, b_sinks,
              ffn_w_gate, ffn_w_up, ffn_w_down,
              moe_w_router, moe_w_gate, moe_w_up, moe_w_down):
    xp, xs = x_prompt, x_sample
    a_kp, a_vp, a_ip, a_ks, a_vs, a_is = [], [], [], [], [], []
    b_kp, b_vp, b_ks, b_vs = [], [], [], []
    for i in range(DEPTH):
        j = i // 2
        if i % 2 == 0:
            a_par = (a_w_in[j], a_w_out[j], a_g_q[j], a_g_k[j], a_g_kidx[j], rel_bias)
            ffn = functools.partial(swiglu, w_gate=ffn_w_gate[j], w_up=ffn_w_up[j],
                                    w_down=ffn_w_down[j])
            mix_p = functools.partial(mixer_a_prompt, w_in=a_par[0], w_out=a_par[1], g_q=a_par[2],
                                      g_k=a_par[3], g_kidx=a_par[4], rel_table=a_par[5])
            mix_s = functools.partial(mixer_a_sample, ck=cache_a_k[j], cv=cache_a_v[j],
                                      cki=cache_a_kidx[j], w_in=a_par[0], w_out=a_par[1],
                                      g_q=a_par[2], g_k=a_par[3], g_kidx=a_par[4],
                                      rel_table=a_par[5])
            xp, (k1, v1, i1) = sublayers(xp, c_prompt, mix_p, ffn, norm_g[i], w_ada[i], b_ada[i])
            xs, (k2, v2, i2) = sublayers(xs, c_sample, mix_s, ffn, norm_g[i], w_ada[i], b_ada[i])
            a_kp.append(k1); a_vp.append(v1); a_ip.append(i1)
            a_ks.append(k2); a_vs.append(v2); a_is.append(i2)
        else:
            ffn = functools.partial(moe_swiglu, w_router=moe_w_router[j], w_gate=moe_w_gate[j],
                                    w_up=moe_w_up[j], w_down=moe_w_down[j])
            mix_p = functools.partial(mixer_b_prompt, w_in=b_w_in[j], w_out=b_w_out[j],
                                      g_q=b_g_q[j], g_k=b_g_k[j], sinks=b_sinks[j],
                                      rel_table=rel_bias)
            mix_s = functools.partial(mixer_b_sample, ck=cache_b_k[j], cv=cache_b_v[j],
                                      w_in=b_w_in[j], w_out=b_w_out[j], g_q=b_g_q[j],
                                      g_k=b_g_k[j], sinks=b_sinks[j], rel_table=rel_bias)
            xp, (k1, v1) = sublayers(xp, c_prompt, mix_p, ffn, norm_g[i], w_ada[i], b_ada[i])
            xs, (k2, v2) = sublayers(xs, c_sample, mix_s, ffn, norm_g[i], w_ada[i], b_ada[i])
            b_kp.append(k1); b_vp.append(v1)
            b_ks.append(k2); b_vs.append(v2)
    return (xp, xs,
            jnp.stack(a_kp), jnp.stack(a_vp), jnp.stack(a_ip),
            jnp.stack(a_ks), jnp.stack(a_vs), jnp.stack(a_is),
            jnp.stack(b_kp), jnp.stack(b_vp),
            jnp.stack(b_ks), jnp.stack(b_vs))
```

```python
import math, functools
import jax, jax.numpy as jnp
from jax import lax
import numpy as np
from jax.experimental import pallas as pl
from jax.experimental.pallas import tpu as pltpu

D_MODEL = 1024
BATCH = 2
SEQ = 16384
DEPTH = 2
DEC_BATCH = 32
DEC_SEQ = 64
PAST_LEN = 2048

CHUNK = 64
N_HEADS = 16
HEAD_DIM = 64
ATTN_WIDTH = N_HEADS * HEAD_DIM
KV_HEADS_A = 4
IDX_HEADS = 8
IDX_DIM = 64
TOPK_MAX = 256
QBLOCK = 128
KV_HEADS_B = 2
WINDOW = 128
WIN_CHUNKS = WINDOW // CHUNK
NUM_BUCKETS = 32
MAX_DISTANCE = 1024
D_FF = 3584
N_EXPERTS = 8
TOP_K = 2
D_FF_EXPERT = 3584
N_A_LAYERS = (DEPTH + 1) // 2
N_B_LAYERS = DEPTH // 2
EPS = 1e-6

A_SPLIT = [ATTN_WIDTH,
           ATTN_WIDTH + KV_HEADS_A * HEAD_DIM,
           ATTN_WIDTH + 2 * KV_HEADS_A * HEAD_DIM,
           ATTN_WIDTH + 2 * KV_HEADS_A * HEAD_DIM + IDX_HEADS * IDX_DIM,
           ATTN_WIDTH + 2 * KV_HEADS_A * HEAD_DIM + IDX_HEADS * IDX_DIM + IDX_DIM]
A_IN = A_SPLIT[-1] + IDX_HEADS
B_SPLIT = [ATTN_WIDTH, ATTN_WIDTH + KV_HEADS_B * HEAD_DIM]
B_IN = ATTN_WIDTH + 2 * KV_HEADS_B * HEAD_DIM

F32 = jnp.float32


def rms_norm(x, g):
    xf = x.astype(F32)
    y = xf * lax.rsqrt(jnp.mean(xf * xf, axis=-1, keepdims=True) + EPS)
    return (y * g.astype(F32)).astype(x.dtype)


def rel_bucket(rel):
    half = NUM_BUCKETS // 2
    max_exact = half // 2
    base = jnp.where(rel > 0, half, 0)
    n = jnp.abs(rel)
    nf = jnp.maximum(n, 1).astype(F32)
    large = max_exact + (jnp.log(nf / max_exact) / math.log(MAX_DISTANCE / max_exact)
                         * (half - max_exact)).astype(jnp.int32)
    large = jnp.minimum(large, half - 1)
    return base + jnp.where(n < max_exact, n, large)


def _adaln_kernel(c_ref, w_ref, b_ref, o_ref):
    c = c_ref[...]
    o_ref[...] = jnp.dot(c * jax.nn.sigmoid(c), w_ref[...], preferred_element_type=F32) + b_ref[...]


def adaln(c, w, b):
    n = c.shape[0]
    mod = pl.pallas_call(
        _adaln_kernel,
        grid=(6,),
        in_specs=[pl.BlockSpec((n, D_MODEL), lambda j: (0, 0)),
                  pl.BlockSpec((D_MODEL, D_MODEL), lambda j: (0, j)),
                  pl.BlockSpec((1, D_MODEL), lambda j: (0, j))],
        out_specs=pl.BlockSpec((n, D_MODEL), lambda j: (0, j)),
        out_shape=jax.ShapeDtypeStruct((n, 6 * D_MODEL), F32),
    )(c, w, b[None, :])
    return jnp.split(mod[:, None, :], 6, axis=-1)


def modulate(x, g, shift, scale):
    return rms_norm(x, g) * (1.0 + scale) + shift


def project_a(h, w_in, g_q, g_k, g_kidx):
    b, t, _ = h.shape
    q, k, v, qi, ki, wi = jnp.split(h @ w_in, A_SPLIT, axis=-1)
    q = rms_norm(q.reshape(b, t, N_HEADS, HEAD_DIM), g_q)
    k = rms_norm(k.reshape(b, t, KV_HEADS_A, HEAD_DIM), g_k)
    v = v.reshape(b, t, KV_HEADS_A, HEAD_DIM)
    qi = qi.reshape(b, t, IDX_HEADS, IDX_DIM)
    ki = rms_norm(ki, g_kidx)
    wi = wi * IDX_HEADS ** -0.5
    return q, k, v, qi, ki, wi


def dsa_attend(q, qi, wi, q_pos, k, v, ki, n_sel, rel_table):
    bsz, tq = q.shape[0], q.shape[1]
    length = k.shape[1]
    g = N_HEADS // KV_HEADS_A
    s = jnp.einsum('bthe,bse->bths', qi.astype(F32), ki.astype(F32)) * IDX_DIM ** -0.5
    score = jnp.einsum('bth,bths->bts', wi.astype(F32), jax.nn.relu(s))
    q_chunk = q_pos // CHUNK
    visible = (jnp.arange(length) // CHUNK)[None, :] <= q_chunk[:, None]
    score = jnp.where(visible[None], score, -jnp.inf)
    _, idx = lax.top_k(score, n_sel)
    valid = (idx // CHUNK) <= q_chunk[None, :, None]
    gather = jax.vmap(lambda a, i: a[i])
    kg = gather(k, idx)
    vg = gather(v, idx)
    qg = q.reshape(bsz, tq, KV_HEADS_A, g, HEAD_DIM)
    logits = jnp.einsum('btkgd,btnkd->btkgn', qg, kg).astype(F32) * HEAD_DIM ** -0.5
    bias = rel_table[rel_bucket(idx - q_pos[None, :, None])]
    bias = bias.reshape(bsz, tq, n_sel, KV_HEADS_A, g).transpose(0, 1, 3, 4, 2)
    logits = jnp.where(valid[:, :, None, None, :], logits + bias.astype(F32), -jnp.inf)
    p = jax.nn.softmax(logits, axis=-1).astype(v.dtype)
    out = jnp.einsum('btkgn,btnkd->btkgd', p, vg)
    return out.reshape(bsz, tq, ATTN_WIDTH)


def mixer_a_prompt(h, w_in, w_out, g_q, g_k, g_kidx, rel_table):
    q, k, v, qi, ki, wi = project_a(h, w_in, g_q, g_k, g_kidx)
    bsz, seq = h.shape[0], h.shape[1]
    n_sel = min(TOPK_MAX, seq // 4)
    nb = seq // QBLOCK

    def blocks(a):
        return a.reshape(bsz, nb, QBLOCK, *a.shape[2:]).swapaxes(0, 1)

    starts = jnp.arange(nb, dtype=jnp.int32) * QBLOCK

    def body(xs):
        qb, qib, wib, st = xs
        return dsa_attend(qb, qib, wib, st + jnp.arange(QBLOCK, dtype=jnp.int32),
                          k, v, ki, n_sel, rel_table)

    o = lax.map(body, (blocks(q), blocks(qi), blocks(wi), starts))
    o = o.swapaxes(0, 1).reshape(bsz, seq, ATTN_WIDTH)
    return o @ w_out, (k, v, ki)


def mixer_a_sample(h, ck, cv, cki, w_in, w_out, g_q, g_k, g_kidx, rel_table):
    q, k, v, qi, ki, wi = project_a(h, w_in, g_q, g_k, g_kidx)
    past, n = ck.shape[1], h.shape[1]
    n_sel = min(TOPK_MAX, (past + n) // 4)
    kf = jnp.concatenate([ck.astype(k.dtype), k], axis=1)
    vf = jnp.concatenate([cv.astype(v.dtype), v], axis=1)
    kif = jnp.concatenate([cki.astype(ki.dtype), ki], axis=1)
    q_pos = past + jnp.arange(n, dtype=jnp.int32)
    o = dsa_attend(q, qi, wi, q_pos, kf, vf, kif, n_sel, rel_table)
    return o @ w_out, (k, v, ki)


def project_b(h, w_in, g_q, g_k):
    b, t, _ = h.shape
    q, k, v = jnp.split(h @ w_in, B_SPLIT, axis=-1)
    q = rms_norm(q.reshape(b, t, N_HEADS, HEAD_DIM), g_q)
    k = rms_norm(k.reshape(b, t, KV_HEADS_B, HEAD_DIM), g_k)
    v = v.reshape(b, t, KV_HEADS_B, HEAD_DIM)
    return q, k, v


def band_attend(q, k, v, key_valid, rel, sinks, rel_table):
    bsz, nblk, tq = q.shape[0], q.shape[1], q.shape[2]
    tk = k.shape[2]
    g = N_HEADS // KV_HEADS_B
    qg = q.reshape(bsz, nblk, tq, KV_HEADS_B, g, HEAD_DIM)
    logits = jnp.einsum('bnqkgd,bnskd->bnkgqs', qg, k).astype(F32) * HEAD_DIM ** -0.5
    bias = rel_table[rel_bucket(rel)].reshape(tq, tk, KV_HEADS_B, g).transpose(2, 3, 0, 1)
    logits = jnp.where(key_valid[None, :, None, None, None, :], logits + bias.astype(F32), -jnp.inf)
    sink = sinks.astype(F32).reshape(KV_HEADS_B, g)[:, :, None, None]
    m = jnp.maximum(jnp.max(logits, axis=-1, keepdims=True), sink)
    e = jnp.exp(logits - m)
    p = e / (jnp.sum(e, axis=-1, keepdims=True) + jnp.exp(sink - m))
    out = jnp.einsum('bnkgqs,bnskd->bnqkgd', p.astype(v.dtype), v)
    return out.reshape(bsz, nblk, tq, ATTN_WIDTH)


def mixer_b_prompt(h, w_in, w_out, g_q, g_k, sinks, rel_table):
    q, k, v = project_b(h, w_in, g_q, g_k)
    bsz, seq = h.shape[0], h.shape[1]
    nc = seq // CHUNK
    tk = (WIN_CHUNKS + 1) * CHUNK

    def band(a):
        ac = a.reshape(bsz, nc, CHUNK, *a.shape[2:])
        ap = jnp.pad(ac, ((0, 0), (WIN_CHUNKS, 0), (0, 0), (0, 0), (0, 0)))
        return jnp.concatenate([ap[:, j:j + nc] for j in range(WIN_CHUNKS + 1)], axis=2)

    qc = q.reshape(bsz, nc, CHUNK, N_HEADS, HEAD_DIM)
    band_chunk = (jnp.arange(nc)[:, None] - WIN_CHUNKS
                  + (jnp.arange(tk) // CHUNK)[None, :])
    key_valid = band_chunk >= 0
    rel = jnp.arange(tk)[None, :] - WINDOW - jnp.arange(CHUNK)[:, None]
    o = band_attend(qc, band(k), band(v), key_valid, rel, sinks, rel_table)
    o = o.reshape(bsz, seq, ATTN_WIDTH)
    return o @ w_out, (k[:, -WINDOW:], v[:, -WINDOW:])


def mixer_b_sample(h, ck, cv, w_in, w_out, g_q, g_k, sinks, rel_table):
    q, k, v = project_b(h, w_in, g_q, g_k)
    n = h.shape[1]
    kf = jnp.concatenate([ck.astype(k.dtype), k], axis=1)
    vf = jnp.concatenate([cv.astype(v.dtype), v], axis=1)
    tk = WINDOW + n
    rel = jnp.arange(tk)[None, :] - WINDOW - jnp.arange(n)[:, None]
    key_valid = jnp.ones((1, tk), dtype=bool)
    o = band_attend(q[:, None], kf[:, None], vf[:, None], key_valid, rel, sinks, rel_table)[:, 0]
    return o @ w_out, (kf[:, -WINDOW:], vf[:, -WINDOW:])


def swiglu(h, w_gate, w_up, w_down):
    return (jax.nn.silu(h @ w_gate) * (h @ w_up)) @ w_down


def moe_swiglu(h, w_router, w_gate, w_up, w_down):
    logits = (h @ w_router).astype(F32)
    top_v, top_i = lax.top_k(logits, TOP_K)
    gates = jax.nn.softmax(top_v, axis=-1)
    out = jnp.zeros_like(h)
    for e in range(N_EXPERTS):
        ge = jnp.sum(jnp.where(top_i == e, gates, 0.0), axis=-1)[..., None].astype(h.dtype)
        out = out + ge * swiglu(h, w_gate[e], w_up[e], w_down[e])
    return out


def sublayers(x, c, mix_fn, ffn_fn, g, w, b):
    sh1, sc1, gt1, sh2, sc2, gt2 = adaln(c, w, b)
    m, st = mix_fn(modulate(x, g[0], sh1, sc1))
    x = x + gt1 * m
    x = x + gt2 * ffn_fn(modulate(x, g[1], sh2, sc2))
    return x, st


def kernel(x_prompt, x_sample, c_prompt, c_sample, cache_a_k, cache_a_v, cache_a_kidx,
           cache_b_k, cache_b_v, rel_bias, norm_g, w_ada, b_ada,
           a_w_in, a_w_out, a_g_q, a_g_k, a_g_kidx,
           b_w_in, b_w_out, b_g_q, b_g_k, b_sinks,
           ffn_w_gate, ffn_w_up, ffn_w_down,
           moe_w_router, moe_w_gate, moe_w_up, moe_w_down):
    xp, xs = x_prompt, x_sample
    a_kp, a_vp, a_ip, a_ks, a_vs, a_is = [], [], [], [], [], []
    b_kp, b_vp, b_ks, b_vs = [], [], [], []
    for i in range(DEPTH):
        j = i // 2
        if i % 2 == 0:
            a_par = (a_w_in[j], a_w_out[j], a_g_q[j], a_g_k[j], a_g_kidx[j], rel_bias)
            ffn = functools.partial(swiglu, w_gate=ffn_w_gate[j], w_up=ffn_w_up[j],
                                    w_down=ffn_w_down[j])
            mix_p = functools.partial(mixer_a_prompt, w_in=a_par[0], w_out=a_par[1], g_q=a_par[2],
                                      g_k=a_par[3], g_kidx=a_par[4], rel_table=a_par[5])
            mix_s = functools.partial(mixer_a_sample, ck=cache_a_k[j], cv=cache_a_v[j],
                                      cki=cache_a_kidx[j], w_in=a_par[0], w_out=a_par[1],
                                      g_q=a_par[2], g_k=a_par[3], g_kidx=a_par[4],
                                      rel_table=a_par[5])
            xp, (k1, v1, i1) = sublayers(xp, c_prompt, mix_p, ffn, norm_g[i], w_ada[i], b_ada[i])
            xs, (k2, v2, i2) = sublayers(xs, c_sample, mix_s, ffn, norm_g[i], w_ada[i], b_ada[i])
            a_kp.append(k1); a_vp.append(v1); a_ip.append(i1)
            a_ks.append(k2); a_vs.append(v2); a_is.append(i2)
        else:
            ffn = functools.partial(moe_swiglu, w_router=moe_w_router[j], w_gate=moe_w_gate[j],
                                    w_up=moe_w_up[j], w_down=moe_w_down[j])
            mix_p = functools.partial(mixer_b_prompt, w_in=b_w_in[j], w_out=b_w_out[j],
                                      g_q=b_g_q[j], g_k=b_g_k[j], sinks=b_sinks[j],
                                      rel_table=rel_bias)
            mix_s = functools.partial(mixer_b_sample, ck=cache_b_k[j], cv=cache_b_v[j],
                                      w_in=b_w_in[j], w_out=b_w_out[j], g_q=b_g_q[j],
                                      g_k=b_g_k[j], sinks=b_sinks[j], rel_table=rel_bias)
            xp, (k1, v1) = sublayers(xp, c_prompt, mix_p, ffn, norm_g[i], w_ada[i], b_ada[i])
            xs, (k2, v2) = sublayers(xs, c_sample, mix_s, ffn, norm_g[i], w_ada[i], b_ada[i])
            b_kp.append(k1); b_vp.append(v1)
            b_ks.append(k2); b_vs.append(v2)
    return (xp, xs,
            jnp.stack(a_kp), jnp.stack(a_vp), jnp.stack(a_ip),
            jnp.stack(a_ks), jnp.stack(a_vs), jnp.stack(a_is),
            jnp.stack(b_kp), jnp.stack(b_vp),
            jnp.stack(b_ks), jnp.stack(b_vs))
```

```python
import math, functools
import jax, jax.numpy as jnp
from jax import lax
import numpy as np
from jax.experimental import pallas as pl
from jax.experimental.pallas import tpu as pltpu

D_MODEL = 1024
BATCH = 2
SEQ = 16384
DEPTH = 2
DEC_BATCH = 32
DEC_SEQ = 64
PAST_LEN = 2048

CHUNK = 64
N_HEADS = 16
HEAD_DIM = 64
ATTN_WIDTH = N_HEADS * HEAD_DIM
KV_HEADS_A = 4
IDX_HEADS = 8
IDX_DIM = 64
TOPK_MAX = 256
QBLOCK = 128
KV_HEADS_B = 2
WINDOW = 128
WIN_CHUNKS = WINDOW // CHUNK
NUM_BUCKETS = 32
MAX_DISTANCE = 1024
D_FF = 3584
N_EXPERTS = 8
TOP_K = 2
D_FF_EXPERT = 3584
N_A_LAYERS = (DEPTH + 1) // 2
N_B_LAYERS = DEPTH // 2
EPS = 1e-6

A_SPLIT = [ATTN_WIDTH,
           ATTN_WIDTH + KV_HEADS_A * HEAD_DIM,
           ATTN_WIDTH + 2 * KV_HEADS_A * HEAD_DIM,
           ATTN_WIDTH + 2 * KV_HEADS_A * HEAD_DIM + IDX_HEADS * IDX_DIM,
           ATTN_WIDTH + 2 * KV_HEADS_A * HEAD_DIM + IDX_HEADS * IDX_DIM + IDX_DIM]
A_IN = A_SPLIT[-1] + IDX_HEADS
B_SPLIT = [ATTN_WIDTH, ATTN_WIDTH + KV_HEADS_B * HEAD_DIM]
B_IN = ATTN_WIDTH + 2 * KV_HEADS_B * HEAD_DIM

F32 = jnp.float32


def rms_norm(x, g):
    xf = x.astype(F32)
    y = xf * lax.rsqrt(jnp.mean(xf * xf, axis=-1, keepdims=True) + EPS)
    return (y * g.astype(F32)).astype(x.dtype)


def rel_bucket(rel):
    half = NUM_BUCKETS // 2
    max_exact = half // 2
    base = jnp.where(rel > 0, half, 0)
    n = jnp.abs(rel)
    nf = jnp.maximum(n, 1).astype(F32)
    large = max_exact + (jnp.log(nf / max_exact) / math.log(MAX_DISTANCE / max_exact)
                         * (half - max_exact)).astype(jnp.int32)
    large = jnp.minimum(large, half - 1)
    return base + jnp.where(n < max_exact, n, large)


def _adaln_kernel(c_ref, w_ref, b_ref, o_ref):
    c = c_ref[...]
    o_ref[...] = jnp.dot(c * jax.nn.sigmoid(c), w_ref[...], preferred_element_type=F32) + b_ref[...]


def adaln(c, w, b):
    n = c.shape[0]
    mod = pl.pallas_call(
        _adaln_kernel,
        grid=(6,),
        in_specs=[pl.BlockSpec((n, D_MODEL), lambda j: (0, 0)),
                  pl.BlockSpec((D_MODEL, D_MODEL), lambda j: (0, j)),
                  pl.BlockSpec((1, D_MODEL), lambda j: (0, j))],
        out_specs=pl.BlockSpec((n, D_MODEL), lambda j: (0, j)),
        out_shape=jax.ShapeDtypeStruct((n, 6 * D_MODEL), F32),
    )(c, w, b[None, :])
    return jnp.split(mod[:, None, :], 6, axis=-1)


def modulate(x, g, shift, scale):
    return rms_norm(x, g) * (1.0 + scale) + shift


def project_a(h, w_in, g_q, g_k, g_kidx):
    b, t, _ = h.shape
    q, k, v, qi, ki, wi = jnp.split(h @ w_in, A_SPLIT, axis=-1)
    q = rms_norm(q.reshape(b, t, N_HEADS, HEAD_DIM), g_q)
    k = rms_norm(k.reshape(b, t, KV_HEADS_A, HEAD_DIM), g_k)
    v = v.reshape(b, t, KV_HEADS_A, HEAD_DIM)
    qi = qi.reshape(b, t, IDX_HEADS, IDX_DIM)
    ki = rms_norm(ki, g_kidx)
    wi = wi * IDX_HEADS ** -0.5
    return q, k, v, qi, ki, wi


LANE = 128
DSA_VMEM_BYTES = 48 * 1024 * 1024
NEG_BIG = -1e30
KEY_MIN = -2 ** 31
FAR_BUCKET = NUM_BUCKETS // 2 - 1
NEAR_BLOCKS = 6


def _tri_steps(nqb, tq, tk, p0):
    qbs, kts = [], []
    for qb in range(nqb):
        vis_end = ((p0 + qb * tq + tq - 1) // CHUNK + 1) * CHUNK
        for kt in range(-(-vis_end // tk)):
            qbs.append(qb)
            kts.append(kt)
    return jnp.asarray(np.array(qbs, np.int32)), jnp.asarray(np.array(kts, np.int32))


def _idx_kernel(qb_tab, kt_tab, qi_ref, w_ref, kit_ref, out_ref, *, tq, tk, sw, p0, l_true):
    step = pl.program_id(1)
    q0 = p0 + qb_tab[step] * tq
    k0 = kt_tab[step] * tk
    row = lax.broadcasted_iota(jnp.int32, (tq, 1), 0) + q0
    vis_end = jnp.minimum(((row >> 6) + 1) << 6, l_true)
    wv = w_ref[0]
    for c in range(tk // sw):
        kit = kit_ref[0, :, c * sw:(c + 1) * sw]
        acc = jnp.zeros((tq, sw), F32)
        for h in range(IDX_HEADS):
            sh = jnp.dot(qi_ref[0, h], kit, preferred_element_type=F32)
            acc = acc + wv[:, h:h + 1] * jnp.maximum(sh, 0.0)
        bits = lax.bitcast_convert_type(acc, jnp.int32)
        key = bits ^ ((bits >> 31) & 0x7FFFFFFF)
        kpos = k0 + c * sw + lax.broadcasted_iota(jnp.int32, (tq, sw), 1)
        out_ref[0, :, c * sw:(c + 1) * sw] = jnp.where(kpos < vis_end, key, KEY_MIN)


def _thr_kernel(keys_ref, tau_ref, *, tq, cw, p0, n_sel):
    qb = pl.program_id(1)
    vis_end = (((p0 + qb * tq + tq - 1) >> 6) + 1) << 6
    nch = (vis_end + cw - 1) // cw

    def count_ge(cand):
        def body(j, acc):
            off = pl.multiple_of(j * cw, cw)
            for u in range(cw // LANE):
                x = keys_ref[0, :, pl.ds(off + u * LANE, LANE)]
                acc = acc + jnp.where(x >= cand, 1.0, 0.0)
            return acc
        acc = lax.fori_loop(0, nch, body, jnp.zeros((tq, LANE), F32))
        return jnp.sum(acc, axis=1, keepdims=True)

    zero = jnp.zeros((tq, LANE), jnp.int32)
    tau = jnp.where(count_ge(zero) >= n_sel, zero, KEY_MIN)

    def bit_body(i, tau):
        cand = tau + jnp.left_shift(jnp.int32(1), 30 - i)
        return jnp.where(count_ge(cand) >= n_sel, cand, tau)

    tau = lax.fori_loop(0, 31, bit_body, tau)
    tau_ref[0] = jnp.maximum(tau, KEY_MIN + 1)


def _attn_kernel(qb_tab, kt_tab, q_ref, kt_ref, vx_ref, keys_ref, tau_ref, bias_ref, o_ref, m_ref, acc_ref,
                 *, tq, tk, sw, p0):
    step = pl.program_id(1)
    qb, kt = qb_tab[step], kt_tab[step]
    q0 = p0 + qb * tq
    k0 = kt * tk
    vis_end = (((q0 + tq - 1) >> 6) + 1) << 6
    group = N_HEADS // KV_HEADS_A
    nblk = sw // LANE

    @pl.when(kt == 0)
    def _():
        m_ref[...] = jnp.full(m_ref.shape, NEG_BIG, F32)
        acc_ref[...] = jnp.zeros(acc_ref.shape, F32)

    tau = jnp.concatenate([tau_ref[0]] * nblk, axis=1)

    def sub_tile(c, with_bias):
        off = pl.multiple_of(c * sw, sw)
        negm = jnp.where(keys_ref[0, :, pl.ds(off, sw)] >= tau, 0.0, NEG_BIG)
        if with_bias:
            d0 = ((k0 + off - q0) >> 7) + NEAR_BLOCKS
            bidx = [jnp.clip(d0 + u, 0, NEAR_BLOCKS) for u in range(nblk)]
        for h in range(N_HEADS):
            j = h // group
            s = jnp.dot(q_ref[0, h], kt_ref[0, j, :, pl.ds(off, sw)], preferred_element_type=F32) + negm
            if with_bias:
                s = s + jnp.concatenate([bias_ref[bidx[u], h] for u in range(nblk)], axis=1)
            m_prev = m_ref[h]
            m_cur = jnp.maximum(m_prev, jnp.max(s, axis=1, keepdims=True))
            alpha = jnp.exp(m_prev - m_cur)
            p = jnp.exp(s - jnp.concatenate([m_cur] * nblk, axis=1))
            pv = jnp.dot(p.astype(jnp.bfloat16), vx_ref[0, j, pl.ds(off, sw), :], preferred_element_type=F32)
            acc_ref[h] = alpha * acc_ref[h] + pv
            m_ref[h] = m_cur

    nsub = jnp.minimum(tk // sw, (vis_end - k0 + sw - 1) // sw)
    nfar = jnp.clip((q0 - (NEAR_BLOCKS - 1) * LANE - k0) // sw, 0, nsub)

    def far_body(c, carry):
        sub_tile(c, False)
        return carry

    def near_body(c, carry):
        sub_tile(c, True)
        return carry

    lax.fori_loop(0, nfar, far_body, 0)
    lax.fori_loop(nfar, nsub, near_body, 0)

    @pl.when(k0 + tk >= vis_end)
    def _():
        for h in range(N_HEADS):
            a = acc_ref[h]
            o_ref[0, :, h * HEAD_DIM:(h + 1) * HEAD_DIM] = a[:, :HEAD_DIM] / a[:, HEAD_DIM:HEAD_DIM + 1]


def _bias_tiles(rel_table):
    d = jnp.arange(NEAR_BLOCKS + 1, dtype=jnp.int32)[:, None, None] - NEAR_BLOCKS
    rel = (d * LANE + jnp.arange(LANE, dtype=jnp.int32)[None, None, :]
           - jnp.arange(LANE, dtype=jnp.int32)[None, :, None])
    tiles = rel_table[rel_bucket(rel)] - rel_table[FAR_BUCKET][None, None, None, :]
    tiles = tiles.at[0].set(0.0)
    return tiles.transpose(0, 3, 1, 2).astype(F32)


def _dsa_attention(q, k, v, qi, ki, wi, rel_table, *, p0, n_sel, tq, tk, sw, cw):
    bsz, t = q.shape[0], q.shape[1]
    l_true = k.shape[1]
    l_pad = -(-l_true // tk) * tk
    pad = l_pad - l_true
    nqb = t // tq
    assert t % tq == 0 and p0 % LANE == 0 and (tq == LANE or nqb == 1) and tk % sw == 0 and tk % cw == 0
    qbt, ktt = _tri_steps(nqb, tq, tk, p0)
    nsteps = int(qbt.shape[0])
    bf = jnp.bfloat16
    q16 = (q * HEAD_DIM ** -0.5).astype(bf).transpose(0, 2, 1, 3)
    ktr = jnp.pad(k.astype(bf), ((0, 0), (0, pad), (0, 0), (0, 0))).transpose(0, 2, 3, 1)
    vx = jnp.concatenate([v.astype(bf), jnp.ones(v.shape[:3] + (1,), bf),
                          jnp.zeros(v.shape[:3] + (LANE - HEAD_DIM - 1,), bf)], axis=-1)
    vx = jnp.pad(vx, ((0, 0), (0, pad), (0, 0), (0, 0))).transpose(0, 2, 1, 3)
    qi16 = qi.astype(bf).transpose(0, 2, 1, 3)
    kit = jnp.pad(ki.astype(bf), ((0, 0), (0, pad), (0, 0))).transpose(0, 2, 1)
    w = wi * IDX_DIM ** -0.5
    bias = _bias_tiles(rel_table)[:, :, :tq, :]
    sem = ("arbitrary", "arbitrary")

    keys = pl.pallas_call(
        functools.partial(_idx_kernel, tq=tq, tk=tk, sw=sw, p0=p0, l_true=l_true),
        grid_spec=pltpu.PrefetchScalarGridSpec(
            num_scalar_prefetch=2, grid=(bsz, nsteps),
            in_specs=[pl.BlockSpec((1, IDX_HEADS, tq, IDX_DIM), lambda b, s, qt, kt: (b, 0, qt[s], 0)),
                      pl.BlockSpec((1, tq, IDX_HEADS), lambda b, s, qt, kt: (b, qt[s], 0)),
                      pl.BlockSpec((1, IDX_DIM, tk), lambda b, s, qt, kt: (b, 0, kt[s]))],
            out_specs=pl.BlockSpec((1, tq, tk), lambda b, s, qt, kt: (b, qt[s], kt[s]))),
        out_shape=jax.ShapeDtypeStruct((bsz, t, l_pad), jnp.int32),
        compiler_params=pltpu.CompilerParams(dimension_semantics=sem),
    )(qbt, ktt, qi16, w, kit)

    tau = pl.pallas_call(
        functools.partial(_thr_kernel, tq=tq, cw=cw, p0=p0, n_sel=n_sel),
        grid=(bsz, nqb),
        in_specs=[pl.BlockSpec((1, tq, l_pad), lambda b, i: (b, i, 0))],
        out_specs=pl.BlockSpec((1, tq, LANE), lambda b, i: (b, i, 0)),
        out_shape=jax.ShapeDtypeStruct((bsz, t, LANE), jnp.int32),
        compiler_params=pltpu.CompilerParams(dimension_semantics=sem, vmem_limit_bytes=DSA_VMEM_BYTES),
    )(keys)

    return pl.pallas_call(
        functools.partial(_attn_kernel, tq=tq, tk=tk, sw=sw, p0=p0),
        grid_spec=pltpu.PrefetchScalarGridSpec(
            num_scalar_prefetch=2, grid=(bsz, nsteps),
            in_specs=[pl.BlockSpec((1, N_HEADS, tq, HEAD_DIM), lambda b, s, qt, kt: (b, 0, qt[s], 0)),
                      pl.BlockSpec((1, KV_HEADS_A, HEAD_DIM, tk), lambda b, s, qt, kt: (b, 0, 0, kt[s])),
                      pl.BlockSpec((1, KV_HEADS_A, tk, LANE), lambda b, s, qt, kt: (b, 0, kt[s], 0)),
                      pl.BlockSpec((1, tq, tk), lambda b, s, qt, kt: (b, qt[s], kt[s])),
                      pl.BlockSpec((1, tq, LANE), lambda b, s, qt, kt: (b, qt[s], 0)),
                      pl.BlockSpec((NEAR_BLOCKS + 1, N_HEADS, tq, LANE), lambda b, s, qt, kt: (0, 0, 0, 0))],
            out_specs=pl.BlockSpec((1, tq, ATTN_WIDTH), lambda b, s, qt, kt: (b, qt[s], 0)),
            scratch_shapes=[pltpu.VMEM((N_HEADS, tq, LANE), F32), pltpu.VMEM((N_HEADS, tq, LANE), F32)]),
        out_shape=jax.ShapeDtypeStruct((bsz, t, ATTN_WIDTH), F32),
        compiler_params=pltpu.CompilerParams(dimension_semantics=sem, vmem_limit_bytes=DSA_VMEM_BYTES),
    )(qbt, ktt, q16, ktr, vx, keys, tau, bias)


def mixer_a_prompt(h, w_in, w_out, g_q, g_k, g_kidx, rel_table):
    q, k, v, qi, ki, wi = project_a(h, w_in, g_q, g_k, g_kidx)
    seq = h.shape[1]
    o = _dsa_attention(q, k, v, qi, ki, wi, rel_table, p0=0, n_sel=min(TOPK_MAX, seq // 4),
                       tq=LANE, tk=1024, sw=256, cw=512)
    return o @ w_out, (k, v, ki)


def mixer_a_sample(h, ck, cv, cki, w_in, w_out, g_q, g_k, g_kidx, rel_table):
    q, k, v, qi, ki, wi = project_a(h, w_in, g_q, g_k, g_kidx)
    past, n = ck.shape[1], h.shape[1]
    kf = jnp.concatenate([ck.astype(k.dtype), k], axis=1)
    vf = jnp.concatenate([cv.astype(v.dtype), v], axis=1)
    kif = jnp.concatenate([cki.astype(ki.dtype), ki], axis=1)
    o = _dsa_attention(q, kf, vf, qi, kif, wi, rel_table, p0=past, n_sel=min(TOPK_MAX, (past + n) // 4),
                       tq=n, tk=2304, sw=256, cw=256)
    return o @ w_out, (k, v, ki)


def project_b(h, w_in, g_q, g_k):
    b, t, _ = h.shape
    q, k, v = jnp.split(h @ w_in, B_SPLIT, axis=-1)
    q = rms_norm(q.reshape(b, t, N_HEADS, HEAD_DIM), g_q)
    k = rms_norm(k.reshape(b, t, KV_HEADS_B, HEAD_DIM), g_k)
    v = v.reshape(b, t, KV_HEADS_B, HEAD_DIM)
    return q, k, v


def band_attend(q, k, v, key_valid, rel, sinks, rel_table):
    bsz, nblk, tq = q.shape[0], q.shape[1], q.shape[2]
    tk = k.shape[2]
    g = N_HEADS // KV_HEADS_B
    qg = q.reshape(bsz, nblk, tq, KV_HEADS_B, g, HEAD_DIM)
    logits = jnp.einsum('bnqkgd,bnskd->bnkgqs', qg, k).astype(F32) * HEAD_DIM ** -0.5
    bias = rel_table[rel_bucket(rel)].reshape(tq, tk, KV_HEADS_B, g).transpose(2, 3, 0, 1)
    logits = jnp.where(key_valid[None, :, None, None, None, :], logits + bias.astype(F32), -jnp.inf)
    sink = sinks.astype(F32).reshape(KV_HEADS_B, g)[:, :, None, None]
    m = jnp.maximum(jnp.max(logits, axis=-1, keepdims=True), sink)
    e = jnp.exp(logits - m)
    p = e / (jnp.sum(e, axis=-1, keepdims=True) + jnp.exp(sink - m))
    out = jnp.einsum('bnkgqs,bnskd->bnqkgd', p.astype(v.dtype), v)
    return out.reshape(bsz, nblk, tq, ATTN_WIDTH)


def mixer_b_prompt(h, w_in, w_out, g_q, g_k, sinks, rel_table):
    q, k, v = project_b(h, w_in, g_q, g_k)
    bsz, seq = h.shape[0], h.shape[1]
    nc = seq // CHUNK
    tk = (WIN_CHUNKS + 1) * CHUNK

    def band(a):
        ac = a.reshape(bsz, nc, CHUNK, *a.shape[2:])
        ap = jnp.pad(ac, ((0, 0), (WIN_CHUNKS, 0), (0, 0), (0, 0), (0, 0)))
        return jnp.concatenate([ap[:, j:j + nc] for j in range(WIN_CHUNKS + 1)], axis=2)

    qc = q.reshape(bsz, nc, CHUNK, N_HEADS, HEAD_DIM)
    band_chunk = (jnp.arange(nc)[:, None] - WIN_CHUNKS
                  + (jnp.arange(tk) // CHUNK)[None, :])
    key_valid = band_chunk >= 0
    rel = jnp.arange(tk)[None, :] - WINDOW - jnp.arange(CHUNK)[:, None]
    o = band_attend(qc, band(k), band(v), key_valid, rel, sinks, rel_table)
    o = o.reshape(bsz, seq, ATTN_WIDTH)
    return o @ w_out, (k[:, -WINDOW:], v[:, -WINDOW:])


def mixer_b_sample(h, ck, cv, w_in, w_out, g_q, g_k, sinks, rel_table):
    q, k, v = project_b(h, w_in, g_q, g_k)
    n = h.shape[1]
    kf = jnp.concatenate([ck.astype(k.dtype), k], axis=1)
    vf = jnp.concatenate([cv.astype(v.dtype), v], axis=1)
    tk = WINDOW + n
    rel = jnp.arange(tk)[None, :] - WINDOW - jnp.arange(n)[:, None]
    key_valid = jnp.ones((1, tk), dtype=bool)
    o = band_attend(q[:, None], kf[:, None], vf[:, None], key_valid, rel, sinks, rel_table)[:, 0]
    return o @ w_out, (kf[:, -WINDOW:], vf[:, -WINDOW:])


def swiglu(h, w_gate, w_up, w_down):
    return (jax.nn.silu(h @ w_gate) * (h @ w_up)) @ w_down


def moe_swiglu(h, w_router, w_gate, w_up, w_down):
    logits = (h @ w_router).astype(F32)
    top_v, top_i = lax.top_k(logits, TOP_K)
    gates = jax.nn.softmax(top_v, axis=-1)
    out = jnp.zeros_like(h)
    for e in range(N_EXPERTS):
        ge = jnp.sum(jnp.where(top_i == e, gates, 0.0), axis=-1)[..., None].astype(h.dtype)
        out = out + ge * swiglu(h, w_gate[e], w_up[e], w_down[e])
    return out


def sublayers(x, c, mix_fn, ffn_fn, g, w, b):
    sh1, sc1, gt1, sh2, sc2, gt2 = adaln(c, w, b)
    m, st = mix_fn(modulate(x, g[0], sh1, sc1))
    x = x + gt1 * m
    x = x + gt2 * ffn_fn(modulate(x, g[1], sh2, sc2))
    return x, st


def kernel(x_prompt, x_sample, c_prompt, c_sample, cache_a_k, cache_a_v, cache_a_kidx,
           cache_b_k, cache_b_v, rel_bias, norm_g, w_ada, b_ada,
           a_w_in, a_w_out, a_g_q, a_g_k, a_g_kidx,
           b_w_in, b_w_out, b_g_q, b_g_k, b_sinks,
           ffn_w_gate, ffn_w_up, ffn_w_down,
           moe_w_router, moe_w_gate, moe_w_up, moe_w_down):
    xp, xs = x_prompt, x_sample
    a_kp, a_vp, a_ip, a_ks, a_vs, a_is = [], [], [], [], [], []
    b_kp, b_vp, b_ks, b_vs = [], [], [], []
    for i in range(DEPTH):
        j = i // 2
        if i % 2 == 0:
            a_par = (a_w_in[j], a_w_out[j], a_g_q[j], a_g_k[j], a_g_kidx[j], rel_bias)
            ffn = functools.partial(swiglu, w_gate=ffn_w_gate[j], w_up=ffn_w_up[j],
                                    w_down=ffn_w_down[j])
            mix_p = functools.partial(mixer_a_prompt, w_in=a_par[0], w_out=a_par[1], g_q=a_par[2],
                                      g_k=a_par[3], g_kidx=a_par[4], rel_table=a_par[5])
            mix_s = functools.partial(mixer_a_sample, ck=cache_a_k[j], cv=cache_a_v[j],
                                      cki=cache_a_kidx[j], w_in=a_par[0], w_out=a_par[1],
                                      g_q=a_par[2], g_k=a_par[3], g_kidx=a_par[4],
                                      rel_table=a_par[5])
            xp, (k1, v1, i1) = sublayers(xp, c_prompt, mix_p, ffn, norm_g[i], w_ada[i], b_ada[i])
            xs, (k2, v2, i2) = sublayers(xs, c_sample, mix_s, ffn, norm_g[i], w_ada[i], b_ada[i])
            a_kp.append(k1); a_vp.append(v1); a_ip.append(i1)
            a_ks.append(k2); a_vs.append(v2); a_is.append(i2)
        else:
            ffn = functools.partial(moe_swiglu, w_router=moe_w_router[j], w_gate=moe_w_gate[j],
                                    w_up=moe_w_up[j], w_down=moe_w_down[j])
            mix_p = functools.partial(mixer_b_prompt, w_in=b_w_in[j], w_out=b_w_out[j],
                                      g_q=b_g_q[j], g_k=b_g_k[j], sinks=b_sinks[j],
                                      rel_table=rel_bias)
            mix_s = functools.partial(mixer_b_sample, ck=cache_b_k[j], cv=cache_b_v[j],
                                      w_in=b_w_in[j], w_out=b_w_out[j], g_q=b_g_q[j],
                                      g_k=b_g_k[j], sinks=b_sinks[j], rel_table=rel_bias)
            xp, (k1, v1) = sublayers(xp, c_prompt, mix_p, ffn, norm_g[i], w_ada[i], b_ada[i])
            xs, (k2, v2) = sublayers(xs, c_sample, mix_s, ffn, norm_g[i], w_ada[i], b_ada[i])
            b_kp.append(k1); b_vp.append(v1)
            b_ks.append(k2); b_vs.append(v2)
    return (xp, xs,
            jnp.stack(a_kp), jnp.stack(a_vp), jnp.stack(a_ip),
            jnp.stack(a_ks), jnp.stack(a_vs), jnp.stack(a_is),
            jnp.stack(b_kp), jnp.stack(b_vp),
            jnp.stack(b_ks), jnp.stack(b_vs))
```

```python
import math, functools
import jax, jax.numpy as jnp
from jax import lax
import numpy as np
from jax.experimental import pallas as pl
from jax.experimental.pallas import tpu as pltpu

D_MODEL = 1024
DEPTH = 2
CHUNK = 64
N_HEADS = 16
HEAD_DIM = 64
ATTN_WIDTH = N_HEADS * HEAD_DIM
KV_HEADS_A = 4
IDX_HEADS = 8
IDX_DIM = 64
TOPK_MAX = 256
KV_HEADS_B = 2
WINDOW = 128
NUM_BUCKETS = 32
MAX_DISTANCE = 1024
N_EXPERTS = 8
TOP_K = 2
EPS = 1e-6

F32 = jnp.float32
BF16 = jnp.bfloat16
LANE = 128
VMEM_LIMIT_BYTES = 48 * 1024 * 1024
NEG_BIG = -1e30
KEY_MIN = -2 ** 31
FAR_BUCKET = NUM_BUCKETS // 2 - 1
NEAR_BLOCKS = 6
FF_TILE = 512
MIX_TM_PROMPT = 512
MIX_TM_SAMPLE = 256
ARB2 = ("arbitrary", "arbitrary")


def _params(sem, vmem=True):
    return pltpu.CompilerParams(dimension_semantics=sem, vmem_limit_bytes=VMEM_LIMIT_BYTES if vmem else None)


def rel_bucket(rel):
    half = NUM_BUCKETS // 2
    max_exact = half // 2
    base = jnp.where(rel > 0, half, 0)
    n = jnp.abs(rel)
    nf = jnp.maximum(n, 1).astype(F32)
    large = max_exact + (jnp.log(nf / max_exact) / math.log(MAX_DISTANCE / max_exact)
                         * (half - max_exact)).astype(jnp.int32)
    large = jnp.minimum(large, half - 1)
    return base + jnp.where(n < max_exact, n, large)


def _adaln_kernel(c_ref, w_ref, b_ref, o_ref):
    c = c_ref[...]
    o_ref[...] = jnp.dot(c * jax.nn.sigmoid(c), w_ref[...], preferred_element_type=F32) + b_ref[...]


def adaln(c, w, b):
    n = c.shape[0]
    mod = pl.pallas_call(
        _adaln_kernel,
        grid=(6,),
        in_specs=[pl.BlockSpec((n, D_MODEL), lambda j: (0, 0)),
                  pl.BlockSpec((D_MODEL, D_MODEL), lambda j: (0, j)),
                  pl.BlockSpec((1, D_MODEL), lambda j: (0, j))],
        out_specs=pl.BlockSpec((n, D_MODEL), lambda j: (0, j)),
        out_shape=jax.ShapeDtypeStruct((n, 6 * D_MODEL), F32),
        name="adaln",
    )(c, w, b[None, :])
    return jnp.split(mod[:, None, :], 6, axis=-1)


def _modulate(x, g, shift, scale):
    y = x * lax.rsqrt(jnp.mean(x * x, axis=-1, keepdims=True) + EPS)
    return (y * g) * (1.0 + scale) + shift


def _seg_norm(seg, g):
    return seg * lax.rsqrt(jnp.mean(seg * seg, axis=-1, keepdims=True) + EPS) * g


def _proj_kernel(x_ref, g_ref, sh_ref, sc_ref, w_ref, gq_ref, gk_ref, gki_ref, *outs, n_kv, has_idx):
    h = _modulate(x_ref[0], g_ref[...], sh_ref[0], sc_ref[0]).astype(BF16)
    y = jnp.dot(h, w_ref[...], preferred_element_type=F32)
    tm = y.shape[0]
    kv_w = n_kv * HEAD_DIM
    if has_idx:
        q16, kf, vf, ktr, vx, kif, qi16, kit, wsc = outs
    else:
        q16, kf, vf, ktr, v16 = outs
    for hd in range(N_HEADS):
        seg = y[:, hd * HEAD_DIM:(hd + 1) * HEAD_DIM]
        q16[0, hd] = (_seg_norm(seg, gq_ref[...]) * HEAD_DIM ** -0.5).astype(BF16)
    lane = lax.broadcasted_iota(jnp.int32, (tm, LANE - HEAD_DIM), 1)
    ones_col = jnp.where(lane == 0, 1.0, 0.0).astype(BF16)
    for j in range(n_kv):
        kseg = _seg_norm(y[:, ATTN_WIDTH + j * HEAD_DIM:ATTN_WIDTH + (j + 1) * HEAD_DIM], gk_ref[...])
        vseg = y[:, ATTN_WIDTH + kv_w + j * HEAD_DIM:ATTN_WIDTH + kv_w + (j + 1) * HEAD_DIM]
        kf[0, :, j * HEAD_DIM:(j + 1) * HEAD_DIM] = kseg
        vf[0, :, j * HEAD_DIM:(j + 1) * HEAD_DIM] = vseg
        ktr[0, j] = kseg.T.astype(BF16)
        if has_idx:
            vx[0, j] = jnp.concatenate([vseg.astype(BF16), ones_col], axis=1)
        else:
            v16[0, j] = vseg.astype(BF16)
    if has_idx:
        base = ATTN_WIDTH + 2 * kv_w
        for hd in range(IDX_HEADS):
            qi16[0, hd] = y[:, base + hd * IDX_DIM:base + (hd + 1) * IDX_DIM].astype(BF16)
        base += IDX_HEADS * IDX_DIM
        kiseg = _seg_norm(y[:, base:base + IDX_DIM], gki_ref[...])
        kif[0] = kiseg
        kit[0] = kiseg.T.astype(BF16)
        wsc[0] = y[:, base + IDX_DIM:base + IDX_DIM + IDX_HEADS] * IDX_HEADS ** -0.5 * IDX_DIM ** -0.5


def _project(x, g, shift, scale, w_in, g_q, g_k, g_kidx, *, n_kv, has_idx, tm):
    gsz, t, _ = x.shape
    n_in = w_in.shape[1]
    kv_w = n_kv * HEAD_DIM
    tok = lambda b, i: (b, i, 0)
    head = lambda b, i: (b, 0, i, 0)
    vec = lambda b, i: (b, 0, 0)
    cst = lambda b, i: (0, 0)
    shapes = [((gsz, N_HEADS, t, HEAD_DIM), BF16, (1, N_HEADS, tm, HEAD_DIM), head),
              ((gsz, t, kv_w), F32, (1, tm, kv_w), tok),
              ((gsz, t, kv_w), F32, (1, tm, kv_w), tok),
              ((gsz, n_kv, HEAD_DIM, t), BF16, (1, n_kv, HEAD_DIM, tm), lambda b, i: (b, 0, 0, i))]
    if has_idx:
        shapes += [((gsz, n_kv, t, LANE), BF16, (1, n_kv, tm, LANE), head),
                   ((gsz, t, IDX_DIM), F32, (1, tm, IDX_DIM), tok),
                   ((gsz, IDX_HEADS, t, IDX_DIM), BF16, (1, IDX_HEADS, tm, IDX_DIM), head),
                   ((gsz, IDX_DIM, t), BF16, (1, IDX_DIM, tm), lambda b, i: (b, 0, i)),
                   ((gsz, t, IDX_HEADS), F32, (1, tm, IDX_HEADS), tok)]
    else:
        shapes += [((gsz, n_kv, t, HEAD_DIM), BF16, (1, n_kv, tm, HEAD_DIM), head)]
    return pl.pallas_call(
        functools.partial(_proj_kernel, n_kv=n_kv, has_idx=has_idx),
        grid=(gsz, t // tm),
        in_specs=[pl.BlockSpec((1, tm, D_MODEL), tok), pl.BlockSpec((1, D_MODEL), cst),
                  pl.BlockSpec((1, 1, D_MODEL), vec), pl.BlockSpec((1, 1, D_MODEL), vec),
                  pl.BlockSpec((D_MODEL, n_in), cst), pl.BlockSpec((1, HEAD_DIM), cst),
                  pl.BlockSpec((1, HEAD_DIM), cst), pl.BlockSpec((1, IDX_DIM), cst)],
        out_specs=[pl.BlockSpec(blk, im) for _, _, blk, im in shapes],
        out_shape=[jax.ShapeDtypeStruct(s, d) for s, d, _, _ in shapes],
        compiler_params=_params(ARB2),
        name="mixer_in_proj",
    )(x, g[None, :], shift, scale, w_in.astype(BF16), g_q[None, :], g_k[None, :], g_kidx[None, :])


def _tri_steps(nqb, tq, tk, p0):
    qbs, kts = [], []
    for qb in range(nqb):
        vis_end = ((p0 + qb * tq + tq - 1) // CHUNK + 1) * CHUNK
        for kt in range(-(-vis_end // tk)):
            qbs.append(qb)
            kts.append(kt)
    return jnp.asarray(np.array(qbs, np.int32)), jnp.asarray(np.array(kts, np.int32))


def _idx_kernel(qb_tab, kt_tab, qi_ref, w_ref, kit_ref, out_ref, *, tq, tk, sw, p0, l_true):
    step = pl.program_id(1)
    q0 = p0 + qb_tab[step] * tq
    k0 = kt_tab[step] * tk
    row = lax.broadcasted_iota(jnp.int32, (tq, 1), 0) + q0
    vis_end = jnp.minimum(((row >> 6) + 1) << 6, l_true)
    wv = w_ref[0]
    for c in range(tk // sw):
        kit = kit_ref[0, :, c * sw:(c + 1) * sw]
        acc = jnp.zeros((tq, sw), F32)
        for h in range(IDX_HEADS):
            sh = jnp.dot(qi_ref[0, h], kit, preferred_element_type=F32)
            acc = acc + wv[:, h:h + 1] * jnp.maximum(sh, 0.0)
        bits = lax.bitcast_convert_type(acc, jnp.int32)
        key = bits ^ ((bits >> 31) & 0x7FFFFFFF)
        kpos = k0 + c * sw + lax.broadcasted_iota(jnp.int32, (tq, sw), 1)
        out_ref[0, :, c * sw:(c + 1) * sw] = jnp.where(kpos < vis_end, key, KEY_MIN)


def _thr_kernel(keys_ref, tau_ref, *, tq, cw, p0, n_sel):
    qb = pl.program_id(1)
    vis_end = (((p0 + qb * tq + tq - 1) >> 6) + 1) << 6
    nch = (vis_end + cw - 1) // cw

    def count_ge(cand):
        def body(j, acc):
            off = pl.multiple_of(j * cw, cw)
            for u in range(cw // LANE):
                x = keys_ref[0, :, pl.ds(off + u * LANE, LANE)]
                acc = acc + jnp.where(x >= cand, 1.0, 0.0)
            return acc
        acc = lax.fori_loop(0, nch, body, jnp.zeros((tq, LANE), F32))
        return jnp.sum(acc, axis=1, keepdims=True)

    zero = jnp.zeros((tq, LANE), jnp.int32)
    tau = jnp.where(count_ge(zero) >= n_sel, zero, KEY_MIN)

    def bit_body(i, tau):
        cand = tau + jnp.left_shift(jnp.int32(1), 30 - i)
        return jnp.where(count_ge(cand) >= n_sel, cand, tau)

    tau = lax.fori_loop(0, 31, bit_body, tau)
    tau_ref[0] = jnp.maximum(tau, KEY_MIN + 1)


def _attn_kernel(qb_tab, kt_tab, q_ref, kt_ref, vx_ref, keys_ref, tau_ref, bias_ref, o_ref, m_ref, acc_ref,
                 *, tq, tk, sw, p0):
    step = pl.program_id(1)
    qb, kt = qb_tab[step], kt_tab[step]
    q0 = p0 + qb * tq
    k0 = kt * tk
    vis_end = (((q0 + tq - 1) >> 6) + 1) << 6
    group = N_HEADS // KV_HEADS_A
    nblk = sw // LANE

    @pl.when(kt == 0)
    def _():
        m_ref[...] = jnp.full(m_ref.shape, NEG_BIG, F32)
        acc_ref[...] = jnp.zeros(acc_ref.shape, F32)

    tau = jnp.concatenate([tau_ref[0]] * nblk, axis=1)

    def sub_tile(c, with_bias):
        off = pl.multiple_of(c * sw, sw)
        negm = jnp.where(keys_ref[0, :, pl.ds(off, sw)] >= tau, 0.0, NEG_BIG)
        if with_bias:
            d0 = ((k0 + off - q0) >> 7) + NEAR_BLOCKS
            bidx = [jnp.clip(d0 + u, 0, NEAR_BLOCKS) for u in range(nblk)]
        for h in range(N_HEADS):
            j = h // group
            s = jnp.dot(q_ref[0, h], kt_ref[0, j, :, pl.ds(off, sw)], preferred_element_type=F32) + negm
            if with_bias:
                s = s + jnp.concatenate([bias_ref[bidx[u], h] for u in range(nblk)], axis=1)
            m_prev = m_ref[h]
            m_cur = jnp.maximum(m_prev, jnp.max(s, axis=1, keepdims=True))
            alpha = jnp.exp(m_prev - m_cur)
            p = jnp.exp(s - jnp.concatenate([m_cur] * nblk, axis=1))
            pv = jnp.dot(p.astype(BF16), vx_ref[0, j, pl.ds(off, sw), :], preferred_element_type=F32)
            acc_ref[h] = alpha * acc_ref[h] + pv
            m_ref[h] = m_cur

    nsub = jnp.minimum(tk // sw, (vis_end - k0 + sw - 1) // sw)
    nfar = jnp.clip((q0 - (NEAR_BLOCKS - 1) * LANE - k0) // sw, 0, nsub)

    def far_body(c, carry):
        sub_tile(c, False)
        return carry

    def near_body(c, carry):
        sub_tile(c, True)
        return carry

    lax.fori_loop(0, nfar, far_body, 0)
    lax.fori_loop(nfar, nsub, near_body, 0)

    @pl.when(k0 + tk >= vis_end)
    def _():
        for h in range(N_HEADS):
            a = acc_ref[h]
            o_ref[0, :, h * HEAD_DIM:(h + 1) * HEAD_DIM] = a[:, :HEAD_DIM] / a[:, HEAD_DIM:HEAD_DIM + 1]


def _rel_tiles(rel_table, rel):
    return rel_table[rel_bucket(rel)].transpose(0, 3, 1, 2).astype(F32)


def _dsa_bias_tiles(rel_table):
    d = jnp.arange(NEAR_BLOCKS + 1, dtype=jnp.int32)[:, None, None] - NEAR_BLOCKS
    rel = (d * LANE + jnp.arange(LANE, dtype=jnp.int32)[None, None, :]
           - jnp.arange(LANE, dtype=jnp.int32)[None, :, None])
    tiles = _rel_tiles(rel_table, rel) - rel_table[FAR_BUCKET][None, :, None, None]
    return tiles.at[0].set(0.0)


def _dsa_attention(q16, ktr, vx, qi16, kit, w, bias, *, p0, l_true, n_sel, tq, tk, sw, cw):
    bsz, t = q16.shape[0], q16.shape[2]
    l_pad = ktr.shape[3]
    nqb = t // tq
    assert t % tq == 0 and p0 % LANE == 0 and (tq == LANE or nqb == 1)
    assert l_pad % tk == 0 and tk % sw == 0 and tk % cw == 0
    qbt, ktt = _tri_steps(nqb, tq, tk, p0)
    nsteps = int(qbt.shape[0])

    keys = pl.pallas_call(
        functools.partial(_idx_kernel, tq=tq, tk=tk, sw=sw, p0=p0, l_true=l_true),
        grid_spec=pltpu.PrefetchScalarGridSpec(
            num_scalar_prefetch=2, grid=(bsz, nsteps),
            in_specs=[pl.BlockSpec((1, IDX_HEADS, tq, IDX_DIM), lambda b, s, qt, kt: (b, 0, qt[s], 0)),
                      pl.BlockSpec((1, tq, IDX_HEADS), lambda b, s, qt, kt: (b, qt[s], 0)),
                      pl.BlockSpec((1, IDX_DIM, tk), lambda b, s, qt, kt: (b, 0, kt[s]))],
            out_specs=pl.BlockSpec((1, tq, tk), lambda b, s, qt, kt: (b, qt[s], kt[s]))),
        out_shape=jax.ShapeDtypeStruct((bsz, t, l_pad), jnp.int32),
        compiler_params=_params(ARB2, vmem=False),
        name="dsa_index_keys",
    )(qbt, ktt, qi16, w, kit)

    tau = pl.pallas_call(
        functools.partial(_thr_kernel, tq=tq, cw=cw, p0=p0, n_sel=n_sel),
        grid=(bsz, nqb),
        in_specs=[pl.BlockSpec((1, tq, l_pad), lambda b, i: (b, i, 0))],
        out_specs=pl.BlockSpec((1, tq, LANE), lambda b, i: (b, i, 0)),
        out_shape=jax.ShapeDtypeStruct((bsz, t, LANE), jnp.int32),
        compiler_params=_params(ARB2),
        name="dsa_threshold",
    )(keys)

    return pl.pallas_call(
        functools.partial(_attn_kernel, tq=tq, tk=tk, sw=sw, p0=p0),
        grid_spec=pltpu.PrefetchScalarGridSpec(
            num_scalar_prefetch=2, grid=(bsz, nsteps),
            in_specs=[pl.BlockSpec((1, N_HEADS, tq, HEAD_DIM), lambda b, s, qt, kt: (b, 0, qt[s], 0)),
                      pl.BlockSpec((1, KV_HEADS_A, HEAD_DIM, tk), lambda b, s, qt, kt: (b, 0, 0, kt[s])),
                      pl.BlockSpec((1, KV_HEADS_A, tk, LANE), lambda b, s, qt, kt: (b, 0, kt[s], 0)),
                      pl.BlockSpec((1, tq, tk), lambda b, s, qt, kt: (b, qt[s], kt[s])),
                      pl.BlockSpec((1, tq, LANE), lambda b, s, qt, kt: (b, qt[s], 0)),
                      pl.BlockSpec((NEAR_BLOCKS + 1, N_HEADS, tq, LANE), lambda b, s, qt, kt: (0, 0, 0, 0))],
            out_specs=pl.BlockSpec((1, tq, ATTN_WIDTH), lambda b, s, qt, kt: (b, qt[s], 0)),
            scratch_shapes=[pltpu.VMEM((N_HEADS, tq, LANE), F32), pltpu.VMEM((N_HEADS, tq, LANE), F32)]),
        out_shape=jax.ShapeDtypeStruct((bsz, t, ATTN_WIDTH), F32),
        compiler_params=_params(ARB2),
        name="dsa_masked_attention",
    )(qbt, ktt, q16, ktr, vx, keys, tau, bias[:, :, :tq, :])


def _band_kernel(q_ref, kp_ref, kc_ref, vp_ref, vc_ref, bias_ref, sink_ref, o_ref, *, tq, first_has_no_prev):
    group = N_HEADS // KV_HEADS_B
    extra = None
    if first_has_no_prev:
        col = lax.broadcasted_iota(jnp.int32, (tq, 2 * LANE), 1)
        extra = jnp.where((pl.program_id(1) == 0) & (col < LANE), NEG_BIG, 0.0)
    for h in range(N_HEADS):
        j = h // group
        q = q_ref[0, h]
        s = jnp.concatenate([jnp.dot(q, kp_ref[0, j], preferred_element_type=F32),
                             jnp.dot(q, kc_ref[0, j], preferred_element_type=F32)], axis=1) + bias_ref[h]
        if extra is not None:
            s = s + extra
        sink = sink_ref[h]
        m = jnp.maximum(jnp.max(s, axis=1, keepdims=True), sink[:, :1])
        e = jnp.exp(s - m)
        p = e / (jnp.sum(e, axis=1, keepdims=True) + jnp.exp(sink[:, :1] - m))
        pb = p.astype(BF16)
        o_ref[0, :, h * HEAD_DIM:(h + 1) * HEAD_DIM] = (
            jnp.dot(pb[:, :LANE], vp_ref[0, j], preferred_element_type=F32)
            + jnp.dot(pb[:, LANE:], vc_ref[0, j], preferred_element_type=F32))


def _band_bias(rel_table):
    r = jnp.arange(LANE, dtype=jnp.int32)[:, None]
    s = jnp.arange(2 * LANE, dtype=jnp.int32)[None, :]
    wc, qh = s // CHUNK, r // CHUNK
    band = (wc >= qh) & (wc <= qh + 2)
    tiles = _rel_tiles(rel_table, (s - LANE - r)[None])[0]
    return jnp.where(band[None], tiles, NEG_BIG)


def _band_attention(q16, kp, kc, vp, vc, bias, sinks, *, tq, prev_map, cur_map, first_has_no_prev):
    bsz, t = q16.shape[0], q16.shape[2]
    kblk = lambda m: pl.BlockSpec((1, KV_HEADS_B, HEAD_DIM, LANE), lambda b, i: (b, 0, 0, m(i)))
    vblk = lambda m: pl.BlockSpec((1, KV_HEADS_B, LANE, HEAD_DIM), lambda b, i: (b, 0, m(i), 0))
    sink_rows = jnp.broadcast_to(sinks.astype(F32)[:, None, None], (N_HEADS, 1, LANE))
    return pl.pallas_call(
        functools.partial(_band_kernel, tq=tq, first_has_no_prev=first_has_no_prev),
        grid=(bsz, t // tq),
        in_specs=[pl.BlockSpec((1, N_HEADS, tq, HEAD_DIM), lambda b, i: (b, 0, i, 0)),
                  kblk(prev_map), kblk(cur_map), vblk(prev_map), vblk(cur_map),
                  pl.BlockSpec((N_HEADS, tq, 2 * LANE), lambda b, i: (0, 0, 0)),
                  pl.BlockSpec((N_HEADS, 1, LANE), lambda b, i: (0, 0, 0))],
        out_specs=pl.BlockSpec((1, tq, ATTN_WIDTH), lambda b, i: (b, i, 0)),
        out_shape=jax.ShapeDtypeStruct((bsz, t, ATTN_WIDTH), F32),
        compiler_params=_params(ARB2, vmem=False),
        name="band_attention",
    )(q16, kp, kc, vp, vc, bias[:, :tq, :], sink_rows)


def _mixer_residual(x_ref, o_ref, wo_ref, gt1_ref):
    return x_ref[0] + gt1_ref[0] * jnp.dot(o_ref[0].astype(BF16), wo_ref[...], preferred_element_type=F32)


def _swiglu_chunk(h, wg, wu, wd):
    a = jax.nn.silu(jnp.dot(h, wg, preferred_element_type=F32)) * jnp.dot(h, wu, preferred_element_type=F32)
    return jnp.dot(a.astype(BF16), wd, preferred_element_type=F32)


def _ffn_kernel(x_ref, o_ref, wo_ref, gt1_ref, g_ref, sh_ref, sc_ref, gt2_ref, wg_ref, wu_ref, wd_ref,
                out_ref, xn_scr, h_scr, acc_scr):
    c = pl.program_id(2)

    @pl.when(c == 0)
    def _():
        xn = _mixer_residual(x_ref, o_ref, wo_ref, gt1_ref)
        xn_scr[...] = xn
        h_scr[...] = _modulate(xn, g_ref[...], sh_ref[0], sc_ref[0]).astype(BF16)
        acc_scr[...] = jnp.zeros(acc_scr.shape, F32)

    acc_scr[...] += _swiglu_chunk(h_scr[...], wg_ref[...], wu_ref[...], wd_ref[...])

    @pl.when(c == pl.num_programs(2) - 1)
    def _():
        out_ref[0] = xn_scr[...] + gt2_ref[0] * acc_scr[...]


def _moe_kernel(x_ref, o_ref, wo_ref, gt1_ref, g_ref, sh_ref, sc_ref, gt2_ref, wr_ref, wg_ref, wu_ref, wd_ref,
                out_ref, xn_scr, h_scr, gate_scr, acc_scr, acce_scr):
    e, c = pl.program_id(2), pl.program_id(3)
    last_c = c == pl.num_programs(3) - 1

    @pl.when((e == 0) & (c == 0))
    def _():
        xn = _mixer_residual(x_ref, o_ref, wo_ref, gt1_ref)
        xn_scr[...] = xn
        h = _modulate(xn, g_ref[...], sh_ref[0], sc_ref[0]).astype(BF16)
        h_scr[...] = h
        lane = lax.broadcasted_iota(jnp.int32, (h.shape[0], LANE), 1)
        lanef = lane.astype(F32)
        logits = jnp.where(lane < N_EXPERTS, jnp.dot(h, wr_ref[...], preferred_element_type=F32), -jnp.inf)
        m1 = jnp.max(logits, axis=1, keepdims=True)
        i1 = jnp.min(jnp.where(logits == m1, lanef, float(LANE)), axis=1, keepdims=True)
        rest = jnp.where(lanef == i1, -jnp.inf, logits)
        m2 = jnp.max(rest, axis=1, keepdims=True)
        i2 = jnp.min(jnp.where(rest == m2, lanef, float(LANE)), axis=1, keepdims=True)
        e2 = jnp.exp(m2 - m1)
        den = 1.0 + e2
        gate_scr[...] = jnp.where(lanef == i1, 1.0 / den, 0.0) + jnp.where(lanef == i2, e2 / den, 0.0)
        acc_scr[...] = jnp.zeros(acc_scr.shape, F32)

    @pl.when(c == 0)
    def _():
        acce_scr[...] = jnp.zeros(acce_scr.shape, F32)

    acce_scr[...] += _swiglu_chunk(h_scr[...], wg_ref[0], wu_ref[0], wd_ref[0])

    @pl.when(last_c)
    def _():
        lane = lax.broadcasted_iota(jnp.int32, gate_scr.shape, 1)
        ge = jnp.sum(jnp.where(lane == e, gate_scr[...], 0.0), axis=1, keepdims=True)
        acc_scr[...] += ge * acce_scr[...]

    @pl.when(last_c & (e == pl.num_programs(2) - 1))
    def _():
        out_ref[0] = xn_scr[...] + gt2_ref[0] * acc_scr[...]


def _channel_mixer(x, o, w_out, gt1, g, shift, scale, gt2, w_router, w_gate, w_up, w_down, *, tm):
    gsz, t, _ = x.shape
    moe = w_router is not None
    d_ff = w_gate.shape[-1]
    assert t % tm == 0 and d_ff % FF_TILE == 0
    nc = d_ff // FF_TILE
    nd = 4 if moe else 3

    def fix(f):
        return (lambda b, i, e, c: f(b, i, e, c)) if moe else (lambda b, i, c: f(b, i, 0, c))

    tok = pl.BlockSpec((1, tm, D_MODEL), fix(lambda b, i, e, c: (b, i, 0)))

    def mod_spec(a):
        if a.shape[1] == 1:
            return pl.BlockSpec((1, 1, D_MODEL), fix(lambda b, i, e, c: (b, 0, 0)))
        return tok

    cst = lambda shape: pl.BlockSpec(shape, fix(lambda b, i, e, c: (0, 0)))
    in_specs = [tok, tok, cst((ATTN_WIDTH, D_MODEL)), mod_spec(gt1), cst((1, D_MODEL)),
                mod_spec(shift), mod_spec(scale), mod_spec(gt2)]
    args = [x, o, w_out.astype(BF16), gt1, g[None, :], shift, scale, gt2]
    scratch = [pltpu.VMEM((tm, D_MODEL), F32), pltpu.VMEM((tm, D_MODEL), BF16)]
    if moe:
        wr = jnp.pad(w_router.astype(BF16), ((0, 0), (0, LANE - N_EXPERTS)))
        in_specs += [cst((D_MODEL, LANE)),
                     pl.BlockSpec((1, D_MODEL, FF_TILE), lambda b, i, e, c: (e, 0, c)),
                     pl.BlockSpec((1, D_MODEL, FF_TILE), lambda b, i, e, c: (e, 0, c)),
                     pl.BlockSpec((1, FF_TILE, D_MODEL), lambda b, i, e, c: (e, c, 0))]
        args += [wr, w_gate.astype(BF16), w_up.astype(BF16), w_down.astype(BF16)]
        scratch += [pltpu.VMEM((tm, LANE), F32), pltpu.VMEM((tm, D_MODEL), F32), pltpu.VMEM((tm, D_MODEL), F32)]
        grid, body, name = (gsz, t // tm, N_EXPERTS, nc), _moe_kernel, "moe_channel_mixer"
    else:
        in_specs += [pl.BlockSpec((D_MODEL, FF_TILE), lambda b, i, c: (0, c)),
                     pl.BlockSpec((D_MODEL, FF_TILE), lambda b, i, c: (0, c)),
                     pl.BlockSpec((FF_TILE, D_MODEL), lambda b, i, c: (c, 0))]
        args += [w_gate.astype(BF16), w_up.astype(BF16), w_down.astype(BF16)]
        scratch += [pltpu.VMEM((tm, D_MODEL), F32)]
        grid, body, name = (gsz, t // tm, nc), _ffn_kernel, "swiglu_channel_mixer"
    return pl.pallas_call(
        body, grid=grid, in_specs=in_specs, out_specs=tok,
        out_shape=jax.ShapeDtypeStruct((gsz, t, D_MODEL), F32),
        scratch_shapes=scratch,
        compiler_params=_params(("arbitrary",) * nd),
        name=name,
    )(*args)


def _per_token(a, n):
    return jnp.broadcast_to(a, (a.shape[0], n, a.shape[2])).reshape(1, a.shape[0] * n, a.shape[2])


def _pad_keys(a, axis, l_pad):
    pad = [(0, 0)] * a.ndim
    pad[axis] = (0, l_pad - a.shape[axis])
    return jnp.pad(a, pad)


def _layer_a(xp, xs, cp, cs, ck, cv, cki, rel_bias, norm_g, w_ada, b_ada, w_in, w_out, g_q, g_k, g_kidx,
             ffn_wg, ffn_wu, ffn_wd):
    bias = _dsa_bias_tiles(rel_bias)
    nb, n = xs.shape[0], xs.shape[1]
    past = ck.shape[1]
    outs = []
    for x, c, sample in ((xp, cp, False), (xs, cs, True)):
        sh1, sc1, gt1, sh2, sc2, gt2 = adaln(c, w_ada, b_ada)
        q16, kf, vf, ktr, vx, kif, qi16, kit, wsc = _project(
            x, norm_g[0], sh1, sc1, w_in, g_q, g_k, g_kidx, n_kv=KV_HEADS_A, has_idx=True,
            tm=n if sample else 256)
        if sample:
            l_true, l_pad = past + n, 2304
            kfull = jnp.concatenate([ck, kf.reshape(nb, n, KV_HEADS_A, HEAD_DIM)], axis=1).astype(BF16)
            vfull = jnp.concatenate([cv, vf.reshape(nb, n, KV_HEADS_A, HEAD_DIM)], axis=1).astype(BF16)
            kifull = jnp.concatenate([cki, kif], axis=1).astype(BF16)
            ktr = _pad_keys(kfull, 1, l_pad).transpose(0, 2, 3, 1)
            vx = jnp.concatenate([vfull, jnp.ones(vfull.shape[:3] + (1,), BF16),
                                  jnp.zeros(vfull.shape[:3] + (LANE - HEAD_DIM - 1,), BF16)], axis=-1)
            vx = _pad_keys(vx, 1, l_pad).transpose(0, 2, 1, 3)
            kit = _pad_keys(kifull, 1, l_pad).transpose(0, 2, 1)
            o = _dsa_attention(q16, ktr, vx, qi16, kit, wsc, bias, p0=past, l_true=l_true,
                               n_sel=min(TOPK_MAX, l_true // 4), tq=n, tk=l_pad, sw=256, cw=256)
            flat = lambda a: a.reshape(1, nb * n, a.shape[-1])
            y = _channel_mixer(flat(x), flat(o), w_out, _per_token(gt1, n), norm_g[1], _per_token(sh2, n),
                               _per_token(sc2, n), _per_token(gt2, n), None, ffn_wg, ffn_wu, ffn_wd,
                               tm=MIX_TM_SAMPLE)
            y = y.reshape(x.shape)
        else:
            t = x.shape[1]
            o = _dsa_attention(q16, ktr, vx, qi16, kit, wsc, bias, p0=0, l_true=t,
                               n_sel=min(TOPK_MAX, t // 4), tq=LANE, tk=1024, sw=256, cw=512)
            y = _channel_mixer(x, o, w_out, gt1, norm_g[1], sh2, sc2, gt2, None, ffn_wg, ffn_wu, ffn_wd,
                               tm=MIX_TM_PROMPT)
        bsz, t = x.shape[0], x.shape[1]
        outs.append((y, kf.reshape(bsz, t, KV_HEADS_A, HEAD_DIM), vf.reshape(bsz, t, KV_HEADS_A, HEAD_DIM), kif))
    return outs


def _layer_b(xp, xs, cp, cs, ck, cv, rel_bias, norm_g, w_ada, b_ada, w_in, w_out, g_q, g_k, sinks,
             w_router, moe_wg, moe_wu, moe_wd):
    bias = _band_bias(rel_bias)
    nb, n = xs.shape[0], xs.shape[1]
    outs = []
    for x, c, sample in ((xp, cp, False), (xs, cs, True)):
        sh1, sc1, gt1, sh2, sc2, gt2 = adaln(c, w_ada, b_ada)
        q16, kf, vf, ktr, v16 = _project(x, norm_g[0], sh1, sc1, w_in, g_q, g_k, g_k, n_kv=KV_HEADS_B,
                                         has_idx=False, tm=n if sample else 256)
        bsz, t = x.shape[0], x.shape[1]
        k4 = kf.reshape(bsz, t, KV_HEADS_B, HEAD_DIM)
        v4 = vf.reshape(bsz, t, KV_HEADS_B, HEAD_DIM)
        zero = lambda i: 0
        if sample:
            kp = ck.astype(BF16).transpose(0, 2, 3, 1)
            vp = cv.astype(BF16).transpose(0, 2, 1, 3)
            kc = _pad_keys(ktr, 3, LANE)
            vc = _pad_keys(v16, 2, LANE)
            o = _band_attention(q16, kp, kc, vp, vc, bias, sinks, tq=n, prev_map=zero, cur_map=zero,
                                first_has_no_prev=False)
            flat = lambda a: a.reshape(1, nb * n, a.shape[-1])
            y = _channel_mixer(flat(x), flat(o), w_out, _per_token(gt1, n), norm_g[1], _per_token(sh2, n),
                               _per_token(sc2, n), _per_token(gt2, n), w_router, moe_wg, moe_wu, moe_wd,
                               tm=MIX_TM_SAMPLE)
            y = y.reshape(x.shape)
            k_new = jnp.concatenate([ck, k4], axis=1)[:, -WINDOW:]
            v_new = jnp.concatenate([cv, v4], axis=1)[:, -WINDOW:]
        else:
            o = _band_attention(q16, ktr, ktr, v16, v16, bias, sinks, tq=LANE,
                                prev_map=lambda i: jnp.maximum(i - 1, 0), cur_map=lambda i: i,
                                first_has_no_prev=True)
            y = _channel_mixer(x, o, w_out, gt1, norm_g[1], sh2, sc2, gt2, w_router, moe_wg, moe_wu, moe_wd,
                               tm=MIX_TM_PROMPT)
            k_new, v_new = k4[:, -WINDOW:], v4[:, -WINDOW:]
        outs.append((y, k_new, v_new))
    return outs


def kernel(x_prompt, x_sample, c_prompt, c_sample, cache_a_k, cache_a_v, cache_a_kidx,
           cache_b_k, cache_b_v, rel_bias, norm_g, w_ada, b_ada,
           a_w_in, a_w_out, a_g_q, a_g_k, a_g_kidx,
           b_w_in, b_w_out, b_g_q, b_g_k, b_sinks,
           ffn_w_gate, ffn_w_up, ffn_w_down,
           moe_w_router, moe_w_gate, moe_w_up, moe_w_down):
    xp, xs = x_prompt, x_sample
    a_out = [[] for _ in range(6)]
    b_out = [[] for _ in range(4)]
    for i in range(DEPTH):
        j = i // 2
        if i % 2 == 0:
            (xp, k1, v1, i1), (xs, k2, v2, i2) = _layer_a(
                xp, xs, c_prompt, c_sample, cache_a_k[j], cache_a_v[j], cache_a_kidx[j], rel_bias, norm_g[i],
                w_ada[i], b_ada[i], a_w_in[j], a_w_out[j], a_g_q[j], a_g_k[j], a_g_kidx[j],
                ffn_w_gate[j], ffn_w_up[j], ffn_w_down[j])
            for lst, val in zip(a_out, (k1, v1, i1, k2, v2, i2)):
                lst.append(val)
        else:
            (xp, k1, v1), (xs, k2, v2) = _layer_b(
                xp, xs, c_prompt, c_sample, cache_b_k[j], cache_b_v[j], rel_bias, norm_g[i],
                w_ada[i], b_ada[i], b_w_in[j], b_w_out[j], b_g_q[j], b_g_k[j], b_sinks[j],
                moe_w_router[j], moe_w_gate[j], moe_w_up[j], moe_w_down[j])
            for lst, val in zip(b_out, (k1, v1, k2, v2)):
                lst.append(val)
    return (xp, xs, *(jnp.stack(l) for l in a_out), *(jnp.stack(l) for l in b_out))
```

```python
import math, functools
import jax, jax.numpy as jnp
from jax import lax
import numpy as np
from jax.experimental import pallas as pl
from jax.experimental.pallas import tpu as pltpu

D_MODEL = 1024
DEPTH = 2
CHUNK = 64
N_HEADS = 16
HEAD_DIM = 64
ATTN_WIDTH = N_HEADS * HEAD_DIM
KV_HEADS_A = 4
IDX_HEADS = 8
IDX_DIM = 64
TOPK_MAX = 256
KV_HEADS_B = 2
WINDOW = 128
NUM_BUCKETS = 32
MAX_DISTANCE = 1024
N_EXPERTS = 8
TOP_K = 2
EPS = 1e-6

F32 = jnp.float32
BF16 = jnp.bfloat16
LANE = 128
VMEM_LIMIT_BYTES = 48 * 1024 * 1024
LOG2E = math.log2(math.e)
NEG_BIG = -1e30
KEY_MIN = -2 ** 31
FAR_BUCKET = NUM_BUCKETS // 2 - 1
NEAR_BLOCKS = 6
FF_TILE = 512
MIX_TM_PROMPT = 512
MIX_TM_SAMPLE = 256
ARB2 = ("arbitrary", "arbitrary")


def _params(sem, vmem=True):
    return pltpu.CompilerParams(dimension_semantics=sem, vmem_limit_bytes=VMEM_LIMIT_BYTES if vmem else None)


def rel_bucket(rel):
    half = NUM_BUCKETS // 2
    max_exact = half // 2
    base = jnp.where(rel > 0, half, 0)
    n = jnp.abs(rel)
    nf = jnp.maximum(n, 1).astype(F32)
    large = max_exact + (jnp.log(nf / max_exact) / math.log(MAX_DISTANCE / max_exact)
                         * (half - max_exact)).astype(jnp.int32)
    large = jnp.minimum(large, half - 1)
    return base + jnp.where(n < max_exact, n, large)


def _adaln_kernel(c_ref, w_ref, b_ref, o_ref):
    c = c_ref[...]
    o_ref[...] = jnp.dot(c * jax.nn.sigmoid(c), w_ref[...], preferred_element_type=F32) + b_ref[...]


def adaln(c, w, b):
    n = c.shape[0]
    mod = pl.pallas_call(
        _adaln_kernel,
        grid=(6,),
        in_specs=[pl.BlockSpec((n, D_MODEL), lambda j: (0, 0)),
                  pl.BlockSpec((D_MODEL, D_MODEL), lambda j: (0, j)),
                  pl.BlockSpec((1, D_MODEL), lambda j: (0, j))],
        out_specs=pl.BlockSpec((n, D_MODEL), lambda j: (0, j)),
        out_shape=jax.ShapeDtypeStruct((n, 6 * D_MODEL), F32),
        name="adaln",
    )(c, w, b[None, :])
    return jnp.split(mod[:, None, :], 6, axis=-1)


def _modulate(x, g, shift, scale):
    y = x * lax.rsqrt(jnp.mean(x * x, axis=-1, keepdims=True) + EPS)
    return (y * g) * (1.0 + scale) + shift


def _seg_norm(seg, g):
    return seg * lax.rsqrt(jnp.mean(seg * seg, axis=-1, keepdims=True) + EPS) * g


def _proj_kernel(x_ref, g_ref, sh_ref, sc_ref, w_ref, gq_ref, gk_ref, gki_ref, *outs, n_kv, has_idx):
    h = _modulate(x_ref[0], g_ref[...], sh_ref[0], sc_ref[0]).astype(BF16)
    y = jnp.dot(h, w_ref[...], preferred_element_type=F32)
    tm = y.shape[0]
    kv_w = n_kv * HEAD_DIM
    if has_idx:
        q16, kf, vf, ktr, vx, kif, qi16, kit, wsc = outs
    else:
        q16, kf, vf, ktr, v16 = outs
    for hd in range(N_HEADS):
        seg = y[:, hd * HEAD_DIM:(hd + 1) * HEAD_DIM]
        q_scale = HEAD_DIM ** -0.5 * (LOG2E if has_idx else 1.0)
        q16[0, hd] = (_seg_norm(seg, gq_ref[...]) * q_scale).astype(BF16)
    lane = lax.broadcasted_iota(jnp.int32, (tm, LANE - HEAD_DIM), 1)
    ones_col = jnp.where(lane == 0, 1.0, 0.0).astype(BF16)
    for j in range(n_kv):
        kseg = _seg_norm(y[:, ATTN_WIDTH + j * HEAD_DIM:ATTN_WIDTH + (j + 1) * HEAD_DIM], gk_ref[...])
        vseg = y[:, ATTN_WIDTH + kv_w + j * HEAD_DIM:ATTN_WIDTH + kv_w + (j + 1) * HEAD_DIM]
        kf[0, :, j * HEAD_DIM:(j + 1) * HEAD_DIM] = kseg
        vf[0, :, j * HEAD_DIM:(j + 1) * HEAD_DIM] = vseg
        ktr[0, j] = kseg.T.astype(BF16)
        if has_idx:
            vx[0, j] = jnp.concatenate([vseg.astype(BF16), ones_col], axis=1)
        else:
            v16[0, j] = vseg.astype(BF16)
    if has_idx:
        base = ATTN_WIDTH + 2 * kv_w
        for hd in range(IDX_HEADS):
            qi16[0, hd] = y[:, base + hd * IDX_DIM:base + (hd + 1) * IDX_DIM].astype(BF16)
        base += IDX_HEADS * IDX_DIM
        kiseg = _seg_norm(y[:, base:base + IDX_DIM], gki_ref[...])
        kif[0] = kiseg
        kit[0] = kiseg.T.astype(BF16)
        wsc[0] = y[:, base + IDX_DIM:base + IDX_DIM + IDX_HEADS] * IDX_HEADS ** -0.5 * IDX_DIM ** -0.5


def _project(x, g, shift, scale, w_in, g_q, g_k, g_kidx, *, n_kv, has_idx, tm):
    gsz, t, _ = x.shape
    n_in = w_in.shape[1]
    kv_w = n_kv * HEAD_DIM
    tok = lambda b, i: (b, i, 0)
    head = lambda b, i: (b, 0, i, 0)
    vec = lambda b, i: (b, 0, 0)
    cst = lambda b, i: (0, 0)
    shapes = [((gsz, N_HEADS, t, HEAD_DIM), BF16, (1, N_HEADS, tm, HEAD_DIM), head),
              ((gsz, t, kv_w), F32, (1, tm, kv_w), tok),
              ((gsz, t, kv_w), F32, (1, tm, kv_w), tok),
              ((gsz, n_kv, HEAD_DIM, t), BF16, (1, n_kv, HEAD_DIM, tm), lambda b, i: (b, 0, 0, i))]
    if has_idx:
        shapes += [((gsz, n_kv, t, LANE), BF16, (1, n_kv, tm, LANE), head),
                   ((gsz, t, IDX_DIM), F32, (1, tm, IDX_DIM), tok),
                   ((gsz, IDX_HEADS, t, IDX_DIM), BF16, (1, IDX_HEADS, tm, IDX_DIM), head),
                   ((gsz, IDX_DIM, t), BF16, (1, IDX_DIM, tm), lambda b, i: (b, 0, i)),
                   ((gsz, t, IDX_HEADS), F32, (1, tm, IDX_HEADS), tok)]
    else:
        shapes += [((gsz, n_kv, t, HEAD_DIM), BF16, (1, n_kv, tm, HEAD_DIM), head)]
    return pl.pallas_call(
        functools.partial(_proj_kernel, n_kv=n_kv, has_idx=has_idx),
        grid=(gsz, t // tm),
        in_specs=[pl.BlockSpec((1, tm, D_MODEL), tok), pl.BlockSpec((1, D_MODEL), cst),
                  pl.BlockSpec((1, 1, D_MODEL), vec), pl.BlockSpec((1, 1, D_MODEL), vec),
                  pl.BlockSpec((D_MODEL, n_in), cst), pl.BlockSpec((1, HEAD_DIM), cst),
                  pl.BlockSpec((1, HEAD_DIM), cst), pl.BlockSpec((1, IDX_DIM), cst)],
        out_specs=[pl.BlockSpec(blk, im) for _, _, blk, im in shapes],
        out_shape=[jax.ShapeDtypeStruct(s, d) for s, d, _, _ in shapes],
        compiler_params=_params(ARB2),
        name="mixer_in_proj",
    )(x, g[None, :], shift, scale, w_in.astype(BF16), g_q[None, :], g_k[None, :], g_kidx[None, :])


def _tri_steps(nqb, tq, tk, p0):
    qbs, kts = [], []
    for qb in range(nqb):
        vis_end = ((p0 + qb * tq + tq - 1) // CHUNK + 1) * CHUNK
        for kt in range(-(-vis_end // tk)):
            qbs.append(qb)
            kts.append(kt)
    return jnp.asarray(np.array(qbs, np.int32)), jnp.asarray(np.array(kts, np.int32))


def _idx_kernel(qb_tab, kt_tab, qi_ref, w_ref, kit_ref, out_ref, *, tq, tk, sw, p0, l_true):
    step = pl.program_id(1)
    q0 = p0 + qb_tab[step] * tq
    k0 = kt_tab[step] * tk
    row = lax.broadcasted_iota(jnp.int32, (tq, 1), 0) + q0
    vis_end = jnp.minimum(((row >> 6) + 1) << 6, l_true)
    wv = w_ref[0]
    for c in range(tk // sw):
        kit = kit_ref[0, :, c * sw:(c + 1) * sw]
        acc = jnp.zeros((tq, sw), F32)
        for h in range(IDX_HEADS):
            sh = jnp.dot(qi_ref[0, h], kit, preferred_element_type=F32)
            acc = acc + wv[:, h:h + 1] * jnp.maximum(sh, 0.0)
        bits = lax.bitcast_convert_type(acc, jnp.int32)
        key = bits ^ ((bits >> 31) & 0x7FFFFFFF)
        kpos = k0 + c * sw + lax.broadcasted_iota(jnp.int32, (tq, sw), 1)
        out_ref[0, :, c * sw:(c + 1) * sw] = jnp.where(kpos < vis_end, key, KEY_MIN)


def _thr_kernel(keys_ref, tau_ref, quota_ref, flag_ref, hi_ref, lo_ref, *, tq, cw, p0, n_sel):
    qb = pl.program_id(1)
    vis_end = (((p0 + qb * tq + tq - 1) >> 6) + 1) << 6
    nch = (vis_end + cw - 1) // cw
    i16 = jnp.int16
    lo_min, lo_max = -2 ** 15, 2 ** 15 - 1

    def chunks(fn):
        def body(j, carry):
            off = pl.multiple_of(j * cw, cw)
            for u in range(cw // LANE):
                fn(pl.ds(off + u * LANE, LANE))
            return carry
        lax.fori_loop(0, nch, body, 0)

    def split(cols):
        x = keys_ref[0, :, cols]
        hi_ref[:, cols] = (x >> 16).astype(i16)
        lo_ref[:, cols] = ((x & 0xFFFF) + lo_min).astype(i16)

    chunks(split)

    def count_ge(ref, cand):
        c16 = cand.astype(i16)

        def body(j, acc):
            off = pl.multiple_of(j * cw, cw)
            for u in range(cw // LANE):
                acc = acc + jnp.where(ref[:, pl.ds(off + u * LANE, LANE)] >= c16, i16(1), i16(0))
            return acc
        acc = lax.fori_loop(0, nch, body, jnp.zeros((tq, LANE), i16))
        return jnp.sum(acc.astype(F32), axis=1, keepdims=True)

    def bisect(ref, base, cnt_min):
        zero = jnp.zeros((tq, LANE), jnp.int32)
        c0 = base + count_ge(ref, zero)
        v = jnp.where(c0 >= n_sel, zero, lo_min)
        cnt = jnp.where(c0 >= n_sel, c0, cnt_min)

        def bit_body(i, carry):
            v, cnt = carry
            cand = v + jnp.left_shift(jnp.int32(1), 14 - i)
            c = base + count_ge(ref, cand)
            return jnp.where(c >= n_sel, cand, v), jnp.where(c >= n_sel, c, cnt)
        return lax.fori_loop(0, 15, bit_body, (v, cnt))

    def count_gt(ref, v, base):
        c = base + count_ge(ref, jnp.minimum(v + 1, lo_max))
        return jnp.where(v[:, :1] < lo_max, c, base)

    everything = jnp.full((tq, 1), float(2 ** 24), F32)
    t_hi, n_ge_hi = bisect(hi_ref, 0.0, everything)
    n_above = count_gt(hi_ref, t_hi, 0.0)
    t_hi16 = t_hi.astype(i16)

    def keep_ties(cols):
        lo_ref[:, cols] = jnp.where(hi_ref[:, cols] == t_hi16, lo_ref[:, cols], i16(lo_min))

    chunks(keep_ties)
    t_lo, n_ge = bisect(lo_ref, n_above, n_ge_hi)
    n_gt = count_gt(lo_ref, t_lo, n_above)
    tau_ref[0] = jnp.maximum((t_hi << 16) | (t_lo - lo_min), KEY_MIN + 1)
    quota_ref[0] = jnp.broadcast_to(n_sel - n_gt, (tq, LANE))
    over = jnp.max(jnp.where(n_ge > n_sel, 1, 0), axis=0, keepdims=True)
    flag_ref[0, 0] = jnp.broadcast_to(over, flag_ref.shape[2:])


def _attn_kernel(qb_tab, kt_tab, flag_tab, q_ref, kt_ref, vx_ref, keys_ref, tau_ref, quota_ref, bias_ref, o_ref,
                 m_ref, acc_ref, negm_ref, seen_ref, *, tq, tk, sw, p0, nqb):
    step = pl.program_id(1)
    qb, kt = qb_tab[step], kt_tab[step]
    q0 = p0 + qb * tq
    k0 = kt * tk
    vis_end = (((q0 + tq - 1) >> 6) + 1) << 6
    group = N_HEADS // KV_HEADS_A
    nblk = sw // LANE
    tie_w = 2 * LANE

    @pl.when(kt == 0)
    def _():
        m_ref[...] = jnp.full(m_ref.shape, NEG_BIG, F32)
        acc_ref[...] = jnp.zeros(acc_ref.shape, F32)
        seen_ref[...] = jnp.zeros(seen_ref.shape, F32)

    has_ties = flag_tab[pl.program_id(0) * nqb + qb] != 0
    tau = tau_ref[0]

    @pl.when(jnp.logical_not(has_ties))
    def _():
        for u in range(tk // LANE):
            cols = slice(u * LANE, (u + 1) * LANE)
            negm_ref[:, cols] = jnp.where(keys_ref[0, :, cols] >= tau, 0.0, NEG_BIG)

    @pl.when(has_ties)
    def _():
        tau2 = jnp.concatenate([tau] * (tie_w // LANE), axis=1)
        quota = jnp.concatenate([quota_ref[0]] * (tie_w // LANE), axis=1)
        before = (lax.broadcasted_iota(jnp.int32, (tie_w, tie_w), 0)
                  < lax.broadcasted_iota(jnp.int32, (tie_w, tie_w), 1))
        before = jnp.where(before, 1.0, 0.0).astype(BF16)
        seen = seen_ref[...]
        for u in range(tk // tie_w):
            cols = slice(u * tie_w, (u + 1) * tie_w)
            k = keys_ref[0, :, cols]
            tie = jnp.where(k == tau2, 1.0, 0.0)
            rank = (jnp.dot(tie.astype(BF16), before, preferred_element_type=F32)
                    + jnp.concatenate([seen] * (tie_w // LANE), axis=1))
            keep = jnp.where(k > tau2, 1.0, jnp.where(rank < quota, tie, 0.0))
            negm_ref[:, cols] = jnp.where(keep > 0.0, 0.0, NEG_BIG)
            seen = seen + jnp.sum(tie, axis=1, keepdims=True)
        seen_ref[...] = seen

    def sub_tile(c, with_bias):
        off = c * sw if isinstance(c, int) else pl.multiple_of(c * sw, sw)
        negm = negm_ref[:, pl.ds(off, sw)]
        if with_bias:
            d0 = ((k0 + off - q0) >> 7) + NEAR_BLOCKS
            bidx = [jnp.clip(d0 + u, 0, NEAR_BLOCKS) for u in range(nblk)]
        for h in range(N_HEADS):
            j = h // group
            s = jnp.dot(q_ref[0, h], kt_ref[0, j, :, pl.ds(off, sw)], preferred_element_type=F32) + negm
            if with_bias:
                s = s + jnp.concatenate([bias_ref[bidx[u], h] for u in range(nblk)], axis=1)
            m_prev = m_ref[h]
            m_cur = jnp.maximum(m_prev, jnp.max(s, axis=1, keepdims=True))
            alpha = jnp.exp2(m_prev - m_cur)
            p = jnp.exp2(s - jnp.concatenate([m_cur] * nblk, axis=1))
            pv = jnp.dot(p.astype(BF16), vx_ref[0, j, pl.ds(off, sw), :], preferred_element_type=F32)
            acc_ref[h] = alpha * acc_ref[h] + pv
            m_ref[h] = m_cur

    tile_far = k0 + tk <= q0 - (NEAR_BLOCKS - 1) * LANE

    @pl.when(tile_far)
    def _():
        for c in range(tk // sw):
            sub_tile(c, False)

    @pl.when(jnp.logical_not(tile_far))
    def _():
        nsub = jnp.minimum(tk // sw, (vis_end - k0 + sw - 1) // sw)
        nfar = jnp.clip((q0 - (NEAR_BLOCKS - 1) * LANE - k0) // sw, 0, nsub)

        def far_body(c, carry):
            sub_tile(c, False)
            return carry

        def near_body(c, carry):
            sub_tile(c, True)
            return carry

        lax.fori_loop(0, nfar, far_body, 0)
        lax.fori_loop(nfar, nsub, near_body, 0)

    @pl.when(k0 + tk >= vis_end)
    def _():
        for h in range(N_HEADS):
            a = acc_ref[h]
            o_ref[0, :, h * HEAD_DIM:(h + 1) * HEAD_DIM] = a[:, :HEAD_DIM] / a[:, HEAD_DIM:HEAD_DIM + 1]


def _rel_tiles(rel_table, rel):
    return rel_table[rel_bucket(rel)].transpose(0, 3, 1, 2).astype(F32)


def _dsa_bias_tiles(rel_table):
    d = jnp.arange(NEAR_BLOCKS + 1, dtype=jnp.int32)[:, None, None] - NEAR_BLOCKS
    rel = (d * LANE + jnp.arange(LANE, dtype=jnp.int32)[None, None, :]
           - jnp.arange(LANE, dtype=jnp.int32)[None, :, None])
    tiles = (_rel_tiles(rel_table, rel) - rel_table[FAR_BUCKET][None, :, None, None]) * LOG2E
    return tiles.at[0].set(0.0)


def _dsa_attention(q16, ktr, vx, qi16, kit, w, bias, *, p0, l_true, n_sel, tq, tk, sw, cw):
    bsz, t = q16.shape[0], q16.shape[2]
    l_pad = ktr.shape[3]
    nqb = t // tq
    assert t % tq == 0 and p0 % LANE == 0 and (tq == LANE or nqb == 1)
    assert l_pad % tk == 0 and tk % sw == 0 and tk % cw == 0 and tk % (2 * LANE) == 0
    qbt, ktt = _tri_steps(nqb, tq, tk, p0)
    nsteps = int(qbt.shape[0])

    keys = pl.pallas_call(
        functools.partial(_idx_kernel, tq=tq, tk=tk, sw=sw, p0=p0, l_true=l_true),
        grid_spec=pltpu.PrefetchScalarGridSpec(
            num_scalar_prefetch=2, grid=(bsz, nsteps),
            in_specs=[pl.BlockSpec((1, IDX_HEADS, tq, IDX_DIM), lambda b, s, qt, kt: (b, 0, qt[s], 0)),
                      pl.BlockSpec((1, tq, IDX_HEADS), lambda b, s, qt, kt: (b, qt[s], 0)),
                      pl.BlockSpec((1, IDX_DIM, tk), lambda b, s, qt, kt: (b, 0, kt[s]))],
            out_specs=pl.BlockSpec((1, tq, tk), lambda b, s, qt, kt: (b, qt[s], kt[s]))),
        out_shape=jax.ShapeDtypeStruct((bsz, t, l_pad), jnp.int32),
        compiler_params=_params(ARB2, vmem=False),
        name="dsa_index_keys",
    )(qbt, ktt, qi16, w, kit)

    tau, quota, flags = pl.pallas_call(
        functools.partial(_thr_kernel, tq=tq, cw=cw, p0=p0, n_sel=n_sel),
        grid=(bsz, nqb),
        in_specs=[pl.BlockSpec((1, tq, l_pad), lambda b, i: (b, i, 0))],
        out_specs=[pl.BlockSpec((1, tq, LANE), lambda b, i: (b, i, 0)),
                   pl.BlockSpec((1, tq, LANE), lambda b, i: (b, i, 0)),
                   pl.BlockSpec((1, 1, 8, LANE), lambda b, i: (b, i, 0, 0))],
        out_shape=[jax.ShapeDtypeStruct((bsz, t, LANE), jnp.int32),
                   jax.ShapeDtypeStruct((bsz, t, LANE), F32),
                   jax.ShapeDtypeStruct((bsz, nqb, 8, LANE), jnp.int32)],
        scratch_shapes=[pltpu.VMEM((tq, l_pad), jnp.int16), pltpu.VMEM((tq, l_pad), jnp.int16)],
        compiler_params=_params(ARB2),
        name="dsa_threshold",
    )(keys)
    flags = flags[:, :, 0, 0].reshape(bsz * nqb)

    return pl.pallas_call(
        functools.partial(_attn_kernel, tq=tq, tk=tk, sw=sw, p0=p0, nqb=nqb),
        grid_spec=pltpu.PrefetchScalarGridSpec(
            num_scalar_prefetch=3, grid=(bsz, nsteps),
            in_specs=[pl.BlockSpec((1, N_HEADS, tq, HEAD_DIM), lambda b, s, qt, kt, fl: (b, 0, qt[s], 0)),
                      pl.BlockSpec((1, KV_HEADS_A, HEAD_DIM, tk), lambda b, s, qt, kt, fl: (b, 0, 0, kt[s])),
                      pl.BlockSpec((1, KV_HEADS_A, tk, LANE), lambda b, s, qt, kt, fl: (b, 0, kt[s], 0)),
                      pl.BlockSpec((1, tq, tk), lambda b, s, qt, kt, fl: (b, qt[s], kt[s])),
                      pl.BlockSpec((1, tq, LANE), lambda b, s, qt, kt, fl: (b, qt[s], 0)),
                      pl.BlockSpec((1, tq, LANE), lambda b, s, qt, kt, fl: (b, qt[s], 0)),
                      pl.BlockSpec((NEAR_BLOCKS + 1, N_HEADS, tq, LANE), lambda b, s, qt, kt, fl: (0, 0, 0, 0))],
            out_specs=pl.BlockSpec((1, tq, ATTN_WIDTH), lambda b, s, qt, kt, fl: (b, qt[s], 0)),
            scratch_shapes=[pltpu.VMEM((N_HEADS, tq, LANE), F32), pltpu.VMEM((N_HEADS, tq, LANE), F32),
                            pltpu.VMEM((tq, tk), F32), pltpu.VMEM((tq, LANE), F32)]),
        out_shape=jax.ShapeDtypeStruct((bsz, t, ATTN_WIDTH), F32),
        compiler_params=_params(ARB2),
        name="dsa_masked_attention",
    )(qbt, ktt, flags, q16, ktr, vx, keys, tau, quota, bias[:, :, :tq, :])


def _band_kernel(q_ref, kp_ref, kc_ref, vp_ref, vc_ref, bias_ref, sink_ref, o_ref, *, tq, first_has_no_prev):
    group = N_HEADS // KV_HEADS_B
    extra = None
    if first_has_no_prev:
        col = lax.broadcasted_iota(jnp.int32, (tq, 2 * LANE), 1)
        extra = jnp.where((pl.program_id(1) == 0) & (col < LANE), NEG_BIG, 0.0)
    for h in range(N_HEADS):
        j = h // group
        q = q_ref[0, h]
        s = jnp.concatenate([jnp.dot(q, kp_ref[0, j], preferred_element_type=F32),
                             jnp.dot(q, kc_ref[0, j], preferred_element_type=F32)], axis=1) + bias_ref[h]
        if extra is not None:
            s = s + extra
        sink = sink_ref[h]
        m = jnp.maximum(jnp.max(s, axis=1, keepdims=True), sink[:, :1])
        e = jnp.exp(s - m)
        p = e / (jnp.sum(e, axis=1, keepdims=True) + jnp.exp(sink[:, :1] - m))
        pb = p.astype(BF16)
        o_ref[0, :, h * HEAD_DIM:(h + 1) * HEAD_DIM] = (
            jnp.dot(pb[:, :LANE], vp_ref[0, j], preferred_element_type=F32)
            + jnp.dot(pb[:, LANE:], vc_ref[0, j], preferred_element_type=F32))


def _band_bias(rel_table):
    r = jnp.arange(LANE, dtype=jnp.int32)[:, None]
    s = jnp.arange(2 * LANE, dtype=jnp.int32)[None, :]
    wc, qh = s // CHUNK, r // CHUNK
    band = (wc >= qh) & (wc <= qh + 2)
    tiles = _rel_tiles(rel_table, (s - LANE - r)[None])[0]
    return jnp.where(band[None], tiles, NEG_BIG)


def _band_attention(q16, kp, kc, vp, vc, bias, sinks, *, tq, prev_map, cur_map, first_has_no_prev):
    bsz, t = q16.shape[0], q16.shape[2]
    kblk = lambda m: pl.BlockSpec((1, KV_HEADS_B, HEAD_DIM, LANE), lambda b, i: (b, 0, 0, m(i)))
    vblk = lambda m: pl.BlockSpec((1, KV_HEADS_B, LANE, HEAD_DIM), lambda b, i: (b, 0, m(i), 0))
    sink_rows = jnp.broadcast_to(sinks.astype(F32)[:, None, None], (N_HEADS, 1, LANE))
    return pl.pallas_call(
        functools.partial(_band_kernel, tq=tq, first_has_no_prev=first_has_no_prev),
        grid=(bsz, t // tq),
        in_specs=[pl.BlockSpec((1, N_HEADS, tq, HEAD_DIM), lambda b, i: (b, 0, i, 0)),
                  kblk(prev_map), kblk(cur_map), vblk(prev_map), vblk(cur_map),
                  pl.BlockSpec((N_HEADS, tq, 2 * LANE), lambda b, i: (0, 0, 0)),
                  pl.BlockSpec((N_HEADS, 1, LANE), lambda b, i: (0, 0, 0))],
        out_specs=pl.BlockSpec((1, tq, ATTN_WIDTH), lambda b, i: (b, i, 0)),
        out_shape=jax.ShapeDtypeStruct((bsz, t, ATTN_WIDTH), F32),
        compiler_params=_params(ARB2, vmem=False),
        name="band_attention",
    )(q16, kp, kc, vp, vc, bias[:, :tq, :], sink_rows)


def _mixer_residual(x_ref, o_ref, wo_ref, gt1_ref):
    return x_ref[0] + gt1_ref[0] * jnp.dot(o_ref[0].astype(BF16), wo_ref[...], preferred_element_type=F32)


def _swiglu_chunk(h, wg, wu, wd):
    a = jax.nn.silu(jnp.dot(h, wg, preferred_element_type=F32)) * jnp.dot(h, wu, preferred_element_type=F32)
    return jnp.dot(a.astype(BF16), wd, preferred_element_type=F32)


def _ffn_kernel(x_ref, o_ref, wo_ref, gt1_ref, g_ref, sh_ref, sc_ref, gt2_ref, wg_ref, wu_ref, wd_ref,
                out_ref, xn_scr, h_scr, acc_scr):
    c = pl.program_id(2)

    @pl.when(c == 0)
    def _():
        xn = _mixer_residual(x_ref, o_ref, wo_ref, gt1_ref)
        xn_scr[...] = xn
        h_scr[...] = _modulate(xn, g_ref[...], sh_ref[0], sc_ref[0]).astype(BF16)
        acc_scr[...] = jnp.zeros(acc_scr.shape, F32)

    acc_scr[...] += _swiglu_chunk(h_scr[...], wg_ref[...], wu_ref[...], wd_ref[...])

    @pl.when(c == pl.num_programs(2) - 1)
    def _():
        out_ref[0] = xn_scr[...] + gt2_ref[0] * acc_scr[...]


def _moe_kernel(x_ref, o_ref, wo_ref, gt1_ref, g_ref, sh_ref, sc_ref, gt2_ref, wr_ref, wg_ref, wu_ref, wd_ref,
                out_ref, xn_scr, h_scr, gate_scr, acc_scr, acce_scr):
    e, c = pl.program_id(2), pl.program_id(3)
    last_c = c == pl.num_programs(3) - 1

    @pl.when((e == 0) & (c == 0))
    def _():
        xn = _mixer_residual(x_ref, o_ref, wo_ref, gt1_ref)
        xn_scr[...] = xn
        h = _modulate(xn, g_ref[...], sh_ref[0], sc_ref[0]).astype(BF16)
        h_scr[...] = h
        lane = lax.broadcasted_iota(jnp.int32, (h.shape[0], LANE), 1)
        lanef = lane.astype(F32)
        logits = jnp.where(lane < N_EXPERTS, jnp.dot(h, wr_ref[...], preferred_element_type=F32), -jnp.inf)
        m1 = jnp.max(logits, axis=1, keepdims=True)
        i1 = jnp.min(jnp.where(logits == m1, lanef, float(LANE)), axis=1, keepdims=True)
        rest = jnp.where(lanef == i1, -jnp.inf, logits)
        m2 = jnp.max(rest, axis=1, keepdims=True)
        i2 = jnp.min(jnp.where(rest == m2, lanef, float(LANE)), axis=1, keepdims=True)
        e2 = jnp.exp(m2 - m1)
        den = 1.0 + e2
        gate_scr[...] = jnp.where(lanef == i1, 1.0 / den, 0.0) + jnp.where(lanef == i2, e2 / den, 0.0)
        acc_scr[...] = jnp.zeros(acc_scr.shape, F32)

    @pl.when(c == 0)
    def _():
        acce_scr[...] = jnp.zeros(acce_scr.shape, F32)

    acce_scr[...] += _swiglu_chunk(h_scr[...], wg_ref[0], wu_ref[0], wd_ref[0])

    @pl.when(last_c)
    def _():
        lane = lax.broadcasted_iota(jnp.int32, gate_scr.shape, 1)
        ge = jnp.sum(jnp.where(lane == e, gate_scr[...], 0.0), axis=1, keepdims=True)
        acc_scr[...] += ge * acce_scr[...]

    @pl.when(last_c & (e == pl.num_programs(2) - 1))
    def _():
        out_ref[0] = xn_scr[...] + gt2_ref[0] * acc_scr[...]


def _channel_mixer(x, o, w_out, gt1, g, shift, scale, gt2, w_router, w_gate, w_up, w_down, *, tm):
    gsz, t, _ = x.shape
    moe = w_router is not None
    d_ff = w_gate.shape[-1]
    assert t % tm == 0 and d_ff % FF_TILE == 0
    nc = d_ff // FF_TILE
    nd = 4 if moe else 3

    def fix(f):
        return (lambda b, i, e, c: f(b, i, e, c)) if moe else (lambda b, i, c: f(b, i, 0, c))

    tok = pl.BlockSpec((1, tm, D_MODEL), fix(lambda b, i, e, c: (b, i, 0)))

    def mod_spec(a):
        if a.shape[1] == 1:
            return pl.BlockSpec((1, 1, D_MODEL), fix(lambda b, i, e, c: (b, 0, 0)))
        return tok

    cst = lambda shape: pl.BlockSpec(shape, fix(lambda b, i, e, c: (0, 0)))
    in_specs = [tok, tok, cst((ATTN_WIDTH, D_MODEL)), mod_spec(gt1), cst((1, D_MODEL)),
                mod_spec(shift), mod_spec(scale), mod_spec(gt2)]
    args = [x, o, w_out.astype(BF16), gt1, g[None, :], shift, scale, gt2]
    scratch = [pltpu.VMEM((tm, D_MODEL), F32), pltpu.VMEM((tm, D_MODEL), BF16)]
    if moe:
        wr = jnp.pad(w_router.astype(BF16), ((0, 0), (0, LANE - N_EXPERTS)))
        in_specs += [cst((D_MODEL, LANE)),
                     pl.BlockSpec((1, D_MODEL, FF_TILE), lambda b, i, e, c: (e, 0, c)),
                     pl.BlockSpec((1, D_MODEL, FF_TILE), lambda b, i, e, c: (e, 0, c)),
                     pl.BlockSpec((1, FF_TILE, D_MODEL), lambda b, i, e, c: (e, c, 0))]
        args += [wr, w_gate.astype(BF16), w_up.astype(BF16), w_down.astype(BF16)]
        scratch += [pltpu.VMEM((tm, LANE), F32), pltpu.VMEM((tm, D_MODEL), F32), pltpu.VMEM((tm, D_MODEL), F32)]
        grid, body, name = (gsz, t // tm, N_EXPERTS, nc), _moe_kernel, "moe_channel_mixer"
    else:
        in_specs += [pl.BlockSpec((D_MODEL, FF_TILE), lambda b, i, c: (0, c)),
                     pl.BlockSpec((D_MODEL, FF_TILE), lambda b, i, c: (0, c)),
                     pl.BlockSpec((FF_TILE, D_MODEL), lambda b, i, c: (c, 0))]
        args += [w_gate.astype(BF16), w_up.astype(BF16), w_down.astype(BF16)]
        scratch += [pltpu.VMEM((tm, D_MODEL), F32)]
        grid, body, name = (gsz, t // tm, nc), _ffn_kernel, "swiglu_channel_mixer"
    return pl.pallas_call(
        body, grid=grid, in_specs=in_specs, out_specs=tok,
        out_shape=jax.ShapeDtypeStruct((gsz, t, D_MODEL), F32),
        scratch_shapes=scratch,
        compiler_params=_params(("arbitrary",) * nd),
        name=name,
    )(*args)


def _per_token(a, n):
    return jnp.broadcast_to(a, (a.shape[0], n, a.shape[2])).reshape(1, a.shape[0] * n, a.shape[2])


def _pad_keys(a, axis, l_pad):
    pad = [(0, 0)] * a.ndim
    pad[axis] = (0, l_pad - a.shape[axis])
    return jnp.pad(a, pad)


def _layer_a(xp, xs, cp, cs, ck, cv, cki, rel_bias, norm_g, w_ada, b_ada, w_in, w_out, g_q, g_k, g_kidx,
             ffn_wg, ffn_wu, ffn_wd):
    bias = _dsa_bias_tiles(rel_bias)
    nb, n = xs.shape[0], xs.shape[1]
    past = ck.shape[1]
    outs = []
    for x, c, sample in ((xp, cp, False), (xs, cs, True)):
        sh1, sc1, gt1, sh2, sc2, gt2 = adaln(c, w_ada, b_ada)
        q16, kf, vf, ktr, vx, kif, qi16, kit, wsc = _project(
            x, norm_g[0], sh1, sc1, w_in, g_q, g_k, g_kidx, n_kv=KV_HEADS_A, has_idx=True,
            tm=n if sample else 256)
        if sample:
            l_true, l_pad = past + n, 2304
            kfull = jnp.concatenate([ck, kf.reshape(nb, n, KV_HEADS_A, HEAD_DIM)], axis=1).astype(BF16)
            vfull = jnp.concatenate([cv, vf.reshape(nb, n, KV_HEADS_A, HEAD_DIM)], axis=1).astype(BF16)
            kifull = jnp.concatenate([cki, kif], axis=1).astype(BF16)
            ktr = _pad_keys(kfull, 1, l_pad).transpose(0, 2, 3, 1)
            vx = jnp.concatenate([vfull, jnp.ones(vfull.shape[:3] + (1,), BF16),
                                  jnp.zeros(vfull.shape[:3] + (LANE - HEAD_DIM - 1,), BF16)], axis=-1)
            vx = _pad_keys(vx, 1, l_pad).transpose(0, 2, 1, 3)
            kit = _pad_keys(kifull, 1, l_pad).transpose(0, 2, 1)
            o = _dsa_attention(q16, ktr, vx, qi16, kit, wsc, bias, p0=past, l_true=l_true,
                               n_sel=min(TOPK_MAX, l_true // 4), tq=n, tk=l_pad, sw=256, cw=256)
            flat = lambda a: a.reshape(1, nb * n, a.shape[-1])
            y = _channel_mixer(flat(x), flat(o), w_out, _per_token(gt1, n), norm_g[1], _per_token(sh2, n),
                               _per_token(sc2, n), _per_token(gt2, n), None, ffn_wg, ffn_wu, ffn_wd,
                               tm=MIX_TM_SAMPLE)
            y = y.reshape(x.shape)
        else:
            t = x.shape[1]
            o = _dsa_attention(q16, ktr, vx, qi16, kit, wsc, bias, p0=0, l_true=t,
                               n_sel=min(TOPK_MAX, t // 4), tq=LANE, tk=1024, sw=256, cw=1024)
            y = _channel_mixer(x, o, w_out, gt1, norm_g[1], sh2, sc2, gt2, None, ffn_wg, ffn_wu, ffn_wd,
                               tm=MIX_TM_PROMPT)
        bsz, t = x.shape[0], x.shape[1]
        outs.append((y, kf.reshape(bsz, t, KV_HEADS_A, HEAD_DIM), vf.reshape(bsz, t, KV_HEADS_A, HEAD_DIM), kif))
    return outs


def _layer_b(xp, xs, cp, cs, ck, cv, rel_bias, norm_g, w_ada, b_ada, w_in, w_out, g_q, g_k, sinks,
             w_router, moe_wg, moe_wu, moe_wd):
    bias = _band_bias(rel_bias)
    nb, n = xs.shape[0], xs.shape[1]
    outs = []
    for x, c, sample in ((xp, cp, False), (xs, cs, True)):
        sh1, sc1, gt1, sh2, sc2, gt2 = adaln(c, w_ada, b_ada)
        q16, kf, vf, ktr, v16 = _project(x, norm_g[0], sh1, sc1, w_in, g_q, g_k, g_k, n_kv=KV_HEADS_B,
                                         has_idx=False, tm=n if sample else 256)
        bsz, t = x.shape[0], x.shape[1]
        k4 = kf.reshape(bsz, t, KV_HEADS_B, HEAD_DIM)
        v4 = vf.reshape(bsz, t, KV_HEADS_B, HEAD_DIM)
        zero = lambda i: 0
        if sample:
            kp = ck.astype(BF16).transpose(0, 2, 3, 1)
            vp = cv.astype(BF16).transpose(0, 2, 1, 3)
            kc = _pad_keys(ktr, 3, LANE)
            vc = _pad_keys(v16, 2, LANE)
            o = _band_attention(q16, kp, kc, vp, vc, bias, sinks, tq=n, prev_map=zero, cur_map=zero,
                                first_has_no_prev=False)
            flat = lambda a: a.reshape(1, nb * n, a.shape[-1])
            y = _channel_mixer(flat(x), flat(o), w_out, _per_token(gt1, n), norm_g[1], _per_token(sh2, n),
                               _per_token(sc2, n), _per_token(gt2, n), w_router, moe_wg, moe_wu, moe_wd,
                               tm=MIX_TM_SAMPLE)
            y = y.reshape(x.shape)
            k_new = jnp.concatenate([ck, k4], axis=1)[:, -WINDOW:]
            v_new = jnp.concatenate([cv, v4], axis=1)[:, -WINDOW:]
        else:
            o = _band_attention(q16, ktr, ktr, v16, v16, bias, sinks, tq=LANE,
                                prev_map=lambda i: jnp.maximum(i - 1, 0), cur_map=lambda i: i,
                                first_has_no_prev=True)
            y = _channel_mixer(x, o, w_out, gt1, norm_g[1], sh2, sc2, gt2, w_router, moe_wg, moe_wu, moe_wd,
                               tm=MIX_TM_PROMPT)
            k_new, v_new = k4[:, -WINDOW:], v4[:, -WINDOW:]
        outs.append((y, k_new, v_new))
    return outs


def kernel(x_prompt, x_sample, c_prompt, c_sample, cache_a_k, cache_a_v, cache_a_kidx,
           cache_b_k, cache_b_v, rel_bias, norm_g, w_ada, b_ada,
           a_w_in, a_w_out, a_g_q, a_g_k, a_g_kidx,
           b_w_in, b_w_out, b_g_q, b_g_k, b_sinks,
           ffn_w_gate, ffn_w_up, ffn_w_down,
           moe_w_router, moe_w_gate, moe_w_up, moe_w_down):
    xp, xs = x_prompt, x_sample
    a_out = [[] for _ in range(6)]
    b_out = [[] for _ in range(4)]
    for i in range(DEPTH):
        j = i // 2
        if i % 2 == 0:
            (xp, k1, v1, i1), (xs, k2, v2, i2) = _layer_a(
                xp, xs, c_prompt, c_sample, cache_a_k[j], cache_a_v[j], cache_a_kidx[j], rel_bias, norm_g[i],
                w_ada[i], b_ada[i], a_w_in[j], a_w_out[j], a_g_q[j], a_g_k[j], a_g_kidx[j],
                ffn_w_gate[j], ffn_w_up[j], ffn_w_down[j])
            for lst, val in zip(a_out, (k1, v1, i1, k2, v2, i2)):
                lst.append(val)
        else:
            (xp, k1, v1), (xs, k2, v2) = _layer_b(
                xp, xs, c_prompt, c_sample, cache_b_k[j], cache_b_v[j], rel_bias, norm_g[i],
                w_ada[i], b_ada[i], b_w_in[j], b_w_out[j], b_g_q[j], b_g_k[j], b_sinks[j],
                moe_w_router[j], moe_w_gate[j], moe_w_up[j], moe_w_down[j])
            for lst, val in zip(b_out, (k1, v1, k2, v2)):
                lst.append(val)
    return (xp, xs, *(jnp.stack(l) for l in a_out), *(jnp.stack(l) for l in b_out))
```

```python
import math, functools
import jax, jax.numpy as jnp
from jax import lax
import numpy as np
from jax.experimental import pallas as pl
from jax.experimental.pallas import tpu as pltpu

D_MODEL = 1024
DEPTH = 2
CHUNK = 64
N_HEADS = 16
HEAD_DIM = 64
ATTN_WIDTH = N_HEADS * HEAD_DIM
KV_HEADS_A = 4
IDX_HEADS = 8
IDX_DIM = 64
TOPK_MAX = 256
KV_HEADS_B = 2
WINDOW = 128
NUM_BUCKETS = 32
MAX_DISTANCE = 1024
N_EXPERTS = 8
TOP_K = 2
EPS = 1e-6

F32 = jnp.float32
BF16 = jnp.bfloat16
LANE = 128
VMEM_LIMIT_BYTES = 48 * 1024 * 1024
LOG2E = math.log2(math.e)
NEG_BIG = -1e30
KEY_MIN = -2 ** 31
FAR_BUCKET = NUM_BUCKETS // 2 - 1
NEAR_BLOCKS = 6
FF_TILE = 512
MIX_TM_PROMPT = 512
MIX_TM_SAMPLE = 256
MOE_ROW_TILE = 512
MOE_COMBINE_TM = 256
ARB2 = ("arbitrary", "arbitrary")


def _params(sem, vmem=True):
    return pltpu.CompilerParams(dimension_semantics=sem, vmem_limit_bytes=VMEM_LIMIT_BYTES if vmem else None)


def rel_bucket(rel):
    half = NUM_BUCKETS // 2
    max_exact = half // 2
    base = jnp.where(rel > 0, half, 0)
    n = jnp.abs(rel)
    nf = jnp.maximum(n, 1).astype(F32)
    large = max_exact + (jnp.log(nf / max_exact) / math.log(MAX_DISTANCE / max_exact)
                         * (half - max_exact)).astype(jnp.int32)
    large = jnp.minimum(large, half - 1)
    return base + jnp.where(n < max_exact, n, large)


def _adaln_kernel(c_ref, w_ref, b_ref, o_ref):
    c = c_ref[...]
    o_ref[...] = jnp.dot(c * jax.nn.sigmoid(c), w_ref[...], preferred_element_type=F32) + b_ref[...]


def adaln(c, w, b):
    n = c.shape[0]
    mod = pl.pallas_call(
        _adaln_kernel,
        grid=(6,),
        in_specs=[pl.BlockSpec((n, D_MODEL), lambda j: (0, 0)),
                  pl.BlockSpec((D_MODEL, D_MODEL), lambda j: (0, j)),
                  pl.BlockSpec((1, D_MODEL), lambda j: (0, j))],
        out_specs=pl.BlockSpec((n, D_MODEL), lambda j: (0, j)),
        out_shape=jax.ShapeDtypeStruct((n, 6 * D_MODEL), F32),
        name="adaln",
    )(c, w, b[None, :])
    return jnp.split(mod[:, None, :], 6, axis=-1)


def _modulate(x, g, shift, scale):
    y = x * lax.rsqrt(jnp.mean(x * x, axis=-1, keepdims=True) + EPS)
    return (y * g) * (1.0 + scale) + shift


def _seg_norm(seg, g):
    return seg * lax.rsqrt(jnp.mean(seg * seg, axis=-1, keepdims=True) + EPS) * g


def _proj_kernel(x_ref, g_ref, sh_ref, sc_ref, w_ref, gq_ref, gk_ref, gki_ref, *outs, n_kv, has_idx):
    h = _modulate(x_ref[0], g_ref[...], sh_ref[0], sc_ref[0]).astype(BF16)
    y = jnp.dot(h, w_ref[...], preferred_element_type=F32)
    tm = y.shape[0]
    kv_w = n_kv * HEAD_DIM
    if has_idx:
        q16, kf, vf, ktr, vx, kif, qi16, kit, wsc = outs
    else:
        q16, kf, vf, ktr, v16 = outs
    for hd in range(N_HEADS):
        seg = y[:, hd * HEAD_DIM:(hd + 1) * HEAD_DIM]
        q_scale = HEAD_DIM ** -0.5 * (LOG2E if has_idx else 1.0)
        q16[0, hd] = (_seg_norm(seg, gq_ref[...]) * q_scale).astype(BF16)
    lane = lax.broadcasted_iota(jnp.int32, (tm, LANE - HEAD_DIM), 1)
    ones_col = jnp.where(lane == 0, 1.0, 0.0).astype(BF16)
    for j in range(n_kv):
        kseg = _seg_norm(y[:, ATTN_WIDTH + j * HEAD_DIM:ATTN_WIDTH + (j + 1) * HEAD_DIM], gk_ref[...])
        vseg = y[:, ATTN_WIDTH + kv_w + j * HEAD_DIM:ATTN_WIDTH + kv_w + (j + 1) * HEAD_DIM]
        kf[0, :, j * HEAD_DIM:(j + 1) * HEAD_DIM] = kseg
        vf[0, :, j * HEAD_DIM:(j + 1) * HEAD_DIM] = vseg
        ktr[0, j] = kseg.T.astype(BF16)
        if has_idx:
            vx[0, j] = jnp.concatenate([vseg.astype(BF16), ones_col], axis=1)
        else:
            v16[0, j] = vseg.astype(BF16)
    if has_idx:
        base = ATTN_WIDTH + 2 * kv_w
        for hd in range(IDX_HEADS):
            qi16[0, hd] = y[:, base + hd * IDX_DIM:base + (hd + 1) * IDX_DIM].astype(BF16)
        base += IDX_HEADS * IDX_DIM
        kiseg = _seg_norm(y[:, base:base + IDX_DIM], gki_ref[...])
        kif[0] = kiseg
        kit[0] = kiseg.T.astype(BF16)
        wsc[0] = y[:, base + IDX_DIM:base + IDX_DIM + IDX_HEADS] * IDX_HEADS ** -0.5 * IDX_DIM ** -0.5


def _project(x, g, shift, scale, w_in, g_q, g_k, g_kidx, *, n_kv, has_idx, tm):
    gsz, t, _ = x.shape
    n_in = w_in.shape[1]
    kv_w = n_kv * HEAD_DIM
    tok = lambda b, i: (b, i, 0)
    head = lambda b, i: (b, 0, i, 0)
    vec = lambda b, i: (b, 0, 0)
    cst = lambda b, i: (0, 0)
    shapes = [((gsz, N_HEADS, t, HEAD_DIM), BF16, (1, N_HEADS, tm, HEAD_DIM), head),
              ((gsz, t, kv_w), F32, (1, tm, kv_w), tok),
              ((gsz, t, kv_w), F32, (1, tm, kv_w), tok),
              ((gsz, n_kv, HEAD_DIM, t), BF16, (1, n_kv, HEAD_DIM, tm), lambda b, i: (b, 0, 0, i))]
    if has_idx:
        shapes += [((gsz, n_kv, t, LANE), BF16, (1, n_kv, tm, LANE), head),
                   ((gsz, t, IDX_DIM), F32, (1, tm, IDX_DIM), tok),
                   ((gsz, IDX_HEADS, t, IDX_DIM), BF16, (1, IDX_HEADS, tm, IDX_DIM), head),
                   ((gsz, IDX_DIM, t), BF16, (1, IDX_DIM, tm), lambda b, i: (b, 0, i)),
                   ((gsz, t, IDX_HEADS), F32, (1, tm, IDX_HEADS), tok)]
    else:
        shapes += [((gsz, n_kv, t, HEAD_DIM), BF16, (1, n_kv, tm, HEAD_DIM), head)]
    return pl.pallas_call(
        functools.partial(_proj_kernel, n_kv=n_kv, has_idx=has_idx),
        grid=(gsz, t // tm),
        in_specs=[pl.BlockSpec((1, tm, D_MODEL), tok), pl.BlockSpec((1, D_MODEL), cst),
                  pl.BlockSpec((1, 1, D_MODEL), vec), pl.BlockSpec((1, 1, D_MODEL), vec),
                  pl.BlockSpec((D_MODEL, n_in), cst), pl.BlockSpec((1, HEAD_DIM), cst),
                  pl.BlockSpec((1, HEAD_DIM), cst), pl.BlockSpec((1, IDX_DIM), cst)],
        out_specs=[pl.BlockSpec(blk, im) for _, _, blk, im in shapes],
        out_shape=[jax.ShapeDtypeStruct(s, d) for s, d, _, _ in shapes],
        compiler_params=_params(ARB2),
        name="mixer_in_proj",
    )(x, g[None, :], shift, scale, w_in.astype(BF16), g_q[None, :], g_k[None, :], g_kidx[None, :])


def _tri_steps(nqb, tq, tk, p0):
    qbs, kts = [], []
    for qb in range(nqb):
        vis_end = ((p0 + qb * tq + tq - 1) // CHUNK + 1) * CHUNK
        for kt in range(-(-vis_end // tk)):
            qbs.append(qb)
            kts.append(kt)
    return jnp.asarray(np.array(qbs, np.int32)), jnp.asarray(np.array(kts, np.int32))


def _idx_kernel(qb_tab, kt_tab, qi_ref, w_ref, kit_ref, out_ref, *, tq, tk, sw, p0, l_true):
    step = pl.program_id(1)
    q0 = p0 + qb_tab[step] * tq
    k0 = kt_tab[step] * tk
    row = lax.broadcasted_iota(jnp.int32, (tq, 1), 0) + q0
    vis_end = jnp.minimum(((row >> 6) + 1) << 6, l_true)
    wv = w_ref[0]
    for c in range(tk // sw):
        kit = kit_ref[0, :, c * sw:(c + 1) * sw]
        acc = jnp.zeros((tq, sw), F32)
        for h in range(IDX_HEADS):
            sh = jnp.dot(qi_ref[0, h], kit, preferred_element_type=F32)
            acc = acc + wv[:, h:h + 1] * jnp.maximum(sh, 0.0)
        bits = lax.bitcast_convert_type(acc, jnp.int32)
        key = bits ^ ((bits >> 31) & 0x7FFFFFFF)
        kpos = k0 + c * sw + lax.broadcasted_iota(jnp.int32, (tq, sw), 1)
        out_ref[0, :, c * sw:(c + 1) * sw] = jnp.where(kpos < vis_end, key, KEY_MIN)


def _thr_kernel(keys_ref, tau_ref, quota_ref, flag_ref, *, tq, cw, p0, n_sel):
    qb = pl.program_id(1)
    vis_end = (((p0 + qb * tq + tq - 1) >> 6) + 1) << 6
    nch = (vis_end + cw - 1) // cw
    key_max = 2 ** 31 - 1

    def count_ge(cand):
        def body(j, acc):
            off = pl.multiple_of(j * cw, cw)
            for u in range(cw // LANE):
                x = keys_ref[0, :, pl.ds(off + u * LANE, LANE)]
                acc = acc + jnp.where(x >= cand, 1.0, 0.0)
            return acc
        acc = lax.fori_loop(0, nch, body, jnp.zeros((tq, LANE), F32))
        return jnp.sum(acc, axis=1, keepdims=True)

    zero = jnp.zeros((tq, LANE), jnp.int32)
    c0 = count_ge(zero)
    tau = jnp.where(c0 >= n_sel, zero, KEY_MIN)
    n_ge = jnp.where(c0 >= n_sel, c0, float(2 ** 24))

    def bit_body(i, carry):
        tau, n_ge = carry
        cand = tau + jnp.left_shift(jnp.int32(1), 30 - i)
        c = count_ge(cand)
        return jnp.where(c >= n_sel, cand, tau), jnp.where(c >= n_sel, c, n_ge)

    tau, n_ge = lax.fori_loop(0, 31, bit_body, (tau, n_ge))
    n_gt = jnp.where(tau[:, :1] < key_max, count_ge(jnp.minimum(tau, key_max - 1) + 1), 0.0)
    tau_ref[0] = jnp.maximum(tau, KEY_MIN + 1)
    quota_ref[0] = jnp.broadcast_to(n_sel - n_gt, (tq, LANE))
    over = jnp.max(jnp.where(n_ge > n_sel, 1, 0), axis=0, keepdims=True)
    flag_ref[0, 0] = jnp.broadcast_to(over, flag_ref.shape[2:])


def _attn_kernel(qb_tab, kt_tab, flag_tab, q_ref, kt_ref, vx_ref, keys_ref, tau_ref, quota_ref, bias_ref, o_ref,
                 m_ref, acc_ref, negm_ref, seen_ref, *, tq, tk, sw, p0, nqb):
    step = pl.program_id(1)
    qb, kt = qb_tab[step], kt_tab[step]
    q0 = p0 + qb * tq
    k0 = kt * tk
    vis_end = (((q0 + tq - 1) >> 6) + 1) << 6
    group = N_HEADS // KV_HEADS_A
    nblk = sw // LANE
    tie_w = 2 * LANE

    @pl.when(kt == 0)
    def _():
        m_ref[...] = jnp.full(m_ref.shape, NEG_BIG, F32)
        acc_ref[...] = jnp.zeros(acc_ref.shape, F32)
        seen_ref[...] = jnp.zeros(seen_ref.shape, F32)

    has_ties = flag_tab[pl.program_id(0) * nqb + qb] != 0
    tau = tau_ref[0]

    @pl.when(jnp.logical_not(has_ties))
    def _():
        for u in range(tk // LANE):
            cols = slice(u * LANE, (u + 1) * LANE)
            negm_ref[:, cols] = jnp.where(keys_ref[0, :, cols] >= tau, 0.0, NEG_BIG)

    @pl.when(has_ties)
    def _():
        tau2 = jnp.concatenate([tau] * (tie_w // LANE), axis=1)
        quota = jnp.concatenate([quota_ref[0]] * (tie_w // LANE), axis=1)
        before = (lax.broadcasted_iota(jnp.int32, (tie_w, tie_w), 0)
                  < lax.broadcasted_iota(jnp.int32, (tie_w, tie_w), 1))
        before = jnp.where(before, 1.0, 0.0).astype(BF16)
        seen = seen_ref[...]
        for u in range(tk // tie_w):
            cols = slice(u * tie_w, (u + 1) * tie_w)
            k = keys_ref[0, :, cols]
            tie = jnp.where(k == tau2, 1.0, 0.0)
            rank = (jnp.dot(tie.astype(BF16), before, preferred_element_type=F32)
                    + jnp.concatenate([seen] * (tie_w // LANE), axis=1))
            keep = jnp.where(k > tau2, 1.0, jnp.where(rank < quota, tie, 0.0))
            negm_ref[:, cols] = jnp.where(keep > 0.0, 0.0, NEG_BIG)
            seen = seen + jnp.sum(tie, axis=1, keepdims=True)
        seen_ref[...] = seen

    def sub_tile(c, with_bias):
        off = c * sw if isinstance(c, int) else pl.multiple_of(c * sw, sw)
        negm = negm_ref[:, pl.ds(off, sw)]
        if with_bias:
            d0 = ((k0 + off - q0) >> 7) + NEAR_BLOCKS
            bidx = [jnp.clip(d0 + u, 0, NEAR_BLOCKS) for u in range(nblk)]
        for h in range(N_HEADS):
            j = h // group
            s = jnp.dot(q_ref[0, h], kt_ref[0, j, :, pl.ds(off, sw)], preferred_element_type=F32) + negm
            if with_bias:
                s = s + jnp.concatenate([bias_ref[bidx[u], h] for u in range(nblk)], axis=1)
            m_prev = m_ref[h]
            m_cur = jnp.maximum(m_prev, jnp.max(s, axis=1, keepdims=True))
            alpha = jnp.exp2(m_prev - m_cur)
            p = jnp.exp2(s - jnp.concatenate([m_cur] * nblk, axis=1))
            pv = jnp.dot(p.astype(BF16), vx_ref[0, j, pl.ds(off, sw), :], preferred_element_type=F32)
            acc_ref[h] = alpha * acc_ref[h] + pv
            m_ref[h] = m_cur

    tile_far = k0 + tk <= q0 - (NEAR_BLOCKS - 1) * LANE

    @pl.when(tile_far)
    def _():
        for c in range(tk // sw):
            sub_tile(c, False)

    @pl.when(jnp.logical_not(tile_far))
    def _():
        nsub = jnp.minimum(tk // sw, (vis_end - k0 + sw - 1) // sw)
        nfar = jnp.clip((q0 - (NEAR_BLOCKS - 1) * LANE - k0) // sw, 0, nsub)

        def far_body(c, carry):
            sub_tile(c, False)
            return carry

        def near_body(c, carry):
            sub_tile(c, True)
            return carry

        lax.fori_loop(0, nfar, far_body, 0)
        lax.fori_loop(nfar, nsub, near_body, 0)

    @pl.when(k0 + tk >= vis_end)
    def _():
        for h in range(N_HEADS):
            a = acc_ref[h]
            o_ref[0, :, h * HEAD_DIM:(h + 1) * HEAD_DIM] = a[:, :HEAD_DIM] / a[:, HEAD_DIM:HEAD_DIM + 1]


def _rel_tiles(rel_table, rel):
    return rel_table[rel_bucket(rel)].transpose(0, 3, 1, 2).astype(F32)


def _dsa_bias_tiles(rel_table):
    d = jnp.arange(NEAR_BLOCKS + 1, dtype=jnp.int32)[:, None, None] - NEAR_BLOCKS
    rel = (d * LANE + jnp.arange(LANE, dtype=jnp.int32)[None, None, :]
           - jnp.arange(LANE, dtype=jnp.int32)[None, :, None])
    tiles = (_rel_tiles(rel_table, rel) - rel_table[FAR_BUCKET][None, :, None, None]) * LOG2E
    return tiles.at[0].set(0.0)


def _dsa_attention(q16, ktr, vx, qi16, kit, w, bias, *, p0, l_true, n_sel, tq, tk, sw, cw):
    bsz, t = q16.shape[0], q16.shape[2]
    l_pad = ktr.shape[3]
    nqb = t // tq
    assert t % tq == 0 and p0 % LANE == 0 and (tq == LANE or nqb == 1)
    assert l_pad % tk == 0 and tk % sw == 0 and tk % cw == 0 and tk % (2 * LANE) == 0
    qbt, ktt = _tri_steps(nqb, tq, tk, p0)
    nsteps = int(qbt.shape[0])

    keys = pl.pallas_call(
        functools.partial(_idx_kernel, tq=tq, tk=tk, sw=sw, p0=p0, l_true=l_true),
        grid_spec=pltpu.PrefetchScalarGridSpec(
            num_scalar_prefetch=2, grid=(bsz, nsteps),
            in_specs=[pl.BlockSpec((1, IDX_HEADS, tq, IDX_DIM), lambda b, s, qt, kt: (b, 0, qt[s], 0)),
                      pl.BlockSpec((1, tq, IDX_HEADS), lambda b, s, qt, kt: (b, qt[s], 0)),
                      pl.BlockSpec((1, IDX_DIM, tk), lambda b, s, qt, kt: (b, 0, kt[s]))],
            out_specs=pl.BlockSpec((1, tq, tk), lambda b, s, qt, kt: (b, qt[s], kt[s]))),
        out_shape=jax.ShapeDtypeStruct((bsz, t, l_pad), jnp.int32),
        compiler_params=_params(ARB2, vmem=False),
        name="dsa_index_keys",
    )(qbt, ktt, qi16, w, kit)

    tau, quota, flags = pl.pallas_call(
        functools.partial(_thr_kernel, tq=tq, cw=cw, p0=p0, n_sel=n_sel),
        grid=(bsz, nqb),
        in_specs=[pl.BlockSpec((1, tq, l_pad), lambda b, i: (b, i, 0))],
        out_specs=[pl.BlockSpec((1, tq, LANE), lambda b, i: (b, i, 0)),
                   pl.BlockSpec((1, tq, LANE), lambda b, i: (b, i, 0)),
                   pl.BlockSpec((1, 1, 8, LANE), lambda b, i: (b, i, 0, 0))],
        out_shape=[jax.ShapeDtypeStruct((bsz, t, LANE), jnp.int32),
                   jax.ShapeDtypeStruct((bsz, t, LANE), F32),
                   jax.ShapeDtypeStruct((bsz, nqb, 8, LANE), jnp.int32)],
        compiler_params=_params(ARB2),
        name="dsa_threshold",
    )(keys)
    flags = flags[:, :, 0, 0].reshape(bsz * nqb)

    return pl.pallas_call(
        functools.partial(_attn_kernel, tq=tq, tk=tk, sw=sw, p0=p0, nqb=nqb),
        grid_spec=pltpu.PrefetchScalarGridSpec(
            num_scalar_prefetch=3, grid=(bsz, nsteps),
            in_specs=[pl.BlockSpec((1, N_HEADS, tq, HEAD_DIM), lambda b, s, qt, kt, fl: (b, 0, qt[s], 0)),
                      pl.BlockSpec((1, KV_HEADS_A, HEAD_DIM, tk), lambda b, s, qt, kt, fl: (b, 0, 0, kt[s])),
                      pl.BlockSpec((1, KV_HEADS_A, tk, LANE), lambda b, s, qt, kt, fl: (b, 0, kt[s], 0)),
                      pl.BlockSpec((1, tq, tk), lambda b, s, qt, kt, fl: (b, qt[s], kt[s])),
                      pl.BlockSpec((1, tq, LANE), lambda b, s, qt, kt, fl: (b, qt[s], 0)),
                      pl.BlockSpec((1, tq, LANE), lambda b, s, qt, kt, fl: (b, qt[s], 0)),
                      pl.BlockSpec((NEAR_BLOCKS + 1, N_HEADS, tq, LANE), lambda b, s, qt, kt, fl: (0, 0, 0, 0))],
            out_specs=pl.BlockSpec((1, tq, ATTN_WIDTH), lambda b, s, qt, kt, fl: (b, qt[s], 0)),
            scratch_shapes=[pltpu.VMEM((N_HEADS, tq, LANE), F32), pltpu.VMEM((N_HEADS, tq, LANE), F32),
                            pltpu.VMEM((tq, tk), F32), pltpu.VMEM((tq, LANE), F32)]),
        out_shape=jax.ShapeDtypeStruct((bsz, t, ATTN_WIDTH), F32),
        compiler_params=_params(ARB2),
        name="dsa_masked_attention",
    )(qbt, ktt, flags, q16, ktr, vx, keys, tau, quota, bias[:, :, :tq, :])


def _band_kernel(q_ref, kp_ref, kc_ref, vp_ref, vc_ref, bias_ref, sink_ref, o_ref, *, tq, first_has_no_prev):
    group = N_HEADS // KV_HEADS_B
    extra = None
    if first_has_no_prev:
        col = lax.broadcasted_iota(jnp.int32, (tq, 2 * LANE), 1)
        extra = jnp.where((pl.program_id(1) == 0) & (col < LANE), NEG_BIG, 0.0)
    for h in range(N_HEADS):
        j = h // group
        q = q_ref[0, h]
        s = jnp.concatenate([jnp.dot(q, kp_ref[0, j], preferred_element_type=F32),
                             jnp.dot(q, kc_ref[0, j], preferred_element_type=F32)], axis=1) + bias_ref[h]
        if extra is not None:
            s = s + extra
        sink = sink_ref[h]
        m = jnp.maximum(jnp.max(s, axis=1, keepdims=True), sink[:, :1])
        e = jnp.exp(s - m)
        p = e / (jnp.sum(e, axis=1, keepdims=True) + jnp.exp(sink[:, :1] - m))
        pb = p.astype(BF16)
        o_ref[0, :, h * HEAD_DIM:(h + 1) * HEAD_DIM] = (
            jnp.dot(pb[:, :LANE], vp_ref[0, j], preferred_element_type=F32)
            + jnp.dot(pb[:, LANE:], vc_ref[0, j], preferred_element_type=F32))


def _band_bias(rel_table):
    r = jnp.arange(LANE, dtype=jnp.int32)[:, None]
    s = jnp.arange(2 * LANE, dtype=jnp.int32)[None, :]
    wc, qh = s // CHUNK, r // CHUNK
    band = (wc >= qh) & (wc <= qh + 2)
    tiles = _rel_tiles(rel_table, (s - LANE - r)[None])[0]
    return jnp.where(band[None], tiles, NEG_BIG)


def _band_attention(q16, kp, kc, vp, vc, bias, sinks, *, tq, prev_map, cur_map, first_has_no_prev):
    bsz, t = q16.shape[0], q16.shape[2]
    kblk = lambda m: pl.BlockSpec((1, KV_HEADS_B, HEAD_DIM, LANE), lambda b, i: (b, 0, 0, m(i)))
    vblk = lambda m: pl.BlockSpec((1, KV_HEADS_B, LANE, HEAD_DIM), lambda b, i: (b, 0, m(i), 0))
    sink_rows = jnp.broadcast_to(sinks.astype(F32)[:, None, None], (N_HEADS, 1, LANE))
    return pl.pallas_call(
        functools.partial(_band_kernel, tq=tq, first_has_no_prev=first_has_no_prev),
        grid=(bsz, t // tq),
        in_specs=[pl.BlockSpec((1, N_HEADS, tq, HEAD_DIM), lambda b, i: (b, 0, i, 0)),
                  kblk(prev_map), kblk(cur_map), vblk(prev_map), vblk(cur_map),
                  pl.BlockSpec((N_HEADS, tq, 2 * LANE), lambda b, i: (0, 0, 0)),
                  pl.BlockSpec((N_HEADS, 1, LANE), lambda b, i: (0, 0, 0))],
        out_specs=pl.BlockSpec((1, tq, ATTN_WIDTH), lambda b, i: (b, i, 0)),
        out_shape=jax.ShapeDtypeStruct((bsz, t, ATTN_WIDTH), F32),
        compiler_params=_params(ARB2, vmem=False),
        name="band_attention",
    )(q16, kp, kc, vp, vc, bias[:, :tq, :], sink_rows)


def _mixer_residual(x_ref, o_ref, wo_ref, gt1_ref):
    return x_ref[0] + gt1_ref[0] * jnp.dot(o_ref[0].astype(BF16), wo_ref[...], preferred_element_type=F32)


def _swiglu_chunk(h, wg, wu, wd):
    a = jax.nn.silu(jnp.dot(h, wg, preferred_element_type=F32)) * jnp.dot(h, wu, preferred_element_type=F32)
    return jnp.dot(a.astype(BF16), wd, preferred_element_type=F32)


def _ffn_kernel(x_ref, o_ref, wo_ref, gt1_ref, g_ref, sh_ref, sc_ref, gt2_ref, wg_ref, wu_ref, wd_ref,
                out_ref, xn_scr, h_scr, acc_scr):
    c = pl.program_id(2)

    @pl.when(c == 0)
    def _():
        xn = _mixer_residual(x_ref, o_ref, wo_ref, gt1_ref)
        xn_scr[...] = xn
        h_scr[...] = _modulate(xn, g_ref[...], sh_ref[0], sc_ref[0]).astype(BF16)
        acc_scr[...] = jnp.zeros(acc_scr.shape, F32)

    acc_scr[...] += _swiglu_chunk(h_scr[...], wg_ref[...], wu_ref[...], wd_ref[...])

    @pl.when(c == pl.num_programs(2) - 1)
    def _():
        out_ref[0] = xn_scr[...] + gt2_ref[0] * acc_scr[...]


def _moe_pre_kernel(x_ref, o_ref, wo_ref, gt1_ref, g_ref, sh_ref, sc_ref, wr_ref, xn_ref, h_ref, gate_ref, sel_ref):
    xn = _mixer_residual(x_ref, o_ref, wo_ref, gt1_ref)
    xn_ref[0] = xn
    h = _modulate(xn, g_ref[...], sh_ref[0], sc_ref[0])
    h_ref[0] = h
    lane = lax.broadcasted_iota(jnp.int32, (h.shape[0], LANE), 1)
    lanef = lane.astype(F32)
    logits = jnp.dot(h.astype(BF16), wr_ref[...], preferred_element_type=F32)
    logits = jnp.where(lane < N_EXPERTS, logits, -jnp.inf)
    m1 = jnp.max(logits, axis=1, keepdims=True)
    i1 = jnp.min(jnp.where(logits == m1, lanef, float(LANE)), axis=1, keepdims=True)
    rest = jnp.where(lanef == i1, -jnp.inf, logits)
    m2 = jnp.max(rest, axis=1, keepdims=True)
    i2 = jnp.min(jnp.where(rest == m2, lanef, float(LANE)), axis=1, keepdims=True)
    e2 = jnp.exp(m2 - m1)
    den = 1.0 + e2
    gates = jnp.where(lanef == i1, 1.0 / den, 0.0) + jnp.where(lanef == i2, e2 / den, 0.0)
    sel = jnp.where(lanef == i1, 1.0, 0.0) + jnp.where(lanef == i2, 1.0, 0.0)
    gate_ref[0] = gates[:, :N_EXPERTS]
    sel_ref[0] = sel[:, :N_EXPERTS]


def _moe_expert_kernel(te_tab, tv_tab, src_hbm, dst_hbm, h_hbm, wg_ref, wu_ref, wd_ref, y_hbm,
                       src_smem, dst_smem, hbuf, hb, acc, sem_idx, sem_in, sem_out, *, rows):
    i, c = pl.program_id(0), pl.program_id(1)
    valid = tv_tab[i] != 0
    all_rows_in = pltpu.make_async_copy(h_hbm.at[pl.ds(0, rows), :], hbuf, sem_in)
    all_rows_out = pltpu.make_async_copy(acc, y_hbm.at[pl.ds(0, rows), :], sem_out)

    @pl.when(valid & (c == 0))
    def _():
        idx_copies = [pltpu.make_async_copy(src_hbm.at[i], src_smem, sem_idx.at[0]),
                      pltpu.make_async_copy(dst_hbm.at[i], dst_smem, sem_idx.at[1])]
        for cp in idx_copies:
            cp.start()
        for cp in idx_copies:
            cp.wait()

        def gather_row(r, carry):
            pltpu.make_async_copy(h_hbm.at[pl.ds(src_smem[r], 1), :], hbuf.at[pl.ds(r, 1), :], sem_in).start()
            return carry
        lax.fori_loop(0, rows, gather_row, 0, unroll=8)
        all_rows_in.wait()
        hb[...] = hbuf[...].astype(BF16)
        acc[...] = jnp.zeros(acc.shape, F32)

    @pl.when(valid)
    def _():
        acc[...] += _swiglu_chunk(hb[...], wg_ref[0], wu_ref[0], wd_ref[0])

    @pl.when(valid & (c == pl.num_programs(1) - 1))
    def _():
        def scatter_row(r, carry):
            pltpu.make_async_copy(acc.at[pl.ds(r, 1), :], y_hbm.at[pl.ds(dst_smem[r], 1), :], sem_out).start()
            return carry
        lax.fori_loop(0, rows, scatter_row, 0, unroll=8)
        all_rows_out.wait()


def _moe_combine_kernel(xn_ref, gt2_ref, g2_ref, ya_ref, yb_ref, out_ref):
    g2 = g2_ref[0]
    out_ref[0] = xn_ref[0] + gt2_ref[0] * (g2[:, 0:1] * ya_ref[...] + g2[:, 1:2] * yb_ref[...])


def _token_specs(tm, nd):
    pick = (lambda f: lambda b, i: f(b, i)) if nd == 2 else (lambda f: lambda b, i, c: f(b, i))
    tok = pl.BlockSpec((1, tm, D_MODEL), pick(lambda b, i: (b, i, 0)))

    def mod_spec(a):
        return pl.BlockSpec((1, 1, D_MODEL), pick(lambda b, i: (b, 0, 0))) if a.shape[1] == 1 else tok

    cst = lambda shape: pl.BlockSpec(shape, pick(lambda b, i: (0, 0)))
    return tok, mod_spec, cst


def _swiglu_mixer(x, o, w_out, gt1, g, shift, scale, gt2, w_gate, w_up, w_down, *, tm):
    gsz, t, _ = x.shape
    d_ff = w_gate.shape[-1]
    assert t % tm == 0 and d_ff % FF_TILE == 0
    tok, mod_spec, cst = _token_specs(tm, 3)
    return pl.pallas_call(
        _ffn_kernel, grid=(gsz, t // tm, d_ff // FF_TILE),
        in_specs=[tok, tok, cst((ATTN_WIDTH, D_MODEL)), mod_spec(gt1), cst((1, D_MODEL)),
                  mod_spec(shift), mod_spec(scale), mod_spec(gt2),
                  pl.BlockSpec((D_MODEL, FF_TILE), lambda b, i, c: (0, c)),
                  pl.BlockSpec((D_MODEL, FF_TILE), lambda b, i, c: (0, c)),
                  pl.BlockSpec((FF_TILE, D_MODEL), lambda b, i, c: (c, 0))],
        out_specs=tok,
        out_shape=jax.ShapeDtypeStruct((gsz, t, D_MODEL), F32),
        scratch_shapes=[pltpu.VMEM((tm, D_MODEL), F32), pltpu.VMEM((tm, D_MODEL), BF16),
                        pltpu.VMEM((tm, D_MODEL), F32)],
        compiler_params=_params(("arbitrary",) * 3),
        name="swiglu_channel_mixer",
    )(x, o, w_out.astype(BF16), gt1, g[None, :], shift, scale, gt2,
      w_gate.astype(BF16), w_up.astype(BF16), w_down.astype(BF16))


def _moe_pre(x, o, w_out, gt1, g, shift, scale, w_router, *, tm):
    gsz, t, _ = x.shape
    assert t % tm == 0
    tok, mod_spec, cst = _token_specs(tm, 2)
    small = pl.BlockSpec((1, tm, N_EXPERTS), lambda b, i: (b, i, 0))
    wr = jnp.pad(w_router.astype(BF16), ((0, 0), (0, LANE - N_EXPERTS)))
    return pl.pallas_call(
        _moe_pre_kernel, grid=(gsz, t // tm),
        in_specs=[tok, tok, cst((ATTN_WIDTH, D_MODEL)), mod_spec(gt1), cst((1, D_MODEL)),
                  mod_spec(shift), mod_spec(scale), cst((D_MODEL, LANE))],
        out_specs=[tok, tok, small, small],
        out_shape=[jax.ShapeDtypeStruct((gsz, t, D_MODEL), F32), jax.ShapeDtypeStruct((gsz, t, D_MODEL), F32),
                   jax.ShapeDtypeStruct((gsz, t, N_EXPERTS), F32), jax.ShapeDtypeStruct((gsz, t, N_EXPERTS), F32)],
        compiler_params=_params(ARB2),
        name="moe_pre_router",
    )(x, o, w_out.astype(BF16), gt1, g[None, :], shift, scale, wr)


def _moe_routing(gates, sel, rows):
    n = gates.shape[0]
    mi = (sel > 0).astype(jnp.int32)
    padded = (mi.sum(0) + rows - 1) // rows * rows
    ends = jnp.cumsum(padded)
    rank = jnp.cumsum(mi, axis=0) - mi
    slot = jnp.cumsum(mi, axis=1) - mi
    n_rows = (2 * n + N_EXPERTS * rows) // rows * rows
    dest = jnp.where(mi > 0, (ends - padded)[None, :] + rank, n_rows).reshape(-1)
    tok = jnp.broadcast_to(jnp.arange(n, dtype=jnp.int32)[:, None], mi.shape)
    src = jnp.zeros((n_rows,), jnp.int32).at[dest].set(tok.reshape(-1), mode="drop")
    spare = 2 * n + jnp.arange(n_rows, dtype=jnp.int32) % rows
    dst = spare.at[dest].set((slot * n + tok).reshape(-1), mode="drop")
    n_tiles = n_rows // rows
    tile_start = jnp.arange(n_tiles, dtype=jnp.int32) * rows
    tile_expert = jnp.minimum((tile_start[:, None] >= ends[None, :]).sum(1), N_EXPERTS - 1).astype(jnp.int32)
    tile_valid = (tile_start < ends[-1]).astype(jnp.int32)
    g2 = jnp.stack([jnp.sum(jnp.where((mi > 0) & (slot == k), gates, 0.0), axis=1) for k in range(TOP_K)], axis=1)
    return src.reshape(n_tiles, rows), dst.reshape(n_tiles, rows), tile_expert, tile_valid, g2


def _moe_experts(h_all, src, dst, tile_expert, tile_valid, w_gate, w_up, w_down, *, rows):
    n = h_all.shape[0]
    n_tiles = src.shape[0]
    d_ff = w_gate.shape[-1]
    any_spec = pl.BlockSpec(memory_space=pl.ANY)
    return pl.pallas_call(
        functools.partial(_moe_expert_kernel, rows=rows),
        grid_spec=pltpu.PrefetchScalarGridSpec(
            num_scalar_prefetch=2, grid=(n_tiles, d_ff // FF_TILE),
            in_specs=[any_spec, any_spec, any_spec,
                      pl.BlockSpec((1, D_MODEL, FF_TILE), lambda i, c, te, tv: (te[i], 0, c)),
                      pl.BlockSpec((1, D_MODEL, FF_TILE), lambda i, c, te, tv: (te[i], 0, c)),
                      pl.BlockSpec((1, FF_TILE, D_MODEL), lambda i, c, te, tv: (te[i], c, 0))],
            out_specs=any_spec,
            scratch_shapes=[pltpu.SMEM((rows,), jnp.int32), pltpu.SMEM((rows,), jnp.int32),
                            pltpu.VMEM((rows, D_MODEL), F32), pltpu.VMEM((rows, D_MODEL), BF16),
                            pltpu.VMEM((rows, D_MODEL), F32), pltpu.SemaphoreType.DMA((2,)),
                            pltpu.SemaphoreType.DMA(()), pltpu.SemaphoreType.DMA(())]),
        out_shape=jax.ShapeDtypeStruct((2 * n + rows, D_MODEL), F32),
        compiler_params=_params(ARB2),
        name="moe_routed_experts",
    )(tile_expert, tile_valid, src, dst, h_all, w_gate.astype(BF16), w_up.astype(BF16), w_down.astype(BF16))


def _moe_combine(xn, gt2, g2, y, *, tm, n_all, tok0):
    gsz, t, _ = xn.shape
    assert t % tm == 0 and n_all % tm == 0 and tok0 % tm == 0
    tok, mod_spec, _ = _token_specs(tm, 2)
    nt = t // tm
    yspec = lambda slot: pl.BlockSpec((tm, D_MODEL), lambda b, i: ((slot * n_all + tok0) // tm + b * nt + i, 0))
    return pl.pallas_call(
        _moe_combine_kernel, grid=(gsz, nt),
        in_specs=[tok, mod_spec(gt2), pl.BlockSpec((1, tm, TOP_K), lambda b, i: (b, i, 0)), yspec(0), yspec(1)],
        out_specs=tok,
        out_shape=jax.ShapeDtypeStruct((gsz, t, D_MODEL), F32),
        compiler_params=_params(ARB2),
        name="moe_combine",
    )(xn, gt2, g2, y, y)


def _per_token(a, n):
    return jnp.broadcast_to(a, (a.shape[0], n, a.shape[2])).reshape(1, a.shape[0] * n, a.shape[2])


def _pad_keys(a, axis, l_pad):
    pad = [(0, 0)] * a.ndim
    pad[axis] = (0, l_pad - a.shape[axis])
    return jnp.pad(a, pad)


def _layer_a(xp, xs, cp, cs, ck, cv, cki, rel_bias, norm_g, w_ada, b_ada, w_in, w_out, g_q, g_k, g_kidx,
             ffn_wg, ffn_wu, ffn_wd):
    bias = _dsa_bias_tiles(rel_bias)
    nb, n = xs.shape[0], xs.shape[1]
    past = ck.shape[1]
    outs = []
    for x, c, sample in ((xp, cp, False), (xs, cs, True)):
        sh1, sc1, gt1, sh2, sc2, gt2 = adaln(c, w_ada, b_ada)
        q16, kf, vf, ktr, vx, kif, qi16, kit, wsc = _project(
            x, norm_g[0], sh1, sc1, w_in, g_q, g_k, g_kidx, n_kv=KV_HEADS_A, has_idx=True,
            tm=n if sample else 256)
        if sample:
            l_true, l_pad = past + n, 2304
            kfull = jnp.concatenate([ck, kf.reshape(nb, n, KV_HEADS_A, HEAD_DIM)], axis=1).astype(BF16)
            vfull = jnp.concatenate([cv, vf.reshape(nb, n, KV_HEADS_A, HEAD_DIM)], axis=1).astype(BF16)
            kifull = jnp.concatenate([cki, kif], axis=1).astype(BF16)
            ktr = _pad_keys(kfull, 1, l_pad).transpose(0, 2, 3, 1)
            vx = jnp.concatenate([vfull, jnp.ones(vfull.shape[:3] + (1,), BF16),
                                  jnp.zeros(vfull.shape[:3] + (LANE - HEAD_DIM - 1,), BF16)], axis=-1)
            vx = _pad_keys(vx, 1, l_pad).transpose(0, 2, 1, 3)
            kit = _pad_keys(kifull, 1, l_pad).transpose(0, 2, 1)
            o = _dsa_attention(q16, ktr, vx, qi16, kit, wsc, bias, p0=past, l_true=l_true,
                               n_sel=min(TOPK_MAX, l_true // 4), tq=n, tk=l_pad, sw=256, cw=256)
            flat = lambda a: a.reshape(1, nb * n, a.shape[-1])
            y = _swiglu_mixer(flat(x), flat(o), w_out, _per_token(gt1, n), norm_g[1], _per_token(sh2, n),
                              _per_token(sc2, n), _per_token(gt2, n), ffn_wg, ffn_wu, ffn_wd, tm=MIX_TM_SAMPLE)
            y = y.reshape(x.shape)
        else:
            t = x.shape[1]
            o = _dsa_attention(q16, ktr, vx, qi16, kit, wsc, bias, p0=0, l_true=t,
                               n_sel=min(TOPK_MAX, t // 4), tq=LANE, tk=1024, sw=256, cw=1024)
            y = _swiglu_mixer(x, o, w_out, gt1, norm_g[1], sh2, sc2, gt2, ffn_wg, ffn_wu, ffn_wd, tm=MIX_TM_PROMPT)
        bsz, t = x.shape[0], x.shape[1]
        outs.append((y, kf.reshape(bsz, t, KV_HEADS_A, HEAD_DIM), vf.reshape(bsz, t, KV_HEADS_A, HEAD_DIM), kif))
    return outs


def _layer_b(xp, xs, cp, cs, ck, cv, rel_bias, norm_g, w_ada, b_ada, w_in, w_out, g_q, g_k, sinks,
             w_router, moe_wg, moe_wu, moe_wd):
    bias = _band_bias(rel_bias)
    nb, n = xs.shape[0], xs.shape[1]
    streams = []
    for x, c, sample in ((xp, cp, False), (xs, cs, True)):
        sh1, sc1, gt1, sh2, sc2, gt2 = adaln(c, w_ada, b_ada)
        q16, kf, vf, ktr, v16 = _project(x, norm_g[0], sh1, sc1, w_in, g_q, g_k, g_k, n_kv=KV_HEADS_B,
                                         has_idx=False, tm=n if sample else 256)
        bsz, t = x.shape[0], x.shape[1]
        k4 = kf.reshape(bsz, t, KV_HEADS_B, HEAD_DIM)
        v4 = vf.reshape(bsz, t, KV_HEADS_B, HEAD_DIM)
        zero = lambda i: 0
        if sample:
            kp = ck.astype(BF16).transpose(0, 2, 3, 1)
            vp = cv.astype(BF16).transpose(0, 2, 1, 3)
            o = _band_attention(q16, kp, _pad_keys(ktr, 3, LANE), vp, _pad_keys(v16, 2, LANE), bias, sinks,
                                tq=n, prev_map=zero, cur_map=zero, first_has_no_prev=False)
            flat = lambda a: a.reshape(1, nb * n, a.shape[-1])
            xn, h, gates, sel = _moe_pre(flat(x), flat(o), w_out, _per_token(gt1, n), norm_g[1],
                                         _per_token(sh2, n), _per_token(sc2, n), w_router, tm=MIX_TM_SAMPLE)
            gt2 = _per_token(gt2, n)
            k_new = jnp.concatenate([ck, k4], axis=1)[:, -WINDOW:]
            v_new = jnp.concatenate([cv, v4], axis=1)[:, -WINDOW:]
        else:
            o = _band_attention(q16, ktr, ktr, v16, v16, bias, sinks, tq=LANE,
                                prev_map=lambda i: jnp.maximum(i - 1, 0), cur_map=lambda i: i,
                                first_has_no_prev=True)
            xn, h, gates, sel = _moe_pre(x, o, w_out, gt1, norm_g[1], sh2, sc2, w_router, tm=MIX_TM_PROMPT)
            k_new, v_new = k4[:, -WINDOW:], v4[:, -WINDOW:]
        streams.append((x.shape, xn, h, gates, sel, gt2, k_new, v_new))

    rows2d = lambda a: a.reshape(-1, a.shape[-1])
    h_all = jnp.concatenate([rows2d(st[2]) for st in streams])
    n_all = h_all.shape[0]
    src, dst, tile_expert, tile_valid, g2 = _moe_routing(
        jnp.concatenate([rows2d(st[3]) for st in streams]), jnp.concatenate([rows2d(st[4]) for st in streams]),
        MOE_ROW_TILE)
    y = _moe_experts(h_all, src, dst, tile_expert, tile_valid, moe_wg, moe_wu, moe_wd, rows=MOE_ROW_TILE)
    outs, tok0 = [], 0
    for shape, xn, _, _, _, gt2, k_new, v_new in streams:
        cnt = xn.shape[0] * xn.shape[1]
        out = _moe_combine(xn, gt2, g2[tok0:tok0 + cnt].reshape(xn.shape[0], xn.shape[1], TOP_K), y,
                           tm=MOE_COMBINE_TM, n_all=n_all, tok0=tok0)
        outs.append((out.reshape(shape), k_new, v_new))
        tok0 += cnt
    return outs


def kernel(x_prompt, x_sample, c_prompt, c_sample, cache_a_k, cache_a_v, cache_a_kidx,
           cache_b_k, cache_b_v, rel_bias, norm_g, w_ada, b_ada,
           a_w_in, a_w_out, a_g_q, a_g_k, a_g_kidx,
           b_w_in, b_w_out, b_g_q, b_g_k, b_sinks,
           ffn_w_gate, ffn_w_up, ffn_w_down,
           moe_w_router, moe_w_gate, moe_w_up, moe_w_down):
    xp, xs = x_prompt, x_sample
    a_out = [[] for _ in range(6)]
    b_out = [[] for _ in range(4)]
    for i in range(DEPTH):
        j = i // 2
        if i % 2 == 0:
            (xp, k1, v1, i1), (xs, k2, v2, i2) = _layer_a(
                xp, xs, c_prompt, c_sample, cache_a_k[j], cache_a_v[j], cache_a_kidx[j], rel_bias, norm_g[i],
                w_ada[i], b_ada[i], a_w_in[j], a_w_out[j], a_g_q[j], a_g_k[j], a_g_kidx[j],
                ffn_w_gate[j], ffn_w_up[j], ffn_w_down[j])
            for lst, val in zip(a_out, (k1, v1, i1, k2, v2, i2)):
                lst.append(val)
        else:
            (xp, k1, v1), (xs, k2, v2) = _layer_b(
                xp, xs, c_prompt, c_sample, cache_b_k[j], cache_b_v[j], rel_bias, norm_g[i],
                w_ada[i], b_ada[i], b_w_in[j], b_w_out[j], b_g_q[j], b_g_k[j], b_sinks[j],
                moe_w_router[j], moe_w_gate[j], moe_w_up[j], moe_w_down[j])
            for lst, val in zip(b_out, (k1, v1, k2, v2)):
                lst.append(val)
    return (xp, xs, *(jnp.stack(l) for l in a_out), *(jnp.stack(l) for l in b_out))
```

```python
import math, functools
import jax, jax.numpy as jnp
from jax import lax
import numpy as np
from jax.experimental import pallas as pl
from jax.experimental.pallas import tpu as pltpu

D_MODEL = 1024
DEPTH = 2
CHUNK = 64
N_HEADS = 16
HEAD_DIM = 64
ATTN_WIDTH = N_HEADS * HEAD_DIM
KV_HEADS_A = 4
IDX_HEADS = 8
IDX_DIM = 64
TOPK_MAX = 256
KV_HEADS_B = 2
WINDOW = 128
NUM_BUCKETS = 32
MAX_DISTANCE = 1024
N_EXPERTS = 8
TOP_K = 2
EPS = 1e-6

F32 = jnp.float32
BF16 = jnp.bfloat16
LANE = 128
VMEM_LIMIT_BYTES = 48 * 1024 * 1024
LOG2E = math.log2(math.e)
NEG_BIG = -1e30
KEY_MIN = -2 ** 31
FAR_BUCKET = NUM_BUCKETS // 2 - 1
NEAR_BLOCKS = 6
FF_TILE = 512
MIX_TM_PROMPT = 512
MIX_TM_SAMPLE = 256
MOE_ROW_TILE = 512
MOE_COMBINE_TM = 256
ARB2 = ("arbitrary", "arbitrary")


def _params(sem, vmem=True):
    return pltpu.CompilerParams(dimension_semantics=sem, vmem_limit_bytes=VMEM_LIMIT_BYTES if vmem else None)


def rel_bucket(rel):
    half = NUM_BUCKETS // 2
    max_exact = half // 2
    base = jnp.where(rel > 0, half, 0)
    n = jnp.abs(rel)
    nf = jnp.maximum(n, 1).astype(F32)
    large = max_exact + (jnp.log(nf / max_exact) / math.log(MAX_DISTANCE / max_exact)
                         * (half - max_exact)).astype(jnp.int32)
    large = jnp.minimum(large, half - 1)
    return base + jnp.where(n < max_exact, n, large)


def _adaln_kernel(c_ref, w_ref, b_ref, o_ref):
    c = c_ref[...]
    o_ref[...] = jnp.dot(c * jax.nn.sigmoid(c), w_ref[...], preferred_element_type=F32) + b_ref[...]


def adaln(c, w, b):
    n = c.shape[0]
    mod = pl.pallas_call(
        _adaln_kernel,
        grid=(6,),
        in_specs=[pl.BlockSpec((n, D_MODEL), lambda j: (0, 0)),
                  pl.BlockSpec((D_MODEL, D_MODEL), lambda j: (0, j)),
                  pl.BlockSpec((1, D_MODEL), lambda j: (0, j))],
        out_specs=pl.BlockSpec((n, D_MODEL), lambda j: (0, j)),
        out_shape=jax.ShapeDtypeStruct((n, 6 * D_MODEL), F32),
        name="adaln",
    )(c, w, b[None, :])
    return jnp.split(mod[:, None, :], 6, axis=-1)


def _modulate(x, g, shift, scale):
    y = x * lax.rsqrt(jnp.mean(x * x, axis=-1, keepdims=True) + EPS)
    return (y * g) * (1.0 + scale) + shift


def _seg_norm(seg, g):
    return seg * lax.rsqrt(jnp.mean(seg * seg, axis=-1, keepdims=True) + EPS) * g


def _proj_kernel(x_ref, g_ref, sh_ref, sc_ref, w_ref, gq_ref, gk_ref, gki_ref, *outs, n_kv, has_idx):
    h = _modulate(x_ref[0], g_ref[...], sh_ref[0], sc_ref[0]).astype(BF16)
    y = jnp.dot(h, w_ref[...], preferred_element_type=F32)
    tm = y.shape[0]
    kv_w = n_kv * HEAD_DIM
    if has_idx:
        q16, kf, vf, ktr, vx, kif, qi16, kit, wsc = outs
    else:
        q16, kf, vf, ktr, v16 = outs
    for hd in range(N_HEADS):
        seg = y[:, hd * HEAD_DIM:(hd + 1) * HEAD_DIM]
        q_scale = HEAD_DIM ** -0.5 * (LOG2E if has_idx else 1.0)
        q16[0, hd] = (_seg_norm(seg, gq_ref[...]) * q_scale).astype(BF16)
    lane = lax.broadcasted_iota(jnp.int32, (tm, LANE - HEAD_DIM), 1)
    ones_col = jnp.where(lane == 0, 1.0, 0.0).astype(BF16)
    for j in range(n_kv):
        kseg = _seg_norm(y[:, ATTN_WIDTH + j * HEAD_DIM:ATTN_WIDTH + (j + 1) * HEAD_DIM], gk_ref[...])
        vseg = y[:, ATTN_WIDTH + kv_w + j * HEAD_DIM:ATTN_WIDTH + kv_w + (j + 1) * HEAD_DIM]
        kf[0, :, j * HEAD_DIM:(j + 1) * HEAD_DIM] = kseg
        vf[0, :, j * HEAD_DIM:(j + 1) * HEAD_DIM] = vseg
        ktr[0, j] = kseg.T.astype(BF16)
        if has_idx:
            vx[0, j] = jnp.concatenate([vseg.astype(BF16), ones_col], axis=1)
        else:
            v16[0, j] = vseg.astype(BF16)
    if has_idx:
        base = ATTN_WIDTH + 2 * kv_w
        for hd in range(IDX_HEADS):
            qi16[0, hd] = y[:, base + hd * IDX_DIM:base + (hd + 1) * IDX_DIM].astype(BF16)
        base += IDX_HEADS * IDX_DIM
        kiseg = _seg_norm(y[:, base:base + IDX_DIM], gki_ref[...])
        kif[0] = kiseg
        kit[0] = kiseg.T.astype(BF16)
        wsc[0] = y[:, base + IDX_DIM:base + IDX_DIM + IDX_HEADS] * IDX_HEADS ** -0.5 * IDX_DIM ** -0.5


def _project(x, g, shift, scale, w_in, g_q, g_k, g_kidx, *, n_kv, has_idx, tm):
    gsz, t, _ = x.shape
    n_in = w_in.shape[1]
    kv_w = n_kv * HEAD_DIM
    tok = lambda b, i: (b, i, 0)
    head = lambda b, i: (b, 0, i, 0)
    vec = lambda b, i: (b, 0, 0)
    cst = lambda b, i: (0, 0)
    shapes = [((gsz, N_HEADS, t, HEAD_DIM), BF16, (1, N_HEADS, tm, HEAD_DIM), head),
              ((gsz, t, kv_w), F32, (1, tm, kv_w), tok),
              ((gsz, t, kv_w), F32, (1, tm, kv_w), tok),
              ((gsz, n_kv, HEAD_DIM, t), BF16, (1, n_kv, HEAD_DIM, tm), lambda b, i: (b, 0, 0, i))]
    if has_idx:
        shapes += [((gsz, n_kv, t, LANE), BF16, (1, n_kv, tm, LANE), head),
                   ((gsz, t, IDX_DIM), F32, (1, tm, IDX_DIM), tok),
                   ((gsz, IDX_HEADS, t, IDX_DIM), BF16, (1, IDX_HEADS, tm, IDX_DIM), head),
                   ((gsz, IDX_DIM, t), BF16, (1, IDX_DIM, tm), lambda b, i: (b, 0, i)),
                   ((gsz, t, IDX_HEADS), F32, (1, tm, IDX_HEADS), tok)]
    else:
        shapes += [((gsz, n_kv, t, HEAD_DIM), BF16, (1, n_kv, tm, HEAD_DIM), head)]
    return pl.pallas_call(
        functools.partial(_proj_kernel, n_kv=n_kv, has_idx=has_idx),
        grid=(gsz, t // tm),
        in_specs=[pl.BlockSpec((1, tm, D_MODEL), tok), pl.BlockSpec((1, D_MODEL), cst),
                  pl.BlockSpec((1, 1, D_MODEL), vec), pl.BlockSpec((1, 1, D_MODEL), vec),
                  pl.BlockSpec((D_MODEL, n_in), cst), pl.BlockSpec((1, HEAD_DIM), cst),
                  pl.BlockSpec((1, HEAD_DIM), cst), pl.BlockSpec((1, IDX_DIM), cst)],
        out_specs=[pl.BlockSpec(blk, im) for _, _, blk, im in shapes],
        out_shape=[jax.ShapeDtypeStruct(s, d) for s, d, _, _ in shapes],
        compiler_params=_params(ARB2),
        name="mixer_in_proj",
    )(x, g[None, :], shift, scale, w_in.astype(BF16), g_q[None, :], g_k[None, :], g_kidx[None, :])


def _tri_steps(nqb, tq, tk, p0):
    qbs, kts = [], []
    for qb in range(nqb):
        vis_end = ((p0 + qb * tq + tq - 1) // CHUNK + 1) * CHUNK
        for kt in range(-(-vis_end // tk)):
            qbs.append(qb)
            kts.append(kt)
    return jnp.asarray(np.array(qbs, np.int32)), jnp.asarray(np.array(kts, np.int32))


def _idx_kernel(qb_tab, kt_tab, qi_ref, w_ref, kit_ref, out_ref, *, tq, tk, sw, p0, l_true):
    step = pl.program_id(1)
    q0 = p0 + qb_tab[step] * tq
    k0 = kt_tab[step] * tk
    row = lax.broadcasted_iota(jnp.int32, (tq, 1), 0) + q0
    vis_end = jnp.minimum(((row >> 6) + 1) << 6, l_true)
    wv = w_ref[0]
    for c in range(tk // sw):
        kit = kit_ref[0, :, c * sw:(c + 1) * sw]
        acc = jnp.zeros((tq, sw), F32)
        for h in range(IDX_HEADS):
            sh = jnp.dot(qi_ref[0, h], kit, preferred_element_type=F32)
            acc = acc + wv[:, h:h + 1] * jnp.maximum(sh, 0.0)
        bits = lax.bitcast_convert_type(acc, jnp.int32)
        key = bits ^ ((bits >> 31) & 0x7FFFFFFF)
        kpos = k0 + c * sw + lax.broadcasted_iota(jnp.int32, (tq, sw), 1)
        out_ref[0, :, c * sw:(c + 1) * sw] = jnp.where(kpos < vis_end, key, KEY_MIN)


def _thr_kernel(keys_ref, tau_ref, quota_ref, flag_ref, *, tq, cw, p0, n_sel):
    qb = pl.program_id(1)
    vis_end = (((p0 + qb * tq + tq - 1) >> 6) + 1) << 6
    nch = (vis_end + cw - 1) // cw
    key_max = 2 ** 31 - 1

    def count_ge(cand):
        def body(j, acc):
            off = pl.multiple_of(j * cw, cw)
            for u in range(cw // LANE):
                x = keys_ref[0, :, pl.ds(off + u * LANE, LANE)]
                acc = acc + jnp.where(x >= cand, 1.0, 0.0)
            return acc
        acc = lax.fori_loop(0, nch, body, jnp.zeros((tq, LANE), F32))
        return jnp.sum(acc, axis=1, keepdims=True)

    zero = jnp.zeros((tq, LANE), jnp.int32)
    c0 = count_ge(zero)
    tau = jnp.where(c0 >= n_sel, zero, KEY_MIN)
    n_ge = jnp.where(c0 >= n_sel, c0, float(2 ** 24))

    def bit_body(carry):
        i, tau, n_ge = carry
        cand = tau + jnp.left_shift(jnp.int32(1), 30 - i)
        c = count_ge(cand)
        return i + 1, jnp.where(c >= n_sel, cand, tau), jnp.where(c >= n_sel, c, n_ge)

    def unresolved(carry):
        i, _, n_ge = carry
        return (i < 31) & (jnp.max(jnp.abs(n_ge - n_sel)) > 0.0)

    _, tau, n_ge = lax.while_loop(unresolved, bit_body, (jnp.int32(0), tau, n_ge))
    n_gt = jnp.where(tau[:, :1] < key_max, count_ge(jnp.minimum(tau, key_max - 1) + 1), 0.0)
    tau_ref[0] = jnp.maximum(tau, KEY_MIN + 1)
    quota_ref[0] = jnp.broadcast_to(n_sel - n_gt, (tq, LANE))
    over = jnp.max(jnp.where(n_ge > n_sel, 1, 0), axis=0, keepdims=True)
    flag_ref[0, 0] = jnp.broadcast_to(over, flag_ref.shape[2:])


def _attn_kernel(qb_tab, kt_tab, flag_tab, q_ref, kt_ref, vx_ref, keys_ref, tau_ref, quota_ref, bias_ref, o_ref,
                 m_ref, acc_ref, negm_ref, seen_ref, *, tq, tk, sw, p0, nqb):
    step = pl.program_id(1)
    qb, kt = qb_tab[step], kt_tab[step]
    q0 = p0 + qb * tq
    k0 = kt * tk
    vis_end = (((q0 + tq - 1) >> 6) + 1) << 6
    group = N_HEADS // KV_HEADS_A
    nblk = sw // LANE
    tie_w = 2 * LANE

    @pl.when(kt == 0)
    def _():
        m_ref[...] = jnp.full(m_ref.shape, NEG_BIG, F32)
        acc_ref[...] = jnp.zeros(acc_ref.shape, F32)
        seen_ref[...] = jnp.zeros(seen_ref.shape, F32)

    has_ties = flag_tab[pl.program_id(0) * nqb + qb] != 0
    tau = tau_ref[0]

    @pl.when(jnp.logical_not(has_ties))
    def _():
        for u in range(tk // LANE):
            cols = slice(u * LANE, (u + 1) * LANE)
            negm_ref[:, cols] = jnp.where(keys_ref[0, :, cols] >= tau, 0.0, NEG_BIG)

    @pl.when(has_ties)
    def _():
        tau2 = jnp.concatenate([tau] * (tie_w // LANE), axis=1)
        quota = jnp.concatenate([quota_ref[0]] * (tie_w // LANE), axis=1)
        before = (lax.broadcasted_iota(jnp.int32, (tie_w, tie_w), 0)
                  < lax.broadcasted_iota(jnp.int32, (tie_w, tie_w), 1))
        before = jnp.where(before, 1.0, 0.0).astype(BF16)
        seen = seen_ref[...]
        for u in range(tk // tie_w):
            cols = slice(u * tie_w, (u + 1) * tie_w)
            k = keys_ref[0, :, cols]
            tie = jnp.where(k == tau2, 1.0, 0.0)
            rank = (jnp.dot(tie.astype(BF16), before, preferred_element_type=F32)
                    + jnp.concatenate([seen] * (tie_w // LANE), axis=1))
            keep = jnp.where(k > tau2, 1.0, jnp.where(rank < quota, tie, 0.0))
            negm_ref[:, cols] = jnp.where(keep > 0.0, 0.0, NEG_BIG)
            seen = seen + jnp.sum(tie, axis=1, keepdims=True)
        seen_ref[...] = seen

    def sub_tile(c, with_bias):
        off = c * sw if isinstance(c, int) else pl.multiple_of(c * sw, sw)
        negm = negm_ref[:, pl.ds(off, sw)]
        if with_bias:
            d0 = ((k0 + off - q0) >> 7) + NEAR_BLOCKS
            bidx = [jnp.clip(d0 + u, 0, NEAR_BLOCKS) for u in range(nblk)]
        for h in range(N_HEADS):
            j = h // group
            s = jnp.dot(q_ref[0, h], kt_ref[0, j, :, pl.ds(off, sw)], preferred_element_type=F32) + negm
            if with_bias:
                s = s + jnp.concatenate([bias_ref[bidx[u], h] for u in range(nblk)], axis=1)
            m_prev = m_ref[h]
            m_cur = jnp.maximum(m_prev, jnp.max(s, axis=1, keepdims=True))
            alpha = jnp.exp2(m_prev - m_cur)
            p = jnp.exp2(s - jnp.concatenate([m_cur] * nblk, axis=1))
            pv = jnp.dot(p.astype(BF16), vx_ref[0, j, pl.ds(off, sw), :], preferred_element_type=F32)
            acc_ref[h] = alpha * acc_ref[h] + pv
            m_ref[h] = m_cur

    tile_far = k0 + tk <= q0 - (NEAR_BLOCKS - 1) * LANE

    @pl.when(tile_far)
    def _():
        for c in range(tk // sw):
            sub_tile(c, False)

    @pl.when(jnp.logical_not(tile_far))
    def _():
        nsub = jnp.minimum(tk // sw, (vis_end - k0 + sw - 1) // sw)
        nfar = jnp.clip((q0 - (NEAR_BLOCKS - 1) * LANE - k0) // sw, 0, nsub)

        def far_body(c, carry):
            sub_tile(c, False)
            return carry

        def near_body(c, carry):
            sub_tile(c, True)
            return carry

        lax.fori_loop(0, nfar, far_body, 0)
        lax.fori_loop(nfar, nsub, near_body, 0)

    @pl.when(k0 + tk >= vis_end)
    def _():
        for h in range(N_HEADS):
            a = acc_ref[h]
            o_ref[0, :, h * HEAD_DIM:(h + 1) * HEAD_DIM] = a[:, :HEAD_DIM] / a[:, HEAD_DIM:HEAD_DIM + 1]


def _rel_tiles(rel_table, rel):
    onehot = jax.nn.one_hot(rel_bucket(rel), NUM_BUCKETS, dtype=F32)
    return jnp.einsum("nrsk,kh->nhrs", onehot, rel_table.astype(F32), precision=lax.Precision.HIGHEST)


def _dsa_bias_tiles(rel_table):
    d = jnp.arange(NEAR_BLOCKS + 1, dtype=jnp.int32)[:, None, None] - NEAR_BLOCKS
    rel = (d * LANE + jnp.arange(LANE, dtype=jnp.int32)[None, None, :]
           - jnp.arange(LANE, dtype=jnp.int32)[None, :, None])
    tiles = (_rel_tiles(rel_table, rel) - rel_table[FAR_BUCKET][None, :, None, None]) * LOG2E
    return tiles.at[0].set(0.0)


def _dsa_attention(q16, ktr, vx, qi16, kit, w, bias, *, p0, l_true, n_sel, tq, tk, sw, cw):
    bsz, t = q16.shape[0], q16.shape[2]
    l_pad = ktr.shape[3]
    nqb = t // tq
    assert t % tq == 0 and p0 % LANE == 0 and (tq == LANE or nqb == 1)
    assert l_pad % tk == 0 and tk % sw == 0 and tk % cw == 0 and tk % (2 * LANE) == 0
    qbt, ktt = _tri_steps(nqb, tq, tk, p0)
    nsteps = int(qbt.shape[0])

    keys = pl.pallas_call(
        functools.partial(_idx_kernel, tq=tq, tk=tk, sw=sw, p0=p0, l_true=l_true),
        grid_spec=pltpu.PrefetchScalarGridSpec(
            num_scalar_prefetch=2, grid=(bsz, nsteps),
            in_specs=[pl.BlockSpec((1, IDX_HEADS, tq, IDX_DIM), lambda b, s, qt, kt: (b, 0, qt[s], 0)),
                      pl.BlockSpec((1, tq, IDX_HEADS), lambda b, s, qt, kt: (b, qt[s], 0)),
                      pl.BlockSpec((1, IDX_DIM, tk), lambda b, s, qt, kt: (b, 0, kt[s]))],
            out_specs=pl.BlockSpec((1, tq, tk), lambda b, s, qt, kt: (b, qt[s], kt[s]))),
        out_shape=jax.ShapeDtypeStruct((bsz, t, l_pad), jnp.int32),
        compiler_params=_params(ARB2, vmem=False),
        name="dsa_index_keys",
    )(qbt, ktt, qi16, w, kit)

    tau, quota, flags = pl.pallas_call(
        functools.partial(_thr_kernel, tq=tq, cw=cw, p0=p0, n_sel=n_sel),
        grid=(bsz, nqb),
        in_specs=[pl.BlockSpec((1, tq, l_pad), lambda b, i: (b, i, 0))],
        out_specs=[pl.BlockSpec((1, tq, LANE), lambda b, i: (b, i, 0)),
                   pl.BlockSpec((1, tq, LANE), lambda b, i: (b, i, 0)),
                   pl.BlockSpec((1, 1, 8, LANE), lambda b, i: (b, i, 0, 0))],
        out_shape=[jax.ShapeDtypeStruct((bsz, t, LANE), jnp.int32),
                   jax.ShapeDtypeStruct((bsz, t, LANE), F32),
                   jax.ShapeDtypeStruct((bsz, nqb, 8, LANE), jnp.int32)],
        compiler_params=_params(ARB2),
        name="dsa_threshold",
    )(keys)
    flags = flags[:, :, 0, 0].reshape(bsz * nqb)

    return pl.pallas_call(
        functools.partial(_attn_kernel, tq=tq, tk=tk, sw=sw, p0=p0, nqb=nqb),
        grid_spec=pltpu.PrefetchScalarGridSpec(
            num_scalar_prefetch=3, grid=(bsz, nsteps),
            in_specs=[pl.BlockSpec((1, N_HEADS, tq, HEAD_DIM), lambda b, s, qt, kt, fl: (b, 0, qt[s], 0)),
                      pl.BlockSpec((1, KV_HEADS_A, HEAD_DIM, tk), lambda b, s, qt, kt, fl: (b, 0, 0, kt[s])),
                      pl.BlockSpec((1, KV_HEADS_A, tk, LANE), lambda b, s, qt, kt, fl: (b, 0, kt[s], 0)),
                      pl.BlockSpec((1, tq, tk), lambda b, s, qt, kt, fl: (b, qt[s], kt[s])),
                      pl.BlockSpec((1, tq, LANE), lambda b, s, qt, kt, fl: (b, qt[s], 0)),
                      pl.BlockSpec((1, tq, LANE), lambda b, s, qt, kt, fl: (b, qt[s], 0)),
                      pl.BlockSpec((NEAR_BLOCKS + 1, N_HEADS, tq, LANE), lambda b, s, qt, kt, fl: (0, 0, 0, 0))],
            out_specs=pl.BlockSpec((1, tq, ATTN_WIDTH), lambda b, s, qt, kt, fl: (b, qt[s], 0)),
            scratch_shapes=[pltpu.VMEM((N_HEADS, tq, LANE), F32), pltpu.VMEM((N_HEADS, tq, LANE), F32),
                            pltpu.VMEM((tq, tk), F32), pltpu.VMEM((tq, LANE), F32)]),
        out_shape=jax.ShapeDtypeStruct((bsz, t, ATTN_WIDTH), F32),
        compiler_params=_params(ARB2),
        name="dsa_masked_attention",
    )(qbt, ktt, flags, q16, ktr, vx, keys, tau, quota, bias[:, :, :tq, :])


def _band_kernel(q_ref, kp_ref, kc_ref, vp_ref, vc_ref, bias_ref, sink_ref, o_ref, *, tq, first_has_no_prev):
    group = N_HEADS // KV_HEADS_B
    extra = None
    if first_has_no_prev:
        col = lax.broadcasted_iota(jnp.int32, (tq, 2 * LANE), 1)
        extra = jnp.where((pl.program_id(1) == 0) & (col < LANE), NEG_BIG, 0.0)
    for h in range(N_HEADS):
        j = h // group
        q = q_ref[0, h]
        s = jnp.concatenate([jnp.dot(q, kp_ref[0, j], preferred_element_type=F32),
                             jnp.dot(q, kc_ref[0, j], preferred_element_type=F32)], axis=1) + bias_ref[h]
        if extra is not None:
            s = s + extra
        sink = sink_ref[h]
        m = jnp.maximum(jnp.max(s, axis=1, keepdims=True), sink[:, :1])
        e = jnp.exp(s - m)
        p = e / (jnp.sum(e, axis=1, keepdims=True) + jnp.exp(sink[:, :1] - m))
        pb = p.astype(BF16)
        o_ref[0, :, h * HEAD_DIM:(h + 1) * HEAD_DIM] = (
            jnp.dot(pb[:, :LANE], vp_ref[0, j], preferred_element_type=F32)
            + jnp.dot(pb[:, LANE:], vc_ref[0, j], preferred_element_type=F32))


def _band_bias(rel_table):
    r = jnp.arange(LANE, dtype=jnp.int32)[:, None]
    s = jnp.arange(2 * LANE, dtype=jnp.int32)[None, :]
    wc, qh = s // CHUNK, r // CHUNK
    band = (wc >= qh) & (wc <= qh + 2)
    tiles = _rel_tiles(rel_table, (s - LANE - r)[None])[0]
    return jnp.where(band[None], tiles, NEG_BIG)


def _band_attention(q16, kp, kc, vp, vc, bias, sinks, *, tq, prev_map, cur_map, first_has_no_prev):
    bsz, t = q16.shape[0], q16.shape[2]
    kblk = lambda m: pl.BlockSpec((1, KV_HEADS_B, HEAD_DIM, LANE), lambda b, i: (b, 0, 0, m(i)))
    vblk = lambda m: pl.BlockSpec((1, KV_HEADS_B, LANE, HEAD_DIM), lambda b, i: (b, 0, m(i), 0))
    sink_rows = jnp.broadcast_to(sinks.astype(F32)[:, None, None], (N_HEADS, 1, LANE))
    return pl.pallas_call(
        functools.partial(_band_kernel, tq=tq, first_has_no_prev=first_has_no_prev),
        grid=(bsz, t // tq),
        in_specs=[pl.BlockSpec((1, N_HEADS, tq, HEAD_DIM), lambda b, i: (b, 0, i, 0)),
                  kblk(prev_map), kblk(cur_map), vblk(prev_map), vblk(cur_map),
                  pl.BlockSpec((N_HEADS, tq, 2 * LANE), lambda b, i: (0, 0, 0)),
                  pl.BlockSpec((N_HEADS, 1, LANE), lambda b, i: (0, 0, 0))],
        out_specs=pl.BlockSpec((1, tq, ATTN_WIDTH), lambda b, i: (b, i, 0)),
        out_shape=jax.ShapeDtypeStruct((bsz, t, ATTN_WIDTH), F32),
        compiler_params=_params(ARB2, vmem=False),
        name="band_attention",
    )(q16, kp, kc, vp, vc, bias[:, :tq, :], sink_rows)


def _mixer_residual(x_ref, o_ref, wo_ref, gt1_ref):
    return x_ref[0] + gt1_ref[0] * jnp.dot(o_ref[0].astype(BF16), wo_ref[...], preferred_element_type=F32)


def _swiglu_chunk(h, wg, wu, wd):
    a = jax.nn.silu(jnp.dot(h, wg, preferred_element_type=F32)) * jnp.dot(h, wu, preferred_element_type=F32)
    return jnp.dot(a.astype(BF16), wd, preferred_element_type=F32)


def _ffn_kernel(x_ref, o_ref, wo_ref, gt1_ref, g_ref, sh_ref, sc_ref, gt2_ref, wg_ref, wu_ref, wd_ref,
                out_ref, xn_scr, h_scr, acc_scr):
    c = pl.program_id(2)

    @pl.when(c == 0)
    def _():
        xn = _mixer_residual(x_ref, o_ref, wo_ref, gt1_ref)
        xn_scr[...] = xn
        h_scr[...] = _modulate(xn, g_ref[...], sh_ref[0], sc_ref[0]).astype(BF16)
        acc_scr[...] = jnp.zeros(acc_scr.shape, F32)

    acc_scr[...] += _swiglu_chunk(h_scr[...], wg_ref[...], wu_ref[...], wd_ref[...])

    @pl.when(c == pl.num_programs(2) - 1)
    def _():
        out_ref[0] = xn_scr[...] + gt2_ref[0] * acc_scr[...]


def _moe_pre_kernel(x_ref, o_ref, wo_ref, gt1_ref, g_ref, sh_ref, sc_ref, wr_ref, xn_ref, h_ref, gate_ref, sel_ref):
    xn = _mixer_residual(x_ref, o_ref, wo_ref, gt1_ref)
    xn_ref[0] = xn
    h = _modulate(xn, g_ref[...], sh_ref[0], sc_ref[0])
    h_ref[0] = h
    lane = lax.broadcasted_iota(jnp.int32, (h.shape[0], LANE), 1)
    lanef = lane.astype(F32)
    logits = jnp.dot(h.astype(BF16), wr_ref[...], preferred_element_type=F32)
    logits = jnp.where(lane < N_EXPERTS, logits, -jnp.inf)
    m1 = jnp.max(logits, axis=1, keepdims=True)
    i1 = jnp.min(jnp.where(logits == m1, lanef, float(LANE)), axis=1, keepdims=True)
    rest = jnp.where(lanef == i1, -jnp.inf, logits)
    m2 = jnp.max(rest, axis=1, keepdims=True)
    i2 = jnp.min(jnp.where(rest == m2, lanef, float(LANE)), axis=1, keepdims=True)
    e2 = jnp.exp(m2 - m1)
    den = 1.0 + e2
    gates = jnp.where(lanef == i1, 1.0 / den, 0.0) + jnp.where(lanef == i2, e2 / den, 0.0)
    sel = jnp.where(lanef == i1, 1.0, 0.0) + jnp.where(lanef == i2, 1.0, 0.0)
    gate_ref[0] = gates[:, :N_EXPERTS]
    sel_ref[0] = sel[:, :N_EXPERTS]


def _moe_expert_kernel(te_tab, tv_tab, src_hbm, dst_hbm, h_hbm, wg_ref, wu_ref, wd_ref, y_hbm,
                       src_smem, dst_smem, hbuf, hb, acc, sem_idx, sem_in, sem_out, *, rows):
    i, c = pl.program_id(0), pl.program_id(1)
    valid = tv_tab[i] != 0
    all_rows_in = pltpu.make_async_copy(h_hbm.at[pl.ds(0, rows), :], hbuf, sem_in)
    all_rows_out = pltpu.make_async_copy(acc, y_hbm.at[pl.ds(0, rows), :], sem_out)

    @pl.when(valid & (c == 0))
    def _():
        idx_copies = [pltpu.make_async_copy(src_hbm.at[i], src_smem, sem_idx.at[0]),
                      pltpu.make_async_copy(dst_hbm.at[i], dst_smem, sem_idx.at[1])]
        for cp in idx_copies:
            cp.start()
        for cp in idx_copies:
            cp.wait()

        def gather_row(r, carry):
            pltpu.make_async_copy(h_hbm.at[pl.ds(src_smem[r], 1), :], hbuf.at[pl.ds(r, 1), :], sem_in).start()
            return carry
        lax.fori_loop(0, rows, gather_row, 0, unroll=8)
        all_rows_in.wait()
        hb[...] = hbuf[...].astype(BF16)
        acc[...] = jnp.zeros(acc.shape, F32)

    @pl.when(valid)
    def _():
        acc[...] += _swiglu_chunk(hb[...], wg_ref[0], wu_ref[0], wd_ref[0])

    @pl.when(valid & (c == pl.num_programs(1) - 1))
    def _():
        def scatter_row(r, carry):
            pltpu.make_async_copy(acc.at[pl.ds(r, 1), :], y_hbm.at[pl.ds(dst_smem[r], 1), :], sem_out).start()
            return carry
        lax.fori_loop(0, rows, scatter_row, 0, unroll=8)
        all_rows_out.wait()


def _moe_combine_kernel(xn_ref, gt2_ref, g2_ref, ya_ref, yb_ref, out_ref):
    g2 = g2_ref[0]
    out_ref[0] = xn_ref[0] + gt2_ref[0] * (g2[:, 0:1] * ya_ref[...] + g2[:, 1:2] * yb_ref[...])


def _token_specs(tm, nd):
    pick = (lambda f: lambda b, i: f(b, i)) if nd == 2 else (lambda f: lambda b, i, c: f(b, i))
    tok = pl.BlockSpec((1, tm, D_MODEL), pick(lambda b, i: (b, i, 0)))

    def mod_spec(a):
        return pl.BlockSpec((1, 1, D_MODEL), pick(lambda b, i: (b, 0, 0))) if a.shape[1] == 1 else tok

    cst = lambda shape: pl.BlockSpec(shape, pick(lambda b, i: (0, 0)))
    return tok, mod_spec, cst


def _swiglu_mixer(x, o, w_out, gt1, g, shift, scale, gt2, w_gate, w_up, w_down, *, tm):
    gsz, t, _ = x.shape
    d_ff = w_gate.shape[-1]
    assert t % tm == 0 and d_ff % FF_TILE == 0
    tok, mod_spec, cst = _token_specs(tm, 3)
    return pl.pallas_call(
        _ffn_kernel, grid=(gsz, t // tm, d_ff // FF_TILE),
        in_specs=[tok, tok, cst((ATTN_WIDTH, D_MODEL)), mod_spec(gt1), cst((1, D_MODEL)),
                  mod_spec(shift), mod_spec(scale), mod_spec(gt2),
                  pl.BlockSpec((D_MODEL, FF_TILE), lambda b, i, c: (0, c)),
                  pl.BlockSpec((D_MODEL, FF_TILE), lambda b, i, c: (0, c)),
                  pl.BlockSpec((FF_TILE, D_MODEL), lambda b, i, c: (c, 0))],
        out_specs=tok,
        out_shape=jax.ShapeDtypeStruct((gsz, t, D_MODEL), F32),
        scratch_shapes=[pltpu.VMEM((tm, D_MODEL), F32), pltpu.VMEM((tm, D_MODEL), BF16),
                        pltpu.VMEM((tm, D_MODEL), F32)],
        compiler_params=_params(("arbitrary",) * 3),
        name="swiglu_channel_mixer",
    )(x, o, w_out.astype(BF16), gt1, g[None, :], shift, scale, gt2,
      w_gate.astype(BF16), w_up.astype(BF16), w_down.astype(BF16))


def _moe_pre(x, o, w_out, gt1, g, shift, scale, w_router, *, tm):
    gsz, t, _ = x.shape
    assert t % tm == 0
    tok, mod_spec, cst = _token_specs(tm, 2)
    small = pl.BlockSpec((1, tm, N_EXPERTS), lambda b, i: (b, i, 0))
    wr = jnp.pad(w_router.astype(BF16), ((0, 0), (0, LANE - N_EXPERTS)))
    return pl.pallas_call(
        _moe_pre_kernel, grid=(gsz, t // tm),
        in_specs=[tok, tok, cst((ATTN_WIDTH, D_MODEL)), mod_spec(gt1), cst((1, D_MODEL)),
                  mod_spec(shift), mod_spec(scale), cst((D_MODEL, LANE))],
        out_specs=[tok, tok, small, small],
        out_shape=[jax.ShapeDtypeStruct((gsz, t, D_MODEL), F32), jax.ShapeDtypeStruct((gsz, t, D_MODEL), F32),
                   jax.ShapeDtypeStruct((gsz, t, N_EXPERTS), F32), jax.ShapeDtypeStruct((gsz, t, N_EXPERTS), F32)],
        compiler_params=_params(ARB2),
        name="moe_pre_router",
    )(x, o, w_out.astype(BF16), gt1, g[None, :], shift, scale, wr)


def _moe_routing(gates, sel, rows):
    n = gates.shape[0]
    member = sel > 0
    mi = member.astype(jnp.int32)
    counts = mi.sum(0)
    padded = (counts + rows - 1) // rows * rows
    ends = jnp.cumsum(padded)
    slot = jnp.cumsum(mi, axis=1) - mi
    expert_of = [jnp.sum(jnp.where(member & (slot == k), jnp.arange(N_EXPERTS)[None, :], 0), axis=1)
                 for k in range(TOP_K)]
    g2 = jnp.stack([jnp.sum(jnp.where(member & (slot == k), gates, 0.0), axis=1) for k in range(TOP_K)], axis=1)
    n_rows = (2 * n + N_EXPERTS * rows) // rows * rows
    n_pad = n_rows - 2 * n
    pad_ends = jnp.cumsum(padded - counts)
    pad_expert = (jnp.arange(n_pad, dtype=jnp.int32)[:, None] >= pad_ends[None, :]).sum(1)
    tok = jnp.arange(n, dtype=jnp.int32)
    sort_key = jnp.concatenate([expert_of[0], expert_of[1], pad_expert]).astype(jnp.int32)
    pair = jnp.concatenate([tok, n + tok, jnp.full((n_pad,), -1, jnp.int32)])
    _, pair = lax.sort((sort_key, pair), num_keys=1, is_stable=True)
    spare = 2 * n + jnp.arange(n_rows, dtype=jnp.int32) % rows
    src = jnp.where(pair >= 0, pair % n, 0)
    dst = jnp.where(pair >= 0, pair, spare)
    n_tiles = n_rows // rows
    tile_start = jnp.arange(n_tiles, dtype=jnp.int32) * rows
    tile_expert = jnp.minimum((tile_start[:, None] >= ends[None, :]).sum(1), N_EXPERTS - 1).astype(jnp.int32)
    tile_valid = (tile_start < ends[-1]).astype(jnp.int32)
    return src.reshape(n_tiles, rows), dst.reshape(n_tiles, rows), tile_expert, tile_valid, g2


def _moe_experts(h_all, src, dst, tile_expert, tile_valid, w_gate, w_up, w_down, *, rows):
    n = h_all.shape[0]
    n_tiles = src.shape[0]
    d_ff = w_gate.shape[-1]
    any_spec = pl.BlockSpec(memory_space=pl.ANY)
    return pl.pallas_call(
        functools.partial(_moe_expert_kernel, rows=rows),
        grid_spec=pltpu.PrefetchScalarGridSpec(
            num_scalar_prefetch=2, grid=(n_tiles, d_ff // FF_TILE),
            in_specs=[any_spec, any_spec, any_spec,
                      pl.BlockSpec((1, D_MODEL, FF_TILE), lambda i, c, te, tv: (te[i], 0, c)),
                      pl.BlockSpec((1, D_MODEL, FF_TILE), lambda i, c, te, tv: (te[i], 0, c)),
                      pl.BlockSpec((1, FF_TILE, D_MODEL), lambda i, c, te, tv: (te[i], c, 0))],
            out_specs=any_spec,
            scratch_shapes=[pltpu.SMEM((rows,), jnp.int32), pltpu.SMEM((rows,), jnp.int32),
                            pltpu.VMEM((rows, D_MODEL), F32), pltpu.VMEM((rows, D_MODEL), BF16),
                            pltpu.VMEM((rows, D_MODEL), F32), pltpu.SemaphoreType.DMA((2,)),
                            pltpu.SemaphoreType.DMA(()), pltpu.SemaphoreType.DMA(())]),
        out_shape=jax.ShapeDtypeStruct((2 * n + rows, D_MODEL), F32),
        compiler_params=_params(ARB2),
        name="moe_routed_experts",
    )(tile_expert, tile_valid, src, dst, h_all, w_gate.astype(BF16), w_up.astype(BF16), w_down.astype(BF16))


def _moe_combine(xn, gt2, g2, y, *, tm, n_all, tok0):
    gsz, t, _ = xn.shape
    assert t % tm == 0 and n_all % tm == 0 and tok0 % tm == 0
    tok, mod_spec, _ = _token_specs(tm, 2)
    nt = t // tm
    yspec = lambda slot: pl.BlockSpec((tm, D_MODEL), lambda b, i: ((slot * n_all + tok0) // tm + b * nt + i, 0))
    return pl.pallas_call(
        _moe_combine_kernel, grid=(gsz, nt),
        in_specs=[tok, mod_spec(gt2), pl.BlockSpec((1, tm, TOP_K), lambda b, i: (b, i, 0)), yspec(0), yspec(1)],
        out_specs=tok,
        out_shape=jax.ShapeDtypeStruct((gsz, t, D_MODEL), F32),
        compiler_params=_params(ARB2),
        name="moe_combine",
    )(xn, gt2, g2, y, y)


def _per_token(a, n):
    return jnp.broadcast_to(a, (a.shape[0], n, a.shape[2])).reshape(1, a.shape[0] * n, a.shape[2])


def _pad_keys(a, axis, l_pad):
    pad = [(0, 0)] * a.ndim
    pad[axis] = (0, l_pad - a.shape[axis])
    return jnp.pad(a, pad)


def _layer_a(xp, xs, cp, cs, ck, cv, cki, rel_bias, norm_g, w_ada, b_ada, w_in, w_out, g_q, g_k, g_kidx,
             ffn_wg, ffn_wu, ffn_wd):
    bias = _dsa_bias_tiles(rel_bias)
    nb, n = xs.shape[0], xs.shape[1]
    past = ck.shape[1]
    outs = []
    for x, c, sample in ((xp, cp, False), (xs, cs, True)):
        sh1, sc1, gt1, sh2, sc2, gt2 = adaln(c, w_ada, b_ada)
        q16, kf, vf, ktr, vx, kif, qi16, kit, wsc = _project(
            x, norm_g[0], sh1, sc1, w_in, g_q, g_k, g_kidx, n_kv=KV_HEADS_A, has_idx=True,
            tm=n if sample else 256)
        if sample:
            l_true, l_pad = past + n, 2304
            kfull = jnp.concatenate([ck, kf.reshape(nb, n, KV_HEADS_A, HEAD_DIM)], axis=1).astype(BF16)
            vfull = jnp.concatenate([cv, vf.reshape(nb, n, KV_HEADS_A, HEAD_DIM)], axis=1).astype(BF16)
            kifull = jnp.concatenate([cki, kif], axis=1).astype(BF16)
            ktr = _pad_keys(kfull, 1, l_pad).transpose(0, 2, 3, 1)
            vx = jnp.concatenate([vfull, jnp.ones(vfull.shape[:3] + (1,), BF16),
                                  jnp.zeros(vfull.shape[:3] + (LANE - HEAD_DIM - 1,), BF16)], axis=-1)
            vx = _pad_keys(vx, 1, l_pad).transpose(0, 2, 1, 3)
            kit = _pad_keys(kifull, 1, l_pad).transpose(0, 2, 1)
            o = _dsa_attention(q16, ktr, vx, qi16, kit, wsc, bias, p0=past, l_true=l_true,
                               n_sel=min(TOPK_MAX, l_true // 4), tq=n, tk=l_pad, sw=256, cw=256)
            flat = lambda a: a.reshape(1, nb * n, a.shape[-1])
            y = _swiglu_mixer(flat(x), flat(o), w_out, _per_token(gt1, n), norm_g[1], _per_token(sh2, n),
                              _per_token(sc2, n), _per_token(gt2, n), ffn_wg, ffn_wu, ffn_wd, tm=MIX_TM_SAMPLE)
            y = y.reshape(x.shape)
        else:
            t = x.shape[1]
            o = _dsa_attention(q16, ktr, vx, qi16, kit, wsc, bias, p0=0, l_true=t,
                               n_sel=min(TOPK_MAX, t // 4), tq=LANE, tk=1024, sw=256, cw=1024)
            y = _swiglu_mixer(x, o, w_out, gt1, norm_g[1], sh2, sc2, gt2, ffn_wg, ffn_wu, ffn_wd, tm=MIX_TM_PROMPT)
        bsz, t = x.shape[0], x.shape[1]
        outs.append((y, kf.reshape(bsz, t, KV_HEADS_A, HEAD_DIM), vf.reshape(bsz, t, KV_HEADS_A, HEAD_DIM), kif))
    return outs


def _layer_b(xp, xs, cp, cs, ck, cv, rel_bias, norm_g, w_ada, b_ada, w_in, w_out, g_q, g_k, sinks,
             w_router, moe_wg, moe_wu, moe_wd):
    bias = _band_bias(rel_bias)
    nb, n = xs.shape[0], xs.shape[1]
    streams = []
    for x, c, sample in ((xp, cp, False), (xs, cs, True)):
        sh1, sc1, gt1, sh2, sc2, gt2 = adaln(c, w_ada, b_ada)
        q16, kf, vf, ktr, v16 = _project(x, norm_g[0], sh1, sc1, w_in, g_q, g_k, g_k, n_kv=KV_HEADS_B,
                                         has_idx=False, tm=n if sample else 256)
        bsz, t = x.shape[0], x.shape[1]
        k4 = kf.reshape(bsz, t, KV_HEADS_B, HEAD_DIM)
        v4 = vf.reshape(bsz, t, KV_HEADS_B, HEAD_DIM)
        zero = lambda i: 0
        if sample:
            kp = ck.astype(BF16).transpose(0, 2, 3, 1)
            vp = cv.astype(BF16).transpose(0, 2, 1, 3)
            o = _band_attention(q16, kp, _pad_keys(ktr, 3, LANE), vp, _pad_keys(v16, 2, LANE), bias, sinks,
                                tq=n, prev_map=zero, cur_map=zero, first_has_no_prev=False)
            flat = lambda a: a.reshape(1, nb * n, a.shape[-1])
            xn, h, gates, sel = _moe_pre(flat(x), flat(o), w_out, _per_token(gt1, n), norm_g[1],
                                         _per_token(sh2, n), _per_token(sc2, n), w_router, tm=MIX_TM_SAMPLE)
            gt2 = _per_token(gt2, n)
            k_new = jnp.concatenate([ck, k4], axis=1)[:, -WINDOW:]
            v_new = jnp.concatenate([cv, v4], axis=1)[:, -WINDOW:]
        else:
            o = _band_attention(q16, ktr, ktr, v16, v16, bias, sinks, tq=LANE,
                                prev_map=lambda i: jnp.maximum(i - 1, 0), cur_map=lambda i: i,
                                first_has_no_prev=True)
            xn, h, gates, sel = _moe_pre(x, o, w_out, gt1, norm_g[1], sh2, sc2, w_router, tm=MIX_TM_PROMPT)
            k_new, v_new = k4[:, -WINDOW:], v4[:, -WINDOW:]
        streams.append((x.shape, xn, h, gates, sel, gt2, k_new, v_new))

    rows2d = lambda a: a.reshape(-1, a.shape[-1])
    h_all = jnp.concatenate([rows2d(st[2]) for st in streams])
    n_all = h_all.shape[0]
    src, dst, tile_expert, tile_valid, g2 = _moe_routing(
        jnp.concatenate([rows2d(st[3]) for st in streams]), jnp.concatenate([rows2d(st[4]) for st in streams]),
        MOE_ROW_TILE)
    y = _moe_experts(h_all, src, dst, tile_expert, tile_valid, moe_wg, moe_wu, moe_wd, rows=MOE_ROW_TILE)
    outs, tok0 = [], 0
    for shape, xn, _, _, _, gt2, k_new, v_new in streams:
        cnt = xn.shape[0] * xn.shape[1]
        out = _moe_combine(xn, gt2, g2[tok0:tok0 + cnt].reshape(xn.shape[0], xn.shape[1], TOP_K), y,
                           tm=MOE_COMBINE_TM, n_all=n_all, tok0=tok0)
        outs.append((out.reshape(shape), k_new, v_new))
        tok0 += cnt
    return outs


def kernel(x_prompt, x_sample, c_prompt, c_sample, cache_a_k, cache_a_v, cache_a_kidx,
           cache_b_k, cache_b_v, rel_bias, norm_g, w_ada, b_ada,
           a_w_in, a_w_out, a_g_q, a_g_k, a_g_kidx,
           b_w_in, b_w_out, b_g_q, b_g_k, b_sinks,
           ffn_w_gate, ffn_w_up, ffn_w_down,
           moe_w_router, moe_w_gate, moe_w_up, moe_w_down):
    xp, xs = x_prompt, x_sample
    a_out = [[] for _ in range(6)]
    b_out = [[] for _ in range(4)]
    for i in range(DEPTH):
        j = i // 2
        if i % 2 == 0:
            (xp, k1, v1, i1), (xs, k2, v2, i2) = _layer_a(
                xp, xs, c_prompt, c_sample, cache_a_k[j], cache_a_v[j], cache_a_kidx[j], rel_bias, norm_g[i],
                w_ada[i], b_ada[i], a_w_in[j], a_w_out[j], a_g_q[j], a_g_k[j], a_g_kidx[j],
                ffn_w_gate[j], ffn_w_up[j], ffn_w_down[j])
            for lst, val in zip(a_out, (k1, v1, i1, k2, v2, i2)):
                lst.append(val)
        else:
            (xp, k1, v1), (xs, k2, v2) = _layer_b(
                xp, xs, c_prompt, c_sample, cache_b_k[j], cache_b_v[j], rel_bias, norm_g[i],
                w_ada[i], b_ada[i], b_w_in[j], b_w_out[j], b_g_q[j], b_g_k[j], b_sinks[j],
                moe_w_router[j], moe_w_gate[j], moe_w_up[j], moe_w_down[j])
            for lst, val in zip(b_out, (k1, v1, k2, v2)):
                lst.append(val)
    return (xp, xs, *(jnp.stack(l) for l in a_out), *(jnp.stack(l) for l in b_out))
```

```python
import math, functools
import jax, jax.numpy as jnp
from jax import lax
import numpy as np
from jax.experimental import pallas as pl
from jax.experimental.pallas import tpu as pltpu

D_MODEL = 1024
DEPTH = 2
CHUNK = 64
N_HEADS = 16
HEAD_DIM = 64
ATTN_WIDTH = N_HEADS * HEAD_DIM
KV_HEADS_A = 4
IDX_HEADS = 8
IDX_DIM = 64
TOPK_MAX = 256
KV_HEADS_B = 2
WINDOW = 128
NUM_BUCKETS = 32
MAX_DISTANCE = 1024
N_EXPERTS = 8
TOP_K = 2
EPS = 1e-6

F32 = jnp.float32
BF16 = jnp.bfloat16
LANE = 128
VMEM_LIMIT_BYTES = 48 * 1024 * 1024
LOG2E = math.log2(math.e)
NEG_BIG = -1e30
KEY_MIN = -2 ** 31
FAR_BUCKET = NUM_BUCKETS // 2 - 1
NEAR_BLOCKS = 6
FF_TILE = 512
MIX_TM_PROMPT = 1024
MIX_TM_SAMPLE = 256
MOE_ROW_TILE = 512
MOE_COMBINE_TM = 256
DMA_BURST = 8
ARB2 = ("arbitrary", "arbitrary")


def _params(sem, vmem=True):
    return pltpu.CompilerParams(dimension_semantics=sem, vmem_limit_bytes=VMEM_LIMIT_BYTES if vmem else None)


def rel_bucket(rel):
    half = NUM_BUCKETS // 2
    max_exact = half // 2
    base = jnp.where(rel > 0, half, 0)
    n = jnp.abs(rel)
    nf = jnp.maximum(n, 1).astype(F32)
    large = max_exact + (jnp.log(nf / max_exact) / math.log(MAX_DISTANCE / max_exact)
                         * (half - max_exact)).astype(jnp.int32)
    large = jnp.minimum(large, half - 1)
    return base + jnp.where(n < max_exact, n, large)


def _adaln_kernel(c_ref, w_ref, b_ref, o_ref):
    c = c_ref[...]
    o_ref[...] = jnp.dot(c * jax.nn.sigmoid(c), w_ref[...], preferred_element_type=F32) + b_ref[...]


def adaln(c, w, b):
    n = c.shape[0]
    mod = pl.pallas_call(
        _adaln_kernel,
        grid=(6,),
        in_specs=[pl.BlockSpec((n, D_MODEL), lambda j: (0, 0)),
                  pl.BlockSpec((D_MODEL, D_MODEL), lambda j: (0, j)),
                  pl.BlockSpec((1, D_MODEL), lambda j: (0, j))],
        out_specs=pl.BlockSpec((n, D_MODEL), lambda j: (0, j)),
        out_shape=jax.ShapeDtypeStruct((n, 6 * D_MODEL), F32),
        name="adaln",
    )(c, w, b[None, :])
    return jnp.split(mod[:, None, :], 6, axis=-1)


def _modulate(x, g, shift, scale):
    y = x * lax.rsqrt(jnp.mean(x * x, axis=-1, keepdims=True) + EPS)
    return (y * g) * (1.0 + scale) + shift


def _seg_norm(seg, g):
    return seg * lax.rsqrt(jnp.mean(seg * seg, axis=-1, keepdims=True) + EPS) * g


def _proj_kernel(x_ref, g_ref, sh_ref, sc_ref, w_ref, gq_ref, gk_ref, gki_ref, *outs, n_kv, has_idx):
    h = _modulate(x_ref[0], g_ref[...], sh_ref[0], sc_ref[0]).astype(BF16)
    y = jnp.dot(h, w_ref[...], preferred_element_type=F32)
    tm = y.shape[0]
    kv_w = n_kv * HEAD_DIM
    if has_idx:
        q16, kf, vf, ktr, vx, kif, qi16, kit, wsc = outs
    else:
        q16, kf, vf, ktr, v16 = outs
    for hd in range(N_HEADS):
        seg = y[:, hd * HEAD_DIM:(hd + 1) * HEAD_DIM]
        q_scale = HEAD_DIM ** -0.5 * (LOG2E if has_idx else 1.0)
        q16[0, hd] = (_seg_norm(seg, gq_ref[...]) * q_scale).astype(BF16)
    lane = lax.broadcasted_iota(jnp.int32, (tm, LANE - HEAD_DIM), 1)
    ones_col = jnp.where(lane == 0, 1.0, 0.0).astype(BF16)
    for j in range(n_kv):
        kseg = _seg_norm(y[:, ATTN_WIDTH + j * HEAD_DIM:ATTN_WIDTH + (j + 1) * HEAD_DIM], gk_ref[...])
        vseg = y[:, ATTN_WIDTH + kv_w + j * HEAD_DIM:ATTN_WIDTH + kv_w + (j + 1) * HEAD_DIM]
        kf[0, :, j * HEAD_DIM:(j + 1) * HEAD_DIM] = kseg
        vf[0, :, j * HEAD_DIM:(j + 1) * HEAD_DIM] = vseg
        ktr[0, j] = kseg.T.astype(BF16)
        if has_idx:
            vx[0, j] = jnp.concatenate([vseg.astype(BF16), ones_col], axis=1)
        else:
            v16[0, j] = vseg.astype(BF16)
    if has_idx:
        base = ATTN_WIDTH + 2 * kv_w
        for hd in range(IDX_HEADS):
            qi16[0, hd] = y[:, base + hd * IDX_DIM:base + (hd + 1) * IDX_DIM].astype(BF16)
        base += IDX_HEADS * IDX_DIM
        kiseg = _seg_norm(y[:, base:base + IDX_DIM], gki_ref[...])
        kif[0] = kiseg
        kit[0] = kiseg.T.astype(BF16)
        wsc[0] = y[:, base + IDX_DIM:base + IDX_DIM + IDX_HEADS] * IDX_HEADS ** -0.5 * IDX_DIM ** -0.5


def _project(x, g, shift, scale, w_in, g_q, g_k, g_kidx, *, n_kv, has_idx, tm):
    gsz, t, _ = x.shape
    n_in = w_in.shape[1]
    kv_w = n_kv * HEAD_DIM
    tok = lambda b, i: (b, i, 0)
    head = lambda b, i: (b, 0, i, 0)
    vec = lambda b, i: (b, 0, 0)
    cst = lambda b, i: (0, 0)
    shapes = [((gsz, N_HEADS, t, HEAD_DIM), BF16, (1, N_HEADS, tm, HEAD_DIM), head),
              ((gsz, t, kv_w), F32, (1, tm, kv_w), tok),
              ((gsz, t, kv_w), F32, (1, tm, kv_w), tok),
              ((gsz, n_kv, HEAD_DIM, t), BF16, (1, n_kv, HEAD_DIM, tm), lambda b, i: (b, 0, 0, i))]
    if has_idx:
        shapes += [((gsz, n_kv, t, LANE), BF16, (1, n_kv, tm, LANE), head),
                   ((gsz, t, IDX_DIM), F32, (1, tm, IDX_DIM), tok),
                   ((gsz, IDX_HEADS, t, IDX_DIM), BF16, (1, IDX_HEADS, tm, IDX_DIM), head),
                   ((gsz, IDX_DIM, t), BF16, (1, IDX_DIM, tm), lambda b, i: (b, 0, i)),
                   ((gsz, t, IDX_HEADS), F32, (1, tm, IDX_HEADS), tok)]
    else:
        shapes += [((gsz, n_kv, t, HEAD_DIM), BF16, (1, n_kv, tm, HEAD_DIM), head)]
    return pl.pallas_call(
        functools.partial(_proj_kernel, n_kv=n_kv, has_idx=has_idx),
        grid=(gsz, t // tm),
        in_specs=[pl.BlockSpec((1, tm, D_MODEL), tok), pl.BlockSpec((1, D_MODEL), cst),
                  pl.BlockSpec((1, 1, D_MODEL), vec), pl.BlockSpec((1, 1, D_MODEL), vec),
                  pl.BlockSpec((D_MODEL, n_in), cst), pl.BlockSpec((1, HEAD_DIM), cst),
                  pl.BlockSpec((1, HEAD_DIM), cst), pl.BlockSpec((1, IDX_DIM), cst)],
        out_specs=[pl.BlockSpec(blk, im) for _, _, blk, im in shapes],
        out_shape=[jax.ShapeDtypeStruct(s, d) for s, d, _, _ in shapes],
        compiler_params=_params(ARB2),
        name="mixer_in_proj",
    )(x, g[None, :], shift, scale, w_in.astype(BF16), g_q[None, :], g_k[None, :], g_kidx[None, :])


def _tri_steps(nqb, tq, tk, p0):
    qbs, kts = [], []
    for qb in range(nqb):
        vis_end = ((p0 + qb * tq + tq - 1) // CHUNK + 1) * CHUNK
        for kt in range(-(-vis_end // tk)):
            qbs.append(qb)
            kts.append(kt)
    return jnp.asarray(np.array(qbs, np.int32)), jnp.asarray(np.array(kts, np.int32))


def _idx_kernel(qb_tab, kt_tab, qi_ref, w_ref, kit_ref, out_ref, *, tq, tk, sw, p0, l_true):
    step = pl.program_id(1)
    q0 = p0 + qb_tab[step] * tq
    k0 = kt_tab[step] * tk
    row = lax.broadcasted_iota(jnp.int32, (tq, 1), 0) + q0
    vis_end = jnp.minimum(((row >> 6) + 1) << 6, l_true)
    wv = w_ref[0]
    for c in range(tk // sw):
        kit = kit_ref[0, :, c * sw:(c + 1) * sw]
        acc = jnp.zeros((tq, sw), F32)
        for h in range(IDX_HEADS):
            sh = jnp.dot(qi_ref[0, h], kit, preferred_element_type=F32)
            acc = acc + wv[:, h:h + 1] * jnp.maximum(sh, 0.0)
        bits = lax.bitcast_convert_type(acc, jnp.int32)
        key = bits ^ ((bits >> 31) & 0x7FFFFFFF)
        kpos = k0 + c * sw + lax.broadcasted_iota(jnp.int32, (tq, sw), 1)
        out_ref[0, :, c * sw:(c + 1) * sw] = jnp.where(kpos < vis_end, key, KEY_MIN)


def _thr_kernel(keys_ref, tau_ref, quota_ref, flag_ref, *, tq, cw, p0, n_sel):
    qb = pl.program_id(1)
    vis_end = (((p0 + qb * tq + tq - 1) >> 6) + 1) << 6
    nch = (vis_end + cw - 1) // cw
    key_max = 2 ** 31 - 1

    def count_ge(cand):
        def body(j, acc):
            x = keys_ref[0, :, pl.ds(pl.multiple_of(j * cw, cw), cw)]
            for u in range(cw // LANE):
                acc = acc + jnp.where(x[:, u * LANE:(u + 1) * LANE] >= cand, 1.0, 0.0)
            return acc
        acc = lax.fori_loop(0, nch, body, jnp.zeros((tq, LANE), F32))
        return jnp.sum(acc, axis=1, keepdims=True)

    zero = jnp.zeros((tq, LANE), jnp.int32)
    c0 = count_ge(zero)
    tau = jnp.where(c0 >= n_sel, zero, KEY_MIN)
    n_ge = jnp.where(c0 >= n_sel, c0, float(2 ** 24))

    def bit_body(carry):
        i, tau, n_ge = carry
        cand = tau + jnp.left_shift(jnp.int32(1), 30 - i)
        c = count_ge(cand)
        return i + 1, jnp.where(c >= n_sel, cand, tau), jnp.where(c >= n_sel, c, n_ge)

    def unresolved(carry):
        i, _, n_ge = carry
        return (i < 31) & (jnp.max(jnp.abs(n_ge - n_sel)) > 0.0)

    _, tau, n_ge = lax.while_loop(unresolved, bit_body, (jnp.int32(0), tau, n_ge))
    n_gt = jnp.where(tau[:, :1] < key_max, count_ge(jnp.minimum(tau, key_max - 1) + 1), 0.0)
    tau_ref[0] = jnp.maximum(tau, KEY_MIN + 1)
    quota_ref[0] = jnp.broadcast_to(n_sel - n_gt, (tq, LANE))
    over = jnp.max(jnp.where(n_ge > n_sel, 1, 0), axis=0, keepdims=True)
    flag_ref[0, 0] = jnp.broadcast_to(over, flag_ref.shape[2:])


def _attn_kernel(qb_tab, kt_tab, flag_tab, q_ref, kt_ref, vx_ref, keys_ref, tau_ref, quota_ref, bias_ref, o_ref,
                 m_ref, acc_ref, negm_ref, seen_ref, *, tq, tk, sw, p0, nqb):
    step = pl.program_id(1)
    qb, kt = qb_tab[step], kt_tab[step]
    q0 = p0 + qb * tq
    k0 = kt * tk
    vis_end = (((q0 + tq - 1) >> 6) + 1) << 6
    group = N_HEADS // KV_HEADS_A
    nblk = sw // LANE
    tie_w = 2 * LANE

    @pl.when(kt == 0)
    def _():
        m_ref[...] = jnp.full(m_ref.shape, NEG_BIG, F32)
        acc_ref[...] = jnp.zeros(acc_ref.shape, F32)
        seen_ref[...] = jnp.zeros(seen_ref.shape, F32)

    has_ties = flag_tab[pl.program_id(0) * nqb + qb] != 0
    tau = tau_ref[0]

    @pl.when(jnp.logical_not(has_ties))
    def _():
        for u in range(tk // LANE):
            cols = slice(u * LANE, (u + 1) * LANE)
            negm_ref[:, cols] = jnp.where(keys_ref[0, :, cols] >= tau, 0.0, NEG_BIG)

    @pl.when(has_ties)
    def _():
        tau2 = jnp.concatenate([tau] * (tie_w // LANE), axis=1)
        quota = jnp.concatenate([quota_ref[0]] * (tie_w // LANE), axis=1)
        before = (lax.broadcasted_iota(jnp.int32, (tie_w, tie_w), 0)
                  < lax.broadcasted_iota(jnp.int32, (tie_w, tie_w), 1))
        before = jnp.where(before, 1.0, 0.0).astype(BF16)
        seen = seen_ref[...]
        for u in range(tk // tie_w):
            cols = slice(u * tie_w, (u + 1) * tie_w)
            k = keys_ref[0, :, cols]
            tie = jnp.where(k == tau2, 1.0, 0.0)
            rank = (jnp.dot(tie.astype(BF16), before, preferred_element_type=F32)
                    + jnp.concatenate([seen] * (tie_w // LANE), axis=1))
            keep = jnp.where(k > tau2, 1.0, jnp.where(rank < quota, tie, 0.0))
            negm_ref[:, cols] = jnp.where(keep > 0.0, 0.0, NEG_BIG)
            seen = seen + jnp.sum(tie, axis=1, keepdims=True)
        seen_ref[...] = seen

    def sub_tile(c, with_bias):
        off = c * sw if isinstance(c, int) else pl.multiple_of(c * sw, sw)
        negm = negm_ref[:, pl.ds(off, sw)]
        if with_bias:
            d0 = ((k0 + off - q0) >> 7) + NEAR_BLOCKS
            bidx = [jnp.clip(d0 + u, 0, NEAR_BLOCKS) for u in range(nblk)]
        for h in range(N_HEADS):
            j = h // group
            s = jnp.dot(q_ref[0, h], kt_ref[0, j, :, pl.ds(off, sw)], preferred_element_type=F32) + negm
            if with_bias:
                s = s + jnp.concatenate([bias_ref[bidx[u], h] for u in range(nblk)], axis=1)
            m_prev = m_ref[h]
            m_cur = jnp.maximum(m_prev, jnp.max(s, axis=1, keepdims=True))
            alpha = jnp.exp2(m_prev - m_cur)
            p = jnp.exp2(s - jnp.concatenate([m_cur] * nblk, axis=1))
            pv = jnp.dot(p.astype(BF16), vx_ref[0, j, pl.ds(off, sw), :], preferred_element_type=F32)
            acc_ref[h] = alpha * acc_ref[h] + pv
            m_ref[h] = m_cur

    tile_far = k0 + tk <= q0 - (NEAR_BLOCKS - 1) * LANE

    @pl.when(tile_far)
    def _():
        for c in range(tk // sw):
            sub_tile(c, False)

    @pl.when(jnp.logical_not(tile_far))
    def _():
        nsub = jnp.minimum(tk // sw, (vis_end - k0 + sw - 1) // sw)
        nfar = jnp.clip((q0 - (NEAR_BLOCKS - 1) * LANE - k0) // sw, 0, nsub)

        def far_body(c, carry):
            sub_tile(c, False)
            return carry

        def near_body(c, carry):
            sub_tile(c, True)
            return carry

        lax.fori_loop(0, nfar, far_body, 0)
        lax.fori_loop(nfar, nsub, near_body, 0)

    @pl.when(k0 + tk >= vis_end)
    def _():
        for h in range(N_HEADS):
            a = acc_ref[h]
            o_ref[0, :, h * HEAD_DIM:(h + 1) * HEAD_DIM] = a[:, :HEAD_DIM] / a[:, HEAD_DIM:HEAD_DIM + 1]


def _rel_tiles(rel_table, rel):
    onehot = jax.nn.one_hot(rel_bucket(rel), NUM_BUCKETS, dtype=F32)
    return jnp.einsum("nrsk,kh->nhrs", onehot, rel_table.astype(F32), precision=lax.Precision.HIGHEST)


def _dsa_bias_tiles(rel_table):
    d = jnp.arange(NEAR_BLOCKS + 1, dtype=jnp.int32)[:, None, None] - NEAR_BLOCKS
    rel = (d * LANE + jnp.arange(LANE, dtype=jnp.int32)[None, None, :]
           - jnp.arange(LANE, dtype=jnp.int32)[None, :, None])
    tiles = (_rel_tiles(rel_table, rel) - rel_table[FAR_BUCKET][None, :, None, None]) * LOG2E
    return tiles.at[0].set(0.0)


def _dsa_attention(q16, ktr, vx, qi16, kit, w, bias, *, p0, l_true, n_sel, tq, tk, sw, cw):
    bsz, t = q16.shape[0], q16.shape[2]
    l_pad = ktr.shape[3]
    nqb = t // tq
    assert t % tq == 0 and p0 % LANE == 0 and (tq == LANE or nqb == 1)
    assert l_pad % tk == 0 and tk % sw == 0 and tk % cw == 0 and tk % (2 * LANE) == 0
    qbt, ktt = _tri_steps(nqb, tq, tk, p0)
    nsteps = int(qbt.shape[0])

    keys = pl.pallas_call(
        functools.partial(_idx_kernel, tq=tq, tk=tk, sw=sw, p0=p0, l_true=l_true),
        grid_spec=pltpu.PrefetchScalarGridSpec(
            num_scalar_prefetch=2, grid=(bsz, nsteps),
            in_specs=[pl.BlockSpec((1, IDX_HEADS, tq, IDX_DIM), lambda b, s, qt, kt: (b, 0, qt[s], 0)),
                      pl.BlockSpec((1, tq, IDX_HEADS), lambda b, s, qt, kt: (b, qt[s], 0)),
                      pl.BlockSpec((1, IDX_DIM, tk), lambda b, s, qt, kt: (b, 0, kt[s]))],
            out_specs=pl.BlockSpec((1, tq, tk), lambda b, s, qt, kt: (b, qt[s], kt[s]))),
        out_shape=jax.ShapeDtypeStruct((bsz, t, l_pad), jnp.int32),
        compiler_params=_params(ARB2, vmem=False),
        name="dsa_index_keys",
    )(qbt, ktt, qi16, w, kit)

    tau, quota, flags = pl.pallas_call(
        functools.partial(_thr_kernel, tq=tq, cw=cw, p0=p0, n_sel=n_sel),
        grid=(bsz, nqb),
        in_specs=[pl.BlockSpec((1, tq, l_pad), lambda b, i: (b, i, 0))],
        out_specs=[pl.BlockSpec((1, tq, LANE), lambda b, i: (b, i, 0)),
                   pl.BlockSpec((1, tq, LANE), lambda b, i: (b, i, 0)),
                   pl.BlockSpec((1, 1, 8, LANE), lambda b, i: (b, i, 0, 0))],
        out_shape=[jax.ShapeDtypeStruct((bsz, t, LANE), jnp.int32),
                   jax.ShapeDtypeStruct((bsz, t, LANE), F32),
                   jax.ShapeDtypeStruct((bsz, nqb, 8, LANE), jnp.int32)],
        compiler_params=_params(ARB2),
        name="dsa_threshold",
    )(keys)
    flags = flags[:, :, 0, 0].reshape(bsz * nqb)

    return pl.pallas_call(
        functools.partial(_attn_kernel, tq=tq, tk=tk, sw=sw, p0=p0, nqb=nqb),
        grid_spec=pltpu.PrefetchScalarGridSpec(
            num_scalar_prefetch=3, grid=(bsz, nsteps),
            in_specs=[pl.BlockSpec((1, N_HEADS, tq, HEAD_DIM), lambda b, s, qt, kt, fl: (b, 0, qt[s], 0)),
                      pl.BlockSpec((1, KV_HEADS_A, HEAD_DIM, tk), lambda b, s, qt, kt, fl: (b, 0, 0, kt[s])),
                      pl.BlockSpec((1, KV_HEADS_A, tk, LANE), lambda b, s, qt, kt, fl: (b, 0, kt[s], 0)),
                      pl.BlockSpec((1, tq, tk), lambda b, s, qt, kt, fl: (b, qt[s], kt[s])),
                      pl.BlockSpec((1, tq, LANE), lambda b, s, qt, kt, fl: (b, qt[s], 0)),
                      pl.BlockSpec((1, tq, LANE), lambda b, s, qt, kt, fl: (b, qt[s], 0)),
                      pl.BlockSpec((NEAR_BLOCKS + 1, N_HEADS, tq, LANE), lambda b, s, qt, kt, fl: (0, 0, 0, 0))],
            out_specs=pl.BlockSpec((1, tq, ATTN_WIDTH), lambda b, s, qt, kt, fl: (b, qt[s], 0)),
            scratch_shapes=[pltpu.VMEM((N_HEADS, tq, LANE), F32), pltpu.VMEM((N_HEADS, tq, LANE), F32),
                            pltpu.VMEM((tq, tk), F32), pltpu.VMEM((tq, LANE), F32)]),
        out_shape=jax.ShapeDtypeStruct((bsz, t, ATTN_WIDTH), F32),
        compiler_params=_params(ARB2),
        name="dsa_masked_attention",
    )(qbt, ktt, flags, q16, ktr, vx, keys, tau, quota, bias[:, :, :tq, :])


def _band_kernel(q_ref, kp_ref, kc_ref, vp_ref, vc_ref, bias_ref, sink_ref, o_ref, *, tq, first_has_no_prev):
    group = N_HEADS // KV_HEADS_B
    extra = None
    if first_has_no_prev:
        col = lax.broadcasted_iota(jnp.int32, (tq, 2 * LANE), 1)
        extra = jnp.where((pl.program_id(1) == 0) & (col < LANE), NEG_BIG, 0.0)
    for h in range(N_HEADS):
        j = h // group
        q = q_ref[0, h]
        s = jnp.concatenate([jnp.dot(q, kp_ref[0, j], preferred_element_type=F32),
                             jnp.dot(q, kc_ref[0, j], preferred_element_type=F32)], axis=1) + bias_ref[h]
        if extra is not None:
            s = s + extra
        sink = sink_ref[h]
        m = jnp.maximum(jnp.max(s, axis=1, keepdims=True), sink[:, :1])
        e = jnp.exp(s - m)
        p = e / (jnp.sum(e, axis=1, keepdims=True) + jnp.exp(sink[:, :1] - m))
        pb = p.astype(BF16)
        o_ref[0, :, h * HEAD_DIM:(h + 1) * HEAD_DIM] = (
            jnp.dot(pb[:, :LANE], vp_ref[0, j], preferred_element_type=F32)
            + jnp.dot(pb[:, LANE:], vc_ref[0, j], preferred_element_type=F32))


def _band_bias(rel_table):
    r = jnp.arange(LANE, dtype=jnp.int32)[:, None]
    s = jnp.arange(2 * LANE, dtype=jnp.int32)[None, :]
    wc, qh = s // CHUNK, r // CHUNK
    band = (wc >= qh) & (wc <= qh + 2)
    tiles = _rel_tiles(rel_table, (s - LANE - r)[None])[0]
    return jnp.where(band[None], tiles, NEG_BIG)


def _band_attention(q16, kp, kc, vp, vc, bias, sinks, *, tq, prev_map, cur_map, first_has_no_prev):
    bsz, t = q16.shape[0], q16.shape[2]
    kblk = lambda m: pl.BlockSpec((1, KV_HEADS_B, HEAD_DIM, LANE), lambda b, i: (b, 0, 0, m(i)))
    vblk = lambda m: pl.BlockSpec((1, KV_HEADS_B, LANE, HEAD_DIM), lambda b, i: (b, 0, m(i), 0))
    sink_rows = jnp.broadcast_to(sinks.astype(F32)[:, None, None], (N_HEADS, 1, LANE))
    return pl.pallas_call(
        functools.partial(_band_kernel, tq=tq, first_has_no_prev=first_has_no_prev),
        grid=(bsz, t // tq),
        in_specs=[pl.BlockSpec((1, N_HEADS, tq, HEAD_DIM), lambda b, i: (b, 0, i, 0)),
                  kblk(prev_map), kblk(cur_map), vblk(prev_map), vblk(cur_map),
                  pl.BlockSpec((N_HEADS, tq, 2 * LANE), lambda b, i: (0, 0, 0)),
                  pl.BlockSpec((N_HEADS, 1, LANE), lambda b, i: (0, 0, 0))],
        out_specs=pl.BlockSpec((1, tq, ATTN_WIDTH), lambda b, i: (b, i, 0)),
        out_shape=jax.ShapeDtypeStruct((bsz, t, ATTN_WIDTH), F32),
        compiler_params=_params(ARB2, vmem=False),
        name="band_attention",
    )(q16, kp, kc, vp, vc, bias[:, :tq, :], sink_rows)


def _mixer_residual(x_ref, o_ref, wo_ref, gt1_ref):
    return x_ref[0] + gt1_ref[0] * jnp.dot(o_ref[0].astype(BF16), wo_ref[...], preferred_element_type=F32)


def _swiglu_chunk(h, wg, wu, wd):
    a = jax.nn.silu(jnp.dot(h, wg, preferred_element_type=F32)) * jnp.dot(h, wu, preferred_element_type=F32)
    return jnp.dot(a.astype(BF16), wd, preferred_element_type=F32)


def _ffn_kernel(x_ref, o_ref, wo_ref, gt1_ref, g_ref, sh_ref, sc_ref, gt2_ref, wg_ref, wu_ref, wd_ref,
                out_ref, xn_scr, h_scr, acc_scr):
    c = pl.program_id(2)

    @pl.when(c == 0)
    def _():
        xn = _mixer_residual(x_ref, o_ref, wo_ref, gt1_ref)
        xn_scr[...] = xn
        h_scr[...] = _modulate(xn, g_ref[...], sh_ref[0], sc_ref[0]).astype(BF16)
        acc_scr[...] = jnp.zeros(acc_scr.shape, F32)

    acc_scr[...] += _swiglu_chunk(h_scr[...], wg_ref[...], wu_ref[...], wd_ref[...])

    @pl.when(c == pl.num_programs(2) - 1)
    def _():
        out_ref[0] = xn_scr[...] + gt2_ref[0] * acc_scr[...]


def _moe_pre_kernel(x_ref, o_ref, wo_ref, gt1_ref, g_ref, sh_ref, sc_ref, wr_ref, xn_ref, h_ref, gate_ref, sel_ref):
    xn = _mixer_residual(x_ref, o_ref, wo_ref, gt1_ref)
    xn_ref[0] = xn
    h = _modulate(xn, g_ref[...], sh_ref[0], sc_ref[0])
    h_ref[0] = h
    lane = lax.broadcasted_iota(jnp.int32, (h.shape[0], LANE), 1)
    lanef = lane.astype(F32)
    logits = jnp.dot(h.astype(BF16), wr_ref[...], preferred_element_type=F32)
    logits = jnp.where(lane < N_EXPERTS, logits, -jnp.inf)
    m1 = jnp.max(logits, axis=1, keepdims=True)
    i1 = jnp.min(jnp.where(logits == m1, lanef, float(LANE)), axis=1, keepdims=True)
    rest = jnp.where(lanef == i1, -jnp.inf, logits)
    m2 = jnp.max(rest, axis=1, keepdims=True)
    i2 = jnp.min(jnp.where(rest == m2, lanef, float(LANE)), axis=1, keepdims=True)
    e2 = jnp.exp(m2 - m1)
    den = 1.0 + e2
    gates = jnp.where(lanef == i1, 1.0 / den, 0.0) + jnp.where(lanef == i2, e2 / den, 0.0)
    sel = jnp.where(lanef == i1, 1.0, 0.0) + jnp.where(lanef == i2, 1.0, 0.0)
    gate_ref[0] = gates[:, :N_EXPERTS]
    sel_ref[0] = sel[:, :N_EXPERTS]


def _moe_expert_kernel(te_tab, tv_tab, src_hbm, dst_hbm, h_hbm, wg_ref, wu_ref, wd_ref, y_hbm,
                       src_smem, dst_smem, hbuf, hb, acc, sem_idx, sem_in, sem_out, *, rows):
    i, c = pl.program_id(0), pl.program_id(1)
    valid = tv_tab[i] != 0
    all_rows_in = pltpu.make_async_copy(h_hbm.at[pl.ds(0, rows), :], hbuf, sem_in)
    all_rows_out = pltpu.make_async_copy(acc, y_hbm.at[pl.ds(0, rows), :], sem_out)

    @pl.when(valid & (c == 0))
    def _():
        idx_copies = [pltpu.make_async_copy(src_hbm.at[i], src_smem, sem_idx.at[0]),
                      pltpu.make_async_copy(dst_hbm.at[i], dst_smem, sem_idx.at[1])]
        for cp in idx_copies:
            cp.start()
        for cp in idx_copies:
            cp.wait()

        def gather_rows(r0, carry):
            for u in range(DMA_BURST):
                r = r0 * DMA_BURST + u
                pltpu.make_async_copy(h_hbm.at[pl.ds(src_smem[r], 1), :], hbuf.at[pl.ds(r, 1), :],
                                      sem_in).start(priority=u % 2)
            return carry
        lax.fori_loop(0, rows // DMA_BURST, gather_rows, 0)
        all_rows_in.wait()
        hb[...] = hbuf[...].astype(BF16)
        acc[...] = jnp.zeros(acc.shape, F32)

    @pl.when(valid)
    def _():
        acc[...] += _swiglu_chunk(hb[...], wg_ref[0], wu_ref[0], wd_ref[0])

    @pl.when(valid & (c == pl.num_programs(1) - 1))
    def _():
        def scatter_rows(r0, carry):
            for u in range(DMA_BURST):
                r = r0 * DMA_BURST + u
                pltpu.make_async_copy(acc.at[pl.ds(r, 1), :], y_hbm.at[pl.ds(dst_smem[r], 1), :],
                                      sem_out).start(priority=u % 2)
            return carry
        lax.fori_loop(0, rows // DMA_BURST, scatter_rows, 0)
        all_rows_out.wait()


def _moe_combine_kernel(xn_ref, gt2_ref, g2_ref, ya_ref, yb_ref, out_ref):
    g2 = g2_ref[0]
    out_ref[0] = xn_ref[0] + gt2_ref[0] * (g2[:, 0:1] * ya_ref[...] + g2[:, 1:2] * yb_ref[...])


def _token_specs(tm, nd):
    pick = (lambda f: lambda b, i: f(b, i)) if nd == 2 else (lambda f: lambda b, i, c: f(b, i))
    tok = pl.BlockSpec((1, tm, D_MODEL), pick(lambda b, i: (b, i, 0)))

    def mod_spec(a):
        return pl.BlockSpec((1, 1, D_MODEL), pick(lambda b, i: (b, 0, 0))) if a.shape[1] == 1 else tok

    cst = lambda shape: pl.BlockSpec(shape, pick(lambda b, i: (0, 0)))
    return tok, mod_spec, cst


def _swiglu_mixer(x, o, w_out, gt1, g, shift, scale, gt2, w_gate, w_up, w_down, *, tm):
    gsz, t, _ = x.shape
    d_ff = w_gate.shape[-1]
    assert t % tm == 0 and d_ff % FF_TILE == 0
    tok, mod_spec, cst = _token_specs(tm, 3)
    return pl.pallas_call(
        _ffn_kernel, grid=(gsz, t // tm, d_ff // FF_TILE),
        in_specs=[tok, tok, cst((ATTN_WIDTH, D_MODEL)), mod_spec(gt1), cst((1, D_MODEL)),
                  mod_spec(shift), mod_spec(scale), mod_spec(gt2),
                  pl.BlockSpec((D_MODEL, FF_TILE), lambda b, i, c: (0, c)),
                  pl.BlockSpec((D_MODEL, FF_TILE), lambda b, i, c: (0, c)),
                  pl.BlockSpec((FF_TILE, D_MODEL), lambda b, i, c: (c, 0))],
        out_specs=tok,
        out_shape=jax.ShapeDtypeStruct((gsz, t, D_MODEL), F32),
        scratch_shapes=[pltpu.VMEM((tm, D_MODEL), F32), pltpu.VMEM((tm, D_MODEL), BF16),
                        pltpu.VMEM((tm, D_MODEL), F32)],
        compiler_params=_params(("arbitrary",) * 3),
        name="swiglu_channel_mixer",
    )(x, o, w_out.astype(BF16), gt1, g[None, :], shift, scale, gt2,
      w_gate.astype(BF16), w_up.astype(BF16), w_down.astype(BF16))


def _moe_pre(x, o, w_out, gt1, g, shift, scale, w_router, *, tm):
    gsz, t, _ = x.shape
    assert t % tm == 0
    tok, mod_spec, cst = _token_specs(tm, 2)
    small = pl.BlockSpec((1, tm, N_EXPERTS), lambda b, i: (b, i, 0))
    wr = jnp.pad(w_router.astype(BF16), ((0, 0), (0, LANE - N_EXPERTS)))
    return pl.pallas_call(
        _moe_pre_kernel, grid=(gsz, t // tm),
        in_specs=[tok, tok, cst((ATTN_WIDTH, D_MODEL)), mod_spec(gt1), cst((1, D_MODEL)),
                  mod_spec(shift), mod_spec(scale), cst((D_MODEL, LANE))],
        out_specs=[tok, tok, small, small],
        out_shape=[jax.ShapeDtypeStruct((gsz, t, D_MODEL), F32), jax.ShapeDtypeStruct((gsz, t, D_MODEL), F32),
                   jax.ShapeDtypeStruct((gsz, t, N_EXPERTS), F32), jax.ShapeDtypeStruct((gsz, t, N_EXPERTS), F32)],
        compiler_params=_params(ARB2),
        name="moe_pre_router",
    )(x, o, w_out.astype(BF16), gt1, g[None, :], shift, scale, wr)


def _moe_routing(gates, sel, rows):
    n = gates.shape[0]
    member = sel > 0
    mi = member.astype(jnp.int32)
    counts = mi.sum(0)
    padded = (counts + rows - 1) // rows * rows
    ends = jnp.cumsum(padded)
    slot = jnp.cumsum(mi, axis=1) - mi
    expert_of = [jnp.sum(jnp.where(member & (slot == k), jnp.arange(N_EXPERTS)[None, :], 0), axis=1)
                 for k in range(TOP_K)]
    g2 = jnp.stack([jnp.sum(jnp.where(member & (slot == k), gates, 0.0), axis=1) for k in range(TOP_K)], axis=1)
    n_rows = (2 * n + N_EXPERTS * rows) // rows * rows
    n_pad = n_rows - 2 * n
    pad_ends = jnp.cumsum(padded - counts)
    pad_expert = (jnp.arange(n_pad, dtype=jnp.int32)[:, None] >= pad_ends[None, :]).sum(1)
    tok = jnp.arange(n, dtype=jnp.int32)
    sort_key = jnp.concatenate([expert_of[0], expert_of[1], pad_expert]).astype(jnp.int32)
    pair = jnp.concatenate([tok, n + tok, jnp.full((n_pad,), -1, jnp.int32)])
    _, pair = lax.sort((sort_key, pair), num_keys=1, is_stable=True)
    spare = 2 * n + jnp.arange(n_rows, dtype=jnp.int32) % rows
    src = jnp.where(pair >= 0, pair % n, 0)
    dst = jnp.where(pair >= 0, pair, spare)
    n_tiles = n_rows // rows
    tile_start = jnp.arange(n_tiles, dtype=jnp.int32) * rows
    tile_expert = jnp.minimum((tile_start[:, None] >= ends[None, :]).sum(1), N_EXPERTS - 1).astype(jnp.int32)
    tile_valid = (tile_start < ends[-1]).astype(jnp.int32)
    return src.reshape(n_tiles, rows), dst.reshape(n_tiles, rows), tile_expert, tile_valid, g2


def _moe_experts(h_all, src, dst, tile_expert, tile_valid, w_gate, w_up, w_down, *, rows):
    n = h_all.shape[0]
    n_tiles = src.shape[0]
    d_ff = w_gate.shape[-1]
    any_spec = pl.BlockSpec(memory_space=pl.ANY)
    return pl.pallas_call(
        functools.partial(_moe_expert_kernel, rows=rows),
        grid_spec=pltpu.PrefetchScalarGridSpec(
            num_scalar_prefetch=2, grid=(n_tiles, d_ff // FF_TILE),
            in_specs=[any_spec, any_spec, any_spec,
                      pl.BlockSpec((1, D_MODEL, FF_TILE), lambda i, c, te, tv: (te[i], 0, c)),
                      pl.BlockSpec((1, D_MODEL, FF_TILE), lambda i, c, te, tv: (te[i], 0, c)),
                      pl.BlockSpec((1, FF_TILE, D_MODEL), lambda i, c, te, tv: (te[i], c, 0))],
            out_specs=any_spec,
            scratch_shapes=[pltpu.SMEM((rows,), jnp.int32), pltpu.SMEM((rows,), jnp.int32),
                            pltpu.VMEM((rows, D_MODEL), F32), pltpu.VMEM((rows, D_MODEL), BF16),
                            pltpu.VMEM((rows, D_MODEL), F32), pltpu.SemaphoreType.DMA((2,)),
                            pltpu.SemaphoreType.DMA(()), pltpu.SemaphoreType.DMA(())]),
        out_shape=jax.ShapeDtypeStruct((2 * n + rows, D_MODEL), F32),
        compiler_params=_params(ARB2),
        name="moe_routed_experts",
    )(tile_expert, tile_valid, src, dst, h_all, w_gate.astype(BF16), w_up.astype(BF16), w_down.astype(BF16))


def _moe_combine(xn, gt2, g2, y, *, tm, n_all, tok0):
    gsz, t, _ = xn.shape
    assert t % tm == 0 and n_all % tm == 0 and tok0 % tm == 0
    tok, mod_spec, _ = _token_specs(tm, 2)
    nt = t // tm
    yspec = lambda slot: pl.BlockSpec((tm, D_MODEL), lambda b, i: ((slot * n_all + tok0) // tm + b * nt + i, 0))
    return pl.pallas_call(
        _moe_combine_kernel, grid=(gsz, nt),
        in_specs=[tok, mod_spec(gt2), pl.BlockSpec((1, tm, TOP_K), lambda b, i: (b, i, 0)), yspec(0), yspec(1)],
        out_specs=tok,
        out_shape=jax.ShapeDtypeStruct((gsz, t, D_MODEL), F32),
        compiler_params=_params(ARB2),
        name="moe_combine",
    )(xn, gt2, g2, y, y)


def _per_token(a, n):
    return jnp.broadcast_to(a, (a.shape[0], n, a.shape[2])).reshape(1, a.shape[0] * n, a.shape[2])


def _pad_keys(a, axis, l_pad):
    pad = [(0, 0)] * a.ndim
    pad[axis] = (0, l_pad - a.shape[axis])
    return jnp.pad(a, pad)


def _layer_a(xp, xs, cp, cs, ck, cv, cki, rel_bias, norm_g, w_ada, b_ada, w_in, w_out, g_q, g_k, g_kidx,
             ffn_wg, ffn_wu, ffn_wd):
    bias = _dsa_bias_tiles(rel_bias)
    nb, n = xs.shape[0], xs.shape[1]
    past = ck.shape[1]
    outs = []
    for x, c, sample in ((xp, cp, False), (xs, cs, True)):
        sh1, sc1, gt1, sh2, sc2, gt2 = adaln(c, w_ada, b_ada)
        q16, kf, vf, ktr, vx, kif, qi16, kit, wsc = _project(
            x, norm_g[0], sh1, sc1, w_in, g_q, g_k, g_kidx, n_kv=KV_HEADS_A, has_idx=True,
            tm=n if sample else 256)
        if sample:
            l_true, l_pad = past + n, 2304
            kfull = jnp.concatenate([ck, kf.reshape(nb, n, KV_HEADS_A, HEAD_DIM)], axis=1).astype(BF16)
            vfull = jnp.concatenate([cv, vf.reshape(nb, n, KV_HEADS_A, HEAD_DIM)], axis=1).astype(BF16)
            kifull = jnp.concatenate([cki, kif], axis=1).astype(BF16)
            ktr = _pad_keys(kfull, 1, l_pad).transpose(0, 2, 3, 1)
            vx = jnp.concatenate([vfull, jnp.ones(vfull.shape[:3] + (1,), BF16),
                                  jnp.zeros(vfull.shape[:3] + (LANE - HEAD_DIM - 1,), BF16)], axis=-1)
            vx = _pad_keys(vx, 1, l_pad).transpose(0, 2, 1, 3)
            kit = _pad_keys(kifull, 1, l_pad).transpose(0, 2, 1)
            o = _dsa_attention(q16, ktr, vx, qi16, kit, wsc, bias, p0=past, l_true=l_true,
                               n_sel=min(TOPK_MAX, l_true // 4), tq=n, tk=l_pad, sw=256, cw=256)
            flat = lambda a: a.reshape(1, nb * n, a.shape[-1])
            y = _swiglu_mixer(flat(x), flat(o), w_out, _per_token(gt1, n), norm_g[1], _per_token(sh2, n),
                              _per_token(sc2, n), _per_token(gt2, n), ffn_wg, ffn_wu, ffn_wd, tm=MIX_TM_SAMPLE)
            y = y.reshape(x.shape)
        else:
            t = x.shape[1]
            o = _dsa_attention(q16, ktr, vx, qi16, kit, wsc, bias, p0=0, l_true=t,
                               n_sel=min(TOPK_MAX, t // 4), tq=LANE, tk=1024, sw=256, cw=1024)
            y = _swiglu_mixer(x, o, w_out, gt1, norm_g[1], sh2, sc2, gt2, ffn_wg, ffn_wu, ffn_wd, tm=MIX_TM_PROMPT)
        bsz, t = x.shape[0], x.shape[1]
        outs.append((y, kf.reshape(bsz, t, KV_HEADS_A, HEAD_DIM), vf.reshape(bsz, t, KV_HEADS_A, HEAD_DIM), kif))
    return outs


def _layer_b(xp, xs, cp, cs, ck, cv, rel_bias, norm_g, w_ada, b_ada, w_in, w_out, g_q, g_k, sinks,
             w_router, moe_wg, moe_wu, moe_wd):
    bias = _band_bias(rel_bias)
    nb, n = xs.shape[0], xs.shape[1]
    streams = []
    for x, c, sample in ((xp, cp, False), (xs, cs, True)):
        sh1, sc1, gt1, sh2, sc2, gt2 = adaln(c, w_ada, b_ada)
        q16, kf, vf, ktr, v16 = _project(x, norm_g[0], sh1, sc1, w_in, g_q, g_k, g_k, n_kv=KV_HEADS_B,
                                         has_idx=False, tm=n if sample else 256)
        bsz, t = x.shape[0], x.shape[1]
        k4 = kf.reshape(bsz, t, KV_HEADS_B, HEAD_DIM)
        v4 = vf.reshape(bsz, t, KV_HEADS_B, HEAD_DIM)
        zero = lambda i: 0
        if sample:
            kp = ck.astype(BF16).transpose(0, 2, 3, 1)
            vp = cv.astype(BF16).transpose(0, 2, 1, 3)
            o = _band_attention(q16, kp, _pad_keys(ktr, 3, LANE), vp, _pad_keys(v16, 2, LANE), bias, sinks,
                                tq=n, prev_map=zero, cur_map=zero, first_has_no_prev=False)
            flat = lambda a: a.reshape(1, nb * n, a.shape[-1])
            xn, h, gates, sel = _moe_pre(flat(x), flat(o), w_out, _per_token(gt1, n), norm_g[1],
                                         _per_token(sh2, n), _per_token(sc2, n), w_router, tm=MIX_TM_SAMPLE)
            gt2 = _per_token(gt2, n)
            k_new = jnp.concatenate([ck, k4], axis=1)[:, -WINDOW:]
            v_new = jnp.concatenate([cv, v4], axis=1)[:, -WINDOW:]
        else:
            o = _band_attention(q16, ktr, ktr, v16, v16, bias, sinks, tq=LANE,
                                prev_map=lambda i: jnp.maximum(i - 1, 0), cur_map=lambda i: i,
                                first_has_no_prev=True)
            xn, h, gates, sel = _moe_pre(x, o, w_out, gt1, norm_g[1], sh2, sc2, w_router, tm=MIX_TM_PROMPT)
            k_new, v_new = k4[:, -WINDOW:], v4[:, -WINDOW:]
        streams.append((x.shape, xn, h, gates, sel, gt2, k_new, v_new))

    rows2d = lambda a: a.reshape(-1, a.shape[-1])
    h_all = jnp.concatenate([rows2d(st[2]) for st in streams])
    n_all = h_all.shape[0]
    src, dst, tile_expert, tile_valid, g2 = _moe_routing(
        jnp.concatenate([rows2d(st[3]) for st in streams]), jnp.concatenate([rows2d(st[4]) for st in streams]),
        MOE_ROW_TILE)
    y = _moe_experts(h_all, src, dst, tile_expert, tile_valid, moe_wg, moe_wu, moe_wd, rows=MOE_ROW_TILE)
    outs, tok0 = [], 0
    for shape, xn, _, _, _, gt2, k_new, v_new in streams:
        cnt = xn.shape[0] * xn.shape[1]
        out = _moe_combine(xn, gt2, g2[tok0:tok0 + cnt].reshape(xn.shape[0], xn.shape[1], TOP_K), y,
                           tm=MOE_COMBINE_TM, n_all=n_all, tok0=tok0)
        outs.append((out.reshape(shape), k_new, v_new))
        tok0 += cnt
    return outs


def kernel(x_prompt, x_sample, c_prompt, c_sample, cache_a_k, cache_a_v, cache_a_kidx,
           cache_b_k, cache_b_v, rel_bias, norm_g, w_ada, b_ada,
           a_w_in, a_w_out, a_g_q, a_g_k, a_g_kidx,
           b_w_in, b_w_out, b_g_q, b_g_k, b_sinks,
           ffn_w_gate, ffn_w_up, ffn_w_down,
           moe_w_router, moe_w_gate, moe_w_up, moe_w_down):
    xp, xs = x_prompt, x_sample
    a_out = [[] for _ in range(6)]
    b_out = [[] for _ in range(4)]
    for i in range(DEPTH):
        j = i // 2
        if i % 2 == 0:
            (xp, k1, v1, i1), (xs, k2, v2, i2) = _layer_a(
                xp, xs, c_prompt, c_sample, cache_a_k[j], cache_a_v[j], cache_a_kidx[j], rel_bias, norm_g[i],
                w_ada[i], b_ada[i], a_w_in[j], a_w_out[j], a_g_q[j], a_g_k[j], a_g_kidx[j],
                ffn_w_gate[j], ffn_w_up[j], ffn_w_down[j])
            for lst, val in zip(a_out, (k1, v1, i1, k2, v2, i2)):
                lst.append(val)
        else:
            (xp, k1, v1), (xs, k2, v2) = _layer_b(
                xp, xs, c_prompt, c_sample, cache_b_k[j], cache_b_v[j], rel_bias, norm_g[i],
                w_ada[i], b_ada[i], b_w_in[j], b_w_out[j], b_g_q[j], b_g_k[j], b_sinks[j],
                moe_w_router[j], moe_w_gate[j], moe_w_up[j], moe_w_down[j])
            for lst, val in zip(b_out, (k1, v1, k2, v2)):
                lst.append(val)
    return (xp, xs, *(jnp.stack(l) for l in a_out), *(jnp.stack(l) for l in b_out))
```

```python
import math, functools
import jax, jax.numpy as jnp
from jax import lax
import numpy as np
from jax.experimental import pallas as pl
from jax.experimental.pallas import tpu as pltpu

D_MODEL = 1024
DEPTH = 2
CHUNK = 64
N_HEADS = 16
HEAD_DIM = 64
ATTN_WIDTH = N_HEADS * HEAD_DIM
KV_HEADS_A = 4
IDX_HEADS = 8
IDX_DIM = 64
TOPK_MAX = 256
KV_HEADS_B = 2
WINDOW = 128
NUM_BUCKETS = 32
MAX_DISTANCE = 1024
N_EXPERTS = 8
TOP_K = 2
EPS = 1e-6

F32 = jnp.float32
BF16 = jnp.bfloat16
LANE = 128
VMEM_LIMIT_BYTES = 48 * 1024 * 1024
LOG2E = math.log2(math.e)
NEG_BIG = -1e30
KEY_MIN = -2 ** 31
FAR_BUCKET = NUM_BUCKETS // 2 - 1
NEAR_BLOCKS = 6
FF_TILE = 512
MIX_TM_PROMPT = 1024
MIX_TM_SAMPLE = 256
MOE_ROW_TILE = 512
MOE_COMBINE_TM = 256
DMA_BURST = 8
ARB2 = ("arbitrary", "arbitrary")


def _params(sem, vmem=True):
    return pltpu.CompilerParams(dimension_semantics=sem, vmem_limit_bytes=VMEM_LIMIT_BYTES if vmem else None)


def rel_bucket(rel):
    half = NUM_BUCKETS // 2
    max_exact = half // 2
    base = jnp.where(rel > 0, half, 0)
    n = jnp.abs(rel)
    nf = jnp.maximum(n, 1).astype(F32)
    large = max_exact + (jnp.log(nf / max_exact) / math.log(MAX_DISTANCE / max_exact)
                         * (half - max_exact)).astype(jnp.int32)
    large = jnp.minimum(large, half - 1)
    return base + jnp.where(n < max_exact, n, large)


def _adaln_kernel(c_ref, w_ref, b_ref, o_ref):
    c = c_ref[...]
    o_ref[...] = jnp.dot(c * jax.nn.sigmoid(c), w_ref[...], preferred_element_type=F32) + b_ref[...]


def adaln(c, w, b):
    n = c.shape[0]
    mod = pl.pallas_call(
        _adaln_kernel,
        grid=(6,),
        in_specs=[pl.BlockSpec((n, D_MODEL), lambda j: (0, 0)),
                  pl.BlockSpec((D_MODEL, D_MODEL), lambda j: (0, j)),
                  pl.BlockSpec((1, D_MODEL), lambda j: (0, j))],
        out_specs=pl.BlockSpec((n, D_MODEL), lambda j: (0, j)),
        out_shape=jax.ShapeDtypeStruct((n, 6 * D_MODEL), F32),
        name="adaln",
    )(c, w, b[None, :])
    return jnp.split(mod[:, None, :], 6, axis=-1)


def _modulate(x, g, shift, scale):
    y = x * lax.rsqrt(jnp.mean(x * x, axis=-1, keepdims=True) + EPS)
    return (y * g) * (1.0 + scale) + shift


def _seg_norm(seg, g):
    return seg * lax.rsqrt(jnp.mean(seg * seg, axis=-1, keepdims=True) + EPS) * g


def _proj_kernel(x_ref, g_ref, sh_ref, sc_ref, w_ref, gq_ref, gk_ref, gki_ref, *outs, n_kv, has_idx):
    h = _modulate(x_ref[0], g_ref[...], sh_ref[0], sc_ref[0]).astype(BF16)
    y = jnp.dot(h, w_ref[...], preferred_element_type=F32)
    tm = y.shape[0]
    kv_w = n_kv * HEAD_DIM
    if has_idx:
        q16, kf, vf, ktr, vx, kif, qi16, kit, wsc = outs
    else:
        q16, kf, vf, ktr, v16 = outs
    for hd in range(N_HEADS):
        seg = y[:, hd * HEAD_DIM:(hd + 1) * HEAD_DIM]
        q_scale = HEAD_DIM ** -0.5 * (LOG2E if has_idx else 1.0)
        q16[0, hd] = (_seg_norm(seg, gq_ref[...]) * q_scale).astype(BF16)
    lane = lax.broadcasted_iota(jnp.int32, (tm, LANE - HEAD_DIM), 1)
    ones_col = jnp.where(lane == 0, 1.0, 0.0).astype(BF16)
    for j in range(n_kv):
        kseg = _seg_norm(y[:, ATTN_WIDTH + j * HEAD_DIM:ATTN_WIDTH + (j + 1) * HEAD_DIM], gk_ref[...])
        vseg = y[:, ATTN_WIDTH + kv_w + j * HEAD_DIM:ATTN_WIDTH + kv_w + (j + 1) * HEAD_DIM]
        kf[0, :, j * HEAD_DIM:(j + 1) * HEAD_DIM] = kseg
        vf[0, :, j * HEAD_DIM:(j + 1) * HEAD_DIM] = vseg
        ktr[0, j] = kseg.T.astype(BF16)
        if has_idx:
            vx[0, j] = jnp.concatenate([vseg.astype(BF16), ones_col], axis=1)
        else:
            v16[0, j] = vseg.astype(BF16)
    if has_idx:
        base = ATTN_WIDTH + 2 * kv_w
        for hd in range(IDX_HEADS):
            qi16[0, hd] = y[:, base + hd * IDX_DIM:base + (hd + 1) * IDX_DIM].astype(BF16)
        base += IDX_HEADS * IDX_DIM
        kiseg = _seg_norm(y[:, base:base + IDX_DIM], gki_ref[...])
        kif[0] = kiseg
        kit[0] = kiseg.T.astype(BF16)
        wsc[0] = y[:, base + IDX_DIM:base + IDX_DIM + IDX_HEADS] * IDX_HEADS ** -0.5 * IDX_DIM ** -0.5


def _project(x, g, shift, scale, w_in, g_q, g_k, g_kidx, *, n_kv, has_idx, tm):
    gsz, t, _ = x.shape
    n_in = w_in.shape[1]
    kv_w = n_kv * HEAD_DIM
    tok = lambda b, i: (b, i, 0)
    head = lambda b, i: (b, 0, i, 0)
    vec = lambda b, i: (b, 0, 0)
    cst = lambda b, i: (0, 0)
    shapes = [((gsz, N_HEADS, t, HEAD_DIM), BF16, (1, N_HEADS, tm, HEAD_DIM), head),
              ((gsz, t, kv_w), F32, (1, tm, kv_w), tok),
              ((gsz, t, kv_w), F32, (1, tm, kv_w), tok),
              ((gsz, n_kv, HEAD_DIM, t), BF16, (1, n_kv, HEAD_DIM, tm), lambda b, i: (b, 0, 0, i))]
    if has_idx:
        shapes += [((gsz, n_kv, t, LANE), BF16, (1, n_kv, tm, LANE), head),
                   ((gsz, t, IDX_DIM), F32, (1, tm, IDX_DIM), tok),
                   ((gsz, IDX_HEADS, t, IDX_DIM), BF16, (1, IDX_HEADS, tm, IDX_DIM), head),
                   ((gsz, IDX_DIM, t), BF16, (1, IDX_DIM, tm), lambda b, i: (b, 0, i)),
                   ((gsz, t, IDX_HEADS), F32, (1, tm, IDX_HEADS), tok)]
    else:
        shapes += [((gsz, n_kv, t, HEAD_DIM), BF16, (1, n_kv, tm, HEAD_DIM), head)]
    return pl.pallas_call(
        functools.partial(_proj_kernel, n_kv=n_kv, has_idx=has_idx),
        grid=(gsz, t // tm),
        in_specs=[pl.BlockSpec((1, tm, D_MODEL), tok), pl.BlockSpec((1, D_MODEL), cst),
                  pl.BlockSpec((1, 1, D_MODEL), vec), pl.BlockSpec((1, 1, D_MODEL), vec),
                  pl.BlockSpec((D_MODEL, n_in), cst), pl.BlockSpec((1, HEAD_DIM), cst),
                  pl.BlockSpec((1, HEAD_DIM), cst), pl.BlockSpec((1, IDX_DIM), cst)],
        out_specs=[pl.BlockSpec(blk, im) for _, _, blk, im in shapes],
        out_shape=[jax.ShapeDtypeStruct(s, d) for s, d, _, _ in shapes],
        compiler_params=_params(ARB2),
        name="mixer_in_proj",
    )(x, g[None, :], shift, scale, w_in.astype(BF16), g_q[None, :], g_k[None, :], g_kidx[None, :])


def _tri_steps(nqb, tq, tk, p0):
    qbs, kts = [], []
    for qb in range(nqb):
        vis_end = ((p0 + qb * tq + tq - 1) // CHUNK + 1) * CHUNK
        for kt in range(-(-vis_end // tk)):
            qbs.append(qb)
            kts.append(kt)
    return jnp.asarray(np.array(qbs, np.int32)), jnp.asarray(np.array(kts, np.int32))


def _idx_kernel(qb_tab, kt_tab, qi_ref, w_ref, kit_ref, out_ref, *, tq, tk, sw, p0, l_true):
    step = pl.program_id(1)
    q0 = p0 + qb_tab[step] * tq
    k0 = kt_tab[step] * tk
    row = lax.broadcasted_iota(jnp.int32, (tq, 1), 0) + q0
    vis_end = jnp.minimum(((row >> 6) + 1) << 6, l_true)
    wv = w_ref[0]
    for c in range(tk // sw):
        kit = kit_ref[0, :, c * sw:(c + 1) * sw]
        acc = jnp.zeros((tq, sw), F32)
        for h in range(IDX_HEADS):
            sh = jnp.dot(qi_ref[0, h], kit, preferred_element_type=F32)
            acc = acc + wv[:, h:h + 1] * jnp.maximum(sh, 0.0)
        bits = lax.bitcast_convert_type(acc, jnp.int32)
        key = bits ^ ((bits >> 31) & 0x7FFFFFFF)
        kpos = k0 + c * sw + lax.broadcasted_iota(jnp.int32, (tq, sw), 1)
        out_ref[0, :, c * sw:(c + 1) * sw] = jnp.where(kpos < vis_end, key, KEY_MIN)


def _thr_kernel(keys_ref, tau_ref, quota_ref, flag_ref, *, tq, cw, p0, n_sel):
    qb = pl.program_id(1)
    vis_end = (((p0 + qb * tq + tq - 1) >> 6) + 1) << 6
    nch = (vis_end + cw - 1) // cw
    key_max = 2 ** 31 - 1

    def count_ge(cand):
        def body(j, acc):
            x = keys_ref[0, :, pl.ds(pl.multiple_of(j * cw, cw), cw)]
            for u in range(cw // LANE):
                acc = acc + jnp.where(x[:, u * LANE:(u + 1) * LANE] >= cand, 1.0, 0.0)
            return acc
        acc = lax.fori_loop(0, nch, body, jnp.zeros((tq, LANE), F32))
        return jnp.sum(acc, axis=1, keepdims=True)

    zero = jnp.zeros((tq, LANE), jnp.int32)
    c0 = count_ge(zero)
    tau = jnp.where(c0 >= n_sel, zero, KEY_MIN)
    n_ge = jnp.where(c0 >= n_sel, c0, float(2 ** 24))

    def bit_body(carry):
        i, tau, n_ge = carry
        cand = tau + jnp.left_shift(jnp.int32(1), 30 - i)
        c = count_ge(cand)
        return i + 1, jnp.where(c >= n_sel, cand, tau), jnp.where(c >= n_sel, c, n_ge)

    def unresolved(carry):
        i, _, n_ge = carry
        return (i < 31) & (jnp.max(jnp.abs(n_ge - n_sel)) > 0.0)

    _, tau, n_ge = lax.while_loop(unresolved, bit_body, (jnp.int32(0), tau, n_ge))
    n_gt = jnp.where(tau[:, :1] < key_max, count_ge(jnp.minimum(tau, key_max - 1) + 1), 0.0)
    tau_ref[0] = jnp.maximum(tau, KEY_MIN + 1)
    quota_ref[0] = jnp.broadcast_to(n_sel - n_gt, (tq, LANE))
    over = jnp.max(jnp.where(n_ge > n_sel, 1, 0), axis=0, keepdims=True)
    flag_ref[0, 0] = jnp.broadcast_to(over, flag_ref.shape[2:])


def _attn_kernel(qb_tab, kt_tab, flag_tab, q_ref, kt_ref, vx_ref, keys_ref, tau_ref, quota_ref, bias_ref, o_ref,
                 m_ref, acc_ref, negm_ref, seen_ref, *, tq, tk, sw, p0, nqb):
    step = pl.program_id(1)
    qb, kt = qb_tab[step], kt_tab[step]
    q0 = p0 + qb * tq
    k0 = kt * tk
    vis_end = (((q0 + tq - 1) >> 6) + 1) << 6
    group = N_HEADS // KV_HEADS_A
    nblk = sw // LANE
    tie_w = 2 * LANE

    @pl.when(kt == 0)
    def _():
        m_ref[...] = jnp.full(m_ref.shape, NEG_BIG, F32)
        acc_ref[...] = jnp.zeros(acc_ref.shape, F32)
        seen_ref[...] = jnp.zeros(seen_ref.shape, F32)

    has_ties = flag_tab[pl.program_id(0) * nqb + qb] != 0
    tau = tau_ref[0]

    @pl.when(jnp.logical_not(has_ties))
    def _():
        for u in range(tk // LANE):
            cols = slice(u * LANE, (u + 1) * LANE)
            negm_ref[:, cols] = jnp.where(keys_ref[0, :, cols] >= tau, 0.0, NEG_BIG)

    @pl.when(has_ties)
    def _():
        tau2 = jnp.concatenate([tau] * (tie_w // LANE), axis=1)
        quota = jnp.concatenate([quota_ref[0]] * (tie_w // LANE), axis=1)
        before = (lax.broadcasted_iota(jnp.int32, (tie_w, tie_w), 0)
                  < lax.broadcasted_iota(jnp.int32, (tie_w, tie_w), 1))
        before = jnp.where(before, 1.0, 0.0).astype(BF16)
        seen = seen_ref[...]
        for u in range(tk // tie_w):
            cols = slice(u * tie_w, (u + 1) * tie_w)
            k = keys_ref[0, :, cols]
            tie = jnp.where(k == tau2, 1.0, 0.0)
            rank = (jnp.dot(tie.astype(BF16), before, preferred_element_type=F32)
                    + jnp.concatenate([seen] * (tie_w // LANE), axis=1))
            keep = jnp.where(k > tau2, 1.0, jnp.where(rank < quota, tie, 0.0))
            negm_ref[:, cols] = jnp.where(keep > 0.0, 0.0, NEG_BIG)
            seen = seen + jnp.sum(tie, axis=1, keepdims=True)
        seen_ref[...] = seen

    def sub_tile(c, with_bias):
        off = c * sw if isinstance(c, int) else pl.multiple_of(c * sw, sw)
        negm = negm_ref[:, pl.ds(off, sw)]
        if with_bias:
            d0 = ((k0 + off - q0) >> 7) + NEAR_BLOCKS
            bidx = [jnp.clip(d0 + u, 0, NEAR_BLOCKS) for u in range(nblk)]
        for h in range(N_HEADS):
            j = h // group
            s = jnp.dot(q_ref[0, h], kt_ref[0, j, :, pl.ds(off, sw)], preferred_element_type=F32) + negm
            if with_bias:
                s = s + jnp.concatenate([bias_ref[bidx[u], h] for u in range(nblk)], axis=1)
            m_prev = m_ref[h]
            m_cur = jnp.maximum(m_prev, jnp.max(s, axis=1, keepdims=True))
            alpha = jnp.exp2(m_prev - m_cur)
            p = jnp.exp2(s - jnp.concatenate([m_cur] * nblk, axis=1))
            pv = jnp.dot(p.astype(BF16), vx_ref[0, j, pl.ds(off, sw), :], preferred_element_type=F32)
            acc_ref[h] = alpha * acc_ref[h] + pv
            m_ref[h] = m_cur

    tile_far = k0 + tk <= q0 - (NEAR_BLOCKS - 1) * LANE

    @pl.when(tile_far)
    def _():
        for c in range(tk // sw):
            sub_tile(c, False)

    @pl.when(jnp.logical_not(tile_far))
    def _():
        nsub = jnp.minimum(tk // sw, (vis_end - k0 + sw - 1) // sw)
        nfar = jnp.clip((q0 - (NEAR_BLOCKS - 1) * LANE - k0) // sw, 0, nsub)

        def far_body(c, carry):
            sub_tile(c, False)
            return carry

        def near_body(c, carry):
            sub_tile(c, True)
            return carry

        lax.fori_loop(0, nfar, far_body, 0)
        lax.fori_loop(nfar, nsub, near_body, 0)

    @pl.when(k0 + tk >= vis_end)
    def _():
        for h in range(N_HEADS):
            a = acc_ref[h]
            o_ref[0, :, h * HEAD_DIM:(h + 1) * HEAD_DIM] = a[:, :HEAD_DIM] / a[:, HEAD_DIM:HEAD_DIM + 1]


def _rel_tiles(rel_table, rel):
    onehot = jax.nn.one_hot(rel_bucket(rel), NUM_BUCKETS, dtype=F32)
    return jnp.einsum("nrsk,kh->nhrs", onehot, rel_table.astype(F32), precision=lax.Precision.HIGHEST)


def _dsa_bias_tiles(rel_table):
    d = jnp.arange(NEAR_BLOCKS + 1, dtype=jnp.int32)[:, None, None] - NEAR_BLOCKS
    rel = (d * LANE + jnp.arange(LANE, dtype=jnp.int32)[None, None, :]
           - jnp.arange(LANE, dtype=jnp.int32)[None, :, None])
    tiles = (_rel_tiles(rel_table, rel) - rel_table[FAR_BUCKET][None, :, None, None]) * LOG2E
    return tiles.at[0].set(0.0)


def _dsa_attention(q16, ktr, vx, qi16, kit, w, bias, *, p0, l_true, n_sel, tq, tk, sw, cw):
    bsz, t = q16.shape[0], q16.shape[2]
    l_pad = ktr.shape[3]
    nqb = t // tq
    assert t % tq == 0 and p0 % LANE == 0 and (tq == LANE or nqb == 1)
    assert l_pad % tk == 0 and tk % sw == 0 and tk % cw == 0 and tk % (2 * LANE) == 0
    qbt, ktt = _tri_steps(nqb, tq, tk, p0)
    nsteps = int(qbt.shape[0])

    keys = pl.pallas_call(
        functools.partial(_idx_kernel, tq=tq, tk=tk, sw=sw, p0=p0, l_true=l_true),
        grid_spec=pltpu.PrefetchScalarGridSpec(
            num_scalar_prefetch=2, grid=(bsz, nsteps),
            in_specs=[pl.BlockSpec((1, IDX_HEADS, tq, IDX_DIM), lambda b, s, qt, kt: (b, 0, qt[s], 0)),
                      pl.BlockSpec((1, tq, IDX_HEADS), lambda b, s, qt, kt: (b, qt[s], 0)),
                      pl.BlockSpec((1, IDX_DIM, tk), lambda b, s, qt, kt: (b, 0, kt[s]))],
            out_specs=pl.BlockSpec((1, tq, tk), lambda b, s, qt, kt: (b, qt[s], kt[s]))),
        out_shape=jax.ShapeDtypeStruct((bsz, t, l_pad), jnp.int32),
        compiler_params=_params(ARB2, vmem=False),
        name="dsa_index_keys",
    )(qbt, ktt, qi16, w, kit)

    tau, quota, flags = pl.pallas_call(
        functools.partial(_thr_kernel, tq=tq, cw=cw, p0=p0, n_sel=n_sel),
        grid=(bsz, nqb),
        in_specs=[pl.BlockSpec((1, tq, l_pad), lambda b, i: (b, i, 0))],
        out_specs=[pl.BlockSpec((1, tq, LANE), lambda b, i: (b, i, 0)),
                   pl.BlockSpec((1, tq, LANE), lambda b, i: (b, i, 0)),
                   pl.BlockSpec((1, 1, 8, LANE), lambda b, i: (b, i, 0, 0))],
        out_shape=[jax.ShapeDtypeStruct((bsz, t, LANE), jnp.int32),
                   jax.ShapeDtypeStruct((bsz, t, LANE), F32),
                   jax.ShapeDtypeStruct((bsz, nqb, 8, LANE), jnp.int32)],
        compiler_params=_params(ARB2),
        name="dsa_threshold",
    )(keys)
    flags = flags[:, :, 0, 0].reshape(bsz * nqb)

    return pl.pallas_call(
        functools.partial(_attn_kernel, tq=tq, tk=tk, sw=sw, p0=p0, nqb=nqb),
        grid_spec=pltpu.PrefetchScalarGridSpec(
            num_scalar_prefetch=3, grid=(bsz, nsteps),
            in_specs=[pl.BlockSpec((1, N_HEADS, tq, HEAD_DIM), lambda b, s, qt, kt, fl: (b, 0, qt[s], 0)),
                      pl.BlockSpec((1, KV_HEADS_A, HEAD_DIM, tk), lambda b, s, qt, kt, fl: (b, 0, 0, kt[s])),
                      pl.BlockSpec((1, KV_HEADS_A, tk, LANE), lambda b, s, qt, kt, fl: (b, 0, kt[s], 0)),
                      pl.BlockSpec((1, tq, tk), lambda b, s, qt, kt, fl: (b, qt[s], kt[s])),
                      pl.BlockSpec((1, tq, LANE), lambda b, s, qt, kt, fl: (b, qt[s], 0)),
                      pl.BlockSpec((1, tq, LANE), lambda b, s, qt, kt, fl: (b, qt[s], 0)),
                      pl.BlockSpec((NEAR_BLOCKS + 1, N_HEADS, tq, LANE), lambda b, s, qt, kt, fl: (0, 0, 0, 0))],
            out_specs=pl.BlockSpec((1, tq, ATTN_WIDTH), lambda b, s, qt, kt, fl: (b, qt[s], 0)),
            scratch_shapes=[pltpu.VMEM((N_HEADS, tq, LANE), F32), pltpu.VMEM((N_HEADS, tq, LANE), F32),
                            pltpu.VMEM((tq, tk), F32), pltpu.VMEM((tq, LANE), F32)]),
        out_shape=jax.ShapeDtypeStruct((bsz, t, ATTN_WIDTH), F32),
        compiler_params=_params(ARB2),
        name="dsa_masked_attention",
    )(qbt, ktt, flags, q16, ktr, vx, keys, tau, quota, bias[:, :, :tq, :])


def _band_kernel(q_ref, kp_ref, kc_ref, vp_ref, vc_ref, bias_ref, sink_ref, o_ref, *, tq, first_has_no_prev):
    group = N_HEADS // KV_HEADS_B
    extra = None
    if first_has_no_prev:
        col = lax.broadcasted_iota(jnp.int32, (tq, 2 * LANE), 1)
        extra = jnp.where((pl.program_id(1) == 0) & (col < LANE), NEG_BIG, 0.0)
    for h in range(N_HEADS):
        j = h // group
        q = q_ref[0, h]
        s = jnp.concatenate([jnp.dot(q, kp_ref[0, j], preferred_element_type=F32),
                             jnp.dot(q, kc_ref[0, j], preferred_element_type=F32)], axis=1) + bias_ref[h]
        if extra is not None:
            s = s + extra
        sink = sink_ref[h]
        m = jnp.maximum(jnp.max(s, axis=1, keepdims=True), sink[:, :1])
        e = jnp.exp(s - m)
        p = e / (jnp.sum(e, axis=1, keepdims=True) + jnp.exp(sink[:, :1] - m))
        pb = p.astype(BF16)
        o_ref[0, :, h * HEAD_DIM:(h + 1) * HEAD_DIM] = (
            jnp.dot(pb[:, :LANE], vp_ref[0, j], preferred_element_type=F32)
            + jnp.dot(pb[:, LANE:], vc_ref[0, j], preferred_element_type=F32))


def _band_bias(rel_table):
    r = jnp.arange(LANE, dtype=jnp.int32)[:, None]
    s = jnp.arange(2 * LANE, dtype=jnp.int32)[None, :]
    wc, qh = s // CHUNK, r // CHUNK
    band = (wc >= qh) & (wc <= qh + 2)
    tiles = _rel_tiles(rel_table, (s - LANE - r)[None])[0]
    return jnp.where(band[None], tiles, NEG_BIG)


def _band_attention(q16, kp, kc, vp, vc, bias, sinks, *, tq, prev_map, cur_map, first_has_no_prev):
    bsz, t = q16.shape[0], q16.shape[2]
    kblk = lambda m: pl.BlockSpec((1, KV_HEADS_B, HEAD_DIM, LANE), lambda b, i: (b, 0, 0, m(i)))
    vblk = lambda m: pl.BlockSpec((1, KV_HEADS_B, LANE, HEAD_DIM), lambda b, i: (b, 0, m(i), 0))
    sink_rows = jnp.broadcast_to(sinks.astype(F32)[:, None, None], (N_HEADS, 1, LANE))
    return pl.pallas_call(
        functools.partial(_band_kernel, tq=tq, first_has_no_prev=first_has_no_prev),
        grid=(bsz, t // tq),
        in_specs=[pl.BlockSpec((1, N_HEADS, tq, HEAD_DIM), lambda b, i: (b, 0, i, 0)),
                  kblk(prev_map), kblk(cur_map), vblk(prev_map), vblk(cur_map),
                  pl.BlockSpec((N_HEADS, tq, 2 * LANE), lambda b, i: (0, 0, 0)),
                  pl.BlockSpec((N_HEADS, 1, LANE), lambda b, i: (0, 0, 0))],
        out_specs=pl.BlockSpec((1, tq, ATTN_WIDTH), lambda b, i: (b, i, 0)),
        out_shape=jax.ShapeDtypeStruct((bsz, t, ATTN_WIDTH), F32),
        compiler_params=_params(ARB2, vmem=False),
        name="band_attention",
    )(q16, kp, kc, vp, vc, bias[:, :tq, :], sink_rows)


def _mixer_residual(x_ref, o_ref, wo_ref, gt1_ref):
    return x_ref[0] + gt1_ref[0] * jnp.dot(o_ref[0].astype(BF16), wo_ref[...], preferred_element_type=F32)


def _swiglu_chunk(h, wg, wu, wd):
    a = jax.nn.silu(jnp.dot(h, wg, preferred_element_type=F32)) * jnp.dot(h, wu, preferred_element_type=F32)
    return jnp.dot(a.astype(BF16), wd, preferred_element_type=F32)


def _ffn_kernel(x_ref, o_ref, wo_ref, gt1_ref, g_ref, sh_ref, sc_ref, gt2_ref, wg_ref, wu_ref, wd_ref,
                out_ref, xn_scr, h_scr, acc_scr):
    c = pl.program_id(2)

    @pl.when(c == 0)
    def _():
        xn = _mixer_residual(x_ref, o_ref, wo_ref, gt1_ref)
        xn_scr[...] = xn
        h_scr[...] = _modulate(xn, g_ref[...], sh_ref[0], sc_ref[0]).astype(BF16)
        acc_scr[...] = jnp.zeros(acc_scr.shape, F32)

    acc_scr[...] += _swiglu_chunk(h_scr[...], wg_ref[...], wu_ref[...], wd_ref[...])

    @pl.when(c == pl.num_programs(2) - 1)
    def _():
        out_ref[0] = xn_scr[...] + gt2_ref[0] * acc_scr[...]


def _moe_pre_kernel(x_ref, o_ref, wo_ref, gt1_ref, g_ref, sh_ref, sc_ref, wr_ref, xn_ref, h_ref, gate_ref, sel_ref):
    xn = _mixer_residual(x_ref, o_ref, wo_ref, gt1_ref)
    xn_ref[0] = xn
    h = _modulate(xn, g_ref[...], sh_ref[0], sc_ref[0])
    h_ref[0] = h
    lane = lax.broadcasted_iota(jnp.int32, (h.shape[0], LANE), 1)
    lanef = lane.astype(F32)
    logits = jnp.dot(h.astype(BF16), wr_ref[...], preferred_element_type=F32)
    logits = jnp.where(lane < N_EXPERTS, logits, -jnp.inf)
    m1 = jnp.max(logits, axis=1, keepdims=True)
    i1 = jnp.min(jnp.where(logits == m1, lanef, float(LANE)), axis=1, keepdims=True)
    rest = jnp.where(lanef == i1, -jnp.inf, logits)
    m2 = jnp.max(rest, axis=1, keepdims=True)
    i2 = jnp.min(jnp.where(rest == m2, lanef, float(LANE)), axis=1, keepdims=True)
    e2 = jnp.exp(m2 - m1)
    den = 1.0 + e2
    gates = jnp.where(lanef == i1, 1.0 / den, 0.0) + jnp.where(lanef == i2, e2 / den, 0.0)
    sel = jnp.where(lanef == i1, 1.0, 0.0) + jnp.where(lanef == i2, 1.0, 0.0)
    gate_ref[0] = gates[:, :N_EXPERTS]
    sel_ref[0] = sel[:, :N_EXPERTS]


def _moe_expert_kernel(te_tab, tv_tab, src_hbm, dst_hbm, h_hbm, wg_ref, wu_ref, wd_ref, y_hbm,
                       src_smem, dst_smem, hbuf, hb, acc, ybuf, sem_idx, sem_in, sem_out, *, rows):
    i, c = pl.program_id(0), pl.program_id(1)
    n_tiles, last = pl.num_programs(0), pl.num_programs(1) - 1
    slot = i % 2
    valid = tv_tab[i] != 0
    nxt = jnp.minimum(i + 1, n_tiles - 1)
    valid_next = (i + 1 < n_tiles) & (tv_tab[nxt] != 0)

    def start_rows(tile, sl):
        idx_copies = [pltpu.make_async_copy(src_hbm.at[tile], src_smem.at[sl], sem_idx.at[0]),
                      pltpu.make_async_copy(dst_hbm.at[tile], dst_smem.at[sl], sem_idx.at[1])]
        for cp in idx_copies:
            cp.start()
        for cp in idx_copies:
            cp.wait()

        def gather_rows(r0, carry):
            for u in range(DMA_BURST):
                r = r0 * DMA_BURST + u
                pltpu.make_async_copy(h_hbm.at[pl.ds(src_smem[sl, r], 1), :], hbuf.at[sl, pl.ds(r, 1), :],
                                      sem_in.at[sl]).start(priority=u % 2)
            return carry
        lax.fori_loop(0, rows // DMA_BURST, gather_rows, 0)

    all_rows_out = pltpu.make_async_copy(ybuf, y_hbm.at[pl.ds(0, rows), :], sem_out)

    @pl.when(valid & (c == 0))
    def _():
        @pl.when(i == 0)
        def _():
            start_rows(0, 0)
        pltpu.make_async_copy(h_hbm.at[pl.ds(0, rows), :], hbuf.at[slot], sem_in.at[slot]).wait()
        hb[...] = hbuf[slot].astype(BF16)
        acc[...] = jnp.zeros(acc.shape, F32)

    @pl.when(valid_next & (c == 1))
    def _():
        start_rows(nxt, 1 - slot)

    @pl.when(valid)
    def _():
        acc[...] += _swiglu_chunk(hb[...], wg_ref[0], wu_ref[0], wd_ref[0])

    @pl.when(valid & (c == last))
    def _():
        @pl.when(i > 0)
        def _():
            all_rows_out.wait()
        ybuf[...] = acc[...]

        def scatter_rows(r0, carry):
            for u in range(DMA_BURST):
                r = r0 * DMA_BURST + u
                pltpu.make_async_copy(ybuf.at[pl.ds(r, 1), :], y_hbm.at[pl.ds(dst_smem[slot, r], 1), :],
                                      sem_out).start(priority=u % 2)
            return carry
        lax.fori_loop(0, rows // DMA_BURST, scatter_rows, 0)

        @pl.when(jnp.logical_not(valid_next))
        def _():
            all_rows_out.wait()


def _moe_combine_kernel(xn_ref, gt2_ref, g2_ref, ya_ref, yb_ref, out_ref):
    g2 = g2_ref[0]
    out_ref[0] = xn_ref[0] + gt2_ref[0] * (g2[:, 0:1] * ya_ref[...] + g2[:, 1:2] * yb_ref[...])


def _token_specs(tm, nd):
    pick = (lambda f: lambda b, i: f(b, i)) if nd == 2 else (lambda f: lambda b, i, c: f(b, i))
    tok = pl.BlockSpec((1, tm, D_MODEL), pick(lambda b, i: (b, i, 0)))

    def mod_spec(a):
        return pl.BlockSpec((1, 1, D_MODEL), pick(lambda b, i: (b, 0, 0))) if a.shape[1] == 1 else tok

    cst = lambda shape: pl.BlockSpec(shape, pick(lambda b, i: (0, 0)))
    return tok, mod_spec, cst


def _swiglu_mixer(x, o, w_out, gt1, g, shift, scale, gt2, w_gate, w_up, w_down, *, tm):
    gsz, t, _ = x.shape
    d_ff = w_gate.shape[-1]
    assert t % tm == 0 and d_ff % FF_TILE == 0
    tok, mod_spec, cst = _token_specs(tm, 3)
    return pl.pallas_call(
        _ffn_kernel, grid=(gsz, t // tm, d_ff // FF_TILE),
        in_specs=[tok, tok, cst((ATTN_WIDTH, D_MODEL)), mod_spec(gt1), cst((1, D_MODEL)),
                  mod_spec(shift), mod_spec(scale), mod_spec(gt2),
                  pl.BlockSpec((D_MODEL, FF_TILE), lambda b, i, c: (0, c)),
                  pl.BlockSpec((D_MODEL, FF_TILE), lambda b, i, c: (0, c)),
                  pl.BlockSpec((FF_TILE, D_MODEL), lambda b, i, c: (c, 0))],
        out_specs=tok,
        out_shape=jax.ShapeDtypeStruct((gsz, t, D_MODEL), F32),
        scratch_shapes=[pltpu.VMEM((tm, D_MODEL), F32), pltpu.VMEM((tm, D_MODEL), BF16),
                        pltpu.VMEM((tm, D_MODEL), F32)],
        compiler_params=_params(("arbitrary",) * 3),
        name="swiglu_channel_mixer",
    )(x, o, w_out.astype(BF16), gt1, g[None, :], shift, scale, gt2,
      w_gate.astype(BF16), w_up.astype(BF16), w_down.astype(BF16))


def _moe_pre(x, o, w_out, gt1, g, shift, scale, w_router, *, tm):
    gsz, t, _ = x.shape
    assert t % tm == 0
    tok, mod_spec, cst = _token_specs(tm, 2)
    small = pl.BlockSpec((1, tm, N_EXPERTS), lambda b, i: (b, i, 0))
    wr = jnp.pad(w_router.astype(BF16), ((0, 0), (0, LANE - N_EXPERTS)))
    return pl.pallas_call(
        _moe_pre_kernel, grid=(gsz, t // tm),
        in_specs=[tok, tok, cst((ATTN_WIDTH, D_MODEL)), mod_spec(gt1), cst((1, D_MODEL)),
                  mod_spec(shift), mod_spec(scale), cst((D_MODEL, LANE))],
        out_specs=[tok, tok, small, small],
        out_shape=[jax.ShapeDtypeStruct((gsz, t, D_MODEL), F32), jax.ShapeDtypeStruct((gsz, t, D_MODEL), F32),
                   jax.ShapeDtypeStruct((gsz, t, N_EXPERTS), F32), jax.ShapeDtypeStruct((gsz, t, N_EXPERTS), F32)],
        compiler_params=_params(ARB2),
        name="moe_pre_router",
    )(x, o, w_out.astype(BF16), gt1, g[None, :], shift, scale, wr)


def _moe_routing(gates, sel, rows):
    n = gates.shape[0]
    member = sel > 0
    mi = member.astype(jnp.int32)
    counts = mi.sum(0)
    padded = (counts + rows - 1) // rows * rows
    ends = jnp.cumsum(padded)
    slot = jnp.cumsum(mi, axis=1) - mi
    expert_of = [jnp.sum(jnp.where(member & (slot == k), jnp.arange(N_EXPERTS)[None, :], 0), axis=1)
                 for k in range(TOP_K)]
    g2 = jnp.stack([jnp.sum(jnp.where(member & (slot == k), gates, 0.0), axis=1) for k in range(TOP_K)], axis=1)
    n_rows = (2 * n + N_EXPERTS * rows) // rows * rows
    n_pad = n_rows - 2 * n
    pad_ends = jnp.cumsum(padded - counts)
    pad_expert = (jnp.arange(n_pad, dtype=jnp.int32)[:, None] >= pad_ends[None, :]).sum(1)
    tok = jnp.arange(n, dtype=jnp.int32)
    sort_key = jnp.concatenate([expert_of[0], expert_of[1], pad_expert]).astype(jnp.int32)
    pair = jnp.concatenate([tok, n + tok, jnp.full((n_pad,), -1, jnp.int32)])
    _, pair = lax.sort((sort_key, pair), num_keys=1, is_stable=True)
    spare = 2 * n + jnp.arange(n_rows, dtype=jnp.int32) % rows
    src = jnp.where(pair >= 0, pair % n, 0)
    dst = jnp.where(pair >= 0, pair, spare)
    n_tiles = n_rows // rows
    tile_start = jnp.arange(n_tiles, dtype=jnp.int32) * rows
    tile_expert = jnp.minimum((tile_start[:, None] >= ends[None, :]).sum(1), N_EXPERTS - 1).astype(jnp.int32)
    tile_valid = (tile_start < ends[-1]).astype(jnp.int32)
    return src.reshape(n_tiles, rows), dst.reshape(n_tiles, rows), tile_expert, tile_valid, g2


def _moe_experts(h_all, src, dst, tile_expert, tile_valid, w_gate, w_up, w_down, *, rows):
    n = h_all.shape[0]
    n_tiles = src.shape[0]
    d_ff = w_gate.shape[-1]
    assert d_ff // FF_TILE >= 2
    any_spec = pl.BlockSpec(memory_space=pl.ANY)
    return pl.pallas_call(
        functools.partial(_moe_expert_kernel, rows=rows),
        grid_spec=pltpu.PrefetchScalarGridSpec(
            num_scalar_prefetch=2, grid=(n_tiles, d_ff // FF_TILE),
            in_specs=[any_spec, any_spec, any_spec,
                      pl.BlockSpec((1, D_MODEL, FF_TILE), lambda i, c, te, tv: (te[i], 0, c)),
                      pl.BlockSpec((1, D_MODEL, FF_TILE), lambda i, c, te, tv: (te[i], 0, c)),
                      pl.BlockSpec((1, FF_TILE, D_MODEL), lambda i, c, te, tv: (te[i], c, 0))],
            out_specs=any_spec,
            scratch_shapes=[pltpu.SMEM((2, rows), jnp.int32), pltpu.SMEM((2, rows), jnp.int32),
                            pltpu.VMEM((2, rows, D_MODEL), F32), pltpu.VMEM((rows, D_MODEL), BF16),
                            pltpu.VMEM((rows, D_MODEL), F32), pltpu.VMEM((rows, D_MODEL), F32),
                            pltpu.SemaphoreType.DMA((2,)), pltpu.SemaphoreType.DMA((2,)),
                            pltpu.SemaphoreType.DMA(())]),
        out_shape=jax.ShapeDtypeStruct((2 * n + rows, D_MODEL), F32),
        compiler_params=_params(ARB2),
        name="moe_routed_experts",
    )(tile_expert, tile_valid, src, dst, h_all, w_gate.astype(BF16), w_up.astype(BF16), w_down.astype(BF16))


def _moe_combine(xn, gt2, g2, y, *, tm, n_all, tok0):
    gsz, t, _ = xn.shape
    assert t % tm == 0 and n_all % tm == 0 and tok0 % tm == 0
    tok, mod_spec, _ = _token_specs(tm, 2)
    nt = t // tm
    yspec = lambda slot: pl.BlockSpec((tm, D_MODEL), lambda b, i: ((slot * n_all + tok0) // tm + b * nt + i, 0))
    return pl.pallas_call(
        _moe_combine_kernel, grid=(gsz, nt),
        in_specs=[tok, mod_spec(gt2), pl.BlockSpec((1, tm, TOP_K), lambda b, i: (b, i, 0)), yspec(0), yspec(1)],
        out_specs=tok,
        out_shape=jax.ShapeDtypeStruct((gsz, t, D_MODEL), F32),
        compiler_params=_params(ARB2),
        name="moe_combine",
    )(xn, gt2, g2, y, y)


def _per_token(a, n):
    return jnp.broadcast_to(a, (a.shape[0], n, a.shape[2])).reshape(1, a.shape[0] * n, a.shape[2])


def _pad_keys(a, axis, l_pad):
    pad = [(0, 0)] * a.ndim
    pad[axis] = (0, l_pad - a.shape[axis])
    return jnp.pad(a, pad)


def _layer_a(xp, xs, cp, cs, ck, cv, cki, rel_bias, norm_g, w_ada, b_ada, w_in, w_out, g_q, g_k, g_kidx,
             ffn_wg, ffn_wu, ffn_wd):
    bias = _dsa_bias_tiles(rel_bias)
    nb, n = xs.shape[0], xs.shape[1]
    past = ck.shape[1]
    outs = []
    for x, c, sample in ((xp, cp, False), (xs, cs, True)):
        sh1, sc1, gt1, sh2, sc2, gt2 = adaln(c, w_ada, b_ada)
        q16, kf, vf, ktr, vx, kif, qi16, kit, wsc = _project(
            x, norm_g[0], sh1, sc1, w_in, g_q, g_k, g_kidx, n_kv=KV_HEADS_A, has_idx=True,
            tm=n if sample else 256)
        if sample:
            l_true, l_pad = past + n, 2304
            kfull = jnp.concatenate([ck, kf.reshape(nb, n, KV_HEADS_A, HEAD_DIM)], axis=1).astype(BF16)
            vfull = jnp.concatenate([cv, vf.reshape(nb, n, KV_HEADS_A, HEAD_DIM)], axis=1).astype(BF16)
            kifull = jnp.concatenate([cki, kif], axis=1).astype(BF16)
            ktr = _pad_keys(kfull, 1, l_pad).transpose(0, 2, 3, 1)
            vx = jnp.concatenate([vfull, jnp.ones(vfull.shape[:3] + (1,), BF16),
                                  jnp.zeros(vfull.shape[:3] + (LANE - HEAD_DIM - 1,), BF16)], axis=-1)
            vx = _pad_keys(vx, 1, l_pad).transpose(0, 2, 1, 3)
            kit = _pad_keys(kifull, 1, l_pad).transpose(0, 2, 1)
            o = _dsa_attention(q16, ktr, vx, qi16, kit, wsc, bias, p0=past, l_true=l_true,
                               n_sel=min(TOPK_MAX, l_true // 4), tq=n, tk=l_pad, sw=256, cw=256)
            flat = lambda a: a.reshape(1, nb * n, a.shape[-1])
            y = _swiglu_mixer(flat(x), flat(o), w_out, _per_token(gt1, n), norm_g[1], _per_token(sh2, n),
                              _per_token(sc2, n), _per_token(gt2, n), ffn_wg, ffn_wu, ffn_wd, tm=MIX_TM_SAMPLE)
            y = y.reshape(x.shape)
        else:
            t = x.shape[1]
            o = _dsa_attention(q16, ktr, vx, qi16, kit, wsc, bias, p0=0, l_true=t,
                               n_sel=min(TOPK_MAX, t // 4), tq=LANE, tk=1024, sw=256, cw=1024)
            y = _swiglu_mixer(x, o, w_out, gt1, norm_g[1], sh2, sc2, gt2, ffn_wg, ffn_wu, ffn_wd, tm=MIX_TM_PROMPT)
        bsz, t = x.shape[0], x.shape[1]
        outs.append((y, kf.reshape(bsz, t, KV_HEADS_A, HEAD_DIM), vf.reshape(bsz, t, KV_HEADS_A, HEAD_DIM), kif))
    return outs


def _layer_b(xp, xs, cp, cs, ck, cv, rel_bias, norm_g, w_ada, b_ada, w_in, w_out, g_q, g_k, sinks,
             w_router, moe_wg, moe_wu, moe_wd):
    bias = _band_bias(rel_bias)
    nb, n = xs.shape[0], xs.shape[1]
    streams = []
    for x, c, sample in ((xp, cp, False), (xs, cs, True)):
        sh1, sc1, gt1, sh2, sc2, gt2 = adaln(c, w_ada, b_ada)
        q16, kf, vf, ktr, v16 = _project(x, norm_g[0], sh1, sc1, w_in, g_q, g_k, g_k, n_kv=KV_HEADS_B,
                                         has_idx=False, tm=n if sample else 256)
        bsz, t = x.shape[0], x.shape[1]
        k4 = kf.reshape(bsz, t, KV_HEADS_B, HEAD_DIM)
        v4 = vf.reshape(bsz, t, KV_HEADS_B, HEAD_DIM)
        zero = lambda i: 0
        if sample:
            kp = ck.astype(BF16).transpose(0, 2, 3, 1)
            vp = cv.astype(BF16).transpose(0, 2, 1, 3)
            o = _band_attention(q16, kp, _pad_keys(ktr, 3, LANE), vp, _pad_keys(v16, 2, LANE), bias, sinks,
                                tq=n, prev_map=zero, cur_map=zero, first_has_no_prev=False)
            flat = lambda a: a.reshape(1, nb * n, a.shape[-1])
            xn, h, gates, sel = _moe_pre(flat(x), flat(o), w_out, _per_token(gt1, n), norm_g[1],
                                         _per_token(sh2, n), _per_token(sc2, n), w_router, tm=MIX_TM_SAMPLE)
            gt2 = _per_token(gt2, n)
            k_new = jnp.concatenate([ck, k4], axis=1)[:, -WINDOW:]
            v_new = jnp.concatenate([cv, v4], axis=1)[:, -WINDOW:]
        else:
            o = _band_attention(q16, ktr, ktr, v16, v16, bias, sinks, tq=LANE,
                                prev_map=lambda i: jnp.maximum(i - 1, 0), cur_map=lambda i: i,
                                first_has_no_prev=True)
            xn, h, gates, sel = _moe_pre(x, o, w_out, gt1, norm_g[1], sh2, sc2, w_router, tm=MIX_TM_PROMPT)
            k_new, v_new = k4[:, -WINDOW:], v4[:, -WINDOW:]
        streams.append((x.shape, xn, h, gates, sel, gt2, k_new, v_new))

    rows2d = lambda a: a.reshape(-1, a.shape[-1])
    h_all = jnp.concatenate([rows2d(st[2]) for st in streams])
    n_all = h_all.shape[0]
    src, dst, tile_expert, tile_valid, g2 = _moe_routing(
        jnp.concatenate([rows2d(st[3]) for st in streams]), jnp.concatenate([rows2d(st[4]) for st in streams]),
        MOE_ROW_TILE)
    y = _moe_experts(h_all, src, dst, tile_expert, tile_valid, moe_wg, moe_wu, moe_wd, rows=MOE_ROW_TILE)
    outs, tok0 = [], 0
    for shape, xn, _, _, _, gt2, k_new, v_new in streams:
        cnt = xn.shape[0] * xn.shape[1]
        out = _moe_combine(xn, gt2, g2[tok0:tok0 + cnt].reshape(xn.shape[0], xn.shape[1], TOP_K), y,
                           tm=MOE_COMBINE_TM, n_all=n_all, tok0=tok0)
        outs.append((out.reshape(shape), k_new, v_new))
        tok0 += cnt
    return outs


def kernel(x_prompt, x_sample, c_prompt, c_sample, cache_a_k, cache_a_v, cache_a_kidx,
           cache_b_k, cache_b_v, rel_bias, norm_g, w_ada, b_ada,
           a_w_in, a_w_out, a_g_q, a_g_k, a_g_kidx,
           b_w_in, b_w_out, b_g_q, b_g_k, b_sinks,
           ffn_w_gate, ffn_w_up, ffn_w_down,
           moe_w_router, moe_w_gate, moe_w_up, moe_w_down):
    xp, xs = x_prompt, x_sample
    a_out = [[] for _ in range(6)]
    b_out = [[] for _ in range(4)]
    for i in range(DEPTH):
        j = i // 2
        if i % 2 == 0:
            (xp, k1, v1, i1), (xs, k2, v2, i2) = _layer_a(
                xp, xs, c_prompt, c_sample, cache_a_k[j], cache_a_v[j], cache_a_kidx[j], rel_bias, norm_g[i],
                w_ada[i], b_ada[i], a_w_in[j], a_w_out[j], a_g_q[j], a_g_k[j], a_g_kidx[j],
                ffn_w_gate[j], ffn_w_up[j], ffn_w_down[j])
            for lst, val in zip(a_out, (k1, v1, i1, k2, v2, i2)):
                lst.append(val)
        else:
            (xp, k1, v1), (xs, k2, v2) = _layer_b(
                xp, xs, c_prompt, c_sample, cache_b_k[j], cache_b_v[j], rel_bias, norm_g[i],
                w_ada[i], b_ada[i], b_w_in[j], b_w_out[j], b_g_q[j], b_g_k[j], b_sinks[j],
                moe_w_router[j], moe_w_gate[j], moe_w_up[j], moe_w_down[j])
            for lst, val in zip(b_out, (k1, v1, k2, v2)):
                lst.append(val)
    return (xp, xs, *(jnp.stack(l) for l in a_out), *(jnp.stack(l) for l in b_out))
```

```python
import math, functools
import jax, jax.numpy as jnp
from jax import lax
import numpy as np
from jax.experimental import pallas as pl
from jax.experimental.pallas import tpu as pltpu

D_MODEL = 1024
DEPTH = 2
CHUNK = 64
N_HEADS = 16
HEAD_DIM = 64
ATTN_WIDTH = N_HEADS * HEAD_DIM
KV_HEADS_A = 4
IDX_HEADS = 8
IDX_DIM = 64
TOPK_MAX = 256
KV_HEADS_B = 2
WINDOW = 128
NUM_BUCKETS = 32
MAX_DISTANCE = 1024
N_EXPERTS = 8
TOP_K = 2
EPS = 1e-6

F32 = jnp.float32
BF16 = jnp.bfloat16
LANE = 128
VMEM_LIMIT_BYTES = 48 * 1024 * 1024
LOG2E = math.log2(math.e)
NEG_BIG = -1e30
KEY_MIN = -2 ** 31
FAR_BUCKET = NUM_BUCKETS // 2 - 1
NEAR_BLOCKS = 6
FF_TILE = 512
MIX_TM_PROMPT = 1024
MIX_TM_SAMPLE = 256
MOE_ROW_TILE = 512
MOE_COMBINE_TM = 256
DMA_BURST = 8
ARB2 = ("arbitrary", "arbitrary")


def _params(sem, vmem=True):
    return pltpu.CompilerParams(dimension_semantics=sem, vmem_limit_bytes=VMEM_LIMIT_BYTES if vmem else None)


def rel_bucket(rel):
    half = NUM_BUCKETS // 2
    max_exact = half // 2
    base = jnp.where(rel > 0, half, 0)
    n = jnp.abs(rel)
    nf = jnp.maximum(n, 1).astype(F32)
    large = max_exact + (jnp.log(nf / max_exact) / math.log(MAX_DISTANCE / max_exact)
                         * (half - max_exact)).astype(jnp.int32)
    large = jnp.minimum(large, half - 1)
    return base + jnp.where(n < max_exact, n, large)


def _adaln_kernel(c_ref, w_ref, b_ref, o_ref):
    c = c_ref[...]
    o_ref[...] = jnp.dot(c * jax.nn.sigmoid(c), w_ref[...], preferred_element_type=F32) + b_ref[...]


def adaln(c, w, b):
    n = c.shape[0]
    mod = pl.pallas_call(
        _adaln_kernel,
        grid=(6,),
        in_specs=[pl.BlockSpec((n, D_MODEL), lambda j: (0, 0)),
                  pl.BlockSpec((D_MODEL, D_MODEL), lambda j: (0, j)),
                  pl.BlockSpec((1, D_MODEL), lambda j: (0, j))],
        out_specs=pl.BlockSpec((n, D_MODEL), lambda j: (0, j)),
        out_shape=jax.ShapeDtypeStruct((n, 6 * D_MODEL), F32),
        name="adaln",
    )(c, w, b[None, :])
    return jnp.split(mod[:, None, :], 6, axis=-1)


def _modulate(x, g, shift, scale):
    y = x * lax.rsqrt(jnp.mean(x * x, axis=-1, keepdims=True) + EPS)
    return (y * g) * (1.0 + scale) + shift


def _seg_norm(seg, g):
    return seg * lax.rsqrt(jnp.mean(seg * seg, axis=-1, keepdims=True) + EPS) * g


def _proj_kernel(x_ref, g_ref, sh_ref, sc_ref, w_ref, gq_ref, gk_ref, gki_ref, *outs, n_kv, has_idx):
    h = _modulate(x_ref[0], g_ref[...], sh_ref[0], sc_ref[0]).astype(BF16)
    y = jnp.dot(h, w_ref[...], preferred_element_type=F32)
    tm = y.shape[0]
    kv_w = n_kv * HEAD_DIM
    if has_idx:
        q16, kf, vf, ktr, vx, kif, qi16, kit, wsc = outs
    else:
        q16, kf, vf, ktr, v16 = outs
    for hd in range(N_HEADS):
        seg = y[:, hd * HEAD_DIM:(hd + 1) * HEAD_DIM]
        q_scale = HEAD_DIM ** -0.5 * (LOG2E if has_idx else 1.0)
        q16[0, hd] = (_seg_norm(seg, gq_ref[...]) * q_scale).astype(BF16)
    lane = lax.broadcasted_iota(jnp.int32, (tm, LANE - HEAD_DIM), 1)
    ones_col = jnp.where(lane == 0, 1.0, 0.0).astype(BF16)
    for j in range(n_kv):
        kseg = _seg_norm(y[:, ATTN_WIDTH + j * HEAD_DIM:ATTN_WIDTH + (j + 1) * HEAD_DIM], gk_ref[...])
        vseg = y[:, ATTN_WIDTH + kv_w + j * HEAD_DIM:ATTN_WIDTH + kv_w + (j + 1) * HEAD_DIM]
        kf[0, :, j * HEAD_DIM:(j + 1) * HEAD_DIM] = kseg
        vf[0, :, j * HEAD_DIM:(j + 1) * HEAD_DIM] = vseg
        ktr[0, j] = kseg.T.astype(BF16)
        if has_idx:
            vx[0, j] = jnp.concatenate([vseg.astype(BF16), ones_col], axis=1)
        else:
            v16[0, j] = vseg.astype(BF16)
    if has_idx:
        base = ATTN_WIDTH + 2 * kv_w
        for hd in range(IDX_HEADS):
            qi16[0, hd] = y[:, base + hd * IDX_DIM:base + (hd + 1) * IDX_DIM].astype(BF16)
        base += IDX_HEADS * IDX_DIM
        kiseg = _seg_norm(y[:, base:base + IDX_DIM], gki_ref[...])
        kif[0] = kiseg
        kit[0] = kiseg.T.astype(BF16)
        wsc[0] = y[:, base + IDX_DIM:base + IDX_DIM + IDX_HEADS] * IDX_HEADS ** -0.5 * IDX_DIM ** -0.5


def _project(x, g, shift, scale, w_in, g_q, g_k, g_kidx, *, n_kv, has_idx, tm):
    gsz, t, _ = x.shape
    n_in = w_in.shape[1]
    kv_w = n_kv * HEAD_DIM
    tok = lambda b, i: (b, i, 0)
    head = lambda b, i: (b, 0, i, 0)
    vec = lambda b, i: (b, 0, 0)
    cst = lambda b, i: (0, 0)
    shapes = [((gsz, N_HEADS, t, HEAD_DIM), BF16, (1, N_HEADS, tm, HEAD_DIM), head),
              ((gsz, t, kv_w), F32, (1, tm, kv_w), tok),
              ((gsz, t, kv_w), F32, (1, tm, kv_w), tok),
              ((gsz, n_kv, HEAD_DIM, t), BF16, (1, n_kv, HEAD_DIM, tm), lambda b, i: (b, 0, 0, i))]
    if has_idx:
        shapes += [((gsz, n_kv, t, LANE), BF16, (1, n_kv, tm, LANE), head),
                   ((gsz, t, IDX_DIM), F32, (1, tm, IDX_DIM), tok),
                   ((gsz, IDX_HEADS, t, IDX_DIM), BF16, (1, IDX_HEADS, tm, IDX_DIM), head),
                   ((gsz, IDX_DIM, t), BF16, (1, IDX_DIM, tm), lambda b, i: (b, 0, i)),
                   ((gsz, t, IDX_HEADS), F32, (1, tm, IDX_HEADS), tok)]
    else:
        shapes += [((gsz, n_kv, t, HEAD_DIM), BF16, (1, n_kv, tm, HEAD_DIM), head)]
    return pl.pallas_call(
        functools.partial(_proj_kernel, n_kv=n_kv, has_idx=has_idx),
        grid=(gsz, t // tm),
        in_specs=[pl.BlockSpec((1, tm, D_MODEL), tok), pl.BlockSpec((1, D_MODEL), cst),
                  pl.BlockSpec((1, 1, D_MODEL), vec), pl.BlockSpec((1, 1, D_MODEL), vec),
                  pl.BlockSpec((D_MODEL, n_in), cst), pl.BlockSpec((1, HEAD_DIM), cst),
                  pl.BlockSpec((1, HEAD_DIM), cst), pl.BlockSpec((1, IDX_DIM), cst)],
        out_specs=[pl.BlockSpec(blk, im) for _, _, blk, im in shapes],
        out_shape=[jax.ShapeDtypeStruct(s, d) for s, d, _, _ in shapes],
        compiler_params=_params(ARB2),
        name="mixer_in_proj",
    )(x, g[None, :], shift, scale, w_in.astype(BF16), g_q[None, :], g_k[None, :], g_kidx[None, :])


def _tri_steps(nqb, tq, tk, p0):
    qbs, kts = [], []
    for qb in range(nqb):
        vis_end = ((p0 + qb * tq + tq - 1) // CHUNK + 1) * CHUNK
        for kt in range(-(-vis_end // tk)):
            qbs.append(qb)
            kts.append(kt)
    return jnp.asarray(np.array(qbs, np.int32)), jnp.asarray(np.array(kts, np.int32))


def _idx_kernel(qb_tab, kt_tab, qi_ref, w_ref, kit_ref, out_ref, *, tq, tk, sw, p0, l_true):
    step = pl.program_id(1)
    q0 = p0 + qb_tab[step] * tq
    k0 = kt_tab[step] * tk
    row = lax.broadcasted_iota(jnp.int32, (tq, 1), 0) + q0
    vis_end = jnp.minimum(((row >> 6) + 1) << 6, l_true)
    wv = w_ref[0]
    for c in range(tk // sw):
        kit = kit_ref[0, :, c * sw:(c + 1) * sw]
        acc = jnp.zeros((tq, sw), F32)
        for h in range(IDX_HEADS):
            sh = jnp.dot(qi_ref[0, h], kit, preferred_element_type=F32)
            acc = acc + wv[:, h:h + 1] * jnp.maximum(sh, 0.0)
        bits = lax.bitcast_convert_type(acc, jnp.int32)
        key = bits ^ ((bits >> 31) & 0x7FFFFFFF)
        kpos = k0 + c * sw + lax.broadcasted_iota(jnp.int32, (tq, sw), 1)
        out_ref[0, :, c * sw:(c + 1) * sw] = jnp.where(kpos < vis_end, key, KEY_MIN)


def _thr_kernel(keys_ref, tau_ref, quota_ref, flag_ref, *, tq, cw, p0, n_sel):
    qb = pl.program_id(1)
    vis_end = (((p0 + qb * tq + tq - 1) >> 6) + 1) << 6
    nch = (vis_end + cw - 1) // cw
    key_max = 2 ** 31 - 1

    def count_ge(cand):
        def body(j, acc):
            x = keys_ref[0, :, pl.ds(pl.multiple_of(j * cw, cw), cw)]
            for u in range(cw // LANE):
                acc = acc + jnp.where(x[:, u * LANE:(u + 1) * LANE] >= cand, 1.0, 0.0)
            return acc
        acc = lax.fori_loop(0, nch, body, jnp.zeros((tq, LANE), F32))
        return jnp.sum(acc, axis=1, keepdims=True)

    zero = jnp.zeros((tq, LANE), jnp.int32)
    c0 = count_ge(zero)
    tau = jnp.where(c0 >= n_sel, zero, KEY_MIN)
    n_ge = jnp.where(c0 >= n_sel, c0, float(2 ** 24))

    def bit_body(carry):
        i, tau, n_ge = carry
        cand = tau + jnp.left_shift(jnp.int32(1), 30 - i)
        c = count_ge(cand)
        return i + 1, jnp.where(c >= n_sel, cand, tau), jnp.where(c >= n_sel, c, n_ge)

    def unresolved(carry):
        i, _, n_ge = carry
        return (i < 31) & (jnp.max(jnp.abs(n_ge - n_sel)) > 0.0)

    _, tau, n_ge = lax.while_loop(unresolved, bit_body, (jnp.int32(0), tau, n_ge))
    n_gt = jnp.where(tau[:, :1] < key_max, count_ge(jnp.minimum(tau, key_max - 1) + 1), 0.0)
    tau_ref[0] = jnp.maximum(tau, KEY_MIN + 1)
    quota_ref[0] = jnp.broadcast_to(n_sel - n_gt, (tq, LANE))
    over = jnp.max(jnp.where(n_ge > n_sel, 1, 0), axis=0, keepdims=True)
    flag_ref[0, 0] = jnp.broadcast_to(over, flag_ref.shape[2:])


def _attn_kernel(qb_tab, kt_tab, flag_tab, q_ref, kt_ref, vx_ref, keys_ref, tau_ref, quota_ref, bias_ref, o_ref,
                 m_ref, acc_ref, negm_ref, seen_ref, *, tq, tk, sw, p0, nqb):
    step = pl.program_id(1)
    qb, kt = qb_tab[step], kt_tab[step]
    q0 = p0 + qb * tq
    k0 = kt * tk
    vis_end = (((q0 + tq - 1) >> 6) + 1) << 6
    group = N_HEADS // KV_HEADS_A
    nblk = sw // LANE
    tie_w = 2 * LANE

    @pl.when(kt == 0)
    def _():
        m_ref[...] = jnp.full(m_ref.shape, NEG_BIG, F32)
        acc_ref[...] = jnp.zeros(acc_ref.shape, F32)
        seen_ref[...] = jnp.zeros(seen_ref.shape, F32)

    has_ties = flag_tab[pl.program_id(0) * nqb + qb] != 0
    tau = tau_ref[0]

    @pl.when(jnp.logical_not(has_ties))
    def _():
        for u in range(tk // LANE):
            cols = slice(u * LANE, (u + 1) * LANE)
            negm_ref[:, cols] = jnp.where(keys_ref[0, :, cols] >= tau, 0.0, NEG_BIG)

    @pl.when(has_ties)
    def _():
        tau2 = jnp.concatenate([tau] * (tie_w // LANE), axis=1)
        quota = jnp.concatenate([quota_ref[0]] * (tie_w // LANE), axis=1)
        before = (lax.broadcasted_iota(jnp.int32, (tie_w, tie_w), 0)
                  < lax.broadcasted_iota(jnp.int32, (tie_w, tie_w), 1))
        before = jnp.where(before, 1.0, 0.0).astype(BF16)
        seen = seen_ref[...]
        for u in range(tk // tie_w):
            cols = slice(u * tie_w, (u + 1) * tie_w)
            k = keys_ref[0, :, cols]
            tie = jnp.where(k == tau2, 1.0, 0.0)
            rank = (jnp.dot(tie.astype(BF16), before, preferred_element_type=F32)
                    + jnp.concatenate([seen] * (tie_w // LANE), axis=1))
            keep = jnp.where(k > tau2, 1.0, jnp.where(rank < quota, tie, 0.0))
            negm_ref[:, cols] = jnp.where(keep > 0.0, 0.0, NEG_BIG)
            seen = seen + jnp.sum(tie, axis=1, keepdims=True)
        seen_ref[...] = seen

    def sub_tile(c, with_bias):
        off = c * sw if isinstance(c, int) else pl.multiple_of(c * sw, sw)
        negm = negm_ref[:, pl.ds(off, sw)]
        if with_bias:
            d0 = ((k0 + off - q0) >> 7) + NEAR_BLOCKS
            bidx = [jnp.clip(d0 + u, 0, NEAR_BLOCKS) for u in range(nblk)]
        for h in range(N_HEADS):
            j = h // group
            s = jnp.dot(q_ref[0, h], kt_ref[0, j, :, pl.ds(off, sw)], preferred_element_type=F32) + negm
            if with_bias:
                s = s + jnp.concatenate([bias_ref[bidx[u], h] for u in range(nblk)], axis=1)
            m_prev = m_ref[h]
            m_cur = jnp.maximum(m_prev, jnp.max(s, axis=1, keepdims=True))
            alpha = jnp.exp2(m_prev - m_cur)
            p = jnp.exp2(s - jnp.concatenate([m_cur] * nblk, axis=1))
            pv = jnp.dot(p.astype(BF16), vx_ref[0, j, pl.ds(off, sw), :], preferred_element_type=F32)
            acc_ref[h] = alpha * acc_ref[h] + pv
            m_ref[h] = m_cur

    tile_far = k0 + tk <= q0 - (NEAR_BLOCKS - 1) * LANE

    @pl.when(tile_far)
    def _():
        for c in range(tk // sw):
            sub_tile(c, False)

    @pl.when(jnp.logical_not(tile_far))
    def _():
        nsub = jnp.minimum(tk // sw, (vis_end - k0 + sw - 1) // sw)
        nfar = jnp.clip((q0 - (NEAR_BLOCKS - 1) * LANE - k0) // sw, 0, nsub)

        def far_body(c, carry):
            sub_tile(c, False)
            return carry

        def near_body(c, carry):
            sub_tile(c, True)
            return carry

        lax.fori_loop(0, nfar, far_body, 0)
        lax.fori_loop(nfar, nsub, near_body, 0)

    @pl.when(k0 + tk >= vis_end)
    def _():
        for h in range(N_HEADS):
            a = acc_ref[h]
            o_ref[0, :, h * HEAD_DIM:(h + 1) * HEAD_DIM] = a[:, :HEAD_DIM] / a[:, HEAD_DIM:HEAD_DIM + 1]


def _rel_tiles(rel_table, rel):
    onehot = jax.nn.one_hot(rel_bucket(rel), NUM_BUCKETS, dtype=F32)
    return jnp.einsum("nrsk,kh->nhrs", onehot, rel_table.astype(F32), precision=lax.Precision.HIGHEST)


def _dsa_bias_tiles(rel_table):
    d = jnp.arange(NEAR_BLOCKS + 1, dtype=jnp.int32)[:, None, None] - NEAR_BLOCKS
    rel = (d * LANE + jnp.arange(LANE, dtype=jnp.int32)[None, None, :]
           - jnp.arange(LANE, dtype=jnp.int32)[None, :, None])
    tiles = (_rel_tiles(rel_table, rel) - rel_table[FAR_BUCKET][None, :, None, None]) * LOG2E
    return tiles.at[0].set(0.0)


def _dsa_attention(q16, ktr, vx, qi16, kit, w, bias, *, p0, l_true, n_sel, tq, tk, sw, cw):
    bsz, t = q16.shape[0], q16.shape[2]
    l_pad = ktr.shape[3]
    nqb = t // tq
    assert t % tq == 0 and p0 % LANE == 0 and (tq == LANE or nqb == 1)
    assert l_pad % tk == 0 and tk % sw == 0 and tk % cw == 0 and tk % (2 * LANE) == 0
    qbt, ktt = _tri_steps(nqb, tq, tk, p0)
    nsteps = int(qbt.shape[0])

    keys = pl.pallas_call(
        functools.partial(_idx_kernel, tq=tq, tk=tk, sw=sw, p0=p0, l_true=l_true),
        grid_spec=pltpu.PrefetchScalarGridSpec(
            num_scalar_prefetch=2, grid=(bsz, nsteps),
            in_specs=[pl.BlockSpec((1, IDX_HEADS, tq, IDX_DIM), lambda b, s, qt, kt: (b, 0, qt[s], 0)),
                      pl.BlockSpec((1, tq, IDX_HEADS), lambda b, s, qt, kt: (b, qt[s], 0)),
                      pl.BlockSpec((1, IDX_DIM, tk), lambda b, s, qt, kt: (b, 0, kt[s]))],
            out_specs=pl.BlockSpec((1, tq, tk), lambda b, s, qt, kt: (b, qt[s], kt[s]))),
        out_shape=jax.ShapeDtypeStruct((bsz, t, l_pad), jnp.int32),
        compiler_params=_params(ARB2, vmem=False),
        name="dsa_index_keys",
    )(qbt, ktt, qi16, w, kit)

    tau, quota, flags = pl.pallas_call(
        functools.partial(_thr_kernel, tq=tq, cw=cw, p0=p0, n_sel=n_sel),
        grid=(bsz, nqb),
        in_specs=[pl.BlockSpec((1, tq, l_pad), lambda b, i: (b, i, 0))],
        out_specs=[pl.BlockSpec((1, tq, LANE), lambda b, i: (b, i, 0)),
                   pl.BlockSpec((1, tq, LANE), lambda b, i: (b, i, 0)),
                   pl.BlockSpec((1, 1, 8, LANE), lambda b, i: (b, i, 0, 0))],
        out_shape=[jax.ShapeDtypeStruct((bsz, t, LANE), jnp.int32),
                   jax.ShapeDtypeStruct((bsz, t, LANE), F32),
                   jax.ShapeDtypeStruct((bsz, nqb, 8, LANE), jnp.int32)],
        compiler_params=_params(ARB2),
        name="dsa_threshold",
    )(keys)
    flags = flags[:, :, 0, 0].reshape(bsz * nqb)

    return pl.pallas_call(
        functools.partial(_attn_kernel, tq=tq, tk=tk, sw=sw, p0=p0, nqb=nqb),
        grid_spec=pltpu.PrefetchScalarGridSpec(
            num_scalar_prefetch=3, grid=(bsz, nsteps),
            in_specs=[pl.BlockSpec((1, N_HEADS, tq, HEAD_DIM), lambda b, s, qt, kt, fl: (b, 0, qt[s], 0)),
                      pl.BlockSpec((1, KV_HEADS_A, HEAD_DIM, tk), lambda b, s, qt, kt, fl: (b, 0, 0, kt[s])),
                      pl.BlockSpec((1, KV_HEADS_A, tk, LANE), lambda b, s, qt, kt, fl: (b, 0, kt[s], 0)),
                      pl.BlockSpec((1, tq, tk), lambda b, s, qt, kt, fl: (b, qt[s], kt[s])),
                      pl.BlockSpec((1, tq, LANE), lambda b, s, qt, kt, fl: (b, qt[s], 0)),
                      pl.BlockSpec((1, tq, LANE), lambda b, s, qt, kt, fl: (b, qt[s], 0)),
                      pl.BlockSpec((NEAR_BLOCKS + 1, N_HEADS, tq, LANE), lambda b, s, qt, kt, fl: (0, 0, 0, 0))],
            out_specs=pl.BlockSpec((1, tq, ATTN_WIDTH), lambda b, s, qt, kt, fl: (b, qt[s], 0)),
            scratch_shapes=[pltpu.VMEM((N_HEADS, tq, LANE), F32), pltpu.VMEM((N_HEADS, tq, LANE), F32),
                            pltpu.VMEM((tq, tk), F32), pltpu.VMEM((tq, LANE), F32)]),
        out_shape=jax.ShapeDtypeStruct((bsz, t, ATTN_WIDTH), F32),
        compiler_params=_params(ARB2),
        name="dsa_masked_attention",
    )(qbt, ktt, flags, q16, ktr, vx, keys, tau, quota, bias[:, :, :tq, :])


def _band_kernel(q_ref, kp_ref, kc_ref, vp_ref, vc_ref, bias_ref, sink_ref, o_ref, *, tq, first_has_no_prev):
    group = N_HEADS // KV_HEADS_B
    extra = None
    if first_has_no_prev:
        col = lax.broadcasted_iota(jnp.int32, (tq, 2 * LANE), 1)
        extra = jnp.where((pl.program_id(1) == 0) & (col < LANE), NEG_BIG, 0.0)
    for h in range(N_HEADS):
        j = h // group
        q = q_ref[0, h]
        s = jnp.concatenate([jnp.dot(q, kp_ref[0, j], preferred_element_type=F32),
                             jnp.dot(q, kc_ref[0, j], preferred_element_type=F32)], axis=1) + bias_ref[h]
        if extra is not None:
            s = s + extra
        sink = sink_ref[h]
        m = jnp.maximum(jnp.max(s, axis=1, keepdims=True), sink[:, :1])
        e = jnp.exp(s - m)
        p = e / (jnp.sum(e, axis=1, keepdims=True) + jnp.exp(sink[:, :1] - m))
        pb = p.astype(BF16)
        o_ref[0, :, h * HEAD_DIM:(h + 1) * HEAD_DIM] = (
            jnp.dot(pb[:, :LANE], vp_ref[0, j], preferred_element_type=F32)
            + jnp.dot(pb[:, LANE:], vc_ref[0, j], preferred_element_type=F32))


def _band_bias(rel_table):
    r = jnp.arange(LANE, dtype=jnp.int32)[:, None]
    s = jnp.arange(2 * LANE, dtype=jnp.int32)[None, :]
    wc, qh = s // CHUNK, r // CHUNK
    band = (wc >= qh) & (wc <= qh + 2)
    tiles = _rel_tiles(rel_table, (s - LANE - r)[None])[0]
    return jnp.where(band[None], tiles, NEG_BIG)


def _band_attention(q16, kp, kc, vp, vc, bias, sinks, *, tq, prev_map, cur_map, first_has_no_prev):
    bsz, t = q16.shape[0], q16.shape[2]
    kblk = lambda m: pl.BlockSpec((1, KV_HEADS_B, HEAD_DIM, LANE), lambda b, i: (b, 0, 0, m(i)))
    vblk = lambda m: pl.BlockSpec((1, KV_HEADS_B, LANE, HEAD_DIM), lambda b, i: (b, 0, m(i), 0))
    sink_rows = jnp.broadcast_to(sinks.astype(F32)[:, None, None], (N_HEADS, 1, LANE))
    return pl.pallas_call(
        functools.partial(_band_kernel, tq=tq, first_has_no_prev=first_has_no_prev),
        grid=(bsz, t // tq),
        in_specs=[pl.BlockSpec((1, N_HEADS, tq, HEAD_DIM), lambda b, i: (b, 0, i, 0)),
                  kblk(prev_map), kblk(cur_map), vblk(prev_map), vblk(cur_map),
                  pl.BlockSpec((N_HEADS, tq, 2 * LANE), lambda b, i: (0, 0, 0)),
                  pl.BlockSpec((N_HEADS, 1, LANE), lambda b, i: (0, 0, 0))],
        out_specs=pl.BlockSpec((1, tq, ATTN_WIDTH), lambda b, i: (b, i, 0)),
        out_shape=jax.ShapeDtypeStruct((bsz, t, ATTN_WIDTH), F32),
        compiler_params=_params(ARB2, vmem=False),
        name="band_attention",
    )(q16, kp, kc, vp, vc, bias[:, :tq, :], sink_rows)


def _mixer_residual(x_ref, o_ref, wo_ref, gt1_ref):
    return x_ref[0] + gt1_ref[0] * jnp.dot(o_ref[0].astype(BF16), wo_ref[...], preferred_element_type=F32)


def _swiglu_chunk(h, wg, wu, wd):
    a = jax.nn.silu(jnp.dot(h, wg, preferred_element_type=F32)) * jnp.dot(h, wu, preferred_element_type=F32)
    return jnp.dot(a.astype(BF16), wd, preferred_element_type=F32)


def _ffn_kernel(x_ref, o_ref, wo_ref, gt1_ref, g_ref, sh_ref, sc_ref, gt2_ref, wg_ref, wu_ref, wd_ref,
                out_ref, xn_scr, h_scr, acc_scr):
    c = pl.program_id(2)

    @pl.when(c == 0)
    def _():
        xn = _mixer_residual(x_ref, o_ref, wo_ref, gt1_ref)
        xn_scr[...] = xn
        h_scr[...] = _modulate(xn, g_ref[...], sh_ref[0], sc_ref[0]).astype(BF16)
        acc_scr[...] = jnp.zeros(acc_scr.shape, F32)

    acc_scr[...] += _swiglu_chunk(h_scr[...], wg_ref[0], wu_ref[0], wd_ref[...])

    @pl.when(c == pl.num_programs(2) - 1)
    def _():
        out_ref[0] = xn_scr[...] + gt2_ref[0] * acc_scr[...]


def _moe_pre_kernel(x_ref, o_ref, wo_ref, gt1_ref, g_ref, sh_ref, sc_ref, wr_ref, xn_ref, h_ref, gate_ref, sel_ref):
    xn = _mixer_residual(x_ref, o_ref, wo_ref, gt1_ref)
    xn_ref[0] = xn
    h = _modulate(xn, g_ref[...], sh_ref[0], sc_ref[0])
    h_ref[0] = h
    lane = lax.broadcasted_iota(jnp.int32, (h.shape[0], LANE), 1)
    lanef = lane.astype(F32)
    logits = jnp.dot(h.astype(BF16), wr_ref[...], preferred_element_type=F32)
    logits = jnp.where(lane < N_EXPERTS, logits, -jnp.inf)
    m1 = jnp.max(logits, axis=1, keepdims=True)
    i1 = jnp.min(jnp.where(logits == m1, lanef, float(LANE)), axis=1, keepdims=True)
    rest = jnp.where(lanef == i1, -jnp.inf, logits)
    m2 = jnp.max(rest, axis=1, keepdims=True)
    i2 = jnp.min(jnp.where(rest == m2, lanef, float(LANE)), axis=1, keepdims=True)
    e2 = jnp.exp(m2 - m1)
    den = 1.0 + e2
    gates = jnp.where(lanef == i1, 1.0 / den, 0.0) + jnp.where(lanef == i2, e2 / den, 0.0)
    sel = jnp.where(lanef == i1, 1.0, 0.0) + jnp.where(lanef == i2, 1.0, 0.0)
    gate_ref[0] = gates[:, :N_EXPERTS]
    sel_ref[0] = sel[:, :N_EXPERTS]


def _moe_expert_kernel(te_tab, tv_tab, src_hbm, dst_hbm, h_hbm, wg_ref, wu_ref, wd_ref, y_hbm,
                       src_smem, dst_smem, hbuf, hb, acc, ybuf, sem_idx, sem_in, sem_out, *, rows):
    i, c = pl.program_id(0), pl.program_id(1)
    n_tiles, last = pl.num_programs(0), pl.num_programs(1) - 1
    slot = i % 2
    valid = tv_tab[i] != 0
    nxt = jnp.minimum(i + 1, n_tiles - 1)
    valid_next = (i + 1 < n_tiles) & (tv_tab[nxt] != 0)

    def start_rows(tile, sl):
        idx_copies = [pltpu.make_async_copy(src_hbm.at[tile], src_smem.at[sl], sem_idx.at[0]),
                      pltpu.make_async_copy(dst_hbm.at[tile], dst_smem.at[sl], sem_idx.at[1])]
        for cp in idx_copies:
            cp.start()
        for cp in idx_copies:
            cp.wait()

        def gather_rows(r0, carry):
            for u in range(DMA_BURST):
                r = r0 * DMA_BURST + u
                pltpu.make_async_copy(h_hbm.at[pl.ds(src_smem[sl, r], 1), :], hbuf.at[sl, pl.ds(r, 1), :],
                                      sem_in.at[sl]).start(priority=u % 2)
            return carry
        lax.fori_loop(0, rows // DMA_BURST, gather_rows, 0)

    all_rows_out = pltpu.make_async_copy(ybuf, y_hbm.at[pl.ds(0, rows), :], sem_out)

    @pl.when(valid & (c == 0))
    def _():
        @pl.when(i == 0)
        def _():
            start_rows(0, 0)
        pltpu.make_async_copy(h_hbm.at[pl.ds(0, rows), :], hbuf.at[slot], sem_in.at[slot]).wait()
        hb[...] = hbuf[slot].astype(BF16)
        acc[...] = jnp.zeros(acc.shape, F32)

    @pl.when(valid_next & (c == 1))
    def _():
        start_rows(nxt, 1 - slot)

    @pl.when(valid)
    def _():
        acc[...] += _swiglu_chunk(hb[...], wg_ref[0, 0], wu_ref[0, 0], wd_ref[0])

    @pl.when(valid & (c == last))
    def _():
        @pl.when(i > 0)
        def _():
            all_rows_out.wait()
        ybuf[...] = acc[...]

        def scatter_rows(r0, carry):
            for u in range(DMA_BURST):
                r = r0 * DMA_BURST + u
                pltpu.make_async_copy(ybuf.at[pl.ds(r, 1), :], y_hbm.at[pl.ds(dst_smem[slot, r], 1), :],
                                      sem_out).start(priority=u % 2)
            return carry
        lax.fori_loop(0, rows // DMA_BURST, scatter_rows, 0)

        @pl.when(jnp.logical_not(valid_next))
        def _():
            all_rows_out.wait()


def _moe_combine_kernel(xn_ref, gt2_ref, g2_ref, ya_ref, yb_ref, out_ref):
    g2 = g2_ref[0]
    out_ref[0] = xn_ref[0] + gt2_ref[0] * (g2[:, 0:1] * ya_ref[...] + g2[:, 1:2] * yb_ref[...])


def _chunk_major(w):
    lead, (d, f) = w.shape[:-2], w.shape[-2:]
    w = w.astype(BF16).reshape(*lead, d, f // FF_TILE, FF_TILE)
    return jnp.swapaxes(w, -3, -2)


def _token_specs(tm, nd):
    pick = (lambda f: lambda b, i: f(b, i)) if nd == 2 else (lambda f: lambda b, i, c: f(b, i))
    tok = pl.BlockSpec((1, tm, D_MODEL), pick(lambda b, i: (b, i, 0)))

    def mod_spec(a):
        return pl.BlockSpec((1, 1, D_MODEL), pick(lambda b, i: (b, 0, 0))) if a.shape[1] == 1 else tok

    cst = lambda shape: pl.BlockSpec(shape, pick(lambda b, i: (0, 0)))
    return tok, mod_spec, cst


def _swiglu_mixer(x, o, w_out, gt1, g, shift, scale, gt2, w_gate, w_up, w_down, *, tm):
    gsz, t, _ = x.shape
    d_ff = w_gate.shape[-1]
    assert t % tm == 0 and d_ff % FF_TILE == 0
    tok, mod_spec, cst = _token_specs(tm, 3)
    return pl.pallas_call(
        _ffn_kernel, grid=(gsz, t // tm, d_ff // FF_TILE),
        in_specs=[tok, tok, cst((ATTN_WIDTH, D_MODEL)), mod_spec(gt1), cst((1, D_MODEL)),
                  mod_spec(shift), mod_spec(scale), mod_spec(gt2),
                  pl.BlockSpec((1, D_MODEL, FF_TILE), lambda b, i, c: (c, 0, 0)),
                  pl.BlockSpec((1, D_MODEL, FF_TILE), lambda b, i, c: (c, 0, 0)),
                  pl.BlockSpec((FF_TILE, D_MODEL), lambda b, i, c: (c, 0))],
        out_specs=tok,
        out_shape=jax.ShapeDtypeStruct((gsz, t, D_MODEL), F32),
        scratch_shapes=[pltpu.VMEM((tm, D_MODEL), F32), pltpu.VMEM((tm, D_MODEL), BF16),
                        pltpu.VMEM((tm, D_MODEL), F32)],
        compiler_params=_params(("arbitrary",) * 3),
        name="swiglu_channel_mixer",
    )(x, o, w_out.astype(BF16), gt1, g[None, :], shift, scale, gt2,
      _chunk_major(w_gate), _chunk_major(w_up), w_down.astype(BF16))


def _moe_pre(x, o, w_out, gt1, g, shift, scale, w_router, *, tm):
    gsz, t, _ = x.shape
    assert t % tm == 0
    tok, mod_spec, cst = _token_specs(tm, 2)
    small = pl.BlockSpec((1, tm, N_EXPERTS), lambda b, i: (b, i, 0))
    wr = jnp.pad(w_router.astype(BF16), ((0, 0), (0, LANE - N_EXPERTS)))
    return pl.pallas_call(
        _moe_pre_kernel, grid=(gsz, t // tm),
        in_specs=[tok, tok, cst((ATTN_WIDTH, D_MODEL)), mod_spec(gt1), cst((1, D_MODEL)),
                  mod_spec(shift), mod_spec(scale), cst((D_MODEL, LANE))],
        out_specs=[tok, tok, small, small],
        out_shape=[jax.ShapeDtypeStruct((gsz, t, D_MODEL), F32), jax.ShapeDtypeStruct((gsz, t, D_MODEL), F32),
                   jax.ShapeDtypeStruct((gsz, t, N_EXPERTS), F32), jax.ShapeDtypeStruct((gsz, t, N_EXPERTS), F32)],
        compiler_params=_params(ARB2),
        name="moe_pre_router",
    )(x, o, w_out.astype(BF16), gt1, g[None, :], shift, scale, wr)


def _moe_routing(gates, sel, rows):
    n = gates.shape[0]
    member = sel > 0
    mi = member.astype(jnp.int32)
    counts = mi.sum(0)
    padded = (counts + rows - 1) // rows * rows
    ends = jnp.cumsum(padded)
    slot = jnp.cumsum(mi, axis=1) - mi
    expert_of = [jnp.sum(jnp.where(member & (slot == k), jnp.arange(N_EXPERTS)[None, :], 0), axis=1)
                 for k in range(TOP_K)]
    g2 = jnp.stack([jnp.sum(jnp.where(member & (slot == k), gates, 0.0), axis=1) for k in range(TOP_K)], axis=1)
    n_rows = (2 * n + N_EXPERTS * rows) // rows * rows
    n_pad = n_rows - 2 * n
    pad_ends = jnp.cumsum(padded - counts)
    pad_expert = (jnp.arange(n_pad, dtype=jnp.int32)[:, None] >= pad_ends[None, :]).sum(1)
    tok = jnp.arange(n, dtype=jnp.int32)
    sort_key = jnp.concatenate([expert_of[0], expert_of[1], pad_expert]).astype(jnp.int32)
    pair = jnp.concatenate([tok, n + tok, jnp.full((n_pad,), -1, jnp.int32)])
    _, pair = lax.sort((sort_key, pair), num_keys=1, is_stable=True)
    spare = 2 * n + jnp.arange(n_rows, dtype=jnp.int32) % rows
    src = jnp.where(pair >= 0, pair % n, 0)
    dst = jnp.where(pair >= 0, pair, spare)
    n_tiles = n_rows // rows
    tile_start = jnp.arange(n_tiles, dtype=jnp.int32) * rows
    tile_expert = jnp.minimum((tile_start[:, None] >= ends[None, :]).sum(1), N_EXPERTS - 1).astype(jnp.int32)
    tile_valid = (tile_start < ends[-1]).astype(jnp.int32)
    return src.reshape(n_tiles, rows), dst.reshape(n_tiles, rows), tile_expert, tile_valid, g2


def _moe_experts(h_all, src, dst, tile_expert, tile_valid, w_gate, w_up, w_down, *, rows):
    n = h_all.shape[0]
    n_tiles = src.shape[0]
    d_ff = w_gate.shape[-1]
    assert d_ff // FF_TILE >= 2
    any_spec = pl.BlockSpec(memory_space=pl.ANY)
    return pl.pallas_call(
        functools.partial(_moe_expert_kernel, rows=rows),
        grid_spec=pltpu.PrefetchScalarGridSpec(
            num_scalar_prefetch=2, grid=(n_tiles, d_ff // FF_TILE),
            in_specs=[any_spec, any_spec, any_spec,
                      pl.BlockSpec((1, 1, D_MODEL, FF_TILE), lambda i, c, te, tv: (te[i], c, 0, 0)),
                      pl.BlockSpec((1, 1, D_MODEL, FF_TILE), lambda i, c, te, tv: (te[i], c, 0, 0)),
                      pl.BlockSpec((1, FF_TILE, D_MODEL), lambda i, c, te, tv: (te[i], c, 0))],
            out_specs=any_spec,
            scratch_shapes=[pltpu.SMEM((2, rows), jnp.int32), pltpu.SMEM((2, rows), jnp.int32),
                            pltpu.VMEM((2, rows, D_MODEL), F32), pltpu.VMEM((rows, D_MODEL), BF16),
                            pltpu.VMEM((rows, D_MODEL), F32), pltpu.VMEM((rows, D_MODEL), F32),
                            pltpu.SemaphoreType.DMA((2,)), pltpu.SemaphoreType.DMA((2,)),
                            pltpu.SemaphoreType.DMA(())]),
        out_shape=jax.ShapeDtypeStruct((2 * n + rows, D_MODEL), F32),
        compiler_params=_params(ARB2),
        name="moe_routed_experts",
    )(tile_expert, tile_valid, src, dst, h_all, _chunk_major(w_gate), _chunk_major(w_up), w_down.astype(BF16))


def _moe_combine(xn, gt2, g2, y, *, tm, n_all, tok0):
    gsz, t, _ = xn.shape
    assert t % tm == 0 and n_all % tm == 0 and tok0 % tm == 0
    tok, mod_spec, _ = _token_specs(tm, 2)
    nt = t // tm
    yspec = lambda slot: pl.BlockSpec((tm, D_MODEL), lambda b, i: ((slot * n_all + tok0) // tm + b * nt + i, 0))
    return pl.pallas_call(
        _moe_combine_kernel, grid=(gsz, nt),
        in_specs=[tok, mod_spec(gt2), pl.BlockSpec((1, tm, TOP_K), lambda b, i: (b, i, 0)), yspec(0), yspec(1)],
        out_specs=tok,
        out_shape=jax.ShapeDtypeStruct((gsz, t, D_MODEL), F32),
        compiler_params=_params(ARB2),
        name="moe_combine",
    )(xn, gt2, g2, y, y)


def _per_token(a, n):
    return jnp.broadcast_to(a, (a.shape[0], n, a.shape[2])).reshape(1, a.shape[0] * n, a.shape[2])


def _pad_keys(a, axis, l_pad):
    pad = [(0, 0)] * a.ndim
    pad[axis] = (0, l_pad - a.shape[axis])
    return jnp.pad(a, pad)


def _layer_a(xp, xs, cp, cs, ck, cv, cki, rel_bias, norm_g, w_ada, b_ada, w_in, w_out, g_q, g_k, g_kidx,
             ffn_wg, ffn_wu, ffn_wd):
    bias = _dsa_bias_tiles(rel_bias)
    nb, n = xs.shape[0], xs.shape[1]
    past = ck.shape[1]
    outs = []
    for x, c, sample in ((xp, cp, False), (xs, cs, True)):
        sh1, sc1, gt1, sh2, sc2, gt2 = adaln(c, w_ada, b_ada)
        q16, kf, vf, ktr, vx, kif, qi16, kit, wsc = _project(
            x, norm_g[0], sh1, sc1, w_in, g_q, g_k, g_kidx, n_kv=KV_HEADS_A, has_idx=True,
            tm=n if sample else 256)
        if sample:
            l_true, l_pad = past + n, 2304
            kfull = jnp.concatenate([ck, kf.reshape(nb, n, KV_HEADS_A, HEAD_DIM)], axis=1).astype(BF16)
            vfull = jnp.concatenate([cv, vf.reshape(nb, n, KV_HEADS_A, HEAD_DIM)], axis=1).astype(BF16)
            kifull = jnp.concatenate([cki, kif], axis=1).astype(BF16)
            ktr = _pad_keys(kfull, 1, l_pad).transpose(0, 2, 3, 1)
            vx = jnp.concatenate([vfull, jnp.ones(vfull.shape[:3] + (1,), BF16),
                                  jnp.zeros(vfull.shape[:3] + (LANE - HEAD_DIM - 1,), BF16)], axis=-1)
            vx = _pad_keys(vx, 1, l_pad).transpose(0, 2, 1, 3)
            kit = _pad_keys(kifull, 1, l_pad).transpose(0, 2, 1)
            o = _dsa_attention(q16, ktr, vx, qi16, kit, wsc, bias, p0=past, l_true=l_true,
                               n_sel=min(TOPK_MAX, l_true // 4), tq=n, tk=l_pad, sw=256, cw=256)
            flat = lambda a: a.reshape(1, nb * n, a.shape[-1])
            y = _swiglu_mixer(flat(x), flat(o), w_out, _per_token(gt1, n), norm_g[1], _per_token(sh2, n),
                              _per_token(sc2, n), _per_token(gt2, n), ffn_wg, ffn_wu, ffn_wd, tm=MIX_TM_SAMPLE)
            y = y.reshape(x.shape)
        else:
            t = x.shape[1]
            o = _dsa_attention(q16, ktr, vx, qi16, kit, wsc, bias, p0=0, l_true=t,
                               n_sel=min(TOPK_MAX, t // 4), tq=LANE, tk=1024, sw=256, cw=1024)
            y = _swiglu_mixer(x, o, w_out, gt1, norm_g[1], sh2, sc2, gt2, ffn_wg, ffn_wu, ffn_wd, tm=MIX_TM_PROMPT)
        bsz, t = x.shape[0], x.shape[1]
        outs.append((y, kf.reshape(bsz, t, KV_HEADS_A, HEAD_DIM), vf.reshape(bsz, t, KV_HEADS_A, HEAD_DIM), kif))
    return outs


def _layer_b(xp, xs, cp, cs, ck, cv, rel_bias, norm_g, w_ada, b_ada, w_in, w_out, g_q, g_k, sinks,
             w_router, moe_wg, moe_wu, moe_wd):
    bias = _band_bias(rel_bias)
    nb, n = xs.shape[0], xs.shape[1]
    streams = []
    for x, c, sample in ((xp, cp, False), (xs, cs, True)):
        sh1, sc1, gt1, sh2, sc2, gt2 = adaln(c, w_ada, b_ada)
        q16, kf, vf, ktr, v16 = _project(x, norm_g[0], sh1, sc1, w_in, g_q, g_k, g_k, n_kv=KV_HEADS_B,
                                         has_idx=False, tm=n if sample else 256)
        bsz, t = x.shape[0], x.shape[1]
        k4 = kf.reshape(bsz, t, KV_HEADS_B, HEAD_DIM)
        v4 = vf.reshape(bsz, t, KV_HEADS_B, HEAD_DIM)
        zero = lambda i: 0
        if sample:
            kp = ck.astype(BF16).transpose(0, 2, 3, 1)
            vp = cv.astype(BF16).transpose(0, 2, 1, 3)
            o = _band_attention(q16, kp, _pad_keys(ktr, 3, LANE), vp, _pad_keys(v16, 2, LANE), bias, sinks,
                                tq=n, prev_map=zero, cur_map=zero, first_has_no_prev=False)
            flat = lambda a: a.reshape(1, nb * n, a.shape[-1])
            xn, h, gates, sel = _moe_pre(flat(x), flat(o), w_out, _per_token(gt1, n), norm_g[1],
                                         _per_token(sh2, n), _per_token(sc2, n), w_router, tm=MIX_TM_SAMPLE)
            gt2 = _per_token(gt2, n)
            k_new = jnp.concatenate([ck, k4], axis=1)[:, -WINDOW:]
            v_new = jnp.concatenate([cv, v4], axis=1)[:, -WINDOW:]
        else:
            o = _band_attention(q16, ktr, ktr, v16, v16, bias, sinks, tq=LANE,
                                prev_map=lambda i: jnp.maximum(i - 1, 0), cur_map=lambda i: i,
                                first_has_no_prev=True)
            xn, h, gates, sel = _moe_pre(x, o, w_out, gt1, norm_g[1], sh2, sc2, w_router, tm=MIX_TM_PROMPT)
            k_new, v_new = k4[:, -WINDOW:], v4[:, -WINDOW:]
        streams.append((x.shape, xn, h, gates, sel, gt2, k_new, v_new))

    rows2d = lambda a: a.reshape(-1, a.shape[-1])
    h_all = jnp.concatenate([rows2d(st[2]) for st in streams])
    n_all = h_all.shape[0]
    src, dst, tile_expert, tile_valid, g2 = _moe_routing(
        jnp.concatenate([rows2d(st[3]) for st in streams]), jnp.concatenate([rows2d(st[4]) for st in streams]),
        MOE_ROW_TILE)
    y = _moe_experts(h_all, src, dst, tile_expert, tile_valid, moe_wg, moe_wu, moe_wd, rows=MOE_ROW_TILE)
    outs, tok0 = [], 0
    for shape, xn, _, _, _, gt2, k_new, v_new in streams:
        cnt = xn.shape[0] * xn.shape[1]
        out = _moe_combine(xn, gt2, g2[tok0:tok0 + cnt].reshape(xn.shape[0], xn.shape[1], TOP_K), y,
                           tm=MOE_COMBINE_TM, n_all=n_all, tok0=tok0)
        outs.append((out.reshape(shape), k_new, v_new))
        tok0 += cnt
    return outs


def kernel(x_prompt, x_sample, c_prompt, c_sample, cache_a_k, cache_a_v, cache_a_kidx,
           cache_b_k, cache_b_v, rel_bias, norm_g, w_ada, b_ada,
           a_w_in, a_w_out, a_g_q, a_g_k, a_g_kidx,
           b_w_in, b_w_out, b_g_q, b_g_k, b_sinks,
           ffn_w_gate, ffn_w_up, ffn_w_down,
           moe_w_router, moe_w_gate, moe_w_up, moe_w_down):
    xp, xs = x_prompt, x_sample
    a_out = [[] for _ in range(6)]
    b_out = [[] for _ in range(4)]
    for i in range(DEPTH):
        j = i // 2
        if i % 2 == 0:
            (xp, k1, v1, i1), (xs, k2, v2, i2) = _layer_a(
                xp, xs, c_prompt, c_sample, cache_a_k[j], cache_a_v[j], cache_a_kidx[j], rel_bias, norm_g[i],
                w_ada[i], b_ada[i], a_w_in[j], a_w_out[j], a_g_q[j], a_g_k[j], a_g_kidx[j],
                ffn_w_gate[j], ffn_w_up[j], ffn_w_down[j])
            for lst, val in zip(a_out, (k1, v1, i1, k2, v2, i2)):
                lst.append(val)
        else:
            (xp, k1, v1), (xs, k2, v2) = _layer_b(
                xp, xs, c_prompt, c_sample, cache_b_k[j], cache_b_v[j], rel_bias, norm_g[i],
                w_ada[i], b_ada[i], b_w_in[j], b_w_out[j], b_g_q[j], b_g_k[j], b_sinks[j],
                moe_w_router[j], moe_w_gate[j], moe_w_up[j], moe_w_down[j])
            for lst, val in zip(b_out, (k1, v1, k2, v2)):
                lst.append(val)
    return (xp, xs, *(jnp.stack(l) for l in a_out), *(jnp.stack(l) for l in b_out))
```

```python
import math, functools
import jax, jax.numpy as jnp
from jax import lax
import numpy as np
from jax.experimental import pallas as pl
from jax.experimental.pallas import tpu as pltpu

D_MODEL = 1024
DEPTH = 2
CHUNK = 64
N_HEADS = 16
HEAD_DIM = 64
ATTN_WIDTH = N_HEADS * HEAD_DIM
KV_HEADS_A = 4
IDX_HEADS = 8
IDX_DIM = 64
TOPK_MAX = 256
KV_HEADS_B = 2
WINDOW = 128
NUM_BUCKETS = 32
MAX_DISTANCE = 1024
N_EXPERTS = 8
TOP_K = 2
EPS = 1e-6

F32 = jnp.float32
BF16 = jnp.bfloat16
LANE = 128
VMEM_LIMIT_BYTES = 48 * 1024 * 1024
LOG2E = math.log2(math.e)
NEG_BIG = -1e30
KEY_MIN = -2 ** 31
FAR_BUCKET = NUM_BUCKETS // 2 - 1
NEAR_BLOCKS = 6
FF_TILE = 512
MIX_TM_PROMPT = 1024
MIX_TM_SAMPLE = 256
MOE_ROW_TILE = 512
MOE_COMBINE_TM = 256
DMA_BURST = 8
PROJ_TM = 256
DSA_SUB = 2 * LANE
DSA_KEY_TILE = 1024
CHUNK_SHIFT = CHUNK.bit_length() - 1
LANE_SHIFT = LANE.bit_length() - 1
COUNT_ALL = float(2 ** 24)
ARB2 = ("arbitrary", "arbitrary")


def _params(sem, vmem=True):
    return pltpu.CompilerParams(dimension_semantics=sem, vmem_limit_bytes=VMEM_LIMIT_BYTES if vmem else None)


def rel_bucket(rel):
    half = NUM_BUCKETS // 2
    max_exact = half // 2
    base = jnp.where(rel > 0, half, 0)
    n = jnp.abs(rel)
    nf = jnp.maximum(n, 1).astype(F32)
    large = max_exact + (jnp.log(nf / max_exact) / math.log(MAX_DISTANCE / max_exact)
                         * (half - max_exact)).astype(jnp.int32)
    large = jnp.minimum(large, half - 1)
    return base + jnp.where(n < max_exact, n, large)


def _adaln_kernel(c_ref, w_ref, b_ref, o_ref):
    c = c_ref[...]
    o_ref[...] = jnp.dot(c * jax.nn.sigmoid(c), w_ref[...], preferred_element_type=F32) + b_ref[...]


def adaln(c, w, b):
    n = c.shape[0]
    mod = pl.pallas_call(
        _adaln_kernel,
        grid=(6,),
        in_specs=[pl.BlockSpec((n, D_MODEL), lambda j: (0, 0)),
                  pl.BlockSpec((D_MODEL, D_MODEL), lambda j: (0, j)),
                  pl.BlockSpec((1, D_MODEL), lambda j: (0, j))],
        out_specs=pl.BlockSpec((n, D_MODEL), lambda j: (0, j)),
        out_shape=jax.ShapeDtypeStruct((n, 6 * D_MODEL), F32),
        name="adaln",
    )(c, w, b[None, :])
    return jnp.split(mod[:, None, :], 6, axis=-1)


def _modulate(x, g, shift, scale):
    y = x * lax.rsqrt(jnp.mean(x * x, axis=-1, keepdims=True) + EPS)
    return (y * g) * (1.0 + scale) + shift


def _seg_norm(seg, g):
    return seg * lax.rsqrt(jnp.mean(seg * seg, axis=-1, keepdims=True) + EPS) * g


def _proj_kernel(x_ref, g_ref, sh_ref, sc_ref, w_ref, gq_ref, gk_ref, *rest, n_kv, has_idx):
    h = _modulate(x_ref[0], g_ref[...], sh_ref[0], sc_ref[0]).astype(BF16)
    y = jnp.dot(h, w_ref[...], preferred_element_type=F32)
    tm = y.shape[0]
    kv_w = n_kv * HEAD_DIM
    if has_idx:
        gki_ref, q16, kf, vf, ktr, vx, kif, qi16, kit, wsc = rest
    else:
        q16, kf, vf, ktr, v16 = rest
    for hd in range(N_HEADS):
        seg = y[:, hd * HEAD_DIM:(hd + 1) * HEAD_DIM]
        q_scale = HEAD_DIM ** -0.5 * (LOG2E if has_idx else 1.0)
        q16[0, hd] = (_seg_norm(seg, gq_ref[...]) * q_scale).astype(BF16)
    lane = lax.broadcasted_iota(jnp.int32, (tm, LANE - HEAD_DIM), 1)
    ones_col = jnp.where(lane == 0, 1.0, 0.0).astype(BF16)
    for j in range(n_kv):
        kseg = _seg_norm(y[:, ATTN_WIDTH + j * HEAD_DIM:ATTN_WIDTH + (j + 1) * HEAD_DIM], gk_ref[...])
        vseg = y[:, ATTN_WIDTH + kv_w + j * HEAD_DIM:ATTN_WIDTH + kv_w + (j + 1) * HEAD_DIM]
        kf[0, :, j * HEAD_DIM:(j + 1) * HEAD_DIM] = kseg
        vf[0, :, j * HEAD_DIM:(j + 1) * HEAD_DIM] = vseg
        ktr[0, j] = kseg.T.astype(BF16)
        if has_idx:
            vx[0, j] = jnp.concatenate([vseg.astype(BF16), ones_col], axis=1)
        else:
            v16[0, j] = vseg.astype(BF16)
    if has_idx:
        base = ATTN_WIDTH + 2 * kv_w
        for hd in range(IDX_HEADS):
            qi16[0, hd] = y[:, base + hd * IDX_DIM:base + (hd + 1) * IDX_DIM].astype(BF16)
        base += IDX_HEADS * IDX_DIM
        kiseg = _seg_norm(y[:, base:base + IDX_DIM], gki_ref[...])
        kif[0] = kiseg
        kit[0] = kiseg.T.astype(BF16)
        wsc[0] = y[:, base + IDX_DIM:base + IDX_DIM + IDX_HEADS] * IDX_HEADS ** -0.5 * IDX_DIM ** -0.5


def _project(x, g, shift, scale, w_in, g_q, g_k, g_kidx=None, *, n_kv, tm):
    has_idx = g_kidx is not None
    gsz, t, _ = x.shape
    n_in = w_in.shape[1]
    kv_w = n_kv * HEAD_DIM
    tok = lambda b, i: (b, i, 0)
    head = lambda b, i: (b, 0, i, 0)
    vec = lambda b, i: (b, 0, 0)
    cst = lambda b, i: (0, 0)
    shapes = [((gsz, N_HEADS, t, HEAD_DIM), BF16, (1, N_HEADS, tm, HEAD_DIM), head),
              ((gsz, t, kv_w), F32, (1, tm, kv_w), tok),
              ((gsz, t, kv_w), F32, (1, tm, kv_w), tok),
              ((gsz, n_kv, HEAD_DIM, t), BF16, (1, n_kv, HEAD_DIM, tm), lambda b, i: (b, 0, 0, i))]
    if has_idx:
        shapes += [((gsz, n_kv, t, LANE), BF16, (1, n_kv, tm, LANE), head),
                   ((gsz, t, IDX_DIM), F32, (1, tm, IDX_DIM), tok),
                   ((gsz, IDX_HEADS, t, IDX_DIM), BF16, (1, IDX_HEADS, tm, IDX_DIM), head),
                   ((gsz, IDX_DIM, t), BF16, (1, IDX_DIM, tm), lambda b, i: (b, 0, i)),
                   ((gsz, t, IDX_HEADS), F32, (1, tm, IDX_HEADS), tok)]
    else:
        shapes += [((gsz, n_kv, t, HEAD_DIM), BF16, (1, n_kv, tm, HEAD_DIM), head)]
    return pl.pallas_call(
        functools.partial(_proj_kernel, n_kv=n_kv, has_idx=has_idx),
        grid=(gsz, t // tm),
        in_specs=[pl.BlockSpec((1, tm, D_MODEL), tok), pl.BlockSpec((1, D_MODEL), cst),
                  pl.BlockSpec((1, 1, D_MODEL), vec), pl.BlockSpec((1, 1, D_MODEL), vec),
                  pl.BlockSpec((D_MODEL, n_in), cst), pl.BlockSpec((1, HEAD_DIM), cst),
                  pl.BlockSpec((1, HEAD_DIM), cst)] + ([pl.BlockSpec((1, IDX_DIM), cst)] if has_idx else []),
        out_specs=[pl.BlockSpec(blk, im) for _, _, blk, im in shapes],
        out_shape=[jax.ShapeDtypeStruct(s, d) for s, d, _, _ in shapes],
        compiler_params=_params(ARB2),
        name="mixer_in_proj",
    )(x, g[None, :], shift, scale, w_in.astype(BF16), g_q[None, :], g_k[None, :],
      *([g_kidx[None, :]] if has_idx else []))


def _tri_steps(nqb, tq, tk, p0):
    qbs, kts = [], []
    for qb in range(nqb):
        vis_end = ((p0 + qb * tq + tq - 1) // CHUNK + 1) * CHUNK
        for kt in range(-(-vis_end // tk)):
            qbs.append(qb)
            kts.append(kt)
    return jnp.asarray(np.array(qbs, np.int32)), jnp.asarray(np.array(kts, np.int32))


def _idx_kernel(qb_tab, kt_tab, qi_ref, w_ref, kit_ref, out_ref, *, tq, tk, sw, p0, l_true):
    step = pl.program_id(1)
    q0 = p0 + qb_tab[step] * tq
    k0 = kt_tab[step] * tk
    row = lax.broadcasted_iota(jnp.int32, (tq, 1), 0) + q0
    vis_end = jnp.minimum(((row >> CHUNK_SHIFT) + 1) << CHUNK_SHIFT, l_true)
    wv = w_ref[0]
    for c in range(tk // sw):
        kit = kit_ref[0, :, c * sw:(c + 1) * sw]
        acc = jnp.zeros((tq, sw), F32)
        for h in range(IDX_HEADS):
            sh = jnp.dot(qi_ref[0, h], kit, preferred_element_type=F32)
            acc = acc + wv[:, h:h + 1] * jnp.maximum(sh, 0.0)
        bits = lax.bitcast_convert_type(acc, jnp.int32)
        key = bits ^ ((bits >> 31) & 0x7FFFFFFF)
        kpos = k0 + c * sw + lax.broadcasted_iota(jnp.int32, (tq, sw), 1)
        out_ref[0, :, c * sw:(c + 1) * sw] = jnp.where(kpos < vis_end, key, KEY_MIN)


def _thr_kernel(keys_ref, tau_ref, quota_ref, flag_ref, *, tq, cw, p0, n_sel):
    qb = pl.program_id(1)
    vis_end = (((p0 + qb * tq + tq - 1) >> CHUNK_SHIFT) + 1) << CHUNK_SHIFT
    nch = (vis_end + cw - 1) // cw
    key_max = 2 ** 31 - 1

    def count_ge(cand):
        def body(j, acc):
            x = keys_ref[0, :, pl.ds(pl.multiple_of(j * cw, cw), cw)]
            for u in range(cw // LANE):
                acc = acc + jnp.where(x[:, u * LANE:(u + 1) * LANE] >= cand, 1.0, 0.0)
            return acc
        acc = lax.fori_loop(0, nch, body, jnp.zeros((tq, LANE), F32))
        return jnp.sum(acc, axis=1, keepdims=True)

    zero = jnp.zeros((tq, LANE), jnp.int32)
    c0 = count_ge(zero)
    tau = jnp.where(c0 >= n_sel, zero, KEY_MIN)
    n_ge = jnp.where(c0 >= n_sel, c0, COUNT_ALL)

    def bit_body(carry):
        i, tau, n_ge = carry
        cand = tau + jnp.left_shift(jnp.int32(1), 30 - i)
        c = count_ge(cand)
        return i + 1, jnp.where(c >= n_sel, cand, tau), jnp.where(c >= n_sel, c, n_ge)

    def unresolved(carry):
        i, _, n_ge = carry
        return (i < 31) & (jnp.max(jnp.abs(n_ge - n_sel)) > 0.0)

    _, tau, n_ge = lax.while_loop(unresolved, bit_body, (jnp.int32(0), tau, n_ge))
    n_gt = jnp.where(tau[:, :1] < key_max, count_ge(jnp.minimum(tau, key_max - 1) + 1), 0.0)
    tau_ref[0] = jnp.maximum(tau, KEY_MIN + 1)
    quota_ref[0] = jnp.broadcast_to(n_sel - n_gt, (tq, LANE))
    over = jnp.max(jnp.where(n_ge > n_sel, 1, 0), axis=0, keepdims=True)
    flag_ref[0, 0] = jnp.broadcast_to(over, flag_ref.shape[2:])


def _attn_kernel(qb_tab, kt_tab, flag_tab, q_ref, kt_ref, vx_ref, keys_ref, tau_ref, quota_ref, bias_ref, o_ref,
                 m_ref, acc_ref, negm_ref, seen_ref, *, tq, tk, sw, p0, nqb):
    step = pl.program_id(1)
    qb, kt = qb_tab[step], kt_tab[step]
    q0 = p0 + qb * tq
    k0 = kt * tk
    vis_end = (((q0 + tq - 1) >> CHUNK_SHIFT) + 1) << CHUNK_SHIFT
    group = N_HEADS // KV_HEADS_A
    nblk = sw // LANE
    tie_w = 2 * LANE

    @pl.when(kt == 0)
    def _():
        m_ref[...] = jnp.full(m_ref.shape, NEG_BIG, F32)
        acc_ref[...] = jnp.zeros(acc_ref.shape, F32)
        seen_ref[...] = jnp.zeros(seen_ref.shape, F32)

    has_ties = flag_tab[pl.program_id(0) * nqb + qb] != 0
    tau = tau_ref[0]

    @pl.when(jnp.logical_not(has_ties))
    def _():
        for u in range(tk // LANE):
            cols = slice(u * LANE, (u + 1) * LANE)
            negm_ref[:, cols] = jnp.where(keys_ref[0, :, cols] >= tau, 0.0, NEG_BIG)

    @pl.when(has_ties)
    def _():
        tau2 = jnp.concatenate([tau] * (tie_w // LANE), axis=1)
        quota = jnp.concatenate([quota_ref[0]] * (tie_w // LANE), axis=1)
        before = (lax.broadcasted_iota(jnp.int32, (tie_w, tie_w), 0)
                  < lax.broadcasted_iota(jnp.int32, (tie_w, tie_w), 1))
        before = jnp.where(before, 1.0, 0.0).astype(BF16)
        seen = seen_ref[...]
        for u in range(tk // tie_w):
            cols = slice(u * tie_w, (u + 1) * tie_w)
            k = keys_ref[0, :, cols]
            tie = jnp.where(k == tau2, 1.0, 0.0)
            rank = (jnp.dot(tie.astype(BF16), before, preferred_element_type=F32)
                    + jnp.concatenate([seen] * (tie_w // LANE), axis=1))
            keep = jnp.where(k > tau2, 1.0, jnp.where(rank < quota, tie, 0.0))
            negm_ref[:, cols] = jnp.where(keep > 0.0, 0.0, NEG_BIG)
            seen = seen + jnp.sum(tie, axis=1, keepdims=True)
        seen_ref[...] = seen

    def sub_tile(c, with_bias):
        off = c * sw if isinstance(c, int) else pl.multiple_of(c * sw, sw)
        negm = negm_ref[:, pl.ds(off, sw)]
        if with_bias:
            d0 = ((k0 + off - q0) >> LANE_SHIFT) + NEAR_BLOCKS
            bidx = [jnp.clip(d0 + u, 0, NEAR_BLOCKS) for u in range(nblk)]
        for h in range(N_HEADS):
            j = h // group
            s = jnp.dot(q_ref[0, h], kt_ref[0, j, :, pl.ds(off, sw)], preferred_element_type=F32) + negm
            if with_bias:
                s = s + jnp.concatenate([bias_ref[bidx[u], h] for u in range(nblk)], axis=1)
            m_prev = m_ref[h]
            m_cur = jnp.maximum(m_prev, jnp.max(s, axis=1, keepdims=True))
            alpha = jnp.exp2(m_prev - m_cur)
            p = jnp.exp2(s - jnp.concatenate([m_cur] * nblk, axis=1))
            pv = jnp.dot(p.astype(BF16), vx_ref[0, j, pl.ds(off, sw), :], preferred_element_type=F32)
            acc_ref[h] = alpha * acc_ref[h] + pv
            m_ref[h] = m_cur

    tile_far = k0 + tk <= q0 - (NEAR_BLOCKS - 1) * LANE

    @pl.when(tile_far)
    def _():
        for c in range(tk // sw):
            sub_tile(c, False)

    @pl.when(jnp.logical_not(tile_far))
    def _():
        nsub = jnp.minimum(tk // sw, (vis_end - k0 + sw - 1) // sw)
        nfar = jnp.clip((q0 - (NEAR_BLOCKS - 1) * LANE - k0) // sw, 0, nsub)

        def far_body(c, carry):
            sub_tile(c, False)
            return carry

        def near_body(c, carry):
            sub_tile(c, True)
            return carry

        lax.fori_loop(0, nfar, far_body, 0)
        lax.fori_loop(nfar, nsub, near_body, 0)

    @pl.when(k0 + tk >= vis_end)
    def _():
        for h in range(N_HEADS):
            a = acc_ref[h]
            o_ref[0, :, h * HEAD_DIM:(h + 1) * HEAD_DIM] = a[:, :HEAD_DIM] / a[:, HEAD_DIM:HEAD_DIM + 1]


def _rel_tiles(rel_table, rel):
    onehot = jax.nn.one_hot(rel_bucket(rel), NUM_BUCKETS, dtype=F32)
    return jnp.einsum("nrsk,kh->nhrs", onehot, rel_table.astype(F32), precision=lax.Precision.HIGHEST)


def _dsa_bias_tiles(rel_table):
    d = jnp.arange(NEAR_BLOCKS + 1, dtype=jnp.int32)[:, None, None] - NEAR_BLOCKS
    rel = (d * LANE + jnp.arange(LANE, dtype=jnp.int32)[None, None, :]
           - jnp.arange(LANE, dtype=jnp.int32)[None, :, None])
    tiles = (_rel_tiles(rel_table, rel) - rel_table[FAR_BUCKET][None, :, None, None]) * LOG2E
    return tiles.at[0].set(0.0)


def _dsa_attention(q16, ktr, vx, qi16, kit, w, bias, *, p0, l_true, n_sel, tq, tk, sw, cw):
    bsz, t = q16.shape[0], q16.shape[2]
    l_pad = ktr.shape[3]
    nqb = t // tq
    assert t % tq == 0 and p0 % LANE == 0 and (tq == LANE or nqb == 1)
    assert l_pad % tk == 0 and tk % sw == 0 and tk % cw == 0 and tk % (2 * LANE) == 0
    qbt, ktt = _tri_steps(nqb, tq, tk, p0)
    nsteps = int(qbt.shape[0])

    keys = pl.pallas_call(
        functools.partial(_idx_kernel, tq=tq, tk=tk, sw=sw, p0=p0, l_true=l_true),
        grid_spec=pltpu.PrefetchScalarGridSpec(
            num_scalar_prefetch=2, grid=(bsz, nsteps),
            in_specs=[pl.BlockSpec((1, IDX_HEADS, tq, IDX_DIM), lambda b, s, qt, kt: (b, 0, qt[s], 0)),
                      pl.BlockSpec((1, tq, IDX_HEADS), lambda b, s, qt, kt: (b, qt[s], 0)),
                      pl.BlockSpec((1, IDX_DIM, tk), lambda b, s, qt, kt: (b, 0, kt[s]))],
            out_specs=pl.BlockSpec((1, tq, tk), lambda b, s, qt, kt: (b, qt[s], kt[s]))),
        out_shape=jax.ShapeDtypeStruct((bsz, t, l_pad), jnp.int32),
        compiler_params=_params(ARB2, vmem=False),
        name="dsa_index_keys",
    )(qbt, ktt, qi16, w, kit)

    tau, quota, flags = pl.pallas_call(
        functools.partial(_thr_kernel, tq=tq, cw=cw, p0=p0, n_sel=n_sel),
        grid=(bsz, nqb),
        in_specs=[pl.BlockSpec((1, tq, l_pad), lambda b, i: (b, i, 0))],
        out_specs=[pl.BlockSpec((1, tq, LANE), lambda b, i: (b, i, 0)),
                   pl.BlockSpec((1, tq, LANE), lambda b, i: (b, i, 0)),
                   pl.BlockSpec((1, 1, 8, LANE), lambda b, i: (b, i, 0, 0))],
        out_shape=[jax.ShapeDtypeStruct((bsz, t, LANE), jnp.int32),
                   jax.ShapeDtypeStruct((bsz, t, LANE), F32),
                   jax.ShapeDtypeStruct((bsz, nqb, 8, LANE), jnp.int32)],
        compiler_params=_params(ARB2),
        name="dsa_threshold",
    )(keys)
    flags = flags[:, :, 0, 0].reshape(bsz * nqb)

    return pl.pallas_call(
        functools.partial(_attn_kernel, tq=tq, tk=tk, sw=sw, p0=p0, nqb=nqb),
        grid_spec=pltpu.PrefetchScalarGridSpec(
            num_scalar_prefetch=3, grid=(bsz, nsteps),
            in_specs=[pl.BlockSpec((1, N_HEADS, tq, HEAD_DIM), lambda b, s, qt, kt, fl: (b, 0, qt[s], 0)),
                      pl.BlockSpec((1, KV_HEADS_A, HEAD_DIM, tk), lambda b, s, qt, kt, fl: (b, 0, 0, kt[s])),
                      pl.BlockSpec((1, KV_HEADS_A, tk, LANE), lambda b, s, qt, kt, fl: (b, 0, kt[s], 0)),
                      pl.BlockSpec((1, tq, tk), lambda b, s, qt, kt, fl: (b, qt[s], kt[s])),
                      pl.BlockSpec((1, tq, LANE), lambda b, s, qt, kt, fl: (b, qt[s], 0)),
                      pl.BlockSpec((1, tq, LANE), lambda b, s, qt, kt, fl: (b, qt[s], 0)),
                      pl.BlockSpec((NEAR_BLOCKS + 1, N_HEADS, tq, LANE), lambda b, s, qt, kt, fl: (0, 0, 0, 0))],
            out_specs=pl.BlockSpec((1, tq, ATTN_WIDTH), lambda b, s, qt, kt, fl: (b, qt[s], 0)),
            scratch_shapes=[pltpu.VMEM((N_HEADS, tq, LANE), F32), pltpu.VMEM((N_HEADS, tq, LANE), F32),
                            pltpu.VMEM((tq, tk), F32), pltpu.VMEM((tq, LANE), F32)]),
        out_shape=jax.ShapeDtypeStruct((bsz, t, ATTN_WIDTH), F32),
        compiler_params=_params(ARB2),
        name="dsa_masked_attention",
    )(qbt, ktt, flags, q16, ktr, vx, keys, tau, quota, bias[:, :, :tq, :])


def _band_kernel(q_ref, kp_ref, kc_ref, vp_ref, vc_ref, bias_ref, sink_ref, o_ref, *, tq, first_has_no_prev):
    group = N_HEADS // KV_HEADS_B
    extra = None
    if first_has_no_prev:
        col = lax.broadcasted_iota(jnp.int32, (tq, 2 * LANE), 1)
        extra = jnp.where((pl.program_id(1) == 0) & (col < LANE), NEG_BIG, 0.0)
    for h in range(N_HEADS):
        j = h // group
        q = q_ref[0, h]
        s = jnp.concatenate([jnp.dot(q, kp_ref[0, j], preferred_element_type=F32),
                             jnp.dot(q, kc_ref[0, j], preferred_element_type=F32)], axis=1) + bias_ref[h]
        if extra is not None:
            s = s + extra
        sink = sink_ref[h]
        m = jnp.maximum(jnp.max(s, axis=1, keepdims=True), sink[:, :1])
        e = jnp.exp(s - m)
        p = e / (jnp.sum(e, axis=1, keepdims=True) + jnp.exp(sink[:, :1] - m))
        pb = p.astype(BF16)
        o_ref[0, :, h * HEAD_DIM:(h + 1) * HEAD_DIM] = (
            jnp.dot(pb[:, :LANE], vp_ref[0, j], preferred_element_type=F32)
            + jnp.dot(pb[:, LANE:], vc_ref[0, j], preferred_element_type=F32))


def _band_bias(rel_table):
    r = jnp.arange(LANE, dtype=jnp.int32)[:, None]
    s = jnp.arange(2 * LANE, dtype=jnp.int32)[None, :]
    wc, qh = s // CHUNK, r // CHUNK
    band = (wc >= qh) & (wc <= qh + 2)
    tiles = _rel_tiles(rel_table, (s - LANE - r)[None])[0]
    return jnp.where(band[None], tiles, NEG_BIG)


def _band_attention(q16, kp, kc, vp, vc, bias, sinks, *, tq, prev_map, cur_map, first_has_no_prev):
    bsz, t = q16.shape[0], q16.shape[2]
    kblk = lambda m: pl.BlockSpec((1, KV_HEADS_B, HEAD_DIM, LANE), lambda b, i: (b, 0, 0, m(i)))
    vblk = lambda m: pl.BlockSpec((1, KV_HEADS_B, LANE, HEAD_DIM), lambda b, i: (b, 0, m(i), 0))
    sink_rows = jnp.broadcast_to(sinks.astype(F32)[:, None, None], (N_HEADS, 1, LANE))
    return pl.pallas_call(
        functools.partial(_band_kernel, tq=tq, first_has_no_prev=first_has_no_prev),
        grid=(bsz, t // tq),
        in_specs=[pl.BlockSpec((1, N_HEADS, tq, HEAD_DIM), lambda b, i: (b, 0, i, 0)),
                  kblk(prev_map), kblk(cur_map), vblk(prev_map), vblk(cur_map),
                  pl.BlockSpec((N_HEADS, tq, 2 * LANE), lambda b, i: (0, 0, 0)),
                  pl.BlockSpec((N_HEADS, 1, LANE), lambda b, i: (0, 0, 0))],
        out_specs=pl.BlockSpec((1, tq, ATTN_WIDTH), lambda b, i: (b, i, 0)),
        out_shape=jax.ShapeDtypeStruct((bsz, t, ATTN_WIDTH), F32),
        compiler_params=_params(ARB2, vmem=False),
        name="band_attention",
    )(q16, kp, kc, vp, vc, bias[:, :tq, :], sink_rows)


def _mixer_residual(x_ref, o_ref, wo_ref, gt1_ref):
    return x_ref[0] + gt1_ref[0] * jnp.dot(o_ref[0].astype(BF16), wo_ref[...], preferred_element_type=F32)


def _swiglu_chunk(h, wg, wu, wd):
    a = jax.nn.silu(jnp.dot(h, wg, preferred_element_type=F32)) * jnp.dot(h, wu, preferred_element_type=F32)
    return jnp.dot(a.astype(BF16), wd, preferred_element_type=F32)


def _ffn_kernel(x_ref, o_ref, wo_ref, gt1_ref, g_ref, sh_ref, sc_ref, gt2_ref, wg_ref, wu_ref, wd_ref,
                out_ref, xn_scr, h_scr, acc_scr):
    c = pl.program_id(2)

    @pl.when(c == 0)
    def _():
        xn = _mixer_residual(x_ref, o_ref, wo_ref, gt1_ref)
        xn_scr[...] = xn
        h_scr[...] = _modulate(xn, g_ref[...], sh_ref[0], sc_ref[0]).astype(BF16)
        acc_scr[...] = jnp.zeros(acc_scr.shape, F32)

    acc_scr[...] += _swiglu_chunk(h_scr[...], wg_ref[...], wu_ref[...], wd_ref[...])

    @pl.when(c == pl.num_programs(2) - 1)
    def _():
        out_ref[0] = xn_scr[...] + gt2_ref[0] * acc_scr[...]


def _moe_pre_kernel(x_ref, o_ref, wo_ref, gt1_ref, g_ref, sh_ref, sc_ref, wr_ref, xn_ref, h_ref, gate_ref, sel_ref):
    xn = _mixer_residual(x_ref, o_ref, wo_ref, gt1_ref)
    xn_ref[0] = xn
    h = _modulate(xn, g_ref[...], sh_ref[0], sc_ref[0])
    h_ref[0] = h
    lane = lax.broadcasted_iota(jnp.int32, (h.shape[0], LANE), 1)
    lanef = lane.astype(F32)
    logits = jnp.dot(h.astype(BF16), wr_ref[...], preferred_element_type=F32)
    logits = jnp.where(lane < N_EXPERTS, logits, -jnp.inf)
    m1 = jnp.max(logits, axis=1, keepdims=True)
    i1 = jnp.min(jnp.where(logits == m1, lanef, float(LANE)), axis=1, keepdims=True)
    rest = jnp.where(lanef == i1, -jnp.inf, logits)
    m2 = jnp.max(rest, axis=1, keepdims=True)
    i2 = jnp.min(jnp.where(rest == m2, lanef, float(LANE)), axis=1, keepdims=True)
    e2 = jnp.exp(m2 - m1)
    den = 1.0 + e2
    gates = jnp.where(lanef == i1, 1.0 / den, 0.0) + jnp.where(lanef == i2, e2 / den, 0.0)
    sel = jnp.where(lanef == i1, 1.0, 0.0) + jnp.where(lanef == i2, 1.0, 0.0)
    gate_ref[0] = gates[:, :N_EXPERTS]
    sel_ref[0] = sel[:, :N_EXPERTS]


def _moe_expert_kernel(te_tab, tv_tab, src_hbm, dst_hbm, h_hbm, wg_ref, wu_ref, wd_ref, y_hbm,
                       src_smem, dst_smem, hbuf, hb, acc, sem_idx, sem_in, sem_out, *, rows):
    i, c = pl.program_id(0), pl.program_id(1)
    valid = tv_tab[i] != 0
    all_rows_in = pltpu.make_async_copy(h_hbm.at[pl.ds(0, rows), :], hbuf, sem_in)
    all_rows_out = pltpu.make_async_copy(acc, y_hbm.at[pl.ds(0, rows), :], sem_out)

    @pl.when(valid & (c == 0))
    def _():
        idx_copies = [pltpu.make_async_copy(src_hbm.at[i], src_smem, sem_idx.at[0]),
                      pltpu.make_async_copy(dst_hbm.at[i], dst_smem, sem_idx.at[1])]
        for cp in idx_copies:
            cp.start()
        for cp in idx_copies:
            cp.wait()

        def gather_rows(r0, carry):
            for u in range(DMA_BURST):
                r = r0 * DMA_BURST + u
                pltpu.make_async_copy(h_hbm.at[pl.ds(src_smem[r], 1), :], hbuf.at[pl.ds(r, 1), :],
                                      sem_in).start(priority=u % 2)
            return carry
        lax.fori_loop(0, rows // DMA_BURST, gather_rows, 0)
        all_rows_in.wait()
        hb[...] = hbuf[...].astype(BF16)
        acc[...] = jnp.zeros(acc.shape, F32)

    @pl.when(valid)
    def _():
        acc[...] += _swiglu_chunk(hb[...], wg_ref[0], wu_ref[0], wd_ref[0])

    @pl.when(valid & (c == pl.num_programs(1) - 1))
    def _():
        def scatter_rows(r0, carry):
            for u in range(DMA_BURST):
                r = r0 * DMA_BURST + u
                pltpu.make_async_copy(acc.at[pl.ds(r, 1), :], y_hbm.at[pl.ds(dst_smem[r], 1), :],
                                      sem_out).start(priority=u % 2)
            return carry
        lax.fori_loop(0, rows // DMA_BURST, scatter_rows, 0)
        all_rows_out.wait()


def _moe_combine_kernel(xn_ref, gt2_ref, g2_ref, ya_ref, yb_ref, out_ref):
    g2 = g2_ref[0]
    out_ref[0] = xn_ref[0] + gt2_ref[0] * (g2[:, 0:1] * ya_ref[...] + g2[:, 1:2] * yb_ref[...])


def _token_specs(tm, nd):
    pick = (lambda f: lambda b, i: f(b, i)) if nd == 2 else (lambda f: lambda b, i, c: f(b, i))
    tok = pl.BlockSpec((1, tm, D_MODEL), pick(lambda b, i: (b, i, 0)))

    def mod_spec(a):
        return pl.BlockSpec((1, 1, D_MODEL), pick(lambda b, i: (b, 0, 0))) if a.shape[1] == 1 else tok

    cst = lambda shape: pl.BlockSpec(shape, pick(lambda b, i: (0, 0)))
    return tok, mod_spec, cst


def _swiglu_mixer(x, o, w_out, gt1, g, shift, scale, gt2, w_gate, w_up, w_down, *, tm):
    gsz, t, _ = x.shape
    d_ff = w_gate.shape[-1]
    assert t % tm == 0 and d_ff % FF_TILE == 0
    tok, mod_spec, cst = _token_specs(tm, 3)
    return pl.pallas_call(
        _ffn_kernel, grid=(gsz, t // tm, d_ff // FF_TILE),
        in_specs=[tok, tok, cst((ATTN_WIDTH, D_MODEL)), mod_spec(gt1), cst((1, D_MODEL)),
                  mod_spec(shift), mod_spec(scale), mod_spec(gt2),
                  pl.BlockSpec((D_MODEL, FF_TILE), lambda b, i, c: (0, c)),
                  pl.BlockSpec((D_MODEL, FF_TILE), lambda b, i, c: (0, c)),
                  pl.BlockSpec((FF_TILE, D_MODEL), lambda b, i, c: (c, 0))],
        out_specs=tok,
        out_shape=jax.ShapeDtypeStruct((gsz, t, D_MODEL), F32),
        scratch_shapes=[pltpu.VMEM((tm, D_MODEL), F32), pltpu.VMEM((tm, D_MODEL), BF16),
                        pltpu.VMEM((tm, D_MODEL), F32)],
        compiler_params=_params(("arbitrary",) * 3),
        name="swiglu_channel_mixer",
    )(x, o, w_out.astype(BF16), gt1, g[None, :], shift, scale, gt2,
      w_gate.astype(BF16), w_up.astype(BF16), w_down.astype(BF16))


def _moe_pre(x, o, w_out, gt1, g, shift, scale, w_router, *, tm):
    gsz, t, _ = x.shape
    assert t % tm == 0
    tok, mod_spec, cst = _token_specs(tm, 2)
    small = pl.BlockSpec((1, tm, N_EXPERTS), lambda b, i: (b, i, 0))
    wr = jnp.pad(w_router.astype(BF16), ((0, 0), (0, LANE - N_EXPERTS)))
    return pl.pallas_call(
        _moe_pre_kernel, grid=(gsz, t // tm),
        in_specs=[tok, tok, cst((ATTN_WIDTH, D_MODEL)), mod_spec(gt1), cst((1, D_MODEL)),
                  mod_spec(shift), mod_spec(scale), cst((D_MODEL, LANE))],
        out_specs=[tok, tok, small, small],
        out_shape=[jax.ShapeDtypeStruct((gsz, t, D_MODEL), F32), jax.ShapeDtypeStruct((gsz, t, D_MODEL), F32),
                   jax.ShapeDtypeStruct((gsz, t, N_EXPERTS), F32), jax.ShapeDtypeStruct((gsz, t, N_EXPERTS), F32)],
        compiler_params=_params(ARB2),
        name="moe_pre_router",
    )(x, o, w_out.astype(BF16), gt1, g[None, :], shift, scale, wr)


def _moe_routing(gates, sel, rows):
    n = gates.shape[0]
    member = sel > 0
    mi = member.astype(jnp.int32)
    counts = mi.sum(0)
    padded = (counts + rows - 1) // rows * rows
    ends = jnp.cumsum(padded)
    slot = jnp.cumsum(mi, axis=1) - mi
    expert_of = [jnp.sum(jnp.where(member & (slot == k), jnp.arange(N_EXPERTS)[None, :], 0), axis=1)
                 for k in range(TOP_K)]
    g2 = jnp.stack([jnp.sum(jnp.where(member & (slot == k), gates, 0.0), axis=1) for k in range(TOP_K)], axis=1)
    n_rows = (2 * n + N_EXPERTS * rows) // rows * rows
    n_pad = n_rows - 2 * n
    pad_ends = jnp.cumsum(padded - counts)
    pad_expert = (jnp.arange(n_pad, dtype=jnp.int32)[:, None] >= pad_ends[None, :]).sum(1)
    tok = jnp.arange(n, dtype=jnp.int32)
    sort_key = jnp.concatenate([expert_of[0], expert_of[1], pad_expert]).astype(jnp.int32)
    pair = jnp.concatenate([tok, n + tok, jnp.full((n_pad,), -1, jnp.int32)])
    _, pair = lax.sort((sort_key, pair), num_keys=1, is_stable=True)
    spare = 2 * n + jnp.arange(n_rows, dtype=jnp.int32) % rows
    src = jnp.where(pair >= 0, pair % n, 0)
    dst = jnp.where(pair >= 0, pair, spare)
    n_tiles = n_rows // rows
    tile_start = jnp.arange(n_tiles, dtype=jnp.int32) * rows
    tile_expert = jnp.minimum((tile_start[:, None] >= ends[None, :]).sum(1), N_EXPERTS - 1).astype(jnp.int32)
    tile_valid = (tile_start < ends[-1]).astype(jnp.int32)
    return src.reshape(n_tiles, rows), dst.reshape(n_tiles, rows), tile_expert, tile_valid, g2


def _moe_experts(h_all, src, dst, tile_expert, tile_valid, w_gate, w_up, w_down, *, rows):
    n = h_all.shape[0]
    n_tiles = src.shape[0]
    d_ff = w_gate.shape[-1]
    any_spec = pl.BlockSpec(memory_space=pl.ANY)
    return pl.pallas_call(
        functools.partial(_moe_expert_kernel, rows=rows),
        grid_spec=pltpu.PrefetchScalarGridSpec(
            num_scalar_prefetch=2, grid=(n_tiles, d_ff // FF_TILE),
            in_specs=[any_spec, any_spec, any_spec,
                      pl.BlockSpec((1, D_MODEL, FF_TILE), lambda i, c, te, tv: (te[i], 0, c)),
                      pl.BlockSpec((1, D_MODEL, FF_TILE), lambda i, c, te, tv: (te[i], 0, c)),
                      pl.BlockSpec((1, FF_TILE, D_MODEL), lambda i, c, te, tv: (te[i], c, 0))],
            out_specs=any_spec,
            scratch_shapes=[pltpu.SMEM((rows,), jnp.int32), pltpu.SMEM((rows,), jnp.int32),
                            pltpu.VMEM((rows, D_MODEL), F32), pltpu.VMEM((rows, D_MODEL), BF16),
                            pltpu.VMEM((rows, D_MODEL), F32), pltpu.SemaphoreType.DMA((2,)),
                            pltpu.SemaphoreType.DMA(()), pltpu.SemaphoreType.DMA(())]),
        out_shape=jax.ShapeDtypeStruct((2 * n + rows, D_MODEL), F32),
        compiler_params=_params(ARB2),
        name="moe_routed_experts",
    )(tile_expert, tile_valid, src, dst, h_all, w_gate.astype(BF16), w_up.astype(BF16), w_down.astype(BF16))


def _moe_combine(xn, gt2, g2, y, *, tm, n_all, tok0):
    gsz, t, _ = xn.shape
    assert t % tm == 0 and n_all % tm == 0 and tok0 % tm == 0
    tok, mod_spec, _ = _token_specs(tm, 2)
    nt = t // tm
    yspec = lambda slot: pl.BlockSpec((tm, D_MODEL), lambda b, i: ((slot * n_all + tok0) // tm + b * nt + i, 0))
    return pl.pallas_call(
        _moe_combine_kernel, grid=(gsz, nt),
        in_specs=[tok, mod_spec(gt2), pl.BlockSpec((1, tm, TOP_K), lambda b, i: (b, i, 0)), yspec(0), yspec(1)],
        out_specs=tok,
        out_shape=jax.ShapeDtypeStruct((gsz, t, D_MODEL), F32),
        compiler_params=_params(ARB2),
        name="moe_combine",
    )(xn, gt2, g2, y, y)


def _per_token(a, n):
    return jnp.broadcast_to(a, (a.shape[0], n, a.shape[2])).reshape(1, a.shape[0] * n, a.shape[2])


def _pad_keys(a, axis, l_pad):
    pad = [(0, 0)] * a.ndim
    pad[axis] = (0, l_pad - a.shape[axis])
    return jnp.pad(a, pad)


def _layer_a(xp, xs, cp, cs, ck, cv, cki, rel_bias, norm_g, w_ada, b_ada, w_in, w_out, g_q, g_k, g_kidx,
             ffn_wg, ffn_wu, ffn_wd):
    bias = _dsa_bias_tiles(rel_bias)
    nb, n = xs.shape[0], xs.shape[1]
    past = ck.shape[1]
    outs = []
    for x, c, sample in ((xp, cp, False), (xs, cs, True)):
        sh1, sc1, gt1, sh2, sc2, gt2 = adaln(c, w_ada, b_ada)
        q16, kf, vf, ktr, vx, kif, qi16, kit, wsc = _project(
            x, norm_g[0], sh1, sc1, w_in, g_q, g_k, g_kidx, n_kv=KV_HEADS_A,
            tm=n if sample else PROJ_TM)
        if sample:
            l_true = past + n
            l_pad = -(-l_true // DSA_SUB) * DSA_SUB
            kfull = jnp.concatenate([ck, kf.reshape(nb, n, KV_HEADS_A, HEAD_DIM)], axis=1).astype(BF16)
            vfull = jnp.concatenate([cv, vf.reshape(nb, n, KV_HEADS_A, HEAD_DIM)], axis=1).astype(BF16)
            kifull = jnp.concatenate([cki, kif], axis=1).astype(BF16)
            ktr = _pad_keys(kfull, 1, l_pad).transpose(0, 2, 3, 1)
            vx = jnp.concatenate([vfull, jnp.ones(vfull.shape[:3] + (1,), BF16),
                                  jnp.zeros(vfull.shape[:3] + (LANE - HEAD_DIM - 1,), BF16)], axis=-1)
            vx = _pad_keys(vx, 1, l_pad).transpose(0, 2, 1, 3)
            kit = _pad_keys(kifull, 1, l_pad).transpose(0, 2, 1)
            o = _dsa_attention(q16, ktr, vx, qi16, kit, wsc, bias, p0=past, l_true=l_true,
                               n_sel=min(TOPK_MAX, l_true // 4), tq=n, tk=l_pad, sw=DSA_SUB, cw=DSA_SUB)
            flat = lambda a: a.reshape(1, nb * n, a.shape[-1])
            y = _swiglu_mixer(flat(x), flat(o), w_out, _per_token(gt1, n), norm_g[1], _per_token(sh2, n),
                              _per_token(sc2, n), _per_token(gt2, n), ffn_wg, ffn_wu, ffn_wd, tm=MIX_TM_SAMPLE)
            y = y.reshape(x.shape)
        else:
            t = x.shape[1]
            o = _dsa_attention(q16, ktr, vx, qi16, kit, wsc, bias, p0=0, l_true=t,
                               n_sel=min(TOPK_MAX, t // 4), tq=LANE, tk=DSA_KEY_TILE, sw=DSA_SUB,
                               cw=DSA_KEY_TILE)
            y = _swiglu_mixer(x, o, w_out, gt1, norm_g[1], sh2, sc2, gt2, ffn_wg, ffn_wu, ffn_wd, tm=MIX_TM_PROMPT)
        bsz, t = x.shape[0], x.shape[1]
        outs.append((y, kf.reshape(bsz, t, KV_HEADS_A, HEAD_DIM), vf.reshape(bsz, t, KV_HEADS_A, HEAD_DIM), kif))
    return outs


def _layer_b(xp, xs, cp, cs, ck, cv, rel_bias, norm_g, w_ada, b_ada, w_in, w_out, g_q, g_k, sinks,
             w_router, moe_wg, moe_wu, moe_wd):
    bias = _band_bias(rel_bias)
    nb, n = xs.shape[0], xs.shape[1]
    streams = []
    for x, c, sample in ((xp, cp, False), (xs, cs, True)):
        sh1, sc1, gt1, sh2, sc2, gt2 = adaln(c, w_ada, b_ada)
        q16, kf, vf, ktr, v16 = _project(x, norm_g[0], sh1, sc1, w_in, g_q, g_k, n_kv=KV_HEADS_B,
                                         tm=n if sample else PROJ_TM)
        bsz, t = x.shape[0], x.shape[1]
        k4 = kf.reshape(bsz, t, KV_HEADS_B, HEAD_DIM)
        v4 = vf.reshape(bsz, t, KV_HEADS_B, HEAD_DIM)
        zero = lambda i: 0
        if sample:
            kp = ck.astype(BF16).transpose(0, 2, 3, 1)
            vp = cv.astype(BF16).transpose(0, 2, 1, 3)
            o = _band_attention(q16, kp, _pad_keys(ktr, 3, LANE), vp, _pad_keys(v16, 2, LANE), bias, sinks,
                                tq=n, prev_map=zero, cur_map=zero, first_has_no_prev=False)
            flat = lambda a: a.reshape(1, nb * n, a.shape[-1])
            xn, h, gates, sel = _moe_pre(flat(x), flat(o), w_out, _per_token(gt1, n), norm_g[1],
                                         _per_token(sh2, n), _per_token(sc2, n), w_router, tm=MIX_TM_SAMPLE)
            gt2 = _per_token(gt2, n)
            k_new = jnp.concatenate([ck, k4], axis=1)[:, -WINDOW:]
            v_new = jnp.concatenate([cv, v4], axis=1)[:, -WINDOW:]
        else:
            o = _band_attention(q16, ktr, ktr, v16, v16, bias, sinks, tq=LANE,
                                prev_map=lambda i: jnp.maximum(i - 1, 0), cur_map=lambda i: i,
                                first_has_no_prev=True)
            xn, h, gates, sel = _moe_pre(x, o, w_out, gt1, norm_g[1], sh2, sc2, w_router, tm=MIX_TM_PROMPT)
            k_new, v_new = k4[:, -WINDOW:], v4[:, -WINDOW:]
        streams.append((x.shape, xn, h, gates, sel, gt2, k_new, v_new))

    rows2d = lambda a: a.reshape(-1, a.shape[-1])
    h_all = jnp.concatenate([rows2d(st[2]) for st in streams])
    n_all = h_all.shape[0]
    src, dst, tile_expert, tile_valid, g2 = _moe_routing(
        jnp.concatenate([rows2d(st[3]) for st in streams]), jnp.concatenate([rows2d(st[4]) for st in streams]),
        MOE_ROW_TILE)
    y = _moe_experts(h_all, src, dst, tile_expert, tile_valid, moe_wg, moe_wu, moe_wd, rows=MOE_ROW_TILE)
    outs, tok0 = [], 0
    for shape, xn, _, _, _, gt2, k_new, v_new in streams:
        cnt = xn.shape[0] * xn.shape[1]
        out = _moe_combine(xn, gt2, g2[tok0:tok0 + cnt].reshape(xn.shape[0], xn.shape[1], TOP_K), y,
                           tm=MOE_COMBINE_TM, n_all=n_all, tok0=tok0)
        outs.append((out.reshape(shape), k_new, v_new))
        tok0 += cnt
    return outs


def kernel(x_prompt, x_sample, c_prompt, c_sample, cache_a_k, cache_a_v, cache_a_kidx,
           cache_b_k, cache_b_v, rel_bias, norm_g, w_ada, b_ada,
           a_w_in, a_w_out, a_g_q, a_g_k, a_g_kidx,
           b_w_in, b_w_out, b_g_q, b_g_k, b_sinks,
           ffn_w_gate, ffn_w_up, ffn_w_down,
           moe_w_router, moe_w_gate, moe_w_up, moe_w_down):
    xp, xs = x_prompt, x_sample
    a_out = [[] for _ in range(6)]
    b_out = [[] for _ in range(4)]
    for i in range(DEPTH):
        j = i // 2
        if i % 2 == 0:
            (xp, k1, v1, i1), (xs, k2, v2, i2) = _layer_a(
                xp, xs, c_prompt, c_sample, cache_a_k[j], cache_a_v[j], cache_a_kidx[j], rel_bias, norm_g[i],
                w_ada[i], b_ada[i], a_w_in[j], a_w_out[j], a_g_q[j], a_g_k[j], a_g_kidx[j],
                ffn_w_gate[j], ffn_w_up[j], ffn_w_down[j])
            for lst, val in zip(a_out, (k1, v1, i1, k2, v2, i2)):
                lst.append(val)
        else:
            (xp, k1, v1), (xs, k2, v2) = _layer_b(
                xp, xs, c_prompt, c_sample, cache_b_k[j], cache_b_v[j], rel_bias, norm_g[i],
                w_ada[i], b_ada[i], b_w_in[j], b_w_out[j], b_g_q[j], b_g_k[j], b_sinks[j],
                moe_w_router[j], moe_w_gate[j], moe_w_up[j], moe_w_down[j])
            for lst, val in zip(b_out, (k1, v1, k2, v2)):
                lst.append(val)
    return (xp, xs, *(jnp.stack(l) for l in a_out), *(jnp.stack(l) for l in b_out))
```

```python
import math, functools
import jax, jax.numpy as jnp
from jax import lax
import numpy as np
from jax.experimental import pallas as pl
from jax.experimental.pallas import tpu as pltpu

D_MODEL = 1024
DEPTH = 2
CHUNK = 64
N_HEADS = 16
HEAD_DIM = 64
ATTN_WIDTH = N_HEADS * HEAD_DIM
KV_HEADS_A = 4
IDX_HEADS = 8
IDX_DIM = 64
TOPK_MAX = 256
KV_HEADS_B = 2
WINDOW = 128
NUM_BUCKETS = 32
MAX_DISTANCE = 1024
N_EXPERTS = 8
TOP_K = 2
EPS = 1e-6

F32 = jnp.float32
BF16 = jnp.bfloat16
LANE = 128
VMEM_LIMIT_BYTES = 48 * 1024 * 1024
LOG2E = math.log2(math.e)
NEG_BIG = -1e30
KEY_MIN = -2 ** 31
FAR_BUCKET = NUM_BUCKETS // 2 - 1
NEAR_BLOCKS = 6
FF_TILE = 512
MIX_TM_PROMPT = 1024
MIX_TM_SAMPLE = 256
MOE_ROW_TILE = 1024
MOE_COMBINE_TM = 256
DMA_BURST = 8
PROJ_TM = 256
DSA_SUB = 2 * LANE
DSA_KEY_TILE = 1024
CHUNK_SHIFT = CHUNK.bit_length() - 1
LANE_SHIFT = LANE.bit_length() - 1
COUNT_ALL = float(2 ** 24)
ARB2 = ("arbitrary", "arbitrary")


def _params(sem, vmem=True):
    return pltpu.CompilerParams(dimension_semantics=sem, vmem_limit_bytes=VMEM_LIMIT_BYTES if vmem else None)


def rel_bucket(rel):
    half = NUM_BUCKETS // 2
    max_exact = half // 2
    base = jnp.where(rel > 0, half, 0)
    n = jnp.abs(rel)
    nf = jnp.maximum(n, 1).astype(F32)
    large = max_exact + (jnp.log(nf / max_exact) / math.log(MAX_DISTANCE / max_exact)
                         * (half - max_exact)).astype(jnp.int32)
    large = jnp.minimum(large, half - 1)
    return base + jnp.where(n < max_exact, n, large)


def _adaln_kernel(c_ref, w_ref, b_ref, o_ref):
    c = c_ref[...]
    o_ref[...] = jnp.dot(c * jax.nn.sigmoid(c), w_ref[...], preferred_element_type=F32) + b_ref[...]


def adaln(c, w, b):
    n = c.shape[0]
    mod = pl.pallas_call(
        _adaln_kernel,
        grid=(6,),
        in_specs=[pl.BlockSpec((n, D_MODEL), lambda j: (0, 0)),
                  pl.BlockSpec((D_MODEL, D_MODEL), lambda j: (0, j)),
                  pl.BlockSpec((1, D_MODEL), lambda j: (0, j))],
        out_specs=pl.BlockSpec((n, D_MODEL), lambda j: (0, j)),
        out_shape=jax.ShapeDtypeStruct((n, 6 * D_MODEL), F32),
        name="adaln",
    )(c, w, b[None, :])
    return jnp.split(mod[:, None, :], 6, axis=-1)


def _modulate(x, g, shift, scale):
    y = x * lax.rsqrt(jnp.mean(x * x, axis=-1, keepdims=True) + EPS)
    return (y * g) * (1.0 + scale) + shift


def _seg_norm(seg, g):
    return seg * lax.rsqrt(jnp.mean(seg * seg, axis=-1, keepdims=True) + EPS) * g


def _proj_kernel(x_ref, g_ref, sh_ref, sc_ref, w_ref, gq_ref, gk_ref, *rest, n_kv, has_idx):
    h = _modulate(x_ref[0], g_ref[...], sh_ref[0], sc_ref[0]).astype(BF16)
    y = jnp.dot(h, w_ref[...], preferred_element_type=F32)
    tm = y.shape[0]
    kv_w = n_kv * HEAD_DIM
    if has_idx:
        gki_ref, q16, kf, vf, ktr, vx, kif, qi16, kit, wsc = rest
    else:
        q16, kf, vf, ktr, v16 = rest
    for hd in range(N_HEADS):
        seg = y[:, hd * HEAD_DIM:(hd + 1) * HEAD_DIM]
        q_scale = HEAD_DIM ** -0.5 * (LOG2E if has_idx else 1.0)
        q16[0, hd] = (_seg_norm(seg, gq_ref[...]) * q_scale).astype(BF16)
    lane = lax.broadcasted_iota(jnp.int32, (tm, LANE - HEAD_DIM), 1)
    ones_col = jnp.where(lane == 0, 1.0, 0.0).astype(BF16)
    for j in range(n_kv):
        kseg = _seg_norm(y[:, ATTN_WIDTH + j * HEAD_DIM:ATTN_WIDTH + (j + 1) * HEAD_DIM], gk_ref[...])
        vseg = y[:, ATTN_WIDTH + kv_w + j * HEAD_DIM:ATTN_WIDTH + kv_w + (j + 1) * HEAD_DIM]
        kf[0, :, j * HEAD_DIM:(j + 1) * HEAD_DIM] = kseg
        vf[0, :, j * HEAD_DIM:(j + 1) * HEAD_DIM] = vseg
        ktr[0, j] = kseg.T.astype(BF16)
        if has_idx:
            vx[0, j] = jnp.concatenate([vseg.astype(BF16), ones_col], axis=1)
        else:
            v16[0, j] = vseg.astype(BF16)
    if has_idx:
        base = ATTN_WIDTH + 2 * kv_w
        for hd in range(IDX_HEADS):
            qi16[0, hd] = y[:, base + hd * IDX_DIM:base + (hd + 1) * IDX_DIM].astype(BF16)
        base += IDX_HEADS * IDX_DIM
        kiseg = _seg_norm(y[:, base:base + IDX_DIM], gki_ref[...])
        kif[0] = kiseg
        kit[0] = kiseg.T.astype(BF16)
        wsc[0] = y[:, base + IDX_DIM:base + IDX_DIM + IDX_HEADS] * IDX_HEADS ** -0.5 * IDX_DIM ** -0.5


def _project(x, g, shift, scale, w_in, g_q, g_k, g_kidx=None, *, n_kv, tm):
    has_idx = g_kidx is not None
    gsz, t, _ = x.shape
    n_in = w_in.shape[1]
    kv_w = n_kv * HEAD_DIM
    tok = lambda b, i: (b, i, 0)
    head = lambda b, i: (b, 0, i, 0)
    vec = lambda b, i: (b, 0, 0)
    cst = lambda b, i: (0, 0)
    shapes = [((gsz, N_HEADS, t, HEAD_DIM), BF16, (1, N_HEADS, tm, HEAD_DIM), head),
              ((gsz, t, kv_w), F32, (1, tm, kv_w), tok),
              ((gsz, t, kv_w), F32, (1, tm, kv_w), tok),
              ((gsz, n_kv, HEAD_DIM, t), BF16, (1, n_kv, HEAD_DIM, tm), lambda b, i: (b, 0, 0, i))]
    if has_idx:
        shapes += [((gsz, n_kv, t, LANE), BF16, (1, n_kv, tm, LANE), head),
                   ((gsz, t, IDX_DIM), F32, (1, tm, IDX_DIM), tok),
                   ((gsz, IDX_HEADS, t, IDX_DIM), BF16, (1, IDX_HEADS, tm, IDX_DIM), head),
                   ((gsz, IDX_DIM, t), BF16, (1, IDX_DIM, tm), lambda b, i: (b, 0, i)),
                   ((gsz, t, IDX_HEADS), F32, (1, tm, IDX_HEADS), tok)]
    else:
        shapes += [((gsz, n_kv, t, HEAD_DIM), BF16, (1, n_kv, tm, HEAD_DIM), head)]
    return pl.pallas_call(
        functools.partial(_proj_kernel, n_kv=n_kv, has_idx=has_idx),
        grid=(gsz, t // tm),
        in_specs=[pl.BlockSpec((1, tm, D_MODEL), tok), pl.BlockSpec((1, D_MODEL), cst),
                  pl.BlockSpec((1, 1, D_MODEL), vec), pl.BlockSpec((1, 1, D_MODEL), vec),
                  pl.BlockSpec((D_MODEL, n_in), cst), pl.BlockSpec((1, HEAD_DIM), cst),
                  pl.BlockSpec((1, HEAD_DIM), cst)] + ([pl.BlockSpec((1, IDX_DIM), cst)] if has_idx else []),
        out_specs=[pl.BlockSpec(blk, im) for _, _, blk, im in shapes],
        out_shape=[jax.ShapeDtypeStruct(s, d) for s, d, _, _ in shapes],
        compiler_params=_params(ARB2),
        name="mixer_in_proj",
    )(x, g[None, :], shift, scale, w_in.astype(BF16), g_q[None, :], g_k[None, :],
      *([g_kidx[None, :]] if has_idx else []))


def _tri_steps(nqb, tq, tk, p0):
    qbs, kts = [], []
    for qb in range(nqb):
        vis_end = ((p0 + qb * tq + tq - 1) // CHUNK + 1) * CHUNK
        for kt in range(-(-vis_end // tk)):
            qbs.append(qb)
            kts.append(kt)
    return jnp.asarray(np.array(qbs, np.int32)), jnp.asarray(np.array(kts, np.int32))


def _idx_kernel(qb_tab, kt_tab, qi_ref, w_ref, kit_ref, out_ref, *, tq, tk, sw, p0, l_true):
    step = pl.program_id(1)
    q0 = p0 + qb_tab[step] * tq
    k0 = kt_tab[step] * tk
    row = lax.broadcasted_iota(jnp.int32, (tq, 1), 0) + q0
    vis_end = jnp.minimum(((row >> CHUNK_SHIFT) + 1) << CHUNK_SHIFT, l_true)
    wv = w_ref[0]
    for c in range(tk // sw):
        kit = kit_ref[0, :, c * sw:(c + 1) * sw]
        acc = jnp.zeros((tq, sw), F32)
        for h in range(IDX_HEADS):
            sh = jnp.dot(qi_ref[0, h], kit, preferred_element_type=F32)
            acc = acc + wv[:, h:h + 1] * jnp.maximum(sh, 0.0)
        bits = lax.bitcast_convert_type(acc, jnp.int32)
        key = bits ^ ((bits >> 31) & 0x7FFFFFFF)
        kpos = k0 + c * sw + lax.broadcasted_iota(jnp.int32, (tq, sw), 1)
        out_ref[0, :, c * sw:(c + 1) * sw] = jnp.where(kpos < vis_end, key, KEY_MIN)


def _thr_kernel(keys_ref, tau_ref, quota_ref, flag_ref, *, tq, cw, p0, n_sel):
    qb = pl.program_id(1)
    vis_end = (((p0 + qb * tq + tq - 1) >> CHUNK_SHIFT) + 1) << CHUNK_SHIFT
    nch = (vis_end + cw - 1) // cw
    key_max = 2 ** 31 - 1

    def count_ge(cand):
        def body(j, acc):
            x = keys_ref[0, :, pl.ds(pl.multiple_of(j * cw, cw), cw)]
            for u in range(cw // LANE):
                acc = acc + jnp.where(x[:, u * LANE:(u + 1) * LANE] >= cand, 1.0, 0.0)
            return acc
        acc = lax.fori_loop(0, nch, body, jnp.zeros((tq, LANE), F32))
        return jnp.sum(acc, axis=1, keepdims=True)

    zero = jnp.zeros((tq, LANE), jnp.int32)
    c0 = count_ge(zero)
    tau = jnp.where(c0 >= n_sel, zero, KEY_MIN)
    n_ge = jnp.where(c0 >= n_sel, c0, COUNT_ALL)

    def bit_body(carry):
        i, tau, n_ge = carry
        cand = tau + jnp.left_shift(jnp.int32(1), 30 - i)
        c = count_ge(cand)
        return i + 1, jnp.where(c >= n_sel, cand, tau), jnp.where(c >= n_sel, c, n_ge)

    def unresolved(carry):
        i, _, n_ge = carry
        return (i < 31) & (jnp.max(jnp.abs(n_ge - n_sel)) > 0.0)

    _, tau, n_ge = lax.while_loop(unresolved, bit_body, (jnp.int32(0), tau, n_ge))
    n_gt = jnp.where(tau[:, :1] < key_max, count_ge(jnp.minimum(tau, key_max - 1) + 1), 0.0)
    tau_ref[0] = jnp.maximum(tau, KEY_MIN + 1)
    quota_ref[0] = jnp.broadcast_to(n_sel - n_gt, (tq, LANE))
    over = jnp.max(jnp.where(n_ge > n_sel, 1, 0), axis=0, keepdims=True)
    flag_ref[0, 0] = jnp.broadcast_to(over, flag_ref.shape[2:])


def _attn_kernel(qb_tab, kt_tab, flag_tab, q_ref, kt_ref, vx_ref, keys_ref, tau_ref, quota_ref, bias_ref, o_ref,
                 m_ref, acc_ref, negm_ref, seen_ref, *, tq, tk, sw, p0, nqb):
    step = pl.program_id(1)
    qb, kt = qb_tab[step], kt_tab[step]
    q0 = p0 + qb * tq
    k0 = kt * tk
    vis_end = (((q0 + tq - 1) >> CHUNK_SHIFT) + 1) << CHUNK_SHIFT
    group = N_HEADS // KV_HEADS_A
    nblk = sw // LANE
    tie_w = 2 * LANE

    @pl.when(kt == 0)
    def _():
        m_ref[...] = jnp.full(m_ref.shape, NEG_BIG, F32)
        acc_ref[...] = jnp.zeros(acc_ref.shape, F32)
        seen_ref[...] = jnp.zeros(seen_ref.shape, F32)

    has_ties = flag_tab[pl.program_id(0) * nqb + qb] != 0
    tau = tau_ref[0]

    @pl.when(jnp.logical_not(has_ties))
    def _():
        for u in range(tk // LANE):
            cols = slice(u * LANE, (u + 1) * LANE)
            negm_ref[:, cols] = jnp.where(keys_ref[0, :, cols] >= tau, 0.0, NEG_BIG)

    @pl.when(has_ties)
    def _():
        tau2 = jnp.concatenate([tau] * (tie_w // LANE), axis=1)
        quota = jnp.concatenate([quota_ref[0]] * (tie_w // LANE), axis=1)
        before = (lax.broadcasted_iota(jnp.int32, (tie_w, tie_w), 0)
                  < lax.broadcasted_iota(jnp.int32, (tie_w, tie_w), 1))
        before = jnp.where(before, 1.0, 0.0).astype(BF16)
        seen = seen_ref[...]
        for u in range(tk // tie_w):
            cols = slice(u * tie_w, (u + 1) * tie_w)
            k = keys_ref[0, :, cols]
            tie = jnp.where(k == tau2, 1.0, 0.0)
            rank = (jnp.dot(tie.astype(BF16), before, preferred_element_type=F32)
                    + jnp.concatenate([seen] * (tie_w // LANE), axis=1))
            keep = jnp.where(k > tau2, 1.0, jnp.where(rank < quota, tie, 0.0))
            negm_ref[:, cols] = jnp.where(keep > 0.0, 0.0, NEG_BIG)
            seen = seen + jnp.sum(tie, axis=1, keepdims=True)
        seen_ref[...] = seen

    def sub_tile(c, with_bias):
        off = c * sw if isinstance(c, int) else pl.multiple_of(c * sw, sw)
        negm = negm_ref[:, pl.ds(off, sw)]
        if with_bias:
            d0 = ((k0 + off - q0) >> LANE_SHIFT) + NEAR_BLOCKS
            bidx = [jnp.clip(d0 + u, 0, NEAR_BLOCKS) for u in range(nblk)]
        for h in range(N_HEADS):
            j = h // group
            s = jnp.dot(q_ref[0, h], kt_ref[0, j, :, pl.ds(off, sw)], preferred_element_type=F32) + negm
            if with_bias:
                s = s + jnp.concatenate([bias_ref[bidx[u], h] for u in range(nblk)], axis=1)
            m_prev = m_ref[h]
            m_cur = jnp.maximum(m_prev, jnp.max(s, axis=1, keepdims=True))
            alpha = jnp.exp2(m_prev - m_cur)
            p = jnp.exp2(s - jnp.concatenate([m_cur] * nblk, axis=1))
            pv = jnp.dot(p.astype(BF16), vx_ref[0, j, pl.ds(off, sw), :], preferred_element_type=F32)
            acc_ref[h] = alpha * acc_ref[h] + pv
            m_ref[h] = m_cur

    tile_far = k0 + tk <= q0 - (NEAR_BLOCKS - 1) * LANE

    @pl.when(tile_far)
    def _():
        for c in range(tk // sw):
            sub_tile(c, False)

    @pl.when(jnp.logical_not(tile_far))
    def _():
        nsub = jnp.minimum(tk // sw, (vis_end - k0 + sw - 1) // sw)
        nfar = jnp.clip((q0 - (NEAR_BLOCKS - 1) * LANE - k0) // sw, 0, nsub)

        def far_body(c, carry):
            sub_tile(c, False)
            return carry

        def near_body(c, carry):
            sub_tile(c, True)
            return carry

        lax.fori_loop(0, nfar, far_body, 0)
        lax.fori_loop(nfar, nsub, near_body, 0)

    @pl.when(k0 + tk >= vis_end)
    def _():
        for h in range(N_HEADS):
            a = acc_ref[h]
            o_ref[0, :, h * HEAD_DIM:(h + 1) * HEAD_DIM] = a[:, :HEAD_DIM] / a[:, HEAD_DIM:HEAD_DIM + 1]


def _rel_tiles(rel_table, rel):
    onehot = jax.nn.one_hot(rel_bucket(rel), NUM_BUCKETS, dtype=F32)
    return jnp.einsum("nrsk,kh->nhrs", onehot, rel_table.astype(F32), precision=lax.Precision.HIGHEST)


def _dsa_bias_tiles(rel_table):
    d = jnp.arange(NEAR_BLOCKS + 1, dtype=jnp.int32)[:, None, None] - NEAR_BLOCKS
    rel = (d * LANE + jnp.arange(LANE, dtype=jnp.int32)[None, None, :]
           - jnp.arange(LANE, dtype=jnp.int32)[None, :, None])
    tiles = (_rel_tiles(rel_table, rel) - rel_table[FAR_BUCKET][None, :, None, None]) * LOG2E
    return tiles.at[0].set(0.0)


def _dsa_attention(q16, ktr, vx, qi16, kit, w, bias, *, p0, l_true, n_sel, tq, tk, sw, cw):
    bsz, t = q16.shape[0], q16.shape[2]
    l_pad = ktr.shape[3]
    nqb = t // tq
    assert t % tq == 0 and p0 % LANE == 0 and (tq == LANE or nqb == 1)
    assert l_pad % tk == 0 and tk % sw == 0 and tk % cw == 0 and tk % (2 * LANE) == 0
    qbt, ktt = _tri_steps(nqb, tq, tk, p0)
    nsteps = int(qbt.shape[0])

    keys = pl.pallas_call(
        functools.partial(_idx_kernel, tq=tq, tk=tk, sw=sw, p0=p0, l_true=l_true),
        grid_spec=pltpu.PrefetchScalarGridSpec(
            num_scalar_prefetch=2, grid=(bsz, nsteps),
            in_specs=[pl.BlockSpec((1, IDX_HEADS, tq, IDX_DIM), lambda b, s, qt, kt: (b, 0, qt[s], 0)),
                      pl.BlockSpec((1, tq, IDX_HEADS), lambda b, s, qt, kt: (b, qt[s], 0)),
                      pl.BlockSpec((1, IDX_DIM, tk), lambda b, s, qt, kt: (b, 0, kt[s]))],
            out_specs=pl.BlockSpec((1, tq, tk), lambda b, s, qt, kt: (b, qt[s], kt[s]))),
        out_shape=jax.ShapeDtypeStruct((bsz, t, l_pad), jnp.int32),
        compiler_params=_params(ARB2, vmem=False),
        name="dsa_index_keys",
    )(qbt, ktt, qi16, w, kit)

    tau, quota, flags = pl.pallas_call(
        functools.partial(_thr_kernel, tq=tq, cw=cw, p0=p0, n_sel=n_sel),
        grid=(bsz, nqb),
        in_specs=[pl.BlockSpec((1, tq, l_pad), lambda b, i: (b, i, 0))],
        out_specs=[pl.BlockSpec((1, tq, LANE), lambda b, i: (b, i, 0)),
                   pl.BlockSpec((1, tq, LANE), lambda b, i: (b, i, 0)),
                   pl.BlockSpec((1, 1, 8, LANE), lambda b, i: (b, i, 0, 0))],
        out_shape=[jax.ShapeDtypeStruct((bsz, t, LANE), jnp.int32),
                   jax.ShapeDtypeStruct((bsz, t, LANE), F32),
                   jax.ShapeDtypeStruct((bsz, nqb, 8, LANE), jnp.int32)],
        compiler_params=_params(ARB2),
        name="dsa_threshold",
    )(keys)
    flags = flags[:, :, 0, 0].reshape(bsz * nqb)

    return pl.pallas_call(
        functools.partial(_attn_kernel, tq=tq, tk=tk, sw=sw, p0=p0, nqb=nqb),
        grid_spec=pltpu.PrefetchScalarGridSpec(
            num_scalar_prefetch=3, grid=(bsz, nsteps),
            in_specs=[pl.BlockSpec((1, N_HEADS, tq, HEAD_DIM), lambda b, s, qt, kt, fl: (b, 0, qt[s], 0)),
                      pl.BlockSpec((1, KV_HEADS_A, HEAD_DIM, tk), lambda b, s, qt, kt, fl: (b, 0, 0, kt[s])),
                      pl.BlockSpec((1, KV_HEADS_A, tk, LANE), lambda b, s, qt, kt, fl: (b, 0, kt[s], 0)),
                      pl.BlockSpec((1, tq, tk), lambda b, s, qt, kt, fl: (b, qt[s], kt[s])),
                      pl.BlockSpec((1, tq, LANE), lambda b, s, qt, kt, fl: (b, qt[s], 0)),
                      pl.BlockSpec((1, tq, LANE), lambda b, s, qt, kt, fl: (b, qt[s], 0)),
                      pl.BlockSpec((NEAR_BLOCKS + 1, N_HEADS, tq, LANE), lambda b, s, qt, kt, fl: (0, 0, 0, 0))],
            out_specs=pl.BlockSpec((1, tq, ATTN_WIDTH), lambda b, s, qt, kt, fl: (b, qt[s], 0)),
            scratch_shapes=[pltpu.VMEM((N_HEADS, tq, LANE), F32), pltpu.VMEM((N_HEADS, tq, LANE), F32),
                            pltpu.VMEM((tq, tk), F32), pltpu.VMEM((tq, LANE), F32)]),
        out_shape=jax.ShapeDtypeStruct((bsz, t, ATTN_WIDTH), F32),
        compiler_params=_params(ARB2),
        name="dsa_masked_attention",
    )(qbt, ktt, flags, q16, ktr, vx, keys, tau, quota, bias[:, :, :tq, :])


def _band_kernel(q_ref, kp_ref, kc_ref, vp_ref, vc_ref, bias_ref, sink_ref, o_ref, *, tq, first_has_no_prev):
    group = N_HEADS // KV_HEADS_B
    extra = None
    if first_has_no_prev:
        col = lax.broadcasted_iota(jnp.int32, (tq, 2 * LANE), 1)
        extra = jnp.where((pl.program_id(1) == 0) & (col < LANE), NEG_BIG, 0.0)
    for h in range(N_HEADS):
        j = h // group
        q = q_ref[0, h]
        s = jnp.concatenate([jnp.dot(q, kp_ref[0, j], preferred_element_type=F32),
                             jnp.dot(q, kc_ref[0, j], preferred_element_type=F32)], axis=1) + bias_ref[h]
        if extra is not None:
            s = s + extra
        sink = sink_ref[h]
        m = jnp.maximum(jnp.max(s, axis=1, keepdims=True), sink[:, :1])
        e = jnp.exp(s - m)
        p = e / (jnp.sum(e, axis=1, keepdims=True) + jnp.exp(sink[:, :1] - m))
        pb = p.astype(BF16)
        o_ref[0, :, h * HEAD_DIM:(h + 1) * HEAD_DIM] = (
            jnp.dot(pb[:, :LANE], vp_ref[0, j], preferred_element_type=F32)
            + jnp.dot(pb[:, LANE:], vc_ref[0, j], preferred_element_type=F32))


def _band_bias(rel_table):
    r = jnp.arange(LANE, dtype=jnp.int32)[:, None]
    s = jnp.arange(2 * LANE, dtype=jnp.int32)[None, :]
    wc, qh = s // CHUNK, r // CHUNK
    band = (wc >= qh) & (wc <= qh + 2)
    tiles = _rel_tiles(rel_table, (s - LANE - r)[None])[0]
    return jnp.where(band[None], tiles, NEG_BIG)


def _band_attention(q16, kp, kc, vp, vc, bias, sinks, *, tq, prev_map, cur_map, first_has_no_prev):
    bsz, t = q16.shape[0], q16.shape[2]
    kblk = lambda m: pl.BlockSpec((1, KV_HEADS_B, HEAD_DIM, LANE), lambda b, i: (b, 0, 0, m(i)))
    vblk = lambda m: pl.BlockSpec((1, KV_HEADS_B, LANE, HEAD_DIM), lambda b, i: (b, 0, m(i), 0))
    sink_rows = jnp.broadcast_to(sinks.astype(F32)[:, None, None], (N_HEADS, 1, LANE))
    return pl.pallas_call(
        functools.partial(_band_kernel, tq=tq, first_has_no_prev=first_has_no_prev),
        grid=(bsz, t // tq),
        in_specs=[pl.BlockSpec((1, N_HEADS, tq, HEAD_DIM), lambda b, i: (b, 0, i, 0)),
                  kblk(prev_map), kblk(cur_map), vblk(prev_map), vblk(cur_map),
                  pl.BlockSpec((N_HEADS, tq, 2 * LANE), lambda b, i: (0, 0, 0)),
                  pl.BlockSpec((N_HEADS, 1, LANE), lambda b, i: (0, 0, 0))],
        out_specs=pl.BlockSpec((1, tq, ATTN_WIDTH), lambda b, i: (b, i, 0)),
        out_shape=jax.ShapeDtypeStruct((bsz, t, ATTN_WIDTH), F32),
        compiler_params=_params(ARB2, vmem=False),
        name="band_attention",
    )(q16, kp, kc, vp, vc, bias[:, :tq, :], sink_rows)


def _mixer_residual(x_ref, o_ref, wo_ref, gt1_ref):
    return x_ref[0] + gt1_ref[0] * jnp.dot(o_ref[0].astype(BF16), wo_ref[...], preferred_element_type=F32)


def _swiglu_chunk(h, wg, wu, wd):
    a = jax.nn.silu(jnp.dot(h, wg, preferred_element_type=F32)) * jnp.dot(h, wu, preferred_element_type=F32)
    return jnp.dot(a.astype(BF16), wd, preferred_element_type=F32)


def _ffn_kernel(x_ref, o_ref, wo_ref, gt1_ref, g_ref, sh_ref, sc_ref, gt2_ref, wg_ref, wu_ref, wd_ref,
                out_ref, xn_scr, h_scr, acc_scr):
    c = pl.program_id(2)

    @pl.when(c == 0)
    def _():
        xn = _mixer_residual(x_ref, o_ref, wo_ref, gt1_ref)
        xn_scr[...] = xn
        h_scr[...] = _modulate(xn, g_ref[...], sh_ref[0], sc_ref[0]).astype(BF16)
        acc_scr[...] = jnp.zeros(acc_scr.shape, F32)

    acc_scr[...] += _swiglu_chunk(h_scr[...], wg_ref[...], wu_ref[...], wd_ref[...])

    @pl.when(c == pl.num_programs(2) - 1)
    def _():
        out_ref[0] = xn_scr[...] + gt2_ref[0] * acc_scr[...]


def _moe_pre_kernel(x_ref, o_ref, wo_ref, gt1_ref, g_ref, sh_ref, sc_ref, wr_ref, xn_ref, h_ref, gate_ref, sel_ref):
    xn = _mixer_residual(x_ref, o_ref, wo_ref, gt1_ref)
    xn_ref[0] = xn
    h = _modulate(xn, g_ref[...], sh_ref[0], sc_ref[0])
    h_ref[0] = h
    lane = lax.broadcasted_iota(jnp.int32, (h.shape[0], LANE), 1)
    lanef = lane.astype(F32)
    logits = jnp.dot(h.astype(BF16), wr_ref[...], preferred_element_type=F32)
    logits = jnp.where(lane < N_EXPERTS, logits, -jnp.inf)
    m1 = jnp.max(logits, axis=1, keepdims=True)
    i1 = jnp.min(jnp.where(logits == m1, lanef, float(LANE)), axis=1, keepdims=True)
    rest = jnp.where(lanef == i1, -jnp.inf, logits)
    m2 = jnp.max(rest, axis=1, keepdims=True)
    i2 = jnp.min(jnp.where(rest == m2, lanef, float(LANE)), axis=1, keepdims=True)
    e2 = jnp.exp(m2 - m1)
    den = 1.0 + e2
    gates = jnp.where(lanef == i1, 1.0 / den, 0.0) + jnp.where(lanef == i2, e2 / den, 0.0)
    sel = jnp.where(lanef == i1, 1.0, 0.0) + jnp.where(lanef == i2, 1.0, 0.0)
    gate_ref[0] = gates[:, :N_EXPERTS]
    sel_ref[0] = sel[:, :N_EXPERTS]


def _moe_expert_kernel(te_tab, tv_tab, src_hbm, dst_hbm, h_hbm, wg_ref, wu_ref, wd_ref, y_hbm,
                       src_smem, dst_smem, hbuf, hb, acc, sem_idx, sem_in, sem_out, *, rows):
    i, c = pl.program_id(0), pl.program_id(1)
    valid = tv_tab[i] != 0
    all_rows_in = pltpu.make_async_copy(h_hbm.at[pl.ds(0, rows), :], hbuf, sem_in)
    all_rows_out = pltpu.make_async_copy(acc, y_hbm.at[pl.ds(0, rows), :], sem_out)

    @pl.when(valid & (c == 0))
    def _():
        idx_copies = [pltpu.make_async_copy(src_hbm.at[i], src_smem, sem_idx.at[0]),
                      pltpu.make_async_copy(dst_hbm.at[i], dst_smem, sem_idx.at[1])]
        for cp in idx_copies:
            cp.start()
        for cp in idx_copies:
            cp.wait()

        def gather_rows(r0, carry):
            for u in range(DMA_BURST):
                r = r0 * DMA_BURST + u
                pltpu.make_async_copy(h_hbm.at[pl.ds(src_smem[r], 1), :], hbuf.at[pl.ds(r, 1), :],
                                      sem_in).start(priority=u % 2)
            return carry
        lax.fori_loop(0, rows // DMA_BURST, gather_rows, 0)
        all_rows_in.wait()
        hb[...] = hbuf[...].astype(BF16)
        acc[...] = jnp.zeros(acc.shape, F32)

    @pl.when(valid)
    def _():
        acc[...] += _swiglu_chunk(hb[...], wg_ref[0], wu_ref[0], wd_ref[0])

    @pl.when(valid & (c == pl.num_programs(1) - 1))
    def _():
        def scatter_rows(r0, carry):
            for u in range(DMA_BURST):
                r = r0 * DMA_BURST + u
                pltpu.make_async_copy(acc.at[pl.ds(r, 1), :], y_hbm.at[pl.ds(dst_smem[r], 1), :],
                                      sem_out).start(priority=u % 2)
            return carry
        lax.fori_loop(0, rows // DMA_BURST, scatter_rows, 0)
        all_rows_out.wait()


def _moe_combine_kernel(xn_ref, gt2_ref, g2_ref, ya_ref, yb_ref, out_ref):
    g2 = g2_ref[0]
    out_ref[0] = xn_ref[0] + gt2_ref[0] * (g2[:, 0:1] * ya_ref[...] + g2[:, 1:2] * yb_ref[...])


def _token_specs(tm, nd):
    pick = (lambda f: lambda b, i: f(b, i)) if nd == 2 else (lambda f: lambda b, i, c: f(b, i))
    tok = pl.BlockSpec((1, tm, D_MODEL), pick(lambda b, i: (b, i, 0)))

    def mod_spec(a):
        return pl.BlockSpec((1, 1, D_MODEL), pick(lambda b, i: (b, 0, 0))) if a.shape[1] == 1 else tok

    cst = lambda shape: pl.BlockSpec(shape, pick(lambda b, i: (0, 0)))
    return tok, mod_spec, cst


def _swiglu_mixer(x, o, w_out, gt1, g, shift, scale, gt2, w_gate, w_up, w_down, *, tm):
    gsz, t, _ = x.shape
    d_ff = w_gate.shape[-1]
    assert t % tm == 0 and d_ff % FF_TILE == 0
    tok, mod_spec, cst = _token_specs(tm, 3)
    return pl.pallas_call(
        _ffn_kernel, grid=(gsz, t // tm, d_ff // FF_TILE),
        in_specs=[tok, tok, cst((ATTN_WIDTH, D_MODEL)), mod_spec(gt1), cst((1, D_MODEL)),
                  mod_spec(shift), mod_spec(scale), mod_spec(gt2),
                  pl.BlockSpec((D_MODEL, FF_TILE), lambda b, i, c: (0, c)),
                  pl.BlockSpec((D_MODEL, FF_TILE), lambda b, i, c: (0, c)),
                  pl.BlockSpec((FF_TILE, D_MODEL), lambda b, i, c: (c, 0))],
        out_specs=tok,
        out_shape=jax.ShapeDtypeStruct((gsz, t, D_MODEL), F32),
        scratch_shapes=[pltpu.VMEM((tm, D_MODEL), F32), pltpu.VMEM((tm, D_MODEL), BF16),
                        pltpu.VMEM((tm, D_MODEL), F32)],
        compiler_params=_params(("arbitrary",) * 3),
        name="swiglu_channel_mixer",
    )(x, o, w_out.astype(BF16), gt1, g[None, :], shift, scale, gt2,
      w_gate.astype(BF16), w_up.astype(BF16), w_down.astype(BF16))


def _moe_pre(x, o, w_out, gt1, g, shift, scale, w_router, *, tm):
    gsz, t, _ = x.shape
    assert t % tm == 0
    tok, mod_spec, cst = _token_specs(tm, 2)
    small = pl.BlockSpec((1, tm, N_EXPERTS), lambda b, i: (b, i, 0))
    wr = jnp.pad(w_router.astype(BF16), ((0, 0), (0, LANE - N_EXPERTS)))
    return pl.pallas_call(
        _moe_pre_kernel, grid=(gsz, t // tm),
        in_specs=[tok, tok, cst((ATTN_WIDTH, D_MODEL)), mod_spec(gt1), cst((1, D_MODEL)),
                  mod_spec(shift), mod_spec(scale), cst((D_MODEL, LANE))],
        out_specs=[tok, tok, small, small],
        out_shape=[jax.ShapeDtypeStruct((gsz, t, D_MODEL), F32), jax.ShapeDtypeStruct((gsz, t, D_MODEL), F32),
                   jax.ShapeDtypeStruct((gsz, t, N_EXPERTS), F32), jax.ShapeDtypeStruct((gsz, t, N_EXPERTS), F32)],
        compiler_params=_params(ARB2),
        name="moe_pre_router",
    )(x, o, w_out.astype(BF16), gt1, g[None, :], shift, scale, wr)


def _moe_routing(gates, sel, rows):
    n = gates.shape[0]
    member = sel > 0
    mi = member.astype(jnp.int32)
    counts = mi.sum(0)
    padded = (counts + rows - 1) // rows * rows
    ends = jnp.cumsum(padded)
    slot = jnp.cumsum(mi, axis=1) - mi
    expert_of = [jnp.sum(jnp.where(member & (slot == k), jnp.arange(N_EXPERTS)[None, :], 0), axis=1)
                 for k in range(TOP_K)]
    g2 = jnp.stack([jnp.sum(jnp.where(member & (slot == k), gates, 0.0), axis=1) for k in range(TOP_K)], axis=1)
    n_rows = (2 * n + N_EXPERTS * rows) // rows * rows
    n_pad = n_rows - 2 * n
    pad_ends = jnp.cumsum(padded - counts)
    pad_expert = (jnp.arange(n_pad, dtype=jnp.int32)[:, None] >= pad_ends[None, :]).sum(1)
    tok = jnp.arange(n, dtype=jnp.int32)
    sort_key = jnp.concatenate([expert_of[0], expert_of[1], pad_expert]).astype(jnp.int32)
    pair = jnp.concatenate([tok, n + tok, jnp.full((n_pad,), -1, jnp.int32)])
    _, pair = lax.sort((sort_key, pair), num_keys=1, is_stable=True)
    spare = 2 * n + jnp.arange(n_rows, dtype=jnp.int32) % rows
    src = jnp.where(pair >= 0, pair % n, 0)
    dst = jnp.where(pair >= 0, pair, spare)
    n_tiles = n_rows // rows
    tile_start = jnp.arange(n_tiles, dtype=jnp.int32) * rows
    tile_expert = jnp.minimum((tile_start[:, None] >= ends[None, :]).sum(1), N_EXPERTS - 1).astype(jnp.int32)
    tile_valid = (tile_start < ends[-1]).astype(jnp.int32)
    return src.reshape(n_tiles, rows), dst.reshape(n_tiles, rows), tile_expert, tile_valid, g2


def _moe_experts(h_all, src, dst, tile_expert, tile_valid, w_gate, w_up, w_down, *, rows):
    n = h_all.shape[0]
    n_tiles = src.shape[0]
    d_ff = w_gate.shape[-1]
    any_spec = pl.BlockSpec(memory_space=pl.ANY)
    return pl.pallas_call(
        functools.partial(_moe_expert_kernel, rows=rows),
        grid_spec=pltpu.PrefetchScalarGridSpec(
            num_scalar_prefetch=2, grid=(n_tiles, d_ff // FF_TILE),
            in_specs=[any_spec, any_spec, any_spec,
                      pl.BlockSpec((1, D_MODEL, FF_TILE), lambda i, c, te, tv: (te[i], 0, c)),
                      pl.BlockSpec((1, D_MODEL, FF_TILE), lambda i, c, te, tv: (te[i], 0, c)),
                      pl.BlockSpec((1, FF_TILE, D_MODEL), lambda i, c, te, tv: (te[i], c, 0))],
            out_specs=any_spec,
            scratch_shapes=[pltpu.SMEM((rows,), jnp.int32), pltpu.SMEM((rows,), jnp.int32),
                            pltpu.VMEM((rows, D_MODEL), F32), pltpu.VMEM((rows, D_MODEL), BF16),
                            pltpu.VMEM((rows, D_MODEL), F32), pltpu.SemaphoreType.DMA((2,)),
                            pltpu.SemaphoreType.DMA(()), pltpu.SemaphoreType.DMA(())]),
        out_shape=jax.ShapeDtypeStruct((2 * n + rows, D_MODEL), F32),
        compiler_params=_params(ARB2),
        name="moe_routed_experts",
    )(tile_expert, tile_valid, src, dst, h_all, w_gate.astype(BF16), w_up.astype(BF16), w_down.astype(BF16))


def _moe_combine(xn, gt2, g2, y, *, tm, n_all, tok0):
    gsz, t, _ = xn.shape
    assert t % tm == 0 and n_all % tm == 0 and tok0 % tm == 0
    tok, mod_spec, _ = _token_specs(tm, 2)
    nt = t // tm
    yspec = lambda slot: pl.BlockSpec((tm, D_MODEL), lambda b, i: ((slot * n_all + tok0) // tm + b * nt + i, 0))
    return pl.pallas_call(
        _moe_combine_kernel, grid=(gsz, nt),
        in_specs=[tok, mod_spec(gt2), pl.BlockSpec((1, tm, TOP_K), lambda b, i: (b, i, 0)), yspec(0), yspec(1)],
        out_specs=tok,
        out_shape=jax.ShapeDtypeStruct((gsz, t, D_MODEL), F32),
        compiler_params=_params(ARB2),
        name="moe_combine",
    )(xn, gt2, g2, y, y)


def _per_token(a, n):
    return jnp.broadcast_to(a, (a.shape[0], n, a.shape[2])).reshape(1, a.shape[0] * n, a.shape[2])


def _pad_keys(a, axis, l_pad):
    pad = [(0, 0)] * a.ndim
    pad[axis] = (0, l_pad - a.shape[axis])
    return jnp.pad(a, pad)


def _layer_a(xp, xs, cp, cs, ck, cv, cki, rel_bias, norm_g, w_ada, b_ada, w_in, w_out, g_q, g_k, g_kidx,
             ffn_wg, ffn_wu, ffn_wd):
    bias = _dsa_bias_tiles(rel_bias)
    nb, n = xs.shape[0], xs.shape[1]
    past = ck.shape[1]
    outs = []
    for x, c, sample in ((xp, cp, False), (xs, cs, True)):
        sh1, sc1, gt1, sh2, sc2, gt2 = adaln(c, w_ada, b_ada)
        q16, kf, vf, ktr, vx, kif, qi16, kit, wsc = _project(
            x, norm_g[0], sh1, sc1, w_in, g_q, g_k, g_kidx, n_kv=KV_HEADS_A,
            tm=n if sample else PROJ_TM)
        if sample:
            l_true = past + n
            l_pad = -(-l_true // DSA_SUB) * DSA_SUB
            kfull = jnp.concatenate([ck, kf.reshape(nb, n, KV_HEADS_A, HEAD_DIM)], axis=1).astype(BF16)
            vfull = jnp.concatenate([cv, vf.reshape(nb, n, KV_HEADS_A, HEAD_DIM)], axis=1).astype(BF16)
            kifull = jnp.concatenate([cki, kif], axis=1).astype(BF16)
            ktr = _pad_keys(kfull, 1, l_pad).transpose(0, 2, 3, 1)
            vx = jnp.concatenate([vfull, jnp.ones(vfull.shape[:3] + (1,), BF16),
                                  jnp.zeros(vfull.shape[:3] + (LANE - HEAD_DIM - 1,), BF16)], axis=-1)
            vx = _pad_keys(vx, 1, l_pad).transpose(0, 2, 1, 3)
            kit = _pad_keys(kifull, 1, l_pad).transpose(0, 2, 1)
            o = _dsa_attention(q16, ktr, vx, qi16, kit, wsc, bias, p0=past, l_true=l_true,
                               n_sel=min(TOPK_MAX, l_true // 4), tq=n, tk=l_pad, sw=DSA_SUB, cw=DSA_SUB)
            flat = lambda a: a.reshape(1, nb * n, a.shape[-1])
            y = _swiglu_mixer(flat(x), flat(o), w_out, _per_token(gt1, n), norm_g[1], _per_token(sh2, n),
                              _per_token(sc2, n), _per_token(gt2, n), ffn_wg, ffn_wu, ffn_wd, tm=MIX_TM_SAMPLE)
            y = y.reshape(x.shape)
        else:
            t = x.shape[1]
            o = _dsa_attention(q16, ktr, vx, qi16, kit, wsc, bias, p0=0, l_true=t,
                               n_sel=min(TOPK_MAX, t // 4), tq=LANE, tk=DSA_KEY_TILE, sw=DSA_SUB,
                               cw=DSA_KEY_TILE)
            y = _swiglu_mixer(x, o, w_out, gt1, norm_g[1], sh2, sc2, gt2, ffn_wg, ffn_wu, ffn_wd, tm=MIX_TM_PROMPT)
        bsz, t = x.shape[0], x.shape[1]
        outs.append((y, kf.reshape(bsz, t, KV_HEADS_A, HEAD_DIM), vf.reshape(bsz, t, KV_HEADS_A, HEAD_DIM), kif))
    return outs


def _layer_b(xp, xs, cp, cs, ck, cv, rel_bias, norm_g, w_ada, b_ada, w_in, w_out, g_q, g_k, sinks,
             w_router, moe_wg, moe_wu, moe_wd):
    bias = _band_bias(rel_bias)
    nb, n = xs.shape[0], xs.shape[1]
    streams = []
    for x, c, sample in ((xp, cp, False), (xs, cs, True)):
        sh1, sc1, gt1, sh2, sc2, gt2 = adaln(c, w_ada, b_ada)
        q16, kf, vf, ktr, v16 = _project(x, norm_g[0], sh1, sc1, w_in, g_q, g_k, n_kv=KV_HEADS_B,
                                         tm=n if sample else PROJ_TM)
        bsz, t = x.shape[0], x.shape[1]
        k4 = kf.reshape(bsz, t, KV_HEADS_B, HEAD_DIM)
        v4 = vf.reshape(bsz, t, KV_HEADS_B, HEAD_DIM)
        zero = lambda i: 0
        if sample:
            kp = ck.astype(BF16).transpose(0, 2, 3, 1)
            vp = cv.astype(BF16).transpose(0, 2, 1, 3)
            o = _band_attention(q16, kp, _pad_keys(ktr, 3, LANE), vp, _pad_keys(v16, 2, LANE), bias, sinks,
                                tq=n, prev_map=zero, cur_map=zero, first_has_no_prev=False)
            flat = lambda a: a.reshape(1, nb * n, a.shape[-1])
            xn, h, gates, sel = _moe_pre(flat(x), flat(o), w_out, _per_token(gt1, n), norm_g[1],
                                         _per_token(sh2, n), _per_token(sc2, n), w_router, tm=MIX_TM_SAMPLE)
            gt2 = _per_token(gt2, n)
            k_new = jnp.concatenate([ck, k4], axis=1)[:, -WINDOW:]
            v_new = jnp.concatenate([cv, v4], axis=1)[:, -WINDOW:]
        else:
            o = _band_attention(q16, ktr, ktr, v16, v16, bias, sinks, tq=LANE,
                                prev_map=lambda i: jnp.maximum(i - 1, 0), cur_map=lambda i: i,
                                first_has_no_prev=True)
            xn, h, gates, sel = _moe_pre(x, o, w_out, gt1, norm_g[1], sh2, sc2, w_router, tm=MIX_TM_PROMPT)
            k_new, v_new = k4[:, -WINDOW:], v4[:, -WINDOW:]
        streams.append((x.shape, xn, h, gates, sel, gt2, k_new, v_new))

    rows2d = lambda a: a.reshape(-1, a.shape[-1])
    h_all = jnp.concatenate([rows2d(st[2]) for st in streams])
    n_all = h_all.shape[0]
    src, dst, tile_expert, tile_valid, g2 = _moe_routing(
        jnp.concatenate([rows2d(st[3]) for st in streams]), jnp.concatenate([rows2d(st[4]) for st in streams]),
        MOE_ROW_TILE)
    y = _moe_experts(h_all, src, dst, tile_expert, tile_valid, moe_wg, moe_wu, moe_wd, rows=MOE_ROW_TILE)
    outs, tok0 = [], 0
    for shape, xn, _, _, _, gt2, k_new, v_new in streams:
        cnt = xn.shape[0] * xn.shape[1]
        out = _moe_combine(xn, gt2, g2[tok0:tok0 + cnt].reshape(xn.shape[0], xn.shape[1], TOP_K), y,
                           tm=MOE_COMBINE_TM, n_all=n_all, tok0=tok0)
        outs.append((out.reshape(shape), k_new, v_new))
        tok0 += cnt
    return outs


def kernel(x_prompt, x_sample, c_prompt, c_sample, cache_a_k, cache_a_v, cache_a_kidx,
           cache_b_k, cache_b_v, rel_bias, norm_g, w_ada, b_ada,
           a_w_in, a_w_out, a_g_q, a_g_k, a_g_kidx,
           b_w_in, b_w_out, b_g_q, b_g_k, b_sinks,
           ffn_w_gate, ffn_w_up, ffn_w_down,
           moe_w_router, moe_w_gate, moe_w_up, moe_w_down):
    xp, xs = x_prompt, x_sample
    a_out = [[] for _ in range(6)]
    b_out = [[] for _ in range(4)]
    for i in range(DEPTH):
        j = i // 2
        if i % 2 == 0:
            (xp, k1, v1, i1), (xs, k2, v2, i2) = _layer_a(
                xp, xs, c_prompt, c_sample, cache_a_k[j], cache_a_v[j], cache_a_kidx[j], rel_bias, norm_g[i],
                w_ada[i], b_ada[i], a_w_in[j], a_w_out[j], a_g_q[j], a_g_k[j], a_g_kidx[j],
                ffn_w_gate[j], ffn_w_up[j], ffn_w_down[j])
            for lst, val in zip(a_out, (k1, v1, i1, k2, v2, i2)):
                lst.append(val)
        else:
            (xp, k1, v1), (xs, k2, v2) = _layer_b(
                xp, xs, c_prompt, c_sample, cache_b_k[j], cache_b_v[j], rel_bias, norm_g[i],
                w_ada[i], b_ada[i], b_w_in[j], b_w_out[j], b_g_q[j], b_g_k[j], b_sinks[j],
                moe_w_router[j], moe_w_gate[j], moe_w_up[j], moe_w_down[j])
            for lst, val in zip(b_out, (k1, v1, k2, v2)):
                lst.append(val)
    return (xp, xs, *(jnp.stack(l) for l in a_out), *(jnp.stack(l) for l in b_out))
```

```python
import math, functools
import jax, jax.numpy as jnp
from jax import lax
import numpy as np
from jax.experimental import pallas as pl
from jax.experimental.pallas import tpu as pltpu

D_MODEL = 1024
DEPTH = 2
CHUNK = 64
N_HEADS = 16
HEAD_DIM = 64
ATTN_WIDTH = N_HEADS * HEAD_DIM
KV_HEADS_A = 4
IDX_HEADS = 8
IDX_DIM = 64
TOPK_MAX = 256
KV_HEADS_B = 2
WINDOW = 128
NUM_BUCKETS = 32
MAX_DISTANCE = 1024
N_EXPERTS = 8
TOP_K = 2
EPS = 1e-6

F32 = jnp.float32
BF16 = jnp.bfloat16
LANE = 128
VMEM_LIMIT_BYTES = 48 * 1024 * 1024
LOG2E = math.log2(math.e)
NEG_BIG = -1e30
KEY_MIN = -2 ** 31
FAR_BUCKET = NUM_BUCKETS // 2 - 1
NEAR_BLOCKS = 6
FF_TILE = 512
MIX_TM_PROMPT = 1024
MIX_TM_SAMPLE = 256
MOE_ROW_TILE = 1024
MOE_COMBINE_TM = 256
DMA_BURST = 8
PROJ_TM = 256
DSA_SUB = 2 * LANE
DSA_KEY_TILE = 1024
CHUNK_SHIFT = CHUNK.bit_length() - 1
LANE_SHIFT = LANE.bit_length() - 1
COUNT_ALL = float(2 ** 24)
ARB2 = ("arbitrary", "arbitrary")


def _params(sem, vmem=True):
    return pltpu.CompilerParams(dimension_semantics=sem, vmem_limit_bytes=VMEM_LIMIT_BYTES if vmem else None)


def rel_bucket(rel):
    half = NUM_BUCKETS // 2
    max_exact = half // 2
    base = jnp.where(rel > 0, half, 0)
    n = jnp.abs(rel)
    nf = jnp.maximum(n, 1).astype(F32)
    large = max_exact + (jnp.log(nf / max_exact) / math.log(MAX_DISTANCE / max_exact)
                         * (half - max_exact)).astype(jnp.int32)
    large = jnp.minimum(large, half - 1)
    return base + jnp.where(n < max_exact, n, large)


def _adaln_kernel(c_ref, w_ref, b_ref, o_ref):
    c = c_ref[...]
    o_ref[...] = jnp.dot(c * jax.nn.sigmoid(c), w_ref[...], preferred_element_type=F32) + b_ref[...]


def adaln(c, w, b):
    n = c.shape[0]
    mod = pl.pallas_call(
        _adaln_kernel,
        grid=(6,),
        in_specs=[pl.BlockSpec((n, D_MODEL), lambda j: (0, 0)),
                  pl.BlockSpec((D_MODEL, D_MODEL), lambda j: (0, j)),
                  pl.BlockSpec((1, D_MODEL), lambda j: (0, j))],
        out_specs=pl.BlockSpec((n, D_MODEL), lambda j: (0, j)),
        out_shape=jax.ShapeDtypeStruct((n, 6 * D_MODEL), F32),
        name="adaln",
    )(c, w, b[None, :])
    return jnp.split(mod[:, None, :], 6, axis=-1)


def _modulate(x, g, shift, scale):
    y = x * lax.rsqrt(jnp.mean(x * x, axis=-1, keepdims=True) + EPS)
    return (y * g) * (1.0 + scale) + shift


def _seg_norm(seg, g):
    return seg * lax.rsqrt(jnp.mean(seg * seg, axis=-1, keepdims=True) + EPS) * g


def _proj_kernel(x_ref, g_ref, sh_ref, sc_ref, w_ref, gq_ref, gk_ref, *rest, n_kv, has_idx):
    h = _modulate(x_ref[0], g_ref[...], sh_ref[0], sc_ref[0]).astype(BF16)
    y = jnp.dot(h, w_ref[...], preferred_element_type=F32)
    tm = y.shape[0]
    kv_w = n_kv * HEAD_DIM
    if has_idx:
        gki_ref, q16, kf, vf, ktr, vx, kif, qi16, kit, wsc = rest
    else:
        q16, kf, vf, ktr, vx = rest
    for hd in range(N_HEADS):
        seg = y[:, hd * HEAD_DIM:(hd + 1) * HEAD_DIM]
        q16[0, hd] = (_seg_norm(seg, gq_ref[...]) * (HEAD_DIM ** -0.5 * LOG2E)).astype(BF16)
    lane = lax.broadcasted_iota(jnp.int32, (tm, LANE - HEAD_DIM), 1)
    ones_col = jnp.where(lane == 0, 1.0, 0.0).astype(BF16)
    for j in range(n_kv):
        kseg = _seg_norm(y[:, ATTN_WIDTH + j * HEAD_DIM:ATTN_WIDTH + (j + 1) * HEAD_DIM], gk_ref[...])
        vseg = y[:, ATTN_WIDTH + kv_w + j * HEAD_DIM:ATTN_WIDTH + kv_w + (j + 1) * HEAD_DIM]
        kf[0, :, j * HEAD_DIM:(j + 1) * HEAD_DIM] = kseg
        vf[0, :, j * HEAD_DIM:(j + 1) * HEAD_DIM] = vseg
        ktr[0, j] = kseg.T.astype(BF16)
        vx[0, j] = jnp.concatenate([vseg.astype(BF16), ones_col], axis=1)
    if has_idx:
        base = ATTN_WIDTH + 2 * kv_w
        for hd in range(IDX_HEADS):
            qi16[0, hd] = y[:, base + hd * IDX_DIM:base + (hd + 1) * IDX_DIM].astype(BF16)
        base += IDX_HEADS * IDX_DIM
        kiseg = _seg_norm(y[:, base:base + IDX_DIM], gki_ref[...])
        kif[0] = kiseg
        kit[0] = kiseg.T.astype(BF16)
        wsc[0] = y[:, base + IDX_DIM:base + IDX_DIM + IDX_HEADS] * IDX_HEADS ** -0.5 * IDX_DIM ** -0.5


def _project(x, g, shift, scale, w_in, g_q, g_k, g_kidx=None, *, n_kv, tm):
    has_idx = g_kidx is not None
    gsz, t, _ = x.shape
    n_in = w_in.shape[1]
    kv_w = n_kv * HEAD_DIM
    tok = lambda b, i: (b, i, 0)
    head = lambda b, i: (b, 0, i, 0)
    vec = lambda b, i: (b, 0, 0)
    cst = lambda b, i: (0, 0)
    shapes = [((gsz, N_HEADS, t, HEAD_DIM), BF16, (1, N_HEADS, tm, HEAD_DIM), head),
              ((gsz, t, kv_w), F32, (1, tm, kv_w), tok),
              ((gsz, t, kv_w), F32, (1, tm, kv_w), tok),
              ((gsz, n_kv, HEAD_DIM, t), BF16, (1, n_kv, HEAD_DIM, tm), lambda b, i: (b, 0, 0, i)),
              ((gsz, n_kv, t, LANE), BF16, (1, n_kv, tm, LANE), head)]
    if has_idx:
        shapes += [((gsz, t, IDX_DIM), F32, (1, tm, IDX_DIM), tok),
                   ((gsz, IDX_HEADS, t, IDX_DIM), BF16, (1, IDX_HEADS, tm, IDX_DIM), head),
                   ((gsz, IDX_DIM, t), BF16, (1, IDX_DIM, tm), lambda b, i: (b, 0, i)),
                   ((gsz, t, IDX_HEADS), F32, (1, tm, IDX_HEADS), tok)]
    return pl.pallas_call(
        functools.partial(_proj_kernel, n_kv=n_kv, has_idx=has_idx),
        grid=(gsz, t // tm),
        in_specs=[pl.BlockSpec((1, tm, D_MODEL), tok), pl.BlockSpec((1, D_MODEL), cst),
                  pl.BlockSpec((1, 1, D_MODEL), vec), pl.BlockSpec((1, 1, D_MODEL), vec),
                  pl.BlockSpec((D_MODEL, n_in), cst), pl.BlockSpec((1, HEAD_DIM), cst),
                  pl.BlockSpec((1, HEAD_DIM), cst)] + ([pl.BlockSpec((1, IDX_DIM), cst)] if has_idx else []),
        out_specs=[pl.BlockSpec(blk, im) for _, _, blk, im in shapes],
        out_shape=[jax.ShapeDtypeStruct(s, d) for s, d, _, _ in shapes],
        compiler_params=_params(ARB2),
        name="mixer_in_proj",
    )(x, g[None, :], shift, scale, w_in.astype(BF16), g_q[None, :], g_k[None, :],
      *([g_kidx[None, :]] if has_idx else []))


def _tri_steps(nqb, tq, tk, p0):
    qbs, kts = [], []
    for qb in range(nqb):
        vis_end = ((p0 + qb * tq + tq - 1) // CHUNK + 1) * CHUNK
        for kt in range(-(-vis_end // tk)):
            qbs.append(qb)
            kts.append(kt)
    return jnp.asarray(np.array(qbs, np.int32)), jnp.asarray(np.array(kts, np.int32))


def _idx_kernel(qb_tab, kt_tab, qi_ref, w_ref, kit_ref, out_ref, *, tq, tk, sw, p0, l_true):
    step = pl.program_id(1)
    q0 = p0 + qb_tab[step] * tq
    k0 = kt_tab[step] * tk
    row = lax.broadcasted_iota(jnp.int32, (tq, 1), 0) + q0
    vis_end = jnp.minimum(((row >> CHUNK_SHIFT) + 1) << CHUNK_SHIFT, l_true)
    wv = w_ref[0]
    for c in range(tk // sw):
        kit = kit_ref[0, :, c * sw:(c + 1) * sw]
        acc = jnp.zeros((tq, sw), F32)
        for h in range(IDX_HEADS):
            sh = jnp.dot(qi_ref[0, h], kit, preferred_element_type=F32)
            acc = acc + wv[:, h:h + 1] * jnp.maximum(sh, 0.0)
        bits = lax.bitcast_convert_type(acc, jnp.int32)
        key = bits ^ ((bits >> 31) & 0x7FFFFFFF)
        kpos = k0 + c * sw + lax.broadcasted_iota(jnp.int32, (tq, sw), 1)
        out_ref[0, :, c * sw:(c + 1) * sw] = jnp.where(kpos < vis_end, key, KEY_MIN)


def _thr_kernel(keys_ref, tau_ref, quota_ref, flag_ref, *, tq, cw, p0, n_sel):
    qb = pl.program_id(1)
    vis_end = (((p0 + qb * tq + tq - 1) >> CHUNK_SHIFT) + 1) << CHUNK_SHIFT
    nch = (vis_end + cw - 1) // cw
    key_max = 2 ** 31 - 1

    def count_ge(cand):
        def body(j, acc):
            x = keys_ref[0, :, pl.ds(pl.multiple_of(j * cw, cw), cw)]
            for u in range(cw // LANE):
                acc = acc + jnp.where(x[:, u * LANE:(u + 1) * LANE] >= cand, 1.0, 0.0)
            return acc
        acc = lax.fori_loop(0, nch, body, jnp.zeros((tq, LANE), F32))
        return jnp.sum(acc, axis=1, keepdims=True)

    zero = jnp.zeros((tq, LANE), jnp.int32)
    c0 = count_ge(zero)
    tau = jnp.where(c0 >= n_sel, zero, KEY_MIN)
    n_ge = jnp.where(c0 >= n_sel, c0, COUNT_ALL)

    def bit_body(carry):
        i, tau, n_ge = carry
        cand = tau + jnp.left_shift(jnp.int32(1), 30 - i)
        c = count_ge(cand)
        return i + 1, jnp.where(c >= n_sel, cand, tau), jnp.where(c >= n_sel, c, n_ge)

    def unresolved(carry):
        i, _, n_ge = carry
        return (i < 31) & (jnp.max(jnp.abs(n_ge - n_sel)) > 0.0)

    _, tau, n_ge = lax.while_loop(unresolved, bit_body, (jnp.int32(0), tau, n_ge))
    n_gt = jnp.where(tau[:, :1] < key_max, count_ge(jnp.minimum(tau, key_max - 1) + 1), 0.0)
    tau_ref[0] = jnp.maximum(tau, KEY_MIN + 1)
    quota_ref[0] = jnp.broadcast_to(n_sel - n_gt, (tq, LANE))
    over = jnp.max(jnp.where(n_ge > n_sel, 1, 0), axis=0, keepdims=True)
    flag_ref[0, 0] = jnp.broadcast_to(over, flag_ref.shape[2:])


def _attn_kernel(qb_tab, kt_tab, flag_tab, q_ref, kt_ref, vx_ref, keys_ref, tau_ref, quota_ref, bias_ref, o_ref,
                 m_ref, acc_ref, negm_ref, seen_ref, *, tq, tk, sw, p0, nqb):
    step = pl.program_id(1)
    qb, kt = qb_tab[step], kt_tab[step]
    q0 = p0 + qb * tq
    k0 = kt * tk
    vis_end = (((q0 + tq - 1) >> CHUNK_SHIFT) + 1) << CHUNK_SHIFT
    group = N_HEADS // KV_HEADS_A
    nblk = sw // LANE
    tie_w = 2 * LANE

    @pl.when(kt == 0)
    def _():
        m_ref[...] = jnp.full(m_ref.shape, NEG_BIG, F32)
        acc_ref[...] = jnp.zeros(acc_ref.shape, F32)
        seen_ref[...] = jnp.zeros(seen_ref.shape, F32)

    has_ties = flag_tab[pl.program_id(0) * nqb + qb] != 0
    tau = tau_ref[0]

    @pl.when(jnp.logical_not(has_ties))
    def _():
        for u in range(tk // LANE):
            cols = slice(u * LANE, (u + 1) * LANE)
            negm_ref[:, cols] = jnp.where(keys_ref[0, :, cols] >= tau, 0.0, NEG_BIG)

    @pl.when(has_ties)
    def _():
        tau2 = jnp.concatenate([tau] * (tie_w // LANE), axis=1)
        quota = jnp.concatenate([quota_ref[0]] * (tie_w // LANE), axis=1)
        before = (lax.broadcasted_iota(jnp.int32, (tie_w, tie_w), 0)
                  < lax.broadcasted_iota(jnp.int32, (tie_w, tie_w), 1))
        before = jnp.where(before, 1.0, 0.0).astype(BF16)
        seen = seen_ref[...]
        for u in range(tk // tie_w):
            cols = slice(u * tie_w, (u + 1) * tie_w)
            k = keys_ref[0, :, cols]
            tie = jnp.where(k == tau2, 1.0, 0.0)
            rank = (jnp.dot(tie.astype(BF16), before, preferred_element_type=F32)
                    + jnp.concatenate([seen] * (tie_w // LANE), axis=1))
            keep = jnp.where(k > tau2, 1.0, jnp.where(rank < quota, tie, 0.0))
            negm_ref[:, cols] = jnp.where(keep > 0.0, 0.0, NEG_BIG)
            seen = seen + jnp.sum(tie, axis=1, keepdims=True)
        seen_ref[...] = seen

    def sub_tile(c, with_bias):
        off = c * sw if isinstance(c, int) else pl.multiple_of(c * sw, sw)
        negm = negm_ref[:, pl.ds(off, sw)]
        if with_bias:
            d0 = ((k0 + off - q0) >> LANE_SHIFT) + NEAR_BLOCKS
            bidx = [jnp.clip(d0 + u, 0, NEAR_BLOCKS) for u in range(nblk)]
        for h in range(N_HEADS):
            j = h // group
            s = jnp.dot(q_ref[0, h], kt_ref[0, j, :, pl.ds(off, sw)], preferred_element_type=F32) + negm
            if with_bias:
                s = s + jnp.concatenate([bias_ref[bidx[u], h] for u in range(nblk)], axis=1)
            m_prev = m_ref[h]
            m_cur = jnp.maximum(m_prev, jnp.max(s, axis=1, keepdims=True))
            alpha = jnp.exp2(m_prev - m_cur)
            p = jnp.exp2(s - jnp.concatenate([m_cur] * nblk, axis=1))
            pv = jnp.dot(p.astype(BF16), vx_ref[0, j, pl.ds(off, sw), :], preferred_element_type=F32)
            acc_ref[h] = alpha * acc_ref[h] + pv
            m_ref[h] = m_cur

    tile_far = k0 + tk <= q0 - (NEAR_BLOCKS - 1) * LANE

    @pl.when(tile_far)
    def _():
        for c in range(tk // sw):
            sub_tile(c, False)

    @pl.when(jnp.logical_not(tile_far))
    def _():
        nsub = jnp.minimum(tk // sw, (vis_end - k0 + sw - 1) // sw)
        nfar = jnp.clip((q0 - (NEAR_BLOCKS - 1) * LANE - k0) // sw, 0, nsub)

        def far_body(c, carry):
            sub_tile(c, False)
            return carry

        def near_body(c, carry):
            sub_tile(c, True)
            return carry

        lax.fori_loop(0, nfar, far_body, 0)
        lax.fori_loop(nfar, nsub, near_body, 0)

    @pl.when(k0 + tk >= vis_end)
    def _():
        for h in range(N_HEADS):
            a = acc_ref[h]
            o_ref[0, :, h * HEAD_DIM:(h + 1) * HEAD_DIM] = a[:, :HEAD_DIM] / a[:, HEAD_DIM:HEAD_DIM + 1]


def _rel_tiles(rel_table, rel):
    onehot = jax.nn.one_hot(rel_bucket(rel), NUM_BUCKETS, dtype=F32)
    return jnp.einsum("nrsk,kh->nhrs", onehot, rel_table.astype(F32), precision=lax.Precision.HIGHEST)


def _dsa_bias_tiles(rel_table):
    d = jnp.arange(NEAR_BLOCKS + 1, dtype=jnp.int32)[:, None, None] - NEAR_BLOCKS
    rel = (d * LANE + jnp.arange(LANE, dtype=jnp.int32)[None, None, :]
           - jnp.arange(LANE, dtype=jnp.int32)[None, :, None])
    tiles = (_rel_tiles(rel_table, rel) - rel_table[FAR_BUCKET][None, :, None, None]) * LOG2E
    return tiles.at[0].set(0.0)


def _dsa_attention(q16, ktr, vx, qi16, kit, w, bias, *, p0, l_true, n_sel, tq, tk, sw, cw):
    bsz, t = q16.shape[0], q16.shape[2]
    l_pad = ktr.shape[3]
    nqb = t // tq
    assert t % tq == 0 and p0 % LANE == 0 and (tq == LANE or nqb == 1)
    assert l_pad % tk == 0 and tk % sw == 0 and tk % cw == 0 and tk % (2 * LANE) == 0
    qbt, ktt = _tri_steps(nqb, tq, tk, p0)
    nsteps = int(qbt.shape[0])

    keys = pl.pallas_call(
        functools.partial(_idx_kernel, tq=tq, tk=tk, sw=sw, p0=p0, l_true=l_true),
        grid_spec=pltpu.PrefetchScalarGridSpec(
            num_scalar_prefetch=2, grid=(bsz, nsteps),
            in_specs=[pl.BlockSpec((1, IDX_HEADS, tq, IDX_DIM), lambda b, s, qt, kt: (b, 0, qt[s], 0)),
                      pl.BlockSpec((1, tq, IDX_HEADS), lambda b, s, qt, kt: (b, qt[s], 0)),
                      pl.BlockSpec((1, IDX_DIM, tk), lambda b, s, qt, kt: (b, 0, kt[s]))],
            out_specs=pl.BlockSpec((1, tq, tk), lambda b, s, qt, kt: (b, qt[s], kt[s]))),
        out_shape=jax.ShapeDtypeStruct((bsz, t, l_pad), jnp.int32),
        compiler_params=_params(ARB2, vmem=False),
        name="dsa_index_keys",
    )(qbt, ktt, qi16, w, kit)

    tau, quota, flags = pl.pallas_call(
        functools.partial(_thr_kernel, tq=tq, cw=cw, p0=p0, n_sel=n_sel),
        grid=(bsz, nqb),
        in_specs=[pl.BlockSpec((1, tq, l_pad), lambda b, i: (b, i, 0))],
        out_specs=[pl.BlockSpec((1, tq, LANE), lambda b, i: (b, i, 0)),
                   pl.BlockSpec((1, tq, LANE), lambda b, i: (b, i, 0)),
                   pl.BlockSpec((1, 1, 8, LANE), lambda b, i: (b, i, 0, 0))],
        out_shape=[jax.ShapeDtypeStruct((bsz, t, LANE), jnp.int32),
                   jax.ShapeDtypeStruct((bsz, t, LANE), F32),
                   jax.ShapeDtypeStruct((bsz, nqb, 8, LANE), jnp.int32)],
        compiler_params=_params(ARB2),
        name="dsa_threshold",
    )(keys)
    flags = flags[:, :, 0, 0].reshape(bsz * nqb)

    return pl.pallas_call(
        functools.partial(_attn_kernel, tq=tq, tk=tk, sw=sw, p0=p0, nqb=nqb),
        grid_spec=pltpu.PrefetchScalarGridSpec(
            num_scalar_prefetch=3, grid=(bsz, nsteps),
            in_specs=[pl.BlockSpec((1, N_HEADS, tq, HEAD_DIM), lambda b, s, qt, kt, fl: (b, 0, qt[s], 0)),
                      pl.BlockSpec((1, KV_HEADS_A, HEAD_DIM, tk), lambda b, s, qt, kt, fl: (b, 0, 0, kt[s])),
                      pl.BlockSpec((1, KV_HEADS_A, tk, LANE), lambda b, s, qt, kt, fl: (b, 0, kt[s], 0)),
                      pl.BlockSpec((1, tq, tk), lambda b, s, qt, kt, fl: (b, qt[s], kt[s])),
                      pl.BlockSpec((1, tq, LANE), lambda b, s, qt, kt, fl: (b, qt[s], 0)),
                      pl.BlockSpec((1, tq, LANE), lambda b, s, qt, kt, fl: (b, qt[s], 0)),
                      pl.BlockSpec((NEAR_BLOCKS + 1, N_HEADS, tq, LANE), lambda b, s, qt, kt, fl: (0, 0, 0, 0))],
            out_specs=pl.BlockSpec((1, tq, ATTN_WIDTH), lambda b, s, qt, kt, fl: (b, qt[s], 0)),
            scratch_shapes=[pltpu.VMEM((N_HEADS, tq, LANE), F32), pltpu.VMEM((N_HEADS, tq, LANE), F32),
                            pltpu.VMEM((tq, tk), F32), pltpu.VMEM((tq, LANE), F32)]),
        out_shape=jax.ShapeDtypeStruct((bsz, t, ATTN_WIDTH), F32),
        compiler_params=_params(ARB2),
        name="dsa_masked_attention",
    )(qbt, ktt, flags, q16, ktr, vx, keys, tau, quota, bias[:, :, :tq, :])


def _band_kernel(q_ref, kp_ref, kc_ref, vp_ref, vc_ref, bias_ref, sink_ref, o_ref, *, tq, first_has_no_prev):
    group = N_HEADS // KV_HEADS_B
    extra = None
    if first_has_no_prev:
        col = lax.broadcasted_iota(jnp.int32, (tq, 2 * LANE), 1)
        extra = jnp.where((pl.program_id(1) == 0) & (col < LANE), NEG_BIG, 0.0)
    ones = jnp.ones((2 * LANE, LANE), BF16)
    low_half = lax.broadcasted_iota(jnp.int32, (tq, LANE), 1) < HEAD_DIM
    for j in range(KV_HEADS_B):
        heads = slice(j * group, (j + 1) * group)
        kw = jnp.concatenate([kp_ref[0, j], kc_ref[0, j]], axis=1)
        v_low = jnp.concatenate([vp_ref[0, j], vc_ref[0, j]], axis=0)
        v_high = jnp.concatenate([v_low[:, HEAD_DIM:], v_low[:, :HEAD_DIM]], axis=1)
        v_sum = [jnp.concatenate([v, ones], axis=1) for v in (v_low, v_high)]
        s_all = jnp.dot(q_ref[0, heads].reshape(group * tq, HEAD_DIM), kw,
                        preferred_element_type=F32).reshape(group, tq, 2 * LANE)
        for g0 in range(0, group, 2):
            halves = []
            for par in range(2):
                h = j * group + g0 + par
                s = s_all[g0 + par] + bias_ref[h]
                if extra is not None:
                    s = s + extra
                sink = sink_ref[h]
                m = jnp.maximum(jnp.max(s, axis=1, keepdims=True), sink)
                p = jnp.exp2(s - jnp.concatenate([m, m], axis=1)).astype(BF16)
                r = jnp.dot(p, v_sum[par], preferred_element_type=F32)
                halves.append(r[:, :LANE] / (r[:, LANE:] + jnp.exp2(sink - m)))
            h0 = j * group + g0
            o_ref[0, :, h0 * HEAD_DIM:(h0 + 2) * HEAD_DIM] = jnp.where(low_half, halves[0], halves[1])


def _band_bias(rel_table):
    r = jnp.arange(LANE, dtype=jnp.int32)[:, None]
    s = jnp.arange(2 * LANE, dtype=jnp.int32)[None, :]
    wc, qh = s // CHUNK, r // CHUNK
    band = (wc >= qh) & (wc <= qh + 2)
    tiles = _rel_tiles(rel_table, (s - LANE - r)[None])[0]
    return jnp.where(band[None], tiles * LOG2E, NEG_BIG)


def _band_attention(q16, kp, kc, vp, vc, bias, sinks, *, tq, prev_map, cur_map, first_has_no_prev):
    bsz, t = q16.shape[0], q16.shape[2]
    kblk = lambda m: pl.BlockSpec((1, KV_HEADS_B, HEAD_DIM, LANE), lambda b, i: (b, 0, 0, m(i)))
    vblk = lambda m: pl.BlockSpec((1, KV_HEADS_B, LANE, LANE), lambda b, i: (b, 0, m(i), 0))
    sink_rows = jnp.broadcast_to((sinks.astype(F32) * LOG2E)[:, None, None], (N_HEADS, 1, LANE))
    return pl.pallas_call(
        functools.partial(_band_kernel, tq=tq, first_has_no_prev=first_has_no_prev),
        grid=(bsz, t // tq),
        in_specs=[pl.BlockSpec((1, N_HEADS, tq, HEAD_DIM), lambda b, i: (b, 0, i, 0)),
                  kblk(prev_map), kblk(cur_map), vblk(prev_map), vblk(cur_map),
                  pl.BlockSpec((N_HEADS, tq, 2 * LANE), lambda b, i: (0, 0, 0)),
                  pl.BlockSpec((N_HEADS, 1, LANE), lambda b, i: (0, 0, 0))],
        out_specs=pl.BlockSpec((1, tq, ATTN_WIDTH), lambda b, i: (b, i, 0)),
        out_shape=jax.ShapeDtypeStruct((bsz, t, ATTN_WIDTH), F32),
        compiler_params=_params(ARB2, vmem=False),
        name="band_attention",
    )(q16, kp, kc, vp, vc, bias[:, :tq, :], sink_rows)


def _mixer_residual(x_ref, o_ref, wo_ref, gt1_ref):
    return x_ref[0] + gt1_ref[0] * jnp.dot(o_ref[0].astype(BF16), wo_ref[...], preferred_element_type=F32)


def _swiglu_chunk(h, wg, wu, wd):
    a = jax.nn.silu(jnp.dot(h, wg, preferred_element_type=F32)) * jnp.dot(h, wu, preferred_element_type=F32)
    return jnp.dot(a.astype(BF16), wd, preferred_element_type=F32)


def _ffn_kernel(x_ref, o_ref, wo_ref, gt1_ref, g_ref, sh_ref, sc_ref, gt2_ref, wg_ref, wu_ref, wd_ref,
                out_ref, xn_scr, h_scr, acc_scr):
    c = pl.program_id(2)

    @pl.when(c == 0)
    def _():
        xn = _mixer_residual(x_ref, o_ref, wo_ref, gt1_ref)
        xn_scr[...] = xn
        h_scr[...] = _modulate(xn, g_ref[...], sh_ref[0], sc_ref[0]).astype(BF16)
        acc_scr[...] = jnp.zeros(acc_scr.shape, F32)

    acc_scr[...] += _swiglu_chunk(h_scr[...], wg_ref[...], wu_ref[...], wd_ref[...])

    @pl.when(c == pl.num_programs(2) - 1)
    def _():
        out_ref[0] = xn_scr[...] + gt2_ref[0] * acc_scr[...]


def _moe_pre_kernel(x_ref, o_ref, wo_ref, gt1_ref, g_ref, sh_ref, sc_ref, wr_ref, xn_ref, h_ref, gate_ref, sel_ref):
    xn = _mixer_residual(x_ref, o_ref, wo_ref, gt1_ref)
    xn_ref[0] = xn
    h = _modulate(xn, g_ref[...], sh_ref[0], sc_ref[0])
    h_ref[0] = h
    lane = lax.broadcasted_iota(jnp.int32, (h.shape[0], LANE), 1)
    lanef = lane.astype(F32)
    logits = jnp.dot(h.astype(BF16), wr_ref[...], preferred_element_type=F32)
    logits = jnp.where(lane < N_EXPERTS, logits, -jnp.inf)
    m1 = jnp.max(logits, axis=1, keepdims=True)
    i1 = jnp.min(jnp.where(logits == m1, lanef, float(LANE)), axis=1, keepdims=True)
    rest = jnp.where(lanef == i1, -jnp.inf, logits)
    m2 = jnp.max(rest, axis=1, keepdims=True)
    i2 = jnp.min(jnp.where(rest == m2, lanef, float(LANE)), axis=1, keepdims=True)
    e2 = jnp.exp(m2 - m1)
    den = 1.0 + e2
    gates = jnp.where(lanef == i1, 1.0 / den, 0.0) + jnp.where(lanef == i2, e2 / den, 0.0)
    sel = jnp.where(lanef == i1, 1.0, 0.0) + jnp.where(lanef == i2, 1.0, 0.0)
    gate_ref[0] = gates[:, :N_EXPERTS]
    sel_ref[0] = sel[:, :N_EXPERTS]


def _moe_expert_kernel(te_tab, tv_tab, src_hbm, dst_hbm, h_hbm, wg_ref, wu_ref, wd_ref, y_hbm,
                       src_smem, dst_smem, hbuf, hb, acc, sem_idx, sem_in, sem_out, *, rows):
    i, c = pl.program_id(0), pl.program_id(1)
    valid = tv_tab[i] != 0
    all_rows_in = pltpu.make_async_copy(h_hbm.at[pl.ds(0, rows), :], hbuf, sem_in)
    all_rows_out = pltpu.make_async_copy(acc, y_hbm.at[pl.ds(0, rows), :], sem_out)

    @pl.when(valid & (c == 0))
    def _():
        idx_copies = [pltpu.make_async_copy(src_hbm.at[i], src_smem, sem_idx.at[0]),
                      pltpu.make_async_copy(dst_hbm.at[i], dst_smem, sem_idx.at[1])]
        for cp in idx_copies:
            cp.start()
        for cp in idx_copies:
            cp.wait()

        def gather_rows(r0, carry):
            for u in range(DMA_BURST):
                r = r0 * DMA_BURST + u
                pltpu.make_async_copy(h_hbm.at[pl.ds(src_smem[r], 1), :], hbuf.at[pl.ds(r, 1), :],
                                      sem_in).start(priority=u % 2)
            return carry
        lax.fori_loop(0, rows // DMA_BURST, gather_rows, 0)
        all_rows_in.wait()
        hb[...] = hbuf[...].astype(BF16)
        acc[...] = jnp.zeros(acc.shape, F32)

    @pl.when(valid)
    def _():
        acc[...] += _swiglu_chunk(hb[...], wg_ref[0], wu_ref[0], wd_ref[0])

    @pl.when(valid & (c == pl.num_programs(1) - 1))
    def _():
        def scatter_rows(r0, carry):
            for u in range(DMA_BURST):
                r = r0 * DMA_BURST + u
                pltpu.make_async_copy(acc.at[pl.ds(r, 1), :], y_hbm.at[pl.ds(dst_smem[r], 1), :],
                                      sem_out).start(priority=u % 2)
            return carry
        lax.fori_loop(0, rows // DMA_BURST, scatter_rows, 0)
        all_rows_out.wait()


def _moe_combine_kernel(xn_ref, gt2_ref, g2_ref, ya_ref, yb_ref, out_ref):
    g2 = g2_ref[0]
    out_ref[0] = xn_ref[0] + gt2_ref[0] * (g2[:, 0:1] * ya_ref[...] + g2[:, 1:2] * yb_ref[...])


def _token_specs(tm, nd):
    pick = (lambda f: lambda b, i: f(b, i)) if nd == 2 else (lambda f: lambda b, i, c: f(b, i))
    tok = pl.BlockSpec((1, tm, D_MODEL), pick(lambda b, i: (b, i, 0)))

    def mod_spec(a):
        return pl.BlockSpec((1, 1, D_MODEL), pick(lambda b, i: (b, 0, 0))) if a.shape[1] == 1 else tok

    cst = lambda shape: pl.BlockSpec(shape, pick(lambda b, i: (0, 0)))
    return tok, mod_spec, cst


def _swiglu_mixer(x, o, w_out, gt1, g, shift, scale, gt2, w_gate, w_up, w_down, *, tm):
    gsz, t, _ = x.shape
    d_ff = w_gate.shape[-1]
    assert t % tm == 0 and d_ff % FF_TILE == 0
    tok, mod_spec, cst = _token_specs(tm, 3)
    return pl.pallas_call(
        _ffn_kernel, grid=(gsz, t // tm, d_ff // FF_TILE),
        in_specs=[tok, tok, cst((ATTN_WIDTH, D_MODEL)), mod_spec(gt1), cst((1, D_MODEL)),
                  mod_spec(shift), mod_spec(scale), mod_spec(gt2),
                  pl.BlockSpec((D_MODEL, FF_TILE), lambda b, i, c: (0, c)),
                  pl.BlockSpec((D_MODEL, FF_TILE), lambda b, i, c: (0, c)),
                  pl.BlockSpec((FF_TILE, D_MODEL), lambda b, i, c: (c, 0))],
        out_specs=tok,
        out_shape=jax.ShapeDtypeStruct((gsz, t, D_MODEL), F32),
        scratch_shapes=[pltpu.VMEM((tm, D_MODEL), F32), pltpu.VMEM((tm, D_MODEL), BF16),
                        pltpu.VMEM((tm, D_MODEL), F32)],
        compiler_params=_params(("arbitrary",) * 3),
        name="swiglu_channel_mixer",
    )(x, o, w_out.astype(BF16), gt1, g[None, :], shift, scale, gt2,
      w_gate.astype(BF16), w_up.astype(BF16), w_down.astype(BF16))


def _moe_pre(x, o, w_out, gt1, g, shift, scale, w_router, *, tm):
    gsz, t, _ = x.shape
    assert t % tm == 0
    tok, mod_spec, cst = _token_specs(tm, 2)
    small = pl.BlockSpec((1, tm, N_EXPERTS), lambda b, i: (b, i, 0))
    wr = jnp.pad(w_router.astype(BF16), ((0, 0), (0, LANE - N_EXPERTS)))
    return pl.pallas_call(
        _moe_pre_kernel, grid=(gsz, t // tm),
        in_specs=[tok, tok, cst((ATTN_WIDTH, D_MODEL)), mod_spec(gt1), cst((1, D_MODEL)),
                  mod_spec(shift), mod_spec(scale), cst((D_MODEL, LANE))],
        out_specs=[tok, tok, small, small],
        out_shape=[jax.ShapeDtypeStruct((gsz, t, D_MODEL), F32), jax.ShapeDtypeStruct((gsz, t, D_MODEL), F32),
                   jax.ShapeDtypeStruct((gsz, t, N_EXPERTS), F32), jax.ShapeDtypeStruct((gsz, t, N_EXPERTS), F32)],
        compiler_params=_params(ARB2),
        name="moe_pre_router",
    )(x, o, w_out.astype(BF16), gt1, g[None, :], shift, scale, wr)


def _moe_routing(gates, sel, rows):
    n = gates.shape[0]
    member = sel > 0
    mi = member.astype(jnp.int32)
    counts = mi.sum(0)
    padded = (counts + rows - 1) // rows * rows
    ends = jnp.cumsum(padded)
    slot = jnp.cumsum(mi, axis=1) - mi
    expert_of = [jnp.sum(jnp.where(member & (slot == k), jnp.arange(N_EXPERTS)[None, :], 0), axis=1)
                 for k in range(TOP_K)]
    g2 = jnp.stack([jnp.sum(jnp.where(member & (slot == k), gates, 0.0), axis=1) for k in range(TOP_K)], axis=1)
    n_rows = (2 * n + N_EXPERTS * rows) // rows * rows
    n_pad = n_rows - 2 * n
    pad_ends = jnp.cumsum(padded - counts)
    pad_expert = (jnp.arange(n_pad, dtype=jnp.int32)[:, None] >= pad_ends[None, :]).sum(1)
    tok = jnp.arange(n, dtype=jnp.int32)
    sort_key = jnp.concatenate([expert_of[0], expert_of[1], pad_expert]).astype(jnp.int32)
    pair = jnp.concatenate([tok, n + tok, jnp.full((n_pad,), -1, jnp.int32)])
    _, pair = lax.sort((sort_key, pair), num_keys=1, is_stable=True)
    spare = 2 * n + jnp.arange(n_rows, dtype=jnp.int32) % rows
    src = jnp.where(pair >= 0, pair % n, 0)
    dst = jnp.where(pair >= 0, pair, spare)
    n_tiles = n_rows // rows
    tile_start = jnp.arange(n_tiles, dtype=jnp.int32) * rows
    tile_expert = jnp.minimum((tile_start[:, None] >= ends[None, :]).sum(1), N_EXPERTS - 1).astype(jnp.int32)
    tile_valid = (tile_start < ends[-1]).astype(jnp.int32)
    return src.reshape(n_tiles, rows), dst.reshape(n_tiles, rows), tile_expert, tile_valid, g2


def _moe_experts(h_all, src, dst, tile_expert, tile_valid, w_gate, w_up, w_down, *, rows):
    n = h_all.shape[0]
    n_tiles = src.shape[0]
    d_ff = w_gate.shape[-1]
    any_spec = pl.BlockSpec(memory_space=pl.ANY)
    return pl.pallas_call(
        functools.partial(_moe_expert_kernel, rows=rows),
        grid_spec=pltpu.PrefetchScalarGridSpec(
            num_scalar_prefetch=2, grid=(n_tiles, d_ff // FF_TILE),
            in_specs=[any_spec, any_spec, any_spec,
                      pl.BlockSpec((1, D_MODEL, FF_TILE), lambda i, c, te, tv: (te[i], 0, c)),
                      pl.BlockSpec((1, D_MODEL, FF_TILE), lambda i, c, te, tv: (te[i], 0, c)),
                      pl.BlockSpec((1, FF_TILE, D_MODEL), lambda i, c, te, tv: (te[i], c, 0))],
            out_specs=any_spec,
            scratch_shapes=[pltpu.SMEM((rows,), jnp.int32), pltpu.SMEM((rows,), jnp.int32),
                            pltpu.VMEM((rows, D_MODEL), F32), pltpu.VMEM((rows, D_MODEL), BF16),
                            pltpu.VMEM((rows, D_MODEL), F32), pltpu.SemaphoreType.DMA((2,)),
                            pltpu.SemaphoreType.DMA(()), pltpu.SemaphoreType.DMA(())]),
        out_shape=jax.ShapeDtypeStruct((2 * n + rows, D_MODEL), F32),
        compiler_params=_params(ARB2),
        name="moe_routed_experts",
    )(tile_expert, tile_valid, src, dst, h_all, w_gate.astype(BF16), w_up.astype(BF16), w_down.astype(BF16))


def _moe_combine(xn, gt2, g2, y, *, tm, n_all, tok0):
    gsz, t, _ = xn.shape
    assert t % tm == 0 and n_all % tm == 0 and tok0 % tm == 0
    tok, mod_spec, _ = _token_specs(tm, 2)
    nt = t // tm
    yspec = lambda slot: pl.BlockSpec((tm, D_MODEL), lambda b, i: ((slot * n_all + tok0) // tm + b * nt + i, 0))
    return pl.pallas_call(
        _moe_combine_kernel, grid=(gsz, nt),
        in_specs=[tok, mod_spec(gt2), pl.BlockSpec((1, tm, TOP_K), lambda b, i: (b, i, 0)), yspec(0), yspec(1)],
        out_specs=tok,
        out_shape=jax.ShapeDtypeStruct((gsz, t, D_MODEL), F32),
        compiler_params=_params(ARB2),
        name="moe_combine",
    )(xn, gt2, g2, y, y)


def _per_token(a, n):
    return jnp.broadcast_to(a, (a.shape[0], n, a.shape[2])).reshape(1, a.shape[0] * n, a.shape[2])


def _with_ones_column(v):
    return jnp.concatenate([v, jnp.ones(v.shape[:-1] + (1,), BF16),
                            jnp.zeros(v.shape[:-1] + (LANE - HEAD_DIM - 1,), BF16)], axis=-1)


def _pad_keys(a, axis, l_pad):
    pad = [(0, 0)] * a.ndim
    pad[axis] = (0, l_pad - a.shape[axis])
    return jnp.pad(a, pad)


def _layer_a(xp, xs, cp, cs, ck, cv, cki, rel_bias, norm_g, w_ada, b_ada, w_in, w_out, g_q, g_k, g_kidx,
             ffn_wg, ffn_wu, ffn_wd):
    bias = _dsa_bias_tiles(rel_bias)
    nb, n = xs.shape[0], xs.shape[1]
    past = ck.shape[1]
    outs = []
    for x, c, sample in ((xp, cp, False), (xs, cs, True)):
        sh1, sc1, gt1, sh2, sc2, gt2 = adaln(c, w_ada, b_ada)
        q16, kf, vf, ktr, vx, kif, qi16, kit, wsc = _project(
            x, norm_g[0], sh1, sc1, w_in, g_q, g_k, g_kidx, n_kv=KV_HEADS_A,
            tm=n if sample else PROJ_TM)
        if sample:
            l_true = past + n
            l_pad = -(-l_true // DSA_SUB) * DSA_SUB
            kfull = jnp.concatenate([ck, kf.reshape(nb, n, KV_HEADS_A, HEAD_DIM)], axis=1).astype(BF16)
            vfull = jnp.concatenate([cv, vf.reshape(nb, n, KV_HEADS_A, HEAD_DIM)], axis=1).astype(BF16)
            kifull = jnp.concatenate([cki, kif], axis=1).astype(BF16)
            ktr = _pad_keys(kfull, 1, l_pad).transpose(0, 2, 3, 1)
            vx = _pad_keys(_with_ones_column(vfull), 1, l_pad).transpose(0, 2, 1, 3)
            kit = _pad_keys(kifull, 1, l_pad).transpose(0, 2, 1)
            o = _dsa_attention(q16, ktr, vx, qi16, kit, wsc, bias, p0=past, l_true=l_true,
                               n_sel=min(TOPK_MAX, l_true // 4), tq=n, tk=l_pad, sw=DSA_SUB, cw=DSA_SUB)
            flat = lambda a: a.reshape(1, nb * n, a.shape[-1])
            y = _swiglu_mixer(flat(x), flat(o), w_out, _per_token(gt1, n), norm_g[1], _per_token(sh2, n),
                              _per_token(sc2, n), _per_token(gt2, n), ffn_wg, ffn_wu, ffn_wd, tm=MIX_TM_SAMPLE)
            y = y.reshape(x.shape)
        else:
            t = x.shape[1]
            o = _dsa_attention(q16, ktr, vx, qi16, kit, wsc, bias, p0=0, l_true=t,
                               n_sel=min(TOPK_MAX, t // 4), tq=LANE, tk=DSA_KEY_TILE, sw=DSA_SUB,
                               cw=DSA_KEY_TILE)
            y = _swiglu_mixer(x, o, w_out, gt1, norm_g[1], sh2, sc2, gt2, ffn_wg, ffn_wu, ffn_wd, tm=MIX_TM_PROMPT)
        bsz, t = x.shape[0], x.shape[1]
        outs.append((y, kf.reshape(bsz, t, KV_HEADS_A, HEAD_DIM), vf.reshape(bsz, t, KV_HEADS_A, HEAD_DIM), kif))
    return outs


def _layer_b(xp, xs, cp, cs, ck, cv, rel_bias, norm_g, w_ada, b_ada, w_in, w_out, g_q, g_k, sinks,
             w_router, moe_wg, moe_wu, moe_wd):
    bias = _band_bias(rel_bias)
    nb, n = xs.shape[0], xs.shape[1]
    streams = []
    for x, c, sample in ((xp, cp, False), (xs, cs, True)):
        sh1, sc1, gt1, sh2, sc2, gt2 = adaln(c, w_ada, b_ada)
        q16, kf, vf, ktr, vx = _project(x, norm_g[0], sh1, sc1, w_in, g_q, g_k, n_kv=KV_HEADS_B,
                                         tm=n if sample else PROJ_TM)
        bsz, t = x.shape[0], x.shape[1]
        k4 = kf.reshape(bsz, t, KV_HEADS_B, HEAD_DIM)
        v4 = vf.reshape(bsz, t, KV_HEADS_B, HEAD_DIM)
        zero = lambda i: 0
        if sample:
            kp = ck.astype(BF16).transpose(0, 2, 3, 1)
            vp = _with_ones_column(cv.astype(BF16)).transpose(0, 2, 1, 3)
            o = _band_attention(q16, kp, _pad_keys(ktr, 3, LANE), vp, _pad_keys(vx, 2, LANE), bias, sinks,
                                tq=n, prev_map=zero, cur_map=zero, first_has_no_prev=False)
            flat = lambda a: a.reshape(1, nb * n, a.shape[-1])
            xn, h, gates, sel = _moe_pre(flat(x), flat(o), w_out, _per_token(gt1, n), norm_g[1],
                                         _per_token(sh2, n), _per_token(sc2, n), w_router, tm=MIX_TM_SAMPLE)
            gt2 = _per_token(gt2, n)
            k_new = jnp.concatenate([ck, k4], axis=1)[:, -WINDOW:]
            v_new = jnp.concatenate([cv, v4], axis=1)[:, -WINDOW:]
        else:
            o = _band_attention(q16, ktr, ktr, vx, vx, bias, sinks, tq=LANE,
                                prev_map=lambda i: jnp.maximum(i - 1, 0), cur_map=lambda i: i,
                                first_has_no_prev=True)
            xn, h, gates, sel = _moe_pre(x, o, w_out, gt1, norm_g[1], sh2, sc2, w_router, tm=MIX_TM_PROMPT)
            k_new, v_new = k4[:, -WINDOW:], v4[:, -WINDOW:]
        streams.append((x.shape, xn, h, gates, sel, gt2, k_new, v_new))

    rows2d = lambda a: a.reshape(-1, a.shape[-1])
    h_all = jnp.concatenate([rows2d(st[2]) for st in streams])
    n_all = h_all.shape[0]
    src, dst, tile_expert, tile_valid, g2 = _moe_routing(
        jnp.concatenate([rows2d(st[3]) for st in streams]), jnp.concatenate([rows2d(st[4]) for st in streams]),
        MOE_ROW_TILE)
    y = _moe_experts(h_all, src, dst, tile_expert, tile_valid, moe_wg, moe_wu, moe_wd, rows=MOE_ROW_TILE)
    outs, tok0 = [], 0
    for shape, xn, _, _, _, gt2, k_new, v_new in streams:
        cnt = xn.shape[0] * xn.shape[1]
        out = _moe_combine(xn, gt2, g2[tok0:tok0 + cnt].reshape(xn.shape[0], xn.shape[1], TOP_K), y,
                           tm=MOE_COMBINE_TM, n_all=n_all, tok0=tok0)
        outs.append((out.reshape(shape), k_new, v_new))
        tok0 += cnt
    return outs


def kernel(x_prompt, x_sample, c_prompt, c_sample, cache_a_k, cache_a_v, cache_a_kidx,
           cache_b_k, cache_b_v, rel_bias, norm_g, w_ada, b_ada,
           a_w_in, a_w_out, a_g_q, a_g_k, a_g_kidx,
           b_w_in, b_w_out, b_g_q, b_g_k, b_sinks,
           ffn_w_gate, ffn_w_up, ffn_w_down,
           moe_w_router, moe_w_gate, moe_w_up, moe_w_down):
    xp, xs = x_prompt, x_sample
    a_out = [[] for _ in range(6)]
    b_out = [[] for _ in range(4)]
    for i in range(DEPTH):
        j = i // 2
        if i % 2 == 0:
            (xp, k1, v1, i1), (xs, k2, v2, i2) = _layer_a(
                xp, xs, c_prompt, c_sample, cache_a_k[j], cache_a_v[j], cache_a_kidx[j], rel_bias, norm_g[i],
                w_ada[i], b_ada[i], a_w_in[j], a_w_out[j], a_g_q[j], a_g_k[j], a_g_kidx[j],
                ffn_w_gate[j], ffn_w_up[j], ffn_w_down[j])
            for lst, val in zip(a_out, (k1, v1, i1, k2, v2, i2)):
                lst.append(val)
        else:
            (xp, k1, v1), (xs, k2, v2) = _layer_b(
                xp, xs, c_prompt, c_sample, cache_b_k[j], cache_b_v[j], rel_bias, norm_g[i],
                w_ada[i], b_ada[i], b_w_in[j], b_w_out[j], b_g_q[j], b_g_k[j], b_sinks[j],
                moe_w_router[j], moe_w_gate[j], moe_w_up[j], moe_w_down[j])
            for lst, val in zip(b_out, (k1, v1, k2, v2)):
                lst.append(val)
    return (xp, xs, *(jnp.stack(l) for l in a_out), *(jnp.stack(l) for l in b_out))
```

```python
import math, functools
import jax, jax.numpy as jnp
from jax import lax
import numpy as np
from jax.experimental import pallas as pl
from jax.experimental.pallas import tpu as pltpu

D_MODEL = 1024
DEPTH = 2
CHUNK = 64
N_HEADS = 16
HEAD_DIM = 64
ATTN_WIDTH = N_HEADS * HEAD_DIM
KV_HEADS_A = 4
IDX_HEADS = 8
IDX_DIM = 64
TOPK_MAX = 256
KV_HEADS_B = 2
WINDOW = 128
NUM_BUCKETS = 32
MAX_DISTANCE = 1024
N_EXPERTS = 8
TOP_K = 2
EPS = 1e-6

F32 = jnp.float32
BF16 = jnp.bfloat16
LANE = 128
VMEM_LIMIT_BYTES = 48 * 1024 * 1024
LOG2E = math.log2(math.e)
NEG_BIG = -1e30
KEY_MIN = -2 ** 31
FAR_BUCKET = NUM_BUCKETS // 2 - 1
NEAR_BLOCKS = 6
FF_TILE = 512
MIX_TM_PROMPT = 1024
MIX_TM_SAMPLE = 256
MOE_ROW_TILE = 1024
MOE_COMBINE_TM = 256
DMA_BURST = 8
PROJ_TM = 256
DSA_SUB = 2 * LANE
DSA_KEY_TILE = 1024
DSA_INDEX_TILE = 2048
CHUNK_SHIFT = CHUNK.bit_length() - 1
LANE_SHIFT = LANE.bit_length() - 1
COUNT_ALL = float(2 ** 24)
ARB2 = ("arbitrary", "arbitrary")


def _params(sem, vmem=True):
    return pltpu.CompilerParams(dimension_semantics=sem, vmem_limit_bytes=VMEM_LIMIT_BYTES if vmem else None)


def rel_bucket(rel):
    half = NUM_BUCKETS // 2
    max_exact = half // 2
    base = jnp.where(rel > 0, half, 0)
    n = jnp.abs(rel)
    nf = jnp.maximum(n, 1).astype(F32)
    large = max_exact + (jnp.log(nf / max_exact) / math.log(MAX_DISTANCE / max_exact)
                         * (half - max_exact)).astype(jnp.int32)
    large = jnp.minimum(large, half - 1)
    return base + jnp.where(n < max_exact, n, large)


def _adaln_kernel(c_ref, w_ref, b_ref, o_ref):
    c = c_ref[...]
    o_ref[...] = jnp.dot(c * jax.nn.sigmoid(c), w_ref[...], preferred_element_type=F32) + b_ref[...]


def adaln(c, w, b):
    n = c.shape[0]
    mod = pl.pallas_call(
        _adaln_kernel,
        grid=(6,),
        in_specs=[pl.BlockSpec((n, D_MODEL), lambda j: (0, 0)),
                  pl.BlockSpec((D_MODEL, D_MODEL), lambda j: (0, j)),
                  pl.BlockSpec((1, D_MODEL), lambda j: (0, j))],
        out_specs=pl.BlockSpec((n, D_MODEL), lambda j: (0, j)),
        out_shape=jax.ShapeDtypeStruct((n, 6 * D_MODEL), F32),
        name="adaln",
    )(c, w, b[None, :])
    return jnp.split(mod[:, None, :], 6, axis=-1)


def _modulate(x, g, shift, scale):
    y = x * lax.rsqrt(jnp.mean(x * x, axis=-1, keepdims=True) + EPS)
    return (y * g) * (1.0 + scale) + shift


def _seg_norm(seg, g):
    return seg * lax.rsqrt(jnp.mean(seg * seg, axis=-1, keepdims=True) + EPS) * g


def _proj_kernel(x_ref, g_ref, sh_ref, sc_ref, w_ref, gq_ref, gk_ref, *rest, n_kv, has_idx):
    h = _modulate(x_ref[0], g_ref[...], sh_ref[0], sc_ref[0]).astype(BF16)
    y = jnp.dot(h, w_ref[...], preferred_element_type=F32)
    tm = y.shape[0]
    kv_w = n_kv * HEAD_DIM
    if has_idx:
        gki_ref, q16, kf, vf, ktr, vx, kif, qi16, kit, wsc = rest
    else:
        q16, kf, vf, ktr, vx = rest
    for hd in range(N_HEADS):
        seg = y[:, hd * HEAD_DIM:(hd + 1) * HEAD_DIM]
        q16[0, hd] = (_seg_norm(seg, gq_ref[...]) * (HEAD_DIM ** -0.5 * LOG2E)).astype(BF16)
    lane = lax.broadcasted_iota(jnp.int32, (tm, LANE - HEAD_DIM), 1)
    ones_col = jnp.where(lane == 0, 1.0, 0.0).astype(BF16)
    for j in range(n_kv):
        kseg = _seg_norm(y[:, ATTN_WIDTH + j * HEAD_DIM:ATTN_WIDTH + (j + 1) * HEAD_DIM], gk_ref[...])
        vseg = y[:, ATTN_WIDTH + kv_w + j * HEAD_DIM:ATTN_WIDTH + kv_w + (j + 1) * HEAD_DIM]
        kf[0, :, j * HEAD_DIM:(j + 1) * HEAD_DIM] = kseg
        vf[0, :, j * HEAD_DIM:(j + 1) * HEAD_DIM] = vseg
        ktr[0, j] = kseg.T.astype(BF16)
        vx[0, j] = jnp.concatenate([vseg.astype(BF16), ones_col], axis=1)
    if has_idx:
        base = ATTN_WIDTH + 2 * kv_w
        for hd in range(IDX_HEADS):
            qi16[0, hd] = y[:, base + hd * IDX_DIM:base + (hd + 1) * IDX_DIM].astype(BF16)
        base += IDX_HEADS * IDX_DIM
        kiseg = _seg_norm(y[:, base:base + IDX_DIM], gki_ref[...])
        kif[0] = kiseg
        kit[0] = kiseg.T.astype(BF16)
        wsc[0] = y[:, base + IDX_DIM:base + IDX_DIM + IDX_HEADS] * IDX_HEADS ** -0.5 * IDX_DIM ** -0.5


def _project(x, g, shift, scale, w_in, g_q, g_k, g_kidx=None, *, n_kv, tm):
    has_idx = g_kidx is not None
    gsz, t, _ = x.shape
    n_in = w_in.shape[1]
    kv_w = n_kv * HEAD_DIM
    tok = lambda b, i: (b, i, 0)
    head = lambda b, i: (b, 0, i, 0)
    vec = lambda b, i: (b, 0, 0)
    cst = lambda b, i: (0, 0)
    shapes = [((gsz, N_HEADS, t, HEAD_DIM), BF16, (1, N_HEADS, tm, HEAD_DIM), head),
              ((gsz, t, kv_w), F32, (1, tm, kv_w), tok),
              ((gsz, t, kv_w), F32, (1, tm, kv_w), tok),
              ((gsz, n_kv, HEAD_DIM, t), BF16, (1, n_kv, HEAD_DIM, tm), lambda b, i: (b, 0, 0, i)),
              ((gsz, n_kv, t, LANE), BF16, (1, n_kv, tm, LANE), head)]
    if has_idx:
        shapes += [((gsz, t, IDX_DIM), F32, (1, tm, IDX_DIM), tok),
                   ((gsz, IDX_HEADS, t, IDX_DIM), BF16, (1, IDX_HEADS, tm, IDX_DIM), head),
                   ((gsz, IDX_DIM, t), BF16, (1, IDX_DIM, tm), lambda b, i: (b, 0, i)),
                   ((gsz, t, IDX_HEADS), F32, (1, tm, IDX_HEADS), tok)]
    return pl.pallas_call(
        functools.partial(_proj_kernel, n_kv=n_kv, has_idx=has_idx),
        grid=(gsz, t // tm),
        in_specs=[pl.BlockSpec((1, tm, D_MODEL), tok), pl.BlockSpec((1, D_MODEL), cst),
                  pl.BlockSpec((1, 1, D_MODEL), vec), pl.BlockSpec((1, 1, D_MODEL), vec),
                  pl.BlockSpec((D_MODEL, n_in), cst), pl.BlockSpec((1, HEAD_DIM), cst),
                  pl.BlockSpec((1, HEAD_DIM), cst)] + ([pl.BlockSpec((1, IDX_DIM), cst)] if has_idx else []),
        out_specs=[pl.BlockSpec(blk, im) for _, _, blk, im in shapes],
        out_shape=[jax.ShapeDtypeStruct(s, d) for s, d, _, _ in shapes],
        compiler_params=_params(ARB2),
        name="mixer_in_proj",
    )(x, g[None, :], shift, scale, w_in.astype(BF16), g_q[None, :], g_k[None, :],
      *([g_kidx[None, :]] if has_idx else []))


def _tri_steps(nqb, tq, tk, p0):
    qbs, kts = [], []
    for qb in range(nqb):
        vis_end = ((p0 + qb * tq + tq - 1) // CHUNK + 1) * CHUNK
        for kt in range(-(-vis_end // tk)):
            qbs.append(qb)
            kts.append(kt)
    return jnp.asarray(np.array(qbs, np.int32)), jnp.asarray(np.array(kts, np.int32))


def _idx_kernel(qb_tab, kt_tab, qi_ref, w_ref, kit_ref, out_ref, *, tq, tk, sw, p0, l_true):
    step = pl.program_id(1)
    q0 = p0 + qb_tab[step] * tq
    k0 = kt_tab[step] * tk
    row = lax.broadcasted_iota(jnp.int32, (tq, 1), 0) + q0
    vis_end = jnp.minimum(((row >> CHUNK_SHIFT) + 1) << CHUNK_SHIFT, l_true)
    wv = w_ref[0]
    for c in range(tk // sw):
        kit = kit_ref[0, :, c * sw:(c + 1) * sw]
        acc = jnp.zeros((tq, sw), F32)
        for h in range(IDX_HEADS):
            sh = jnp.dot(qi_ref[0, h], kit, preferred_element_type=F32)
            acc = acc + wv[:, h:h + 1] * jnp.maximum(sh, 0.0)
        bits = lax.bitcast_convert_type(acc, jnp.int32)
        key = bits ^ ((bits >> 31) & 0x7FFFFFFF)
        kpos = k0 + c * sw + lax.broadcasted_iota(jnp.int32, (tq, sw), 1)
        out_ref[0, :, c * sw:(c + 1) * sw] = jnp.where(kpos < vis_end, key, KEY_MIN)


def _thr_kernel(keys_ref, tau_ref, quota_ref, flag_ref, *, tq, cw, p0, n_sel):
    qb = pl.program_id(1)
    vis_end = (((p0 + qb * tq + tq - 1) >> CHUNK_SHIFT) + 1) << CHUNK_SHIFT
    nch = (vis_end + cw - 1) // cw
    key_max = 2 ** 31 - 1

    def count_ge(cand):
        def body(j, acc):
            x = keys_ref[0, :, pl.ds(pl.multiple_of(j * cw, cw), cw)]
            for u in range(cw // LANE):
                acc = acc + jnp.where(x[:, u * LANE:(u + 1) * LANE] >= cand, 1.0, 0.0)
            return acc
        acc = lax.fori_loop(0, nch, body, jnp.zeros((tq, LANE), F32))
        return jnp.sum(acc, axis=1, keepdims=True)

    zero = jnp.zeros((tq, LANE), jnp.int32)
    c0 = count_ge(zero)
    tau = jnp.where(c0 >= n_sel, zero, KEY_MIN)
    n_ge = jnp.where(c0 >= n_sel, c0, COUNT_ALL)

    def bit_body(carry):
        i, tau, n_ge, _ = carry
        pending = jnp.max(jnp.abs(n_ge - n_sel))
        cand = tau + jnp.left_shift(jnp.int32(1), 30 - i)
        c = count_ge(cand)
        return i + 1, jnp.where(c >= n_sel, cand, tau), jnp.where(c >= n_sel, c, n_ge), pending

    def unresolved(carry):
        i, _, _, pending = carry
        return (i < 31) & (pending > 0.0)

    _, tau, n_ge, _ = lax.while_loop(unresolved, bit_body, (jnp.int32(0), tau, n_ge, jnp.float32(1.0)))
    n_gt = jnp.where(tau[:, :1] < key_max, count_ge(jnp.minimum(tau, key_max - 1) + 1), 0.0)
    tau_ref[0] = jnp.maximum(tau, KEY_MIN + 1)
    quota_ref[0] = jnp.broadcast_to(n_sel - n_gt, (tq, LANE))
    over = jnp.max(jnp.where(n_ge > n_sel, 1, 0), axis=0, keepdims=True)
    flag_ref[0, 0] = jnp.broadcast_to(over, flag_ref.shape[2:])


def _attn_kernel(qb_tab, kt_tab, flag_tab, q_ref, kt_ref, vx_ref, keys_ref, tau_ref, quota_ref, bias_ref, o_ref,
                 m_ref, acc_ref, negm_ref, seen_ref, *, tq, tk, sw, p0, nqb):
    step = pl.program_id(1)
    qb, kt = qb_tab[step], kt_tab[step]
    q0 = p0 + qb * tq
    k0 = kt * tk
    vis_end = (((q0 + tq - 1) >> CHUNK_SHIFT) + 1) << CHUNK_SHIFT
    group = N_HEADS // KV_HEADS_A
    nblk = sw // LANE
    tie_w = 2 * LANE

    @pl.when(kt == 0)
    def _():
        m_ref[...] = jnp.full(m_ref.shape, NEG_BIG, F32)
        acc_ref[...] = jnp.zeros(acc_ref.shape, F32)
        seen_ref[...] = jnp.zeros(seen_ref.shape, F32)

    has_ties = flag_tab[pl.program_id(0) * nqb + qb] != 0
    tau = tau_ref[0]

    @pl.when(jnp.logical_not(has_ties))
    def _():
        for u in range(tk // LANE):
            cols = slice(u * LANE, (u + 1) * LANE)
            negm_ref[:, cols] = jnp.where(keys_ref[0, :, cols] >= tau, 0.0, NEG_BIG)

    @pl.when(has_ties)
    def _():
        tau2 = jnp.concatenate([tau] * (tie_w // LANE), axis=1)
        quota = jnp.concatenate([quota_ref[0]] * (tie_w // LANE), axis=1)
        before = (lax.broadcasted_iota(jnp.int32, (tie_w, tie_w), 0)
                  < lax.broadcasted_iota(jnp.int32, (tie_w, tie_w), 1))
        before = jnp.where(before, 1.0, 0.0).astype(BF16)
        seen = seen_ref[...]
        for u in range(tk // tie_w):
            cols = slice(u * tie_w, (u + 1) * tie_w)
            k = keys_ref[0, :, cols]
            tie = jnp.where(k == tau2, 1.0, 0.0)
            rank = (jnp.dot(tie.astype(BF16), before, preferred_element_type=F32)
                    + jnp.concatenate([seen] * (tie_w // LANE), axis=1))
            keep = jnp.where(k > tau2, 1.0, jnp.where(rank < quota, tie, 0.0))
            negm_ref[:, cols] = jnp.where(keep > 0.0, 0.0, NEG_BIG)
            seen = seen + jnp.sum(tie, axis=1, keepdims=True)
        seen_ref[...] = seen

    def sub_tile(c, with_bias):
        off = c * sw if isinstance(c, int) else pl.multiple_of(c * sw, sw)
        negm = negm_ref[:, pl.ds(off, sw)]
        if with_bias:
            d0 = ((k0 + off - q0) >> LANE_SHIFT) + NEAR_BLOCKS
            bidx = [jnp.clip(d0 + u, 0, NEAR_BLOCKS) for u in range(nblk)]
        for h in range(N_HEADS):
            j = h // group
            s = jnp.dot(q_ref[0, h], kt_ref[0, j, :, pl.ds(off, sw)], preferred_element_type=F32) + negm
            if with_bias:
                s = s + jnp.concatenate([bias_ref[bidx[u], h] for u in range(nblk)], axis=1)
            m_prev = m_ref[h]
            m_cur = jnp.maximum(m_prev, jnp.max(s, axis=1, keepdims=True))
            alpha = jnp.exp2(m_prev - m_cur)
            p = jnp.exp2(s - jnp.concatenate([m_cur] * nblk, axis=1))
            pv = jnp.dot(p.astype(BF16), vx_ref[0, j, pl.ds(off, sw), :], preferred_element_type=F32)
            acc_ref[h] = alpha * acc_ref[h] + pv
            m_ref[h] = m_cur

    tile_far = k0 + tk <= q0 - (NEAR_BLOCKS - 1) * LANE

    @pl.when(tile_far)
    def _():
        for c in range(tk // sw):
            sub_tile(c, False)

    @pl.when(jnp.logical_not(tile_far))
    def _():
        nsub = jnp.minimum(tk // sw, (vis_end - k0 + sw - 1) // sw)
        nfar = jnp.clip((q0 - (NEAR_BLOCKS - 1) * LANE - k0) // sw, 0, nsub)

        def far_body(c, carry):
            sub_tile(c, False)
            return carry

        def near_body(c, carry):
            sub_tile(c, True)
            return carry

        lax.fori_loop(0, nfar, far_body, 0)
        lax.fori_loop(nfar, nsub, near_body, 0)

    @pl.when(k0 + tk >= vis_end)
    def _():
        for h in range(N_HEADS):
            a = acc_ref[h]
            o_ref[0, :, h * HEAD_DIM:(h + 1) * HEAD_DIM] = a[:, :HEAD_DIM] / a[:, HEAD_DIM:HEAD_DIM + 1]


def _rel_tiles(rel_table, rel):
    onehot = jax.nn.one_hot(rel_bucket(rel), NUM_BUCKETS, dtype=F32)
    return jnp.einsum("nrsk,kh->nhrs", onehot, rel_table.astype(F32), precision=lax.Precision.HIGHEST)


def _dsa_bias_tiles(rel_table):
    d = jnp.arange(NEAR_BLOCKS + 1, dtype=jnp.int32)[:, None, None] - NEAR_BLOCKS
    rel = (d * LANE + jnp.arange(LANE, dtype=jnp.int32)[None, None, :]
           - jnp.arange(LANE, dtype=jnp.int32)[None, :, None])
    tiles = (_rel_tiles(rel_table, rel) - rel_table[FAR_BUCKET][None, :, None, None]) * LOG2E
    return tiles.at[0].set(0.0)


def _dsa_attention(q16, ktr, vx, qi16, kit, w, bias, *, p0, l_true, n_sel, tq, tk, sw, cw, tk_idx):
    bsz, t = q16.shape[0], q16.shape[2]
    l_pad = ktr.shape[3]
    nqb = t // tq
    assert t % tq == 0 and p0 % LANE == 0 and (tq == LANE or nqb == 1)
    assert l_pad % tk == 0 and tk % sw == 0 and tk % cw == 0 and tk % (2 * LANE) == 0
    assert l_pad % tk_idx == 0 and tk_idx % tk == 0
    qbt, ktt = _tri_steps(nqb, tq, tk, p0)
    nsteps = int(qbt.shape[0])
    qbt_idx, ktt_idx = _tri_steps(nqb, tq, tk_idx, p0)

    keys = pl.pallas_call(
        functools.partial(_idx_kernel, tq=tq, tk=tk_idx, sw=sw, p0=p0, l_true=l_true),
        grid_spec=pltpu.PrefetchScalarGridSpec(
            num_scalar_prefetch=2, grid=(bsz, int(qbt_idx.shape[0])),
            in_specs=[pl.BlockSpec((1, IDX_HEADS, tq, IDX_DIM), lambda b, s, qt, kt: (b, 0, qt[s], 0)),
                      pl.BlockSpec((1, tq, IDX_HEADS), lambda b, s, qt, kt: (b, qt[s], 0)),
                      pl.BlockSpec((1, IDX_DIM, tk_idx), lambda b, s, qt, kt: (b, 0, kt[s]))],
            out_specs=pl.BlockSpec((1, tq, tk_idx), lambda b, s, qt, kt: (b, qt[s], kt[s]))),
        out_shape=jax.ShapeDtypeStruct((bsz, t, l_pad), jnp.int32),
        compiler_params=_params(ARB2, vmem=False),
        name="dsa_index_keys",
    )(qbt_idx, ktt_idx, qi16, w, kit)

    tau, quota, flags = pl.pallas_call(
        functools.partial(_thr_kernel, tq=tq, cw=cw, p0=p0, n_sel=n_sel),
        grid=(bsz, nqb),
        in_specs=[pl.BlockSpec((1, tq, l_pad), lambda b, i: (b, i, 0))],
        out_specs=[pl.BlockSpec((1, tq, LANE), lambda b, i: (b, i, 0)),
                   pl.BlockSpec((1, tq, LANE), lambda b, i: (b, i, 0)),
                   pl.BlockSpec((1, 1, 8, LANE), lambda b, i: (b, i, 0, 0))],
        out_shape=[jax.ShapeDtypeStruct((bsz, t, LANE), jnp.int32),
                   jax.ShapeDtypeStruct((bsz, t, LANE), F32),
                   jax.ShapeDtypeStruct((bsz, nqb, 8, LANE), jnp.int32)],
        compiler_params=_params(ARB2),
        name="dsa_threshold",
    )(keys)
    flags = flags[:, :, 0, 0].reshape(bsz * nqb)

    return pl.pallas_call(
        functools.partial(_attn_kernel, tq=tq, tk=tk, sw=sw, p0=p0, nqb=nqb),
        grid_spec=pltpu.PrefetchScalarGridSpec(
            num_scalar_prefetch=3, grid=(bsz, nsteps),
            in_specs=[pl.BlockSpec((1, N_HEADS, tq, HEAD_DIM), lambda b, s, qt, kt, fl: (b, 0, qt[s], 0)),
                      pl.BlockSpec((1, KV_HEADS_A, HEAD_DIM, tk), lambda b, s, qt, kt, fl: (b, 0, 0, kt[s])),
                      pl.BlockSpec((1, KV_HEADS_A, tk, LANE), lambda b, s, qt, kt, fl: (b, 0, kt[s], 0)),
                      pl.BlockSpec((1, tq, tk), lambda b, s, qt, kt, fl: (b, qt[s], kt[s])),
                      pl.BlockSpec((1, tq, LANE), lambda b, s, qt, kt, fl: (b, qt[s], 0)),
                      pl.BlockSpec((1, tq, LANE), lambda b, s, qt, kt, fl: (b, qt[s], 0)),
                      pl.BlockSpec((NEAR_BLOCKS + 1, N_HEADS, tq, LANE), lambda b, s, qt, kt, fl: (0, 0, 0, 0))],
            out_specs=pl.BlockSpec((1, tq, ATTN_WIDTH), lambda b, s, qt, kt, fl: (b, qt[s], 0)),
            scratch_shapes=[pltpu.VMEM((N_HEADS, tq, LANE), F32), pltpu.VMEM((N_HEADS, tq, LANE), F32),
                            pltpu.VMEM((tq, tk), F32), pltpu.VMEM((tq, LANE), F32)]),
        out_shape=jax.ShapeDtypeStruct((bsz, t, ATTN_WIDTH), F32),
        compiler_params=_params(ARB2),
        name="dsa_masked_attention",
    )(qbt, ktt, flags, q16, ktr, vx, keys, tau, quota, bias[:, :, :tq, :])


def _band_kernel(q_ref, kp_ref, kc_ref, vp_ref, vc_ref, bias_ref, sink_ref, o_ref, *, tq, first_has_no_prev):
    group = N_HEADS // KV_HEADS_B
    extra = None
    if first_has_no_prev:
        col = lax.broadcasted_iota(jnp.int32, (tq, 2 * LANE), 1)
        extra = jnp.where((pl.program_id(1) == 0) & (col < LANE), NEG_BIG, 0.0)
    ones = jnp.ones((2 * LANE, LANE), BF16)
    low_half = lax.broadcasted_iota(jnp.int32, (tq, LANE), 1) < HEAD_DIM
    for j in range(KV_HEADS_B):
        heads = slice(j * group, (j + 1) * group)
        kw = jnp.concatenate([kp_ref[0, j], kc_ref[0, j]], axis=1)
        v_low = jnp.concatenate([vp_ref[0, j], vc_ref[0, j]], axis=0)
        v_high = jnp.concatenate([v_low[:, HEAD_DIM:], v_low[:, :HEAD_DIM]], axis=1)
        v_sum = [jnp.concatenate([v, ones], axis=1) for v in (v_low, v_high)]
        s_all = jnp.dot(q_ref[0, heads].reshape(group * tq, HEAD_DIM), kw,
                        preferred_element_type=F32).reshape(group, tq, 2 * LANE)
        for g0 in range(0, group, 2):
            halves = []
            for par in range(2):
                h = j * group + g0 + par
                s = s_all[g0 + par] + bias_ref[h]
                if extra is not None:
                    s = s + extra
                sink = sink_ref[h]
                m = jnp.maximum(jnp.max(s, axis=1, keepdims=True), sink)
                p = jnp.exp2(s - jnp.concatenate([m, m], axis=1)).astype(BF16)
                r = jnp.dot(p, v_sum[par], preferred_element_type=F32)
                halves.append(r[:, :LANE] / (r[:, LANE:] + jnp.exp2(sink - m)))
            h0 = j * group + g0
            o_ref[0, :, h0 * HEAD_DIM:(h0 + 2) * HEAD_DIM] = jnp.where(low_half, halves[0], halves[1])


def _band_bias(rel_table):
    r = jnp.arange(LANE, dtype=jnp.int32)[:, None]
    s = jnp.arange(2 * LANE, dtype=jnp.int32)[None, :]
    wc, qh = s // CHUNK, r // CHUNK
    band = (wc >= qh) & (wc <= qh + 2)
    tiles = _rel_tiles(rel_table, (s - LANE - r)[None])[0]
    return jnp.where(band[None], tiles * LOG2E, NEG_BIG)


def _band_attention(q16, kp, kc, vp, vc, bias, sinks, *, tq, prev_map, cur_map, first_has_no_prev):
    bsz, t = q16.shape[0], q16.shape[2]
    kblk = lambda m: pl.BlockSpec((1, KV_HEADS_B, HEAD_DIM, LANE), lambda b, i: (b, 0, 0, m(i)))
    vblk = lambda m: pl.BlockSpec((1, KV_HEADS_B, LANE, LANE), lambda b, i: (b, 0, m(i), 0))
    sink_rows = jnp.broadcast_to((sinks.astype(F32) * LOG2E)[:, None, None], (N_HEADS, 1, LANE))
    return pl.pallas_call(
        functools.partial(_band_kernel, tq=tq, first_has_no_prev=first_has_no_prev),
        grid=(bsz, t // tq),
        in_specs=[pl.BlockSpec((1, N_HEADS, tq, HEAD_DIM), lambda b, i: (b, 0, i, 0)),
                  kblk(prev_map), kblk(cur_map), vblk(prev_map), vblk(cur_map),
                  pl.BlockSpec((N_HEADS, tq, 2 * LANE), lambda b, i: (0, 0, 0)),
                  pl.BlockSpec((N_HEADS, 1, LANE), lambda b, i: (0, 0, 0))],
        out_specs=pl.BlockSpec((1, tq, ATTN_WIDTH), lambda b, i: (b, i, 0)),
        out_shape=jax.ShapeDtypeStruct((bsz, t, ATTN_WIDTH), F32),
        compiler_params=_params(ARB2, vmem=False),
        name="band_attention",
    )(q16, kp, kc, vp, vc, bias[:, :tq, :], sink_rows)


def _mixer_residual(x_ref, o_ref, wo_ref, gt1_ref):
    return x_ref[0] + gt1_ref[0] * jnp.dot(o_ref[0].astype(BF16), wo_ref[...], preferred_element_type=F32)


def _swiglu_chunk(h, wg, wu, wd):
    a = jax.nn.silu(jnp.dot(h, wg, preferred_element_type=F32)) * jnp.dot(h, wu, preferred_element_type=F32)
    return jnp.dot(a.astype(BF16), wd, preferred_element_type=F32)


def _ffn_kernel(x_ref, o_ref, wo_ref, gt1_ref, g_ref, sh_ref, sc_ref, gt2_ref, wg_ref, wu_ref, wd_ref,
                out_ref, xn_scr, h_scr, acc_scr):
    c = pl.program_id(2)

    @pl.when(c == 0)
    def _():
        xn = _mixer_residual(x_ref, o_ref, wo_ref, gt1_ref)
        xn_scr[...] = xn
        h_scr[...] = _modulate(xn, g_ref[...], sh_ref[0], sc_ref[0]).astype(BF16)
        acc_scr[...] = jnp.zeros(acc_scr.shape, F32)

    acc_scr[...] += _swiglu_chunk(h_scr[...], wg_ref[...], wu_ref[...], wd_ref[...])

    @pl.when(c == pl.num_programs(2) - 1)
    def _():
        out_ref[0] = xn_scr[...] + gt2_ref[0] * acc_scr[...]


def _moe_pre_kernel(x_ref, o_ref, wo_ref, gt1_ref, g_ref, sh_ref, sc_ref, wr_ref, xn_ref, h_ref, gate_ref, sel_ref):
    xn = _mixer_residual(x_ref, o_ref, wo_ref, gt1_ref)
    xn_ref[0] = xn
    h = _modulate(xn, g_ref[...], sh_ref[0], sc_ref[0])
    h_ref[0] = h
    lane = lax.broadcasted_iota(jnp.int32, (h.shape[0], LANE), 1)
    lanef = lane.astype(F32)
    logits = jnp.dot(h.astype(BF16), wr_ref[...], preferred_element_type=F32)
    logits = jnp.where(lane < N_EXPERTS, logits, -jnp.inf)
    m1 = jnp.max(logits, axis=1, keepdims=True)
    i1 = jnp.min(jnp.where(logits == m1, lanef, float(LANE)), axis=1, keepdims=True)
    rest = jnp.where(lanef == i1, -jnp.inf, logits)
    m2 = jnp.max(rest, axis=1, keepdims=True)
    i2 = jnp.min(jnp.where(rest == m2, lanef, float(LANE)), axis=1, keepdims=True)
    e2 = jnp.exp(m2 - m1)
    den = 1.0 + e2
    gates = jnp.where(lanef == i1, 1.0 / den, 0.0) + jnp.where(lanef == i2, e2 / den, 0.0)
    sel = jnp.where(lanef == i1, 1.0, 0.0) + jnp.where(lanef == i2, 1.0, 0.0)
    gate_ref[0] = gates[:, :N_EXPERTS]
    sel_ref[0] = sel[:, :N_EXPERTS]


def _moe_expert_kernel(te_tab, tv_tab, src_hbm, dst_hbm, h_hbm, wg_ref, wu_ref, wd_ref, y_hbm,
                       src_smem, dst_smem, hbuf, hb, acc, sem_idx, sem_in, sem_out, *, rows):
    i, c = pl.program_id(0), pl.program_id(1)
    valid = tv_tab[i] != 0
    all_rows_in = pltpu.make_async_copy(h_hbm.at[pl.ds(0, rows), :], hbuf, sem_in)
    all_rows_out = pltpu.make_async_copy(acc, y_hbm.at[pl.ds(0, rows), :], sem_out)

    @pl.when(valid & (c == 0))
    def _():
        idx_copies = [pltpu.make_async_copy(src_hbm.at[i], src_smem, sem_idx.at[0]),
                      pltpu.make_async_copy(dst_hbm.at[i], dst_smem, sem_idx.at[1])]
        for cp in idx_copies:
            cp.start()
        for cp in idx_copies:
            cp.wait()

        def gather_rows(r0, carry):
            for u in range(DMA_BURST):
                r = r0 * DMA_BURST + u
                pltpu.make_async_copy(h_hbm.at[pl.ds(src_smem[r], 1), :], hbuf.at[pl.ds(r, 1), :],
                                      sem_in).start(priority=u % 2)
            return carry
        lax.fori_loop(0, rows // DMA_BURST, gather_rows, 0)
        all_rows_in.wait()
        hb[...] = hbuf[...].astype(BF16)
        acc[...] = jnp.zeros(acc.shape, F32)

    @pl.when(valid)
    def _():
        acc[...] += _swiglu_chunk(hb[...], wg_ref[0], wu_ref[0], wd_ref[0])

    @pl.when(valid & (c == pl.num_programs(1) - 1))
    def _():
        def scatter_rows(r0, carry):
            for u in range(DMA_BURST):
                r = r0 * DMA_BURST + u
                pltpu.make_async_copy(acc.at[pl.ds(r, 1), :], y_hbm.at[pl.ds(dst_smem[r], 1), :],
                                      sem_out).start(priority=u % 2)
            return carry
        lax.fori_loop(0, rows // DMA_BURST, scatter_rows, 0)
        all_rows_out.wait()


def _moe_combine_kernel(xn_ref, gt2_ref, g2_ref, ya_ref, yb_ref, out_ref):
    g2 = g2_ref[0]
    out_ref[0] = xn_ref[0] + gt2_ref[0] * (g2[:, 0:1] * ya_ref[...] + g2[:, 1:2] * yb_ref[...])


def _token_specs(tm, nd):
    pick = (lambda f: lambda b, i: f(b, i)) if nd == 2 else (lambda f: lambda b, i, c: f(b, i))
    tok = pl.BlockSpec((1, tm, D_MODEL), pick(lambda b, i: (b, i, 0)))

    def mod_spec(a):
        return pl.BlockSpec((1, 1, D_MODEL), pick(lambda b, i: (b, 0, 0))) if a.shape[1] == 1 else tok

    cst = lambda shape: pl.BlockSpec(shape, pick(lambda b, i: (0, 0)))
    return tok, mod_spec, cst


def _swiglu_mixer(x, o, w_out, gt1, g, shift, scale, gt2, w_gate, w_up, w_down, *, tm):
    gsz, t, _ = x.shape
    d_ff = w_gate.shape[-1]
    assert t % tm == 0 and d_ff % FF_TILE == 0
    tok, mod_spec, cst = _token_specs(tm, 3)
    return pl.pallas_call(
        _ffn_kernel, grid=(gsz, t // tm, d_ff // FF_TILE),
        in_specs=[tok, tok, cst((ATTN_WIDTH, D_MODEL)), mod_spec(gt1), cst((1, D_MODEL)),
                  mod_spec(shift), mod_spec(scale), mod_spec(gt2),
                  pl.BlockSpec((D_MODEL, FF_TILE), lambda b, i, c: (0, c)),
                  pl.BlockSpec((D_MODEL, FF_TILE), lambda b, i, c: (0, c)),
                  pl.BlockSpec((FF_TILE, D_MODEL), lambda b, i, c: (c, 0))],
        out_specs=tok,
        out_shape=jax.ShapeDtypeStruct((gsz, t, D_MODEL), F32),
        scratch_shapes=[pltpu.VMEM((tm, D_MODEL), F32), pltpu.VMEM((tm, D_MODEL), BF16),
                        pltpu.VMEM((tm, D_MODEL), F32)],
        compiler_params=_params(("arbitrary",) * 3),
        name="swiglu_channel_mixer",
    )(x, o, w_out.astype(BF16), gt1, g[None, :], shift, scale, gt2,
      w_gate.astype(BF16), w_up.astype(BF16), w_down.astype(BF16))


def _moe_pre(x, o, w_out, gt1, g, shift, scale, w_router, *, tm):
    gsz, t, _ = x.shape
    assert t % tm == 0
    tok, mod_spec, cst = _token_specs(tm, 2)
    small = pl.BlockSpec((1, tm, N_EXPERTS), lambda b, i: (b, i, 0))
    wr = jnp.pad(w_router.astype(BF16), ((0, 0), (0, LANE - N_EXPERTS)))
    return pl.pallas_call(
        _moe_pre_kernel, grid=(gsz, t // tm),
        in_specs=[tok, tok, cst((ATTN_WIDTH, D_MODEL)), mod_spec(gt1), cst((1, D_MODEL)),
                  mod_spec(shift), mod_spec(scale), cst((D_MODEL, LANE))],
        out_specs=[tok, tok, small, small],
        out_shape=[jax.ShapeDtypeStruct((gsz, t, D_MODEL), F32), jax.ShapeDtypeStruct((gsz, t, D_MODEL), F32),
                   jax.ShapeDtypeStruct((gsz, t, N_EXPERTS), F32), jax.ShapeDtypeStruct((gsz, t, N_EXPERTS), F32)],
        compiler_params=_params(ARB2),
        name="moe_pre_router",
    )(x, o, w_out.astype(BF16), gt1, g[None, :], shift, scale, wr)


def _moe_routing(gates, sel, rows):
    n = gates.shape[0]
    member = sel > 0
    mi = member.astype(jnp.int32)
    counts = mi.sum(0)
    padded = (counts + rows - 1) // rows * rows
    ends = jnp.cumsum(padded)
    slot = jnp.cumsum(mi, axis=1) - mi
    expert_of = [jnp.sum(jnp.where(member & (slot == k), jnp.arange(N_EXPERTS)[None, :], 0), axis=1)
                 for k in range(TOP_K)]
    g2 = jnp.stack([jnp.sum(jnp.where(member & (slot == k), gates, 0.0), axis=1) for k in range(TOP_K)], axis=1)
    n_rows = (2 * n + N_EXPERTS * rows) // rows * rows
    n_pad = n_rows - 2 * n
    pad_ends = jnp.cumsum(padded - counts)
    pad_expert = (jnp.arange(n_pad, dtype=jnp.int32)[:, None] >= pad_ends[None, :]).sum(1)
    tok = jnp.arange(n, dtype=jnp.int32)
    sort_key = jnp.concatenate([expert_of[0], expert_of[1], pad_expert]).astype(jnp.int32)
    pair = jnp.concatenate([tok, n + tok, jnp.full((n_pad,), -1, jnp.int32)])
    _, pair = lax.sort((sort_key, pair), num_keys=1, is_stable=True)
    spare = 2 * n + jnp.arange(n_rows, dtype=jnp.int32) % rows
    src = jnp.where(pair >= 0, pair % n, 0)
    dst = jnp.where(pair >= 0, pair, spare)
    n_tiles = n_rows // rows
    tile_start = jnp.arange(n_tiles, dtype=jnp.int32) * rows
    tile_expert = jnp.minimum((tile_start[:, None] >= ends[None, :]).sum(1), N_EXPERTS - 1).astype(jnp.int32)
    tile_valid = (tile_start < ends[-1]).astype(jnp.int32)
    return src.reshape(n_tiles, rows), dst.reshape(n_tiles, rows), tile_expert, tile_valid, g2


def _moe_experts(h_all, src, dst, tile_expert, tile_valid, w_gate, w_up, w_down, *, rows):
    n = h_all.shape[0]
    n_tiles = src.shape[0]
    d_ff = w_gate.shape[-1]
    any_spec = pl.BlockSpec(memory_space=pl.ANY)
    return pl.pallas_call(
        functools.partial(_moe_expert_kernel, rows=rows),
        grid_spec=pltpu.PrefetchScalarGridSpec(
            num_scalar_prefetch=2, grid=(n_tiles, d_ff // FF_TILE),
            in_specs=[any_spec, any_spec, any_spec,
                      pl.BlockSpec((1, D_MODEL, FF_TILE), lambda i, c, te, tv: (te[i], 0, c)),
                      pl.BlockSpec((1, D_MODEL, FF_TILE), lambda i, c, te, tv: (te[i], 0, c)),
                      pl.BlockSpec((1, FF_TILE, D_MODEL), lambda i, c, te, tv: (te[i], c, 0))],
            out_specs=any_spec,
            scratch_shapes=[pltpu.SMEM((rows,), jnp.int32), pltpu.SMEM((rows,), jnp.int32),
                            pltpu.VMEM((rows, D_MODEL), F32), pltpu.VMEM((rows, D_MODEL), BF16),
                            pltpu.VMEM((rows, D_MODEL), F32), pltpu.SemaphoreType.DMA((2,)),
                            pltpu.SemaphoreType.DMA(()), pltpu.SemaphoreType.DMA(())]),
        out_shape=jax.ShapeDtypeStruct((2 * n + rows, D_MODEL), F32),
        compiler_params=_params(ARB2),
        name="moe_routed_experts",
    )(tile_expert, tile_valid, src, dst, h_all, w_gate.astype(BF16), w_up.astype(BF16), w_down.astype(BF16))


def _moe_combine(xn, gt2, g2, y, *, tm, n_all, tok0):
    gsz, t, _ = xn.shape
    assert t % tm == 0 and n_all % tm == 0 and tok0 % tm == 0
    tok, mod_spec, _ = _token_specs(tm, 2)
    nt = t // tm
    yspec = lambda slot: pl.BlockSpec((tm, D_MODEL), lambda b, i: ((slot * n_all + tok0) // tm + b * nt + i, 0))
    return pl.pallas_call(
        _moe_combine_kernel, grid=(gsz, nt),
        in_specs=[tok, mod_spec(gt2), pl.BlockSpec((1, tm, TOP_K), lambda b, i: (b, i, 0)), yspec(0), yspec(1)],
        out_specs=tok,
        out_shape=jax.ShapeDtypeStruct((gsz, t, D_MODEL), F32),
        compiler_params=_params(ARB2),
        name="moe_combine",
    )(xn, gt2, g2, y, y)


def _per_token(a, n):
    return jnp.broadcast_to(a, (a.shape[0], n, a.shape[2])).reshape(1, a.shape[0] * n, a.shape[2])


def _with_ones_column(v):
    return jnp.concatenate([v, jnp.ones(v.shape[:-1] + (1,), BF16),
                            jnp.zeros(v.shape[:-1] + (LANE - HEAD_DIM - 1,), BF16)], axis=-1)


def _pad_keys(a, axis, l_pad):
    pad = [(0, 0)] * a.ndim
    pad[axis] = (0, l_pad - a.shape[axis])
    return jnp.pad(a, pad)


def _layer_a(xp, xs, cp, cs, ck, cv, cki, rel_bias, norm_g, w_ada, b_ada, w_in, w_out, g_q, g_k, g_kidx,
             ffn_wg, ffn_wu, ffn_wd):
    bias = _dsa_bias_tiles(rel_bias)
    nb, n = xs.shape[0], xs.shape[1]
    past = ck.shape[1]
    outs = []
    for x, c, sample in ((xp, cp, False), (xs, cs, True)):
        sh1, sc1, gt1, sh2, sc2, gt2 = adaln(c, w_ada, b_ada)
        q16, kf, vf, ktr, vx, kif, qi16, kit, wsc = _project(
            x, norm_g[0], sh1, sc1, w_in, g_q, g_k, g_kidx, n_kv=KV_HEADS_A,
            tm=n if sample else PROJ_TM)
        if sample:
            l_true = past + n
            l_pad = -(-l_true // DSA_SUB) * DSA_SUB
            kfull = jnp.concatenate([ck, kf.reshape(nb, n, KV_HEADS_A, HEAD_DIM)], axis=1).astype(BF16)
            vfull = jnp.concatenate([cv, vf.reshape(nb, n, KV_HEADS_A, HEAD_DIM)], axis=1).astype(BF16)
            kifull = jnp.concatenate([cki, kif], axis=1).astype(BF16)
            ktr = _pad_keys(kfull, 1, l_pad).transpose(0, 2, 3, 1)
            vx = _pad_keys(_with_ones_column(vfull), 1, l_pad).transpose(0, 2, 1, 3)
            kit = _pad_keys(kifull, 1, l_pad).transpose(0, 2, 1)
            o = _dsa_attention(q16, ktr, vx, qi16, kit, wsc, bias, p0=past, l_true=l_true,
                               n_sel=min(TOPK_MAX, l_true // 4), tq=n, tk=l_pad, sw=DSA_SUB, cw=DSA_SUB,
                               tk_idx=l_pad)
            flat = lambda a: a.reshape(1, nb * n, a.shape[-1])
            y = _swiglu_mixer(flat(x), flat(o), w_out, _per_token(gt1, n), norm_g[1], _per_token(sh2, n),
                              _per_token(sc2, n), _per_token(gt2, n), ffn_wg, ffn_wu, ffn_wd, tm=MIX_TM_SAMPLE)
            y = y.reshape(x.shape)
        else:
            t = x.shape[1]
            o = _dsa_attention(q16, ktr, vx, qi16, kit, wsc, bias, p0=0, l_true=t,
                               n_sel=min(TOPK_MAX, t // 4), tq=LANE, tk=DSA_KEY_TILE, sw=DSA_SUB,
                               cw=DSA_KEY_TILE, tk_idx=DSA_INDEX_TILE)
            y = _swiglu_mixer(x, o, w_out, gt1, norm_g[1], sh2, sc2, gt2, ffn_wg, ffn_wu, ffn_wd, tm=MIX_TM_PROMPT)
        bsz, t = x.shape[0], x.shape[1]
        outs.append((y, kf.reshape(bsz, t, KV_HEADS_A, HEAD_DIM), vf.reshape(bsz, t, KV_HEADS_A, HEAD_DIM), kif))
    return outs


def _layer_b(xp, xs, cp, cs, ck, cv, rel_bias, norm_g, w_ada, b_ada, w_in, w_out, g_q, g_k, sinks,
             w_router, moe_wg, moe_wu, moe_wd):
    bias = _band_bias(rel_bias)
    nb, n = xs.shape[0], xs.shape[1]
    streams = []
    for x, c, sample in ((xp, cp, False), (xs, cs, True)):
        sh1, sc1, gt1, sh2, sc2, gt2 = adaln(c, w_ada, b_ada)
        q16, kf, vf, ktr, vx = _project(x, norm_g[0], sh1, sc1, w_in, g_q, g_k, n_kv=KV_HEADS_B,
                                         tm=n if sample else PROJ_TM)
        bsz, t = x.shape[0], x.shape[1]
        k4 = kf.reshape(bsz, t, KV_HEADS_B, HEAD_DIM)
        v4 = vf.reshape(bsz, t, KV_HEADS_B, HEAD_DIM)
        zero = lambda i: 0
        if sample:
            kp = ck.astype(BF16).transpose(0, 2, 3, 1)
            vp = _with_ones_column(cv.astype(BF16)).transpose(0, 2, 1, 3)
            o = _band_attention(q16, kp, _pad_keys(ktr, 3, LANE), vp, _pad_keys(vx, 2, LANE), bias, sinks,
                                tq=n, prev_map=zero, cur_map=zero, first_has_no_prev=False)
            flat = lambda a: a.reshape(1, nb * n, a.shape[-1])
            xn, h, gates, sel = _moe_pre(flat(x), flat(o), w_out, _per_token(gt1, n), norm_g[1],
                                         _per_token(sh2, n), _per_token(sc2, n), w_router, tm=MIX_TM_SAMPLE)
            gt2 = _per_token(gt2, n)
            k_new = jnp.concatenate([ck, k4], axis=1)[:, -WINDOW:]
            v_new = jnp.concatenate([cv, v4], axis=1)[:, -WINDOW:]
        else:
            o = _band_attention(q16, ktr, ktr, vx, vx, bias, sinks, tq=LANE,
                                prev_map=lambda i: jnp.maximum(i - 1, 0), cur_map=lambda i: i,
                                first_has_no_prev=True)
            xn, h, gates, sel = _moe_pre(x, o, w_out, gt1, norm_g[1], sh2, sc2, w_router, tm=MIX_TM_PROMPT)
            k_new, v_new = k4[:, -WINDOW:], v4[:, -WINDOW:]
        streams.append((x.shape, xn, h, gates, sel, gt2, k_new, v_new))

    rows2d = lambda a: a.reshape(-1, a.shape[-1])
    h_all = jnp.concatenate([rows2d(st[2]) for st in streams])
    n_all = h_all.shape[0]
    src, dst, tile_expert, tile_valid, g2 = _moe_routing(
        jnp.concatenate([rows2d(st[3]) for st in streams]), jnp.concatenate([rows2d(st[4]) for st in streams]),
        MOE_ROW_TILE)
    y = _moe_experts(h_all, src, dst, tile_expert, tile_valid, moe_wg, moe_wu, moe_wd, rows=MOE_ROW_TILE)
    outs, tok0 = [], 0
    for shape, xn, _, _, _, gt2, k_new, v_new in streams:
        cnt = xn.shape[0] * xn.shape[1]
        out = _moe_combine(xn, gt2, g2[tok0:tok0 + cnt].reshape(xn.shape[0], xn.shape[1], TOP_K), y,
                           tm=MOE_COMBINE_TM, n_all=n_all, tok0=tok0)
        outs.append((out.reshape(shape), k_new, v_new))
        tok0 += cnt
    return outs


def kernel(x_prompt, x_sample, c_prompt, c_sample, cache_a_k, cache_a_v, cache_a_kidx,
           cache_b_k, cache_b_v, rel_bias, norm_g, w_ada, b_ada,
           a_w_in, a_w_out, a_g_q, a_g_k, a_g_kidx,
           b_w_in, b_w_out, b_g_q, b_g_k, b_sinks,
           ffn_w_gate, ffn_w_up, ffn_w_down,
           moe_w_router, moe_w_gate, moe_w_up, moe_w_down):
    xp, xs = x_prompt, x_sample
    a_out = [[] for _ in range(6)]
    b_out = [[] for _ in range(4)]
    for i in range(DEPTH):
        j = i // 2
        if i % 2 == 0:
            (xp, k1, v1, i1), (xs, k2, v2, i2) = _layer_a(
                xp, xs, c_prompt, c_sample, cache_a_k[j], cache_a_v[j], cache_a_kidx[j], rel_bias, norm_g[i],
                w_ada[i], b_ada[i], a_w_in[j], a_w_out[j], a_g_q[j], a_g_k[j], a_g_kidx[j],
                ffn_w_gate[j], ffn_w_up[j], ffn_w_down[j])
            for lst, val in zip(a_out, (k1, v1, i1, k2, v2, i2)):
                lst.append(val)
        else:
            (xp, k1, v1), (xs, k2, v2) = _layer_b(
                xp, xs, c_prompt, c_sample, cache_b_k[j], cache_b_v[j], rel_bias, norm_g[i],
                w_ada[i], b_ada[i], b_w_in[j], b_w_out[j], b_g_q[j], b_g_k[j], b_sinks[j],
                moe_w_router[j], moe_w_gate[j], moe_w_up[j], moe_w_down[j])
            for lst, val in zip(b_out, (k1, v1, k2, v2)):
                lst.append(val)
    return (xp, xs, *(jnp.stack(l) for l in a_out), *(jnp.stack(l) for l in b_out))
```

```python
import math, functools
import jax, jax.numpy as jnp
from jax import lax
import numpy as np
from jax.experimental import pallas as pl
from jax.experimental.pallas import tpu as pltpu

D_MODEL = 1024
DEPTH = 2
CHUNK = 64
N_HEADS = 16
HEAD_DIM = 64
ATTN_WIDTH = N_HEADS * HEAD_DIM
KV_HEADS_A = 4
IDX_HEADS = 8
IDX_DIM = 64
TOPK_MAX = 256
KV_HEADS_B = 2
WINDOW = 128
NUM_BUCKETS = 32
MAX_DISTANCE = 1024
N_EXPERTS = 8
TOP_K = 2
EPS = 1e-6

F32 = jnp.float32
BF16 = jnp.bfloat16
LANE = 128
VMEM_LIMIT_BYTES = 48 * 1024 * 1024
LOG2E = math.log2(math.e)
NEG_BIG = -1e30
KEY_MIN = -2 ** 31
FAR_BUCKET = NUM_BUCKETS // 2 - 1
NEAR_BLOCKS = 6
FF_TILE = 512
MIX_TM_PROMPT = 1024
MIX_TM_SAMPLE = 256
MOE_ROW_TILE = 1024
MOE_COMBINE_TM = 256
DMA_BURST = 8
PROJ_TM = 256
DSA_SUB = 2 * LANE
DSA_KEY_TILE = 2048
DSA_COUNT_CHUNK = 1024
CHUNK_SHIFT = CHUNK.bit_length() - 1
LANE_SHIFT = LANE.bit_length() - 1
COUNT_ALL = float(2 ** 24)
ARB2 = ("arbitrary", "arbitrary")


def _params(sem, vmem=True):
    return pltpu.CompilerParams(dimension_semantics=sem, vmem_limit_bytes=VMEM_LIMIT_BYTES if vmem else None)


def rel_bucket(rel):
    half = NUM_BUCKETS // 2
    max_exact = half // 2
    base = jnp.where(rel > 0, half, 0)
    n = jnp.abs(rel)
    nf = jnp.maximum(n, 1).astype(F32)
    large = max_exact + (jnp.log(nf / max_exact) / math.log(MAX_DISTANCE / max_exact)
                         * (half - max_exact)).astype(jnp.int32)
    large = jnp.minimum(large, half - 1)
    return base + jnp.where(n < max_exact, n, large)


def _adaln_kernel(c_ref, w_ref, b_ref, o_ref):
    c = c_ref[...]
    o_ref[...] = jnp.dot(c * jax.nn.sigmoid(c), w_ref[...], preferred_element_type=F32) + b_ref[...]


def adaln(c, w, b):
    n = c.shape[0]
    mod = pl.pallas_call(
        _adaln_kernel,
        grid=(6,),
        in_specs=[pl.BlockSpec((n, D_MODEL), lambda j: (0, 0)),
                  pl.BlockSpec((D_MODEL, D_MODEL), lambda j: (0, j)),
                  pl.BlockSpec((1, D_MODEL), lambda j: (0, j))],
        out_specs=pl.BlockSpec((n, D_MODEL), lambda j: (0, j)),
        out_shape=jax.ShapeDtypeStruct((n, 6 * D_MODEL), F32),
        name="adaln",
    )(c, w, b[None, :])
    return jnp.split(mod[:, None, :], 6, axis=-1)


def _modulate(x, g, shift, scale):
    y = x * lax.rsqrt(jnp.mean(x * x, axis=-1, keepdims=True) + EPS)
    return (y * g) * (1.0 + scale) + shift


def _seg_norm(seg, g):
    return seg * lax.rsqrt(jnp.mean(seg * seg, axis=-1, keepdims=True) + EPS) * g


def _proj_kernel(x_ref, g_ref, sh_ref, sc_ref, w_ref, gq_ref, gk_ref, *rest, n_kv, has_idx):
    h = _modulate(x_ref[0], g_ref[...], sh_ref[0], sc_ref[0]).astype(BF16)
    y = jnp.dot(h, w_ref[...], preferred_element_type=F32)
    tm = y.shape[0]
    kv_w = n_kv * HEAD_DIM
    if has_idx:
        gki_ref, q16, kf, vf, ktr, vx, kif, qi16, kit, wsc = rest
    else:
        q16, kf, vf, ktr, vx = rest
    for hd in range(N_HEADS):
        seg = y[:, hd * HEAD_DIM:(hd + 1) * HEAD_DIM]
        q16[0, hd] = (_seg_norm(seg, gq_ref[...]) * (HEAD_DIM ** -0.5 * LOG2E)).astype(BF16)
    lane = lax.broadcasted_iota(jnp.int32, (tm, LANE - HEAD_DIM), 1)
    ones_col = jnp.where(lane == 0, 1.0, 0.0).astype(BF16)
    for j in range(n_kv):
        kseg = _seg_norm(y[:, ATTN_WIDTH + j * HEAD_DIM:ATTN_WIDTH + (j + 1) * HEAD_DIM], gk_ref[...])
        vseg = y[:, ATTN_WIDTH + kv_w + j * HEAD_DIM:ATTN_WIDTH + kv_w + (j + 1) * HEAD_DIM]
        kf[0, :, j * HEAD_DIM:(j + 1) * HEAD_DIM] = kseg
        vf[0, :, j * HEAD_DIM:(j + 1) * HEAD_DIM] = vseg
        ktr[0, j] = kseg.T.astype(BF16)
        vx[0, j] = jnp.concatenate([vseg.astype(BF16), ones_col], axis=1)
    if has_idx:
        base = ATTN_WIDTH + 2 * kv_w
        for hd in range(IDX_HEADS):
            qi16[0, hd] = y[:, base + hd * IDX_DIM:base + (hd + 1) * IDX_DIM].astype(BF16)
        base += IDX_HEADS * IDX_DIM
        kiseg = _seg_norm(y[:, base:base + IDX_DIM], gki_ref[...])
        kif[0] = kiseg
        kit[0] = kiseg.T.astype(BF16)
        wsc[0] = y[:, base + IDX_DIM:base + IDX_DIM + IDX_HEADS] * IDX_HEADS ** -0.5 * IDX_DIM ** -0.5


def _project(x, g, shift, scale, w_in, g_q, g_k, g_kidx=None, *, n_kv, tm):
    has_idx = g_kidx is not None
    gsz, t, _ = x.shape
    n_in = w_in.shape[1]
    kv_w = n_kv * HEAD_DIM
    tok = lambda b, i: (b, i, 0)
    head = lambda b, i: (b, 0, i, 0)
    vec = lambda b, i: (b, 0, 0)
    cst = lambda b, i: (0, 0)
    shapes = [((gsz, N_HEADS, t, HEAD_DIM), BF16, (1, N_HEADS, tm, HEAD_DIM), head),
              ((gsz, t, kv_w), F32, (1, tm, kv_w), tok),
              ((gsz, t, kv_w), F32, (1, tm, kv_w), tok),
              ((gsz, n_kv, HEAD_DIM, t), BF16, (1, n_kv, HEAD_DIM, tm), lambda b, i: (b, 0, 0, i)),
              ((gsz, n_kv, t, LANE), BF16, (1, n_kv, tm, LANE), head)]
    if has_idx:
        shapes += [((gsz, t, IDX_DIM), F32, (1, tm, IDX_DIM), tok),
                   ((gsz, IDX_HEADS, t, IDX_DIM), BF16, (1, IDX_HEADS, tm, IDX_DIM), head),
                   ((gsz, IDX_DIM, t), BF16, (1, IDX_DIM, tm), lambda b, i: (b, 0, i)),
                   ((gsz, t, IDX_HEADS), F32, (1, tm, IDX_HEADS), tok)]
    return pl.pallas_call(
        functools.partial(_proj_kernel, n_kv=n_kv, has_idx=has_idx),
        grid=(gsz, t // tm),
        in_specs=[pl.BlockSpec((1, tm, D_MODEL), tok), pl.BlockSpec((1, D_MODEL), cst),
                  pl.BlockSpec((1, 1, D_MODEL), vec), pl.BlockSpec((1, 1, D_MODEL), vec),
                  pl.BlockSpec((D_MODEL, n_in), cst), pl.BlockSpec((1, HEAD_DIM), cst),
                  pl.BlockSpec((1, HEAD_DIM), cst)] + ([pl.BlockSpec((1, IDX_DIM), cst)] if has_idx else []),
        out_specs=[pl.BlockSpec(blk, im) for _, _, blk, im in shapes],
        out_shape=[jax.ShapeDtypeStruct(s, d) for s, d, _, _ in shapes],
        compiler_params=_params(ARB2),
        name="mixer_in_proj",
    )(x, g[None, :], shift, scale, w_in.astype(BF16), g_q[None, :], g_k[None, :],
      *([g_kidx[None, :]] if has_idx else []))


def _tri_steps(nqb, tq, tk, p0):
    qbs, kts = [], []
    for qb in range(nqb):
        vis_end = ((p0 + qb * tq + tq - 1) // CHUNK + 1) * CHUNK
        for kt in range(-(-vis_end // tk)):
            qbs.append(qb)
            kts.append(kt)
    return jnp.asarray(np.array(qbs, np.int32)), jnp.asarray(np.array(kts, np.int32))


def _idx_kernel(qb_tab, kt_tab, qi_ref, w_ref, kit_ref, out_ref, *, tq, tk, sw, p0, l_true):
    step = pl.program_id(1)
    q0 = p0 + qb_tab[step] * tq
    k0 = kt_tab[step] * tk
    row = lax.broadcasted_iota(jnp.int32, (tq, 1), 0) + q0
    vis_end = jnp.minimum(((row >> CHUNK_SHIFT) + 1) << CHUNK_SHIFT, l_true)
    wv = w_ref[0]
    for c in range(tk // sw):
        kit = kit_ref[0, :, c * sw:(c + 1) * sw]
        acc = jnp.zeros((tq, sw), F32)
        for h in range(IDX_HEADS):
            sh = jnp.dot(qi_ref[0, h], kit, preferred_element_type=F32)
            acc = acc + wv[:, h:h + 1] * jnp.maximum(sh, 0.0)
        bits = lax.bitcast_convert_type(acc, jnp.int32)
        key = bits ^ ((bits >> 31) & 0x7FFFFFFF)
        kpos = k0 + c * sw + lax.broadcasted_iota(jnp.int32, (tq, sw), 1)
        out_ref[0, :, c * sw:(c + 1) * sw] = jnp.where(kpos < vis_end, key, KEY_MIN)


def _thr_kernel(keys_ref, tau_ref, quota_ref, flag_ref, *, tq, cw, p0, n_sel):
    qb = pl.program_id(1)
    vis_end = (((p0 + qb * tq + tq - 1) >> CHUNK_SHIFT) + 1) << CHUNK_SHIFT
    nch = (vis_end + cw - 1) // cw
    key_max = 2 ** 31 - 1

    def count_ge(cand):
        def body(j, acc):
            x = keys_ref[0, :, pl.ds(pl.multiple_of(j * cw, cw), cw)]
            for u in range(cw // LANE):
                acc = acc + jnp.where(x[:, u * LANE:(u + 1) * LANE] >= cand, 1.0, 0.0)
            return acc
        acc = lax.fori_loop(0, nch, body, jnp.zeros((tq, LANE), F32))
        return jnp.sum(acc, axis=1, keepdims=True)

    zero = jnp.zeros((tq, LANE), jnp.int32)
    c0 = count_ge(zero)
    tau = jnp.where(c0 >= n_sel, zero, KEY_MIN)
    n_ge = jnp.where(c0 >= n_sel, c0, COUNT_ALL)

    def bit_body(carry):
        i, tau, n_ge = carry
        cand = tau + jnp.left_shift(jnp.int32(1), 30 - i)
        c = count_ge(cand)
        return i + 1, jnp.where(c >= n_sel, cand, tau), jnp.where(c >= n_sel, c, n_ge)

    def unresolved(carry):
        i, _, n_ge = carry
        return (i < 31) & (jnp.max(jnp.abs(n_ge - n_sel)) > 0.0)

    _, tau, n_ge = lax.while_loop(unresolved, bit_body, (jnp.int32(0), tau, n_ge))
    n_gt = jnp.where(tau[:, :1] < key_max, count_ge(jnp.minimum(tau, key_max - 1) + 1), 0.0)
    tau_ref[0] = jnp.maximum(tau, KEY_MIN + 1)
    quota_ref[0] = jnp.broadcast_to(n_sel - n_gt, (tq, LANE))
    over = jnp.max(jnp.where(n_ge > n_sel, 1, 0), axis=0, keepdims=True)
    flag_ref[0, 0] = jnp.broadcast_to(over, flag_ref.shape[2:])


def _attn_kernel(qb_tab, kt_tab, flag_tab, q_ref, kt_ref, vx_ref, keys_ref, tau_ref, quota_ref, bias_ref, o_ref,
                 m_ref, acc_ref, negm_ref, seen_ref, *, tq, tk, sw, p0, nqb):
    step = pl.program_id(1)
    qb, kt = qb_tab[step], kt_tab[step]
    q0 = p0 + qb * tq
    k0 = kt * tk
    vis_end = (((q0 + tq - 1) >> CHUNK_SHIFT) + 1) << CHUNK_SHIFT
    group = N_HEADS // KV_HEADS_A
    nblk = sw // LANE
    tie_w = 2 * LANE

    @pl.when(kt == 0)
    def _():
        m_ref[...] = jnp.full(m_ref.shape, NEG_BIG, F32)
        acc_ref[...] = jnp.zeros(acc_ref.shape, F32)
        seen_ref[...] = jnp.zeros(seen_ref.shape, F32)

    has_ties = flag_tab[pl.program_id(0) * nqb + qb] != 0
    tau = tau_ref[0]

    @pl.when(jnp.logical_not(has_ties))
    def _():
        for u in range(tk // LANE):
            cols = slice(u * LANE, (u + 1) * LANE)
            negm_ref[:, cols] = jnp.where(keys_ref[0, :, cols] >= tau, 0.0, NEG_BIG)

    @pl.when(has_ties)
    def _():
        tau2 = jnp.concatenate([tau] * (tie_w // LANE), axis=1)
        quota = jnp.concatenate([quota_ref[0]] * (tie_w // LANE), axis=1)
        before = (lax.broadcasted_iota(jnp.int32, (tie_w, tie_w), 0)
                  < lax.broadcasted_iota(jnp.int32, (tie_w, tie_w), 1))
        before = jnp.where(before, 1.0, 0.0).astype(BF16)
        seen = seen_ref[...]
        for u in range(tk // tie_w):
            cols = slice(u * tie_w, (u + 1) * tie_w)
            k = keys_ref[0, :, cols]
            tie = jnp.where(k == tau2, 1.0, 0.0)
            rank = (jnp.dot(tie.astype(BF16), before, preferred_element_type=F32)
                    + jnp.concatenate([seen] * (tie_w // LANE), axis=1))
            keep = jnp.where(k > tau2, 1.0, jnp.where(rank < quota, tie, 0.0))
            negm_ref[:, cols] = jnp.where(keep > 0.0, 0.0, NEG_BIG)
            seen = seen + jnp.sum(tie, axis=1, keepdims=True)
        seen_ref[...] = seen

    def sub_tile(c, with_bias):
        off = c * sw if isinstance(c, int) else pl.multiple_of(c * sw, sw)
        negm = negm_ref[:, pl.ds(off, sw)]
        if with_bias:
            d0 = ((k0 + off - q0) >> LANE_SHIFT) + NEAR_BLOCKS
            bidx = [jnp.clip(d0 + u, 0, NEAR_BLOCKS) for u in range(nblk)]
        for h in range(N_HEADS):
            j = h // group
            s = jnp.dot(q_ref[0, h], kt_ref[0, j, :, pl.ds(off, sw)], preferred_element_type=F32) + negm
            if with_bias:
                s = s + jnp.concatenate([bias_ref[bidx[u], h] for u in range(nblk)], axis=1)
            m_prev = m_ref[h]
            m_cur = jnp.maximum(m_prev, jnp.max(s, axis=1, keepdims=True))
            alpha = jnp.exp2(m_prev - m_cur)
            p = jnp.exp2(s - jnp.concatenate([m_cur] * nblk, axis=1))
            pv = jnp.dot(p.astype(BF16), vx_ref[0, j, pl.ds(off, sw), :], preferred_element_type=F32)
            acc_ref[h] = alpha * acc_ref[h] + pv
            m_ref[h] = m_cur

    tile_far = k0 + tk <= q0 - (NEAR_BLOCKS - 1) * LANE

    @pl.when(tile_far)
    def _():
        for c in range(tk // sw):
            sub_tile(c, False)

    @pl.when(jnp.logical_not(tile_far))
    def _():
        nsub = jnp.minimum(tk // sw, (vis_end - k0 + sw - 1) // sw)
        nfar = jnp.clip((q0 - (NEAR_BLOCKS - 1) * LANE - k0) // sw, 0, nsub)

        def far_body(c, carry):
            sub_tile(c, False)
            return carry

        def near_body(c, carry):
            sub_tile(c, True)
            return carry

        lax.fori_loop(0, nfar, far_body, 0)
        lax.fori_loop(nfar, nsub, near_body, 0)

    @pl.when(k0 + tk >= vis_end)
    def _():
        for h in range(N_HEADS):
            a = acc_ref[h]
            o_ref[0, :, h * HEAD_DIM:(h + 1) * HEAD_DIM] = a[:, :HEAD_DIM] / a[:, HEAD_DIM:HEAD_DIM + 1]


def _rel_tiles(rel_table, rel):
    onehot = jax.nn.one_hot(rel_bucket(rel), NUM_BUCKETS, dtype=F32)
    return jnp.einsum("nrsk,kh->nhrs", onehot, rel_table.astype(F32), precision=lax.Precision.HIGHEST)


def _dsa_bias_tiles(rel_table):
    d = jnp.arange(NEAR_BLOCKS + 1, dtype=jnp.int32)[:, None, None] - NEAR_BLOCKS
    rel = (d * LANE + jnp.arange(LANE, dtype=jnp.int32)[None, None, :]
           - jnp.arange(LANE, dtype=jnp.int32)[None, :, None])
    tiles = (_rel_tiles(rel_table, rel) - rel_table[FAR_BUCKET][None, :, None, None]) * LOG2E
    return tiles.at[0].set(0.0)


def _dsa_attention(q16, ktr, vx, qi16, kit, w, bias, *, p0, l_true, n_sel, tq, tk, sw, cw, tk_idx):
    bsz, t = q16.shape[0], q16.shape[2]
    l_pad = ktr.shape[3]
    nqb = t // tq
    assert t % tq == 0 and p0 % LANE == 0 and (tq == LANE or nqb == 1)
    assert l_pad % tk == 0 and tk % sw == 0 and tk % cw == 0 and tk % (2 * LANE) == 0
    assert l_pad % tk_idx == 0 and tk_idx % tk == 0
    qbt, ktt = _tri_steps(nqb, tq, tk, p0)
    nsteps = int(qbt.shape[0])
    qbt_idx, ktt_idx = _tri_steps(nqb, tq, tk_idx, p0)

    keys = pl.pallas_call(
        functools.partial(_idx_kernel, tq=tq, tk=tk_idx, sw=sw, p0=p0, l_true=l_true),
        grid_spec=pltpu.PrefetchScalarGridSpec(
            num_scalar_prefetch=2, grid=(bsz, int(qbt_idx.shape[0])),
            in_specs=[pl.BlockSpec((1, IDX_HEADS, tq, IDX_DIM), lambda b, s, qt, kt: (b, 0, qt[s], 0)),
                      pl.BlockSpec((1, tq, IDX_HEADS), lambda b, s, qt, kt: (b, qt[s], 0)),
                      pl.BlockSpec((1, IDX_DIM, tk_idx), lambda b, s, qt, kt: (b, 0, kt[s]))],
            out_specs=pl.BlockSpec((1, tq, tk_idx), lambda b, s, qt, kt: (b, qt[s], kt[s]))),
        out_shape=jax.ShapeDtypeStruct((bsz, t, l_pad), jnp.int32),
        compiler_params=_params(ARB2, vmem=False),
        name="dsa_index_keys",
    )(qbt_idx, ktt_idx, qi16, w, kit)

    tau, quota, flags = pl.pallas_call(
        functools.partial(_thr_kernel, tq=tq, cw=cw, p0=p0, n_sel=n_sel),
        grid=(bsz, nqb),
        in_specs=[pl.BlockSpec((1, tq, l_pad), lambda b, i: (b, i, 0))],
        out_specs=[pl.BlockSpec((1, tq, LANE), lambda b, i: (b, i, 0)),
                   pl.BlockSpec((1, tq, LANE), lambda b, i: (b, i, 0)),
                   pl.BlockSpec((1, 1, 8, LANE), lambda b, i: (b, i, 0, 0))],
        out_shape=[jax.ShapeDtypeStruct((bsz, t, LANE), jnp.int32),
                   jax.ShapeDtypeStruct((bsz, t, LANE), F32),
                   jax.ShapeDtypeStruct((bsz, nqb, 8, LANE), jnp.int32)],
        compiler_params=_params(ARB2),
        name="dsa_threshold",
    )(keys)
    flags = flags[:, :, 0, 0].reshape(bsz * nqb)

    return pl.pallas_call(
        functools.partial(_attn_kernel, tq=tq, tk=tk, sw=sw, p0=p0, nqb=nqb),
        grid_spec=pltpu.PrefetchScalarGridSpec(
            num_scalar_prefetch=3, grid=(bsz, nsteps),
            in_specs=[pl.BlockSpec((1, N_HEADS, tq, HEAD_DIM), lambda b, s, qt, kt, fl: (b, 0, qt[s], 0)),
                      pl.BlockSpec((1, KV_HEADS_A, HEAD_DIM, tk), lambda b, s, qt, kt, fl: (b, 0, 0, kt[s])),
                      pl.BlockSpec((1, KV_HEADS_A, tk, LANE), lambda b, s, qt, kt, fl: (b, 0, kt[s], 0)),
                      pl.BlockSpec((1, tq, tk), lambda b, s, qt, kt, fl: (b, qt[s], kt[s])),
                      pl.BlockSpec((1, tq, LANE), lambda b, s, qt, kt, fl: (b, qt[s], 0)),
                      pl.BlockSpec((1, tq, LANE), lambda b, s, qt, kt, fl: (b, qt[s], 0)),
                      pl.BlockSpec((NEAR_BLOCKS + 1, N_HEADS, tq, LANE), lambda b, s, qt, kt, fl: (0, 0, 0, 0))],
            out_specs=pl.BlockSpec((1, tq, ATTN_WIDTH), lambda b, s, qt, kt, fl: (b, qt[s], 0)),
            scratch_shapes=[pltpu.VMEM((N_HEADS, tq, LANE), F32), pltpu.VMEM((N_HEADS, tq, LANE), F32),
                            pltpu.VMEM((tq, tk), F32), pltpu.VMEM((tq, LANE), F32)]),
        out_shape=jax.ShapeDtypeStruct((bsz, t, ATTN_WIDTH), F32),
        compiler_params=_params(ARB2),
        name="dsa_masked_attention",
    )(qbt, ktt, flags, q16, ktr, vx, keys, tau, quota, bias[:, :, :tq, :])


def _band_kernel(q_ref, kp_ref, kc_ref, vp_ref, vc_ref, bias_ref, sink_ref, o_ref, *, tq, first_has_no_prev):
    group = N_HEADS // KV_HEADS_B
    extra = None
    if first_has_no_prev:
        col = lax.broadcasted_iota(jnp.int32, (tq, 2 * LANE), 1)
        extra = jnp.where((pl.program_id(1) == 0) & (col < LANE), NEG_BIG, 0.0)
    ones = jnp.ones((2 * LANE, LANE), BF16)
    low_half = lax.broadcasted_iota(jnp.int32, (tq, LANE), 1) < HEAD_DIM
    for j in range(KV_HEADS_B):
        heads = slice(j * group, (j + 1) * group)
        kw = jnp.concatenate([kp_ref[0, j], kc_ref[0, j]], axis=1)
        v_low = jnp.concatenate([vp_ref[0, j], vc_ref[0, j]], axis=0)
        v_high = jnp.concatenate([v_low[:, HEAD_DIM:], v_low[:, :HEAD_DIM]], axis=1)
        v_sum = [jnp.concatenate([v, ones], axis=1) for v in (v_low, v_high)]
        s_all = jnp.dot(q_ref[0, heads].reshape(group * tq, HEAD_DIM), kw,
                        preferred_element_type=F32).reshape(group, tq, 2 * LANE)
        for g0 in range(0, group, 2):
            halves = []
            for par in range(2):
                h = j * group + g0 + par
                s = s_all[g0 + par] + bias_ref[h]
                if extra is not None:
                    s = s + extra
                sink = sink_ref[h]
                m = jnp.maximum(jnp.max(s, axis=1, keepdims=True), sink)
                p = jnp.exp2(s - jnp.concatenate([m, m], axis=1)).astype(BF16)
                r = jnp.dot(p, v_sum[par], preferred_element_type=F32)
                halves.append(r[:, :LANE] / (r[:, LANE:] + jnp.exp2(sink - m)))
            h0 = j * group + g0
            o_ref[0, :, h0 * HEAD_DIM:(h0 + 2) * HEAD_DIM] = jnp.where(low_half, halves[0], halves[1])


def _band_bias(rel_table):
    r = jnp.arange(LANE, dtype=jnp.int32)[:, None]
    s = jnp.arange(2 * LANE, dtype=jnp.int32)[None, :]
    wc, qh = s // CHUNK, r // CHUNK
    band = (wc >= qh) & (wc <= qh + 2)
    tiles = _rel_tiles(rel_table, (s - LANE - r)[None])[0]
    return jnp.where(band[None], tiles * LOG2E, NEG_BIG)


def _band_attention(q16, kp, kc, vp, vc, bias, sinks, *, tq, prev_map, cur_map, first_has_no_prev):
    bsz, t = q16.shape[0], q16.shape[2]
    kblk = lambda m: pl.BlockSpec((1, KV_HEADS_B, HEAD_DIM, LANE), lambda b, i: (b, 0, 0, m(i)))
    vblk = lambda m: pl.BlockSpec((1, KV_HEADS_B, LANE, LANE), lambda b, i: (b, 0, m(i), 0))
    sink_rows = jnp.broadcast_to((sinks.astype(F32) * LOG2E)[:, None, None], (N_HEADS, 1, LANE))
    return pl.pallas_call(
        functools.partial(_band_kernel, tq=tq, first_has_no_prev=first_has_no_prev),
        grid=(bsz, t // tq),
        in_specs=[pl.BlockSpec((1, N_HEADS, tq, HEAD_DIM), lambda b, i: (b, 0, i, 0)),
                  kblk(prev_map), kblk(cur_map), vblk(prev_map), vblk(cur_map),
                  pl.BlockSpec((N_HEADS, tq, 2 * LANE), lambda b, i: (0, 0, 0)),
                  pl.BlockSpec((N_HEADS, 1, LANE), lambda b, i: (0, 0, 0))],
        out_specs=pl.BlockSpec((1, tq, ATTN_WIDTH), lambda b, i: (b, i, 0)),
        out_shape=jax.ShapeDtypeStruct((bsz, t, ATTN_WIDTH), F32),
        compiler_params=_params(ARB2, vmem=False),
        name="band_attention",
    )(q16, kp, kc, vp, vc, bias[:, :tq, :], sink_rows)


def _mixer_residual(x_ref, o_ref, wo_ref, gt1_ref):
    return x_ref[0] + gt1_ref[0] * jnp.dot(o_ref[0].astype(BF16), wo_ref[...], preferred_element_type=F32)


def _swiglu_chunk(h, wg, wu, wd):
    a = jax.nn.silu(jnp.dot(h, wg, preferred_element_type=F32)) * jnp.dot(h, wu, preferred_element_type=F32)
    return jnp.dot(a.astype(BF16), wd, preferred_element_type=F32)


def _ffn_kernel(x_ref, o_ref, wo_ref, gt1_ref, g_ref, sh_ref, sc_ref, gt2_ref, wg_ref, wu_ref, wd_ref,
                out_ref, xn_scr, h_scr, acc_scr):
    c = pl.program_id(2)

    @pl.when(c == 0)
    def _():
        xn = _mixer_residual(x_ref, o_ref, wo_ref, gt1_ref)
        xn_scr[...] = xn
        h_scr[...] = _modulate(xn, g_ref[...], sh_ref[0], sc_ref[0]).astype(BF16)
        acc_scr[...] = jnp.zeros(acc_scr.shape, F32)

    acc_scr[...] += _swiglu_chunk(h_scr[...], wg_ref[...], wu_ref[...], wd_ref[...])

    @pl.when(c == pl.num_programs(2) - 1)
    def _():
        out_ref[0] = xn_scr[...] + gt2_ref[0] * acc_scr[...]


def _moe_pre_kernel(x_ref, o_ref, wo_ref, gt1_ref, g_ref, sh_ref, sc_ref, wr_ref, xn_ref, h_ref, gate_ref, sel_ref):
    xn = _mixer_residual(x_ref, o_ref, wo_ref, gt1_ref)
    xn_ref[0] = xn
    h = _modulate(xn, g_ref[...], sh_ref[0], sc_ref[0])
    h_ref[0] = h
    lane = lax.broadcasted_iota(jnp.int32, (h.shape[0], LANE), 1)
    lanef = lane.astype(F32)
    logits = jnp.dot(h.astype(BF16), wr_ref[...], preferred_element_type=F32)
    logits = jnp.where(lane < N_EXPERTS, logits, -jnp.inf)
    m1 = jnp.max(logits, axis=1, keepdims=True)
    i1 = jnp.min(jnp.where(logits == m1, lanef, float(LANE)), axis=1, keepdims=True)
    rest = jnp.where(lanef == i1, -jnp.inf, logits)
    m2 = jnp.max(rest, axis=1, keepdims=True)
    i2 = jnp.min(jnp.where(rest == m2, lanef, float(LANE)), axis=1, keepdims=True)
    e2 = jnp.exp(m2 - m1)
    den = 1.0 + e2
    gates = jnp.where(lanef == i1, 1.0 / den, 0.0) + jnp.where(lanef == i2, e2 / den, 0.0)
    sel = jnp.where(lanef == i1, 1.0, 0.0) + jnp.where(lanef == i2, 1.0, 0.0)
    gate_ref[0] = gates[:, :N_EXPERTS]
    sel_ref[0] = sel[:, :N_EXPERTS]


def _moe_expert_kernel(te_tab, tv_tab, src_hbm, dst_hbm, h_hbm, wg_ref, wu_ref, wd_ref, y_hbm,
                       src_smem, dst_smem, hbuf, hb, acc, sem_idx, sem_in, sem_out, *, rows):
    i, c = pl.program_id(0), pl.program_id(1)
    valid = tv_tab[i] != 0
    all_rows_in = pltpu.make_async_copy(h_hbm.at[pl.ds(0, rows), :], hbuf, sem_in)
    all_rows_out = pltpu.make_async_copy(acc, y_hbm.at[pl.ds(0, rows), :], sem_out)

    @pl.when(valid & (c == 0))
    def _():
        idx_copies = [pltpu.make_async_copy(src_hbm.at[i], src_smem, sem_idx.at[0]),
                      pltpu.make_async_copy(dst_hbm.at[i], dst_smem, sem_idx.at[1])]
        for cp in idx_copies:
            cp.start()
        for cp in idx_copies:
            cp.wait()

        def gather_rows(r0, carry):
            for u in range(DMA_BURST):
                r = r0 * DMA_BURST + u
                pltpu.make_async_copy(h_hbm.at[pl.ds(src_smem[r], 1), :], hbuf.at[pl.ds(r, 1), :],
                                      sem_in).start(priority=u % 2)
            return carry
        lax.fori_loop(0, rows // DMA_BURST, gather_rows, 0)
        all_rows_in.wait()
        hb[...] = hbuf[...].astype(BF16)
        acc[...] = jnp.zeros(acc.shape, F32)

    @pl.when(valid)
    def _():
        acc[...] += _swiglu_chunk(hb[...], wg_ref[0], wu_ref[0], wd_ref[0])

    @pl.when(valid & (c == pl.num_programs(1) - 1))
    def _():
        def scatter_rows(r0, carry):
            for u in range(DMA_BURST):
                r = r0 * DMA_BURST + u
                pltpu.make_async_copy(acc.at[pl.ds(r, 1), :], y_hbm.at[pl.ds(dst_smem[r], 1), :],
                                      sem_out).start(priority=u % 2)
            return carry
        lax.fori_loop(0, rows // DMA_BURST, scatter_rows, 0)
        all_rows_out.wait()


def _moe_combine_kernel(xn_ref, gt2_ref, g2_ref, ya_ref, yb_ref, out_ref):
    g2 = g2_ref[0]
    out_ref[0] = xn_ref[0] + gt2_ref[0] * (g2[:, 0:1] * ya_ref[...] + g2[:, 1:2] * yb_ref[...])


def _token_specs(tm, nd):
    pick = (lambda f: lambda b, i: f(b, i)) if nd == 2 else (lambda f: lambda b, i, c: f(b, i))
    tok = pl.BlockSpec((1, tm, D_MODEL), pick(lambda b, i: (b, i, 0)))

    def mod_spec(a):
        return pl.BlockSpec((1, 1, D_MODEL), pick(lambda b, i: (b, 0, 0))) if a.shape[1] == 1 else tok

    cst = lambda shape: pl.BlockSpec(shape, pick(lambda b, i: (0, 0)))
    return tok, mod_spec, cst


def _swiglu_mixer(x, o, w_out, gt1, g, shift, scale, gt2, w_gate, w_up, w_down, *, tm):
    gsz, t, _ = x.shape
    d_ff = w_gate.shape[-1]
    assert t % tm == 0 and d_ff % FF_TILE == 0
    tok, mod_spec, cst = _token_specs(tm, 3)
    return pl.pallas_call(
        _ffn_kernel, grid=(gsz, t // tm, d_ff // FF_TILE),
        in_specs=[tok, tok, cst((ATTN_WIDTH, D_MODEL)), mod_spec(gt1), cst((1, D_MODEL)),
                  mod_spec(shift), mod_spec(scale), mod_spec(gt2),
                  pl.BlockSpec((D_MODEL, FF_TILE), lambda b, i, c: (0, c)),
                  pl.BlockSpec((D_MODEL, FF_TILE), lambda b, i, c: (0, c)),
                  pl.BlockSpec((FF_TILE, D_MODEL), lambda b, i, c: (c, 0))],
        out_specs=tok,
        out_shape=jax.ShapeDtypeStruct((gsz, t, D_MODEL), F32),
        scratch_shapes=[pltpu.VMEM((tm, D_MODEL), F32), pltpu.VMEM((tm, D_MODEL), BF16),
                        pltpu.VMEM((tm, D_MODEL), F32)],
        compiler_params=_params(("arbitrary",) * 3),
        name="swiglu_channel_mixer",
    )(x, o, w_out.astype(BF16), gt1, g[None, :], shift, scale, gt2,
      w_gate.astype(BF16), w_up.astype(BF16), w_down.astype(BF16))


def _moe_pre(x, o, w_out, gt1, g, shift, scale, w_router, *, tm):
    gsz, t, _ = x.shape
    assert t % tm == 0
    tok, mod_spec, cst = _token_specs(tm, 2)
    small = pl.BlockSpec((1, tm, N_EXPERTS), lambda b, i: (b, i, 0))
    wr = jnp.pad(w_router.astype(BF16), ((0, 0), (0, LANE - N_EXPERTS)))
    return pl.pallas_call(
        _moe_pre_kernel, grid=(gsz, t // tm),
        in_specs=[tok, tok, cst((ATTN_WIDTH, D_MODEL)), mod_spec(gt1), cst((1, D_MODEL)),
                  mod_spec(shift), mod_spec(scale), cst((D_MODEL, LANE))],
        out_specs=[tok, tok, small, small],
        out_shape=[jax.ShapeDtypeStruct((gsz, t, D_MODEL), F32), jax.ShapeDtypeStruct((gsz, t, D_MODEL), F32),
                   jax.ShapeDtypeStruct((gsz, t, N_EXPERTS), F32), jax.ShapeDtypeStruct((gsz, t, N_EXPERTS), F32)],
        compiler_params=_params(ARB2),
        name="moe_pre_router",
    )(x, o, w_out.astype(BF16), gt1, g[None, :], shift, scale, wr)


def _moe_routing(gates, sel, rows):
    n = gates.shape[0]
    member = sel > 0
    mi = member.astype(jnp.int32)
    counts = mi.sum(0)
    padded = (counts + rows - 1) // rows * rows
    ends = jnp.cumsum(padded)
    slot = jnp.cumsum(mi, axis=1) - mi
    expert_of = [jnp.sum(jnp.where(member & (slot == k), jnp.arange(N_EXPERTS)[None, :], 0), axis=1)
                 for k in range(TOP_K)]
    g2 = jnp.stack([jnp.sum(jnp.where(member & (slot == k), gates, 0.0), axis=1) for k in range(TOP_K)], axis=1)
    n_rows = (2 * n + N_EXPERTS * rows) // rows * rows
    n_pad = n_rows - 2 * n
    pad_ends = jnp.cumsum(padded - counts)
    pad_expert = (jnp.arange(n_pad, dtype=jnp.int32)[:, None] >= pad_ends[None, :]).sum(1)
    tok = jnp.arange(n, dtype=jnp.int32)
    sort_key = jnp.concatenate([expert_of[0], expert_of[1], pad_expert]).astype(jnp.int32)
    pair = jnp.concatenate([tok, n + tok, jnp.full((n_pad,), -1, jnp.int32)])
    _, pair = lax.sort((sort_key, pair), num_keys=1, is_stable=True)
    spare = 2 * n + jnp.arange(n_rows, dtype=jnp.int32) % rows
    src = jnp.where(pair >= 0, pair % n, 0)
    dst = jnp.where(pair >= 0, pair, spare)
    n_tiles = n_rows // rows
    tile_start = jnp.arange(n_tiles, dtype=jnp.int32) * rows
    tile_expert = jnp.minimum((tile_start[:, None] >= ends[None, :]).sum(1), N_EXPERTS - 1).astype(jnp.int32)
    tile_valid = (tile_start < ends[-1]).astype(jnp.int32)
    return src.reshape(n_tiles, rows), dst.reshape(n_tiles, rows), tile_expert, tile_valid, g2


def _moe_experts(h_all, src, dst, tile_expert, tile_valid, w_gate, w_up, w_down, *, rows):
    n = h_all.shape[0]
    n_tiles = src.shape[0]
    d_ff = w_gate.shape[-1]
    any_spec = pl.BlockSpec(memory_space=pl.ANY)
    return pl.pallas_call(
        functools.partial(_moe_expert_kernel, rows=rows),
        grid_spec=pltpu.PrefetchScalarGridSpec(
            num_scalar_prefetch=2, grid=(n_tiles, d_ff // FF_TILE),
            in_specs=[any_spec, any_spec, any_spec,
                      pl.BlockSpec((1, D_MODEL, FF_TILE), lambda i, c, te, tv: (te[i], 0, c)),
                      pl.BlockSpec((1, D_MODEL, FF_TILE), lambda i, c, te, tv: (te[i], 0, c)),
                      pl.BlockSpec((1, FF_TILE, D_MODEL), lambda i, c, te, tv: (te[i], c, 0))],
            out_specs=any_spec,
            scratch_shapes=[pltpu.SMEM((rows,), jnp.int32), pltpu.SMEM((rows,), jnp.int32),
                            pltpu.VMEM((rows, D_MODEL), F32), pltpu.VMEM((rows, D_MODEL), BF16),
                            pltpu.VMEM((rows, D_MODEL), F32), pltpu.SemaphoreType.DMA((2,)),
                            pltpu.SemaphoreType.DMA(()), pltpu.SemaphoreType.DMA(())]),
        out_shape=jax.ShapeDtypeStruct((2 * n + rows, D_MODEL), F32),
        compiler_params=_params(ARB2),
        name="moe_routed_experts",
    )(tile_expert, tile_valid, src, dst, h_all, w_gate.astype(BF16), w_up.astype(BF16), w_down.astype(BF16))


def _moe_combine(xn, gt2, g2, y, *, tm, n_all, tok0):
    gsz, t, _ = xn.shape
    assert t % tm == 0 and n_all % tm == 0 and tok0 % tm == 0
    tok, mod_spec, _ = _token_specs(tm, 2)
    nt = t // tm
    yspec = lambda slot: pl.BlockSpec((tm, D_MODEL), lambda b, i: ((slot * n_all + tok0) // tm + b * nt + i, 0))
    return pl.pallas_call(
        _moe_combine_kernel, grid=(gsz, nt),
        in_specs=[tok, mod_spec(gt2), pl.BlockSpec((1, tm, TOP_K), lambda b, i: (b, i, 0)), yspec(0), yspec(1)],
        out_specs=tok,
        out_shape=jax.ShapeDtypeStruct((gsz, t, D_MODEL), F32),
        compiler_params=_params(ARB2),
        name="moe_combine",
    )(xn, gt2, g2, y, y)


def _per_token(a, n):
    return jnp.broadcast_to(a, (a.shape[0], n, a.shape[2])).reshape(1, a.shape[0] * n, a.shape[2])


def _with_ones_column(v):
    return jnp.concatenate([v, jnp.ones(v.shape[:-1] + (1,), BF16),
                            jnp.zeros(v.shape[:-1] + (LANE - HEAD_DIM - 1,), BF16)], axis=-1)


def _pad_keys(a, axis, l_pad):
    pad = [(0, 0)] * a.ndim
    pad[axis] = (0, l_pad - a.shape[axis])
    return jnp.pad(a, pad)


def _layer_a(xp, xs, cp, cs, ck, cv, cki, rel_bias, norm_g, w_ada, b_ada, w_in, w_out, g_q, g_k, g_kidx,
             ffn_wg, ffn_wu, ffn_wd):
    bias = _dsa_bias_tiles(rel_bias)
    nb, n = xs.shape[0], xs.shape[1]
    past = ck.shape[1]
    outs = []
    for x, c, sample in ((xp, cp, False), (xs, cs, True)):
        sh1, sc1, gt1, sh2, sc2, gt2 = adaln(c, w_ada, b_ada)
        q16, kf, vf, ktr, vx, kif, qi16, kit, wsc = _project(
            x, norm_g[0], sh1, sc1, w_in, g_q, g_k, g_kidx, n_kv=KV_HEADS_A,
            tm=n if sample else PROJ_TM)
        if sample:
            l_true = past + n
            l_pad = -(-l_true // DSA_SUB) * DSA_SUB
            kfull = jnp.concatenate([ck, kf.reshape(nb, n, KV_HEADS_A, HEAD_DIM)], axis=1).astype(BF16)
            vfull = jnp.concatenate([cv, vf.reshape(nb, n, KV_HEADS_A, HEAD_DIM)], axis=1).astype(BF16)
            kifull = jnp.concatenate([cki, kif], axis=1).astype(BF16)
            ktr = _pad_keys(kfull, 1, l_pad).transpose(0, 2, 3, 1)
            vx = _pad_keys(_with_ones_column(vfull), 1, l_pad).transpose(0, 2, 1, 3)
            kit = _pad_keys(kifull, 1, l_pad).transpose(0, 2, 1)
            o = _dsa_attention(q16, ktr, vx, qi16, kit, wsc, bias, p0=past, l_true=l_true,
                               n_sel=min(TOPK_MAX, l_true // 4), tq=n, tk=l_pad, sw=DSA_SUB, cw=DSA_SUB,
                               tk_idx=l_pad)
            flat = lambda a: a.reshape(1, nb * n, a.shape[-1])
            y = _swiglu_mixer(flat(x), flat(o), w_out, _per_token(gt1, n), norm_g[1], _per_token(sh2, n),
                              _per_token(sc2, n), _per_token(gt2, n), ffn_wg, ffn_wu, ffn_wd, tm=MIX_TM_SAMPLE)
            y = y.reshape(x.shape)
        else:
            t = x.shape[1]
            o = _dsa_attention(q16, ktr, vx, qi16, kit, wsc, bias, p0=0, l_true=t,
                               n_sel=min(TOPK_MAX, t // 4), tq=LANE, tk=DSA_KEY_TILE, sw=DSA_SUB,
                               cw=DSA_COUNT_CHUNK, tk_idx=DSA_KEY_TILE)
            y = _swiglu_mixer(x, o, w_out, gt1, norm_g[1], sh2, sc2, gt2, ffn_wg, ffn_wu, ffn_wd, tm=MIX_TM_PROMPT)
        bsz, t = x.shape[0], x.shape[1]
        outs.append((y, kf.reshape(bsz, t, KV_HEADS_A, HEAD_DIM), vf.reshape(bsz, t, KV_HEADS_A, HEAD_DIM), kif))
    return outs


def _layer_b(xp, xs, cp, cs, ck, cv, rel_bias, norm_g, w_ada, b_ada, w_in, w_out, g_q, g_k, sinks,
             w_router, moe_wg, moe_wu, moe_wd):
    bias = _band_bias(rel_bias)
    nb, n = xs.shape[0], xs.shape[1]
    streams = []
    for x, c, sample in ((xp, cp, False), (xs, cs, True)):
        sh1, sc1, gt1, sh2, sc2, gt2 = adaln(c, w_ada, b_ada)
        q16, kf, vf, ktr, vx = _project(x, norm_g[0], sh1, sc1, w_in, g_q, g_k, n_kv=KV_HEADS_B,
                                         tm=n if sample else PROJ_TM)
        bsz, t = x.shape[0], x.shape[1]
        k4 = kf.reshape(bsz, t, KV_HEADS_B, HEAD_DIM)
        v4 = vf.reshape(bsz, t, KV_HEADS_B, HEAD_DIM)
        zero = lambda i: 0
        if sample:
            kp = ck.astype(BF16).transpose(0, 2, 3, 1)
            vp = _with_ones_column(cv.astype(BF16)).transpose(0, 2, 1, 3)
            o = _band_attention(q16, kp, _pad_keys(ktr, 3, LANE), vp, _pad_keys(vx, 2, LANE), bias, sinks,
                                tq=n, prev_map=zero, cur_map=zero, first_has_no_prev=False)
            flat = lambda a: a.reshape(1, nb * n, a.shape[-1])
            xn, h, gates, sel = _moe_pre(flat(x), flat(o), w_out, _per_token(gt1, n), norm_g[1],
                                         _per_token(sh2, n), _per_token(sc2, n), w_router, tm=MIX_TM_SAMPLE)
            gt2 = _per_token(gt2, n)
            k_new = jnp.concatenate([ck, k4], axis=1)[:, -WINDOW:]
            v_new = jnp.concatenate([cv, v4], axis=1)[:, -WINDOW:]
        else:
            o = _band_attention(q16, ktr, ktr, vx, vx, bias, sinks, tq=LANE,
                                prev_map=lambda i: jnp.maximum(i - 1, 0), cur_map=lambda i: i,
                                first_has_no_prev=True)
            xn, h, gates, sel = _moe_pre(x, o, w_out, gt1, norm_g[1], sh2, sc2, w_router, tm=MIX_TM_PROMPT)
            k_new, v_new = k4[:, -WINDOW:], v4[:, -WINDOW:]
        streams.append((x.shape, xn, h, gates, sel, gt2, k_new, v_new))

    rows2d = lambda a: a.reshape(-1, a.shape[-1])
    h_all = jnp.concatenate([rows2d(st[2]) for st in streams])
    n_all = h_all.shape[0]
    src, dst, tile_expert, tile_valid, g2 = _moe_routing(
        jnp.concatenate([rows2d(st[3]) for st in streams]), jnp.concatenate([rows2d(st[4]) for st in streams]),
        MOE_ROW_TILE)
    y = _moe_experts(h_all, src, dst, tile_expert, tile_valid, moe_wg, moe_wu, moe_wd, rows=MOE_ROW_TILE)
    outs, tok0 = [], 0
    for shape, xn, _, _, _, gt2, k_new, v_new in streams:
        cnt = xn.shape[0] * xn.shape[1]
        out = _moe_combine(xn, gt2, g2[tok0:tok0 + cnt].reshape(xn.shape[0], xn.shape[1], TOP_K), y,
                           tm=MOE_COMBINE_TM, n_all=n_all, tok0=tok0)
        outs.append((out.reshape(shape), k_new, v_new))
        tok0 += cnt
    return outs


def kernel(x_prompt, x_sample, c_prompt, c_sample, cache_a_k, cache_a_v, cache_a_kidx,
           cache_b_k, cache_b_v, rel_bias, norm_g, w_ada, b_ada,
           a_w_in, a_w_out, a_g_q, a_g_k, a_g_kidx,
           b_w_in, b_w_out, b_g_q, b_g_k, b_sinks,
           ffn_w_gate, ffn_w_up, ffn_w_down,
           moe_w_router, moe_w_gate, moe_w_up, moe_w_down):
    xp, xs = x_prompt, x_sample
    a_out = [[] for _ in range(6)]
    b_out = [[] for _ in range(4)]
    for i in range(DEPTH):
        j = i // 2
        if i % 2 == 0:
            (xp, k1, v1, i1), (xs, k2, v2, i2) = _layer_a(
                xp, xs, c_prompt, c_sample, cache_a_k[j], cache_a_v[j], cache_a_kidx[j], rel_bias, norm_g[i],
                w_ada[i], b_ada[i], a_w_in[j], a_w_out[j], a_g_q[j], a_g_k[j], a_g_kidx[j],
                ffn_w_gate[j], ffn_w_up[j], ffn_w_down[j])
            for lst, val in zip(a_out, (k1, v1, i1, k2, v2, i2)):
                lst.append(val)
        else:
            (xp, k1, v1), (xs, k2, v2) = _layer_b(
                xp, xs, c_prompt, c_sample, cache_b_k[j], cache_b_v[j], rel_bias, norm_g[i],
                w_ada[i], b_ada[i], b_w_in[j], b_w_out[j], b_g_q[j], b_g_k[j], b_sinks[j],
                moe_w_router[j], moe_w_gate[j], moe_w_up[j], moe_w_down[j])
            for lst, val in zip(b_out, (k1, v1, k2, v2)):
                lst.append(val)
    return (xp, xs, *(jnp.stack(l) for l in a_out), *(jnp.stack(l) for l in b_out))
```

```python
import math, functools
import jax, jax.numpy as jnp
from jax import lax
import numpy as np
from jax.experimental import pallas as pl
from jax.experimental.pallas import tpu as pltpu

D_MODEL = 1024
DEPTH = 2
CHUNK = 64
N_HEADS = 16
HEAD_DIM = 64
ATTN_WIDTH = N_HEADS * HEAD_DIM
KV_HEADS_A = 4
IDX_HEADS = 8
IDX_DIM = 64
TOPK_MAX = 256
KV_HEADS_B = 2
WINDOW = 128
NUM_BUCKETS = 32
MAX_DISTANCE = 1024
N_EXPERTS = 8
TOP_K = 2
EPS = 1e-6

F32 = jnp.float32
BF16 = jnp.bfloat16
LANE = 128
VMEM_LIMIT_BYTES = 48 * 1024 * 1024
LOG2E = math.log2(math.e)
NEG_BIG = -1e30
KEY_MIN = -2 ** 31
FAR_BUCKET = NUM_BUCKETS // 2 - 1
NEAR_BLOCKS = 6
FF_TILE = 512
MIX_TM_PROMPT = 1024
MIX_TM_SAMPLE = 256
MOE_ROW_TILE = 1024
MOE_COMBINE_TM = 512
DMA_BURST = 8
PROJ_TM = 512
DSA_SUB = 2 * LANE
DSA_KEY_TILE = 2048
DSA_INDEX_TILE = 4096
DSA_COUNT_CHUNK = 1024
CHUNK_SHIFT = CHUNK.bit_length() - 1
LANE_SHIFT = LANE.bit_length() - 1
COUNT_ALL = float(2 ** 24)
ARB2 = ("arbitrary", "arbitrary")


def _params(sem, vmem=True):
    return pltpu.CompilerParams(dimension_semantics=sem, vmem_limit_bytes=VMEM_LIMIT_BYTES if vmem else None)


def rel_bucket(rel):
    half = NUM_BUCKETS // 2
    max_exact = half // 2
    base = jnp.where(rel > 0, half, 0)
    n = jnp.abs(rel)
    nf = jnp.maximum(n, 1).astype(F32)
    large = max_exact + (jnp.log(nf / max_exact) / math.log(MAX_DISTANCE / max_exact)
                         * (half - max_exact)).astype(jnp.int32)
    large = jnp.minimum(large, half - 1)
    return base + jnp.where(n < max_exact, n, large)


def _adaln_kernel(c_ref, w_ref, b_ref, o_ref):
    c = c_ref[...]
    o_ref[...] = jnp.dot(c * jax.nn.sigmoid(c), w_ref[...], preferred_element_type=F32) + b_ref[...]


def adaln(c, w, b):
    n = c.shape[0]
    mod = pl.pallas_call(
        _adaln_kernel,
        grid=(6,),
        in_specs=[pl.BlockSpec((n, D_MODEL), lambda j: (0, 0)),
                  pl.BlockSpec((D_MODEL, D_MODEL), lambda j: (0, j)),
                  pl.BlockSpec((1, D_MODEL), lambda j: (0, j))],
        out_specs=pl.BlockSpec((n, D_MODEL), lambda j: (0, j)),
        out_shape=jax.ShapeDtypeStruct((n, 6 * D_MODEL), F32),
        name="adaln",
    )(c, w, b[None, :])
    return jnp.split(mod[:, None, :], 6, axis=-1)


def _modulate(x, g, shift, scale):
    y = x * lax.rsqrt(jnp.mean(x * x, axis=-1, keepdims=True) + EPS)
    return (y * g) * (1.0 + scale) + shift


def _seg_norm(seg, g):
    return seg * lax.rsqrt(jnp.mean(seg * seg, axis=-1, keepdims=True) + EPS) * g


def _proj_kernel(x_ref, g_ref, sh_ref, sc_ref, w_ref, gq_ref, gk_ref, *rest, n_kv, has_idx):
    h = _modulate(x_ref[0], g_ref[...], sh_ref[0], sc_ref[0]).astype(BF16)
    y = jnp.dot(h, w_ref[...], preferred_element_type=F32)
    tm = y.shape[0]
    kv_w = n_kv * HEAD_DIM
    if has_idx:
        gki_ref, q16, kf, vf, ktr, vx, kif, qi16, kit, wsc = rest
    else:
        q16, kf, vf, ktr, vx = rest
    for hd in range(N_HEADS):
        seg = y[:, hd * HEAD_DIM:(hd + 1) * HEAD_DIM]
        q16[0, hd] = (_seg_norm(seg, gq_ref[...]) * (HEAD_DIM ** -0.5 * LOG2E)).astype(BF16)
    lane = lax.broadcasted_iota(jnp.int32, (tm, LANE - HEAD_DIM), 1)
    ones_col = jnp.where(lane == 0, 1.0, 0.0).astype(BF16)
    for j in range(n_kv):
        kseg = _seg_norm(y[:, ATTN_WIDTH + j * HEAD_DIM:ATTN_WIDTH + (j + 1) * HEAD_DIM], gk_ref[...])
        vseg = y[:, ATTN_WIDTH + kv_w + j * HEAD_DIM:ATTN_WIDTH + kv_w + (j + 1) * HEAD_DIM]
        kf[0, :, j * HEAD_DIM:(j + 1) * HEAD_DIM] = kseg
        vf[0, :, j * HEAD_DIM:(j + 1) * HEAD_DIM] = vseg
        ktr[0, j] = kseg.T.astype(BF16)
        vx[0, j] = jnp.concatenate([vseg.astype(BF16), ones_col], axis=1)
    if has_idx:
        base = ATTN_WIDTH + 2 * kv_w
        for hd in range(IDX_HEADS):
            qi16[0, hd] = y[:, base + hd * IDX_DIM:base + (hd + 1) * IDX_DIM].astype(BF16)
        base += IDX_HEADS * IDX_DIM
        kiseg = _seg_norm(y[:, base:base + IDX_DIM], gki_ref[...])
        kif[0] = kiseg
        kit[0] = kiseg.T.astype(BF16)
        wsc[0] = y[:, base + IDX_DIM:base + IDX_DIM + IDX_HEADS] * IDX_HEADS ** -0.5 * IDX_DIM ** -0.5


def _project(x, g, shift, scale, w_in, g_q, g_k, g_kidx=None, *, n_kv, tm):
    has_idx = g_kidx is not None
    gsz, t, _ = x.shape
    n_in = w_in.shape[1]
    kv_w = n_kv * HEAD_DIM
    tok = lambda b, i: (b, i, 0)
    head = lambda b, i: (b, 0, i, 0)
    vec = lambda b, i: (b, 0, 0)
    cst = lambda b, i: (0, 0)
    shapes = [((gsz, N_HEADS, t, HEAD_DIM), BF16, (1, N_HEADS, tm, HEAD_DIM), head),
              ((gsz, t, kv_w), F32, (1, tm, kv_w), tok),
              ((gsz, t, kv_w), F32, (1, tm, kv_w), tok),
              ((gsz, n_kv, HEAD_DIM, t), BF16, (1, n_kv, HEAD_DIM, tm), lambda b, i: (b, 0, 0, i)),
              ((gsz, n_kv, t, LANE), BF16, (1, n_kv, tm, LANE), head)]
    if has_idx:
        shapes += [((gsz, t, IDX_DIM), F32, (1, tm, IDX_DIM), tok),
                   ((gsz, IDX_HEADS, t, IDX_DIM), BF16, (1, IDX_HEADS, tm, IDX_DIM), head),
                   ((gsz, IDX_DIM, t), BF16, (1, IDX_DIM, tm), lambda b, i: (b, 0, i)),
                   ((gsz, t, IDX_HEADS), F32, (1, tm, IDX_HEADS), tok)]
    return pl.pallas_call(
        functools.partial(_proj_kernel, n_kv=n_kv, has_idx=has_idx),
        grid=(gsz, t // tm),
        in_specs=[pl.BlockSpec((1, tm, D_MODEL), tok), pl.BlockSpec((1, D_MODEL), cst),
                  pl.BlockSpec((1, 1, D_MODEL), vec), pl.BlockSpec((1, 1, D_MODEL), vec),
                  pl.BlockSpec((D_MODEL, n_in), cst), pl.BlockSpec((1, HEAD_DIM), cst),
                  pl.BlockSpec((1, HEAD_DIM), cst)] + ([pl.BlockSpec((1, IDX_DIM), cst)] if has_idx else []),
        out_specs=[pl.BlockSpec(blk, im) for _, _, blk, im in shapes],
        out_shape=[jax.ShapeDtypeStruct(s, d) for s, d, _, _ in shapes],
        compiler_params=_params(ARB2),
        name="mixer_in_proj",
    )(x, g[None, :], shift, scale, w_in.astype(BF16), g_q[None, :], g_k[None, :],
      *([g_kidx[None, :]] if has_idx else []))


def _tri_steps(nqb, tq, tk, p0):
    qbs, kts = [], []
    for qb in range(nqb):
        vis_end = ((p0 + qb * tq + tq - 1) // CHUNK + 1) * CHUNK
        for kt in range(-(-vis_end // tk)):
            qbs.append(qb)
            kts.append(kt)
    return jnp.asarray(np.array(qbs, np.int32)), jnp.asarray(np.array(kts, np.int32))


def _idx_kernel(qb_tab, kt_tab, qi_ref, w_ref, kit_ref, out_ref, *, tq, tk, sw, p0, l_true):
    step = pl.program_id(1)
    q0 = p0 + qb_tab[step] * tq
    k0 = kt_tab[step] * tk
    row = lax.broadcasted_iota(jnp.int32, (tq, 1), 0) + q0
    vis_end = jnp.minimum(((row >> CHUNK_SHIFT) + 1) << CHUNK_SHIFT, l_true)
    wv = w_ref[0]
    for c in range(tk // sw):
        kit = kit_ref[0, :, c * sw:(c + 1) * sw]
        acc = jnp.zeros((tq, sw), F32)
        for h in range(IDX_HEADS):
            sh = jnp.dot(qi_ref[0, h], kit, preferred_element_type=F32)
            acc = acc + wv[:, h:h + 1] * jnp.maximum(sh, 0.0)
        bits = lax.bitcast_convert_type(acc, jnp.int32)
        key = bits ^ ((bits >> 31) & 0x7FFFFFFF)
        kpos = k0 + c * sw + lax.broadcasted_iota(jnp.int32, (tq, sw), 1)
        out_ref[0, :, c * sw:(c + 1) * sw] = jnp.where(kpos < vis_end, key, KEY_MIN)


def _thr_kernel(keys_ref, tau_ref, quota_ref, flag_ref, *, tq, cw, p0, n_sel):
    qb = pl.program_id(1)
    vis_end = (((p0 + qb * tq + tq - 1) >> CHUNK_SHIFT) + 1) << CHUNK_SHIFT
    nch = (vis_end + cw - 1) // cw
    key_max = 2 ** 31 - 1

    def count_ge(cand):
        def body(j, acc):
            x = keys_ref[0, :, pl.ds(pl.multiple_of(j * cw, cw), cw)]
            for u in range(cw // LANE):
                acc = acc + jnp.where(x[:, u * LANE:(u + 1) * LANE] >= cand, 1.0, 0.0)
            return acc
        acc = lax.fori_loop(0, nch, body, jnp.zeros((tq, LANE), F32))
        return jnp.sum(acc, axis=1, keepdims=True)

    zero = jnp.zeros((tq, LANE), jnp.int32)
    c0 = count_ge(zero)
    tau = jnp.where(c0 >= n_sel, zero, KEY_MIN)
    n_ge = jnp.where(c0 >= n_sel, c0, COUNT_ALL)

    def bit_body(carry):
        i, tau, n_ge = carry
        cand = tau + jnp.left_shift(jnp.int32(1), 30 - i)
        c = count_ge(cand)
        return i + 1, jnp.where(c >= n_sel, cand, tau), jnp.where(c >= n_sel, c, n_ge)

    def unresolved(carry):
        i, _, n_ge = carry
        return (i < 31) & (jnp.max(jnp.abs(n_ge - n_sel)) > 0.0)

    _, tau, n_ge = lax.while_loop(unresolved, bit_body, (jnp.int32(0), tau, n_ge))
    n_gt = jnp.where(tau[:, :1] < key_max, count_ge(jnp.minimum(tau, key_max - 1) + 1), 0.0)
    tau_ref[0] = jnp.maximum(tau, KEY_MIN + 1)
    quota_ref[0] = jnp.broadcast_to(n_sel - n_gt, (tq, LANE))
    over = jnp.max(jnp.where(n_ge > n_sel, 1, 0), axis=0, keepdims=True)
    flag_ref[0, 0] = jnp.broadcast_to(over, flag_ref.shape[2:])


def _attn_kernel(qb_tab, kt_tab, flag_tab, q_ref, kt_ref, vx_ref, keys_ref, tau_ref, quota_ref, bias_ref, o_ref,
                 m_ref, acc_ref, negm_ref, seen_ref, *, tq, tk, sw, p0, nqb):
    step = pl.program_id(1)
    qb, kt = qb_tab[step], kt_tab[step]
    q0 = p0 + qb * tq
    k0 = kt * tk
    vis_end = (((q0 + tq - 1) >> CHUNK_SHIFT) + 1) << CHUNK_SHIFT
    group = N_HEADS // KV_HEADS_A
    nblk = sw // LANE
    tie_w = 2 * LANE

    @pl.when(kt == 0)
    def _():
        m_ref[...] = jnp.full(m_ref.shape, NEG_BIG, F32)
        acc_ref[...] = jnp.zeros(acc_ref.shape, F32)
        seen_ref[...] = jnp.zeros(seen_ref.shape, F32)

    has_ties = flag_tab[pl.program_id(0) * nqb + qb] != 0
    tau = tau_ref[0]

    @pl.when(jnp.logical_not(has_ties))
    def _():
        for u in range(tk // LANE):
            cols = slice(u * LANE, (u + 1) * LANE)
            negm_ref[:, cols] = jnp.where(keys_ref[0, :, cols] >= tau, 0.0, NEG_BIG)

    @pl.when(has_ties)
    def _():
        tau2 = jnp.concatenate([tau] * (tie_w // LANE), axis=1)
        quota = jnp.concatenate([quota_ref[0]] * (tie_w // LANE), axis=1)
        before = (lax.broadcasted_iota(jnp.int32, (tie_w, tie_w), 0)
                  < lax.broadcasted_iota(jnp.int32, (tie_w, tie_w), 1))
        before = jnp.where(before, 1.0, 0.0).astype(BF16)
        seen = seen_ref[...]
        for u in range(tk // tie_w):
            cols = slice(u * tie_w, (u + 1) * tie_w)
            k = keys_ref[0, :, cols]
            tie = jnp.where(k == tau2, 1.0, 0.0)
            rank = (jnp.dot(tie.astype(BF16), before, preferred_element_type=F32)
                    + jnp.concatenate([seen] * (tie_w // LANE), axis=1))
            keep = jnp.where(k > tau2, 1.0, jnp.where(rank < quota, tie, 0.0))
            negm_ref[:, cols] = jnp.where(keep > 0.0, 0.0, NEG_BIG)
            seen = seen + jnp.sum(tie, axis=1, keepdims=True)
        seen_ref[...] = seen

    def sub_tile(c, with_bias):
        off = c * sw if isinstance(c, int) else pl.multiple_of(c * sw, sw)
        negm = negm_ref[:, pl.ds(off, sw)]
        if with_bias:
            d0 = ((k0 + off - q0) >> LANE_SHIFT) + NEAR_BLOCKS
            bidx = [jnp.clip(d0 + u, 0, NEAR_BLOCKS) for u in range(nblk)]
        for h in range(N_HEADS):
            j = h // group
            s = jnp.dot(q_ref[0, h], kt_ref[0, j, :, pl.ds(off, sw)], preferred_element_type=F32) + negm
            if with_bias:
                s = s + jnp.concatenate([bias_ref[bidx[u], h] for u in range(nblk)], axis=1)
            m_prev = m_ref[h]
            m_cur = jnp.maximum(m_prev, jnp.max(s, axis=1, keepdims=True))
            alpha = jnp.exp2(m_prev - m_cur)
            p = jnp.exp2(s - jnp.concatenate([m_cur] * nblk, axis=1))
            pv = jnp.dot(p.astype(BF16), vx_ref[0, j, pl.ds(off, sw), :], preferred_element_type=F32)
            acc_ref[h] = alpha * acc_ref[h] + pv
            m_ref[h] = m_cur

    tile_far = k0 + tk <= q0 - (NEAR_BLOCKS - 1) * LANE

    @pl.when(tile_far)
    def _():
        for c in range(tk // sw):
            sub_tile(c, False)

    @pl.when(jnp.logical_not(tile_far))
    def _():
        nsub = jnp.minimum(tk // sw, (vis_end - k0 + sw - 1) // sw)
        nfar = jnp.clip((q0 - (NEAR_BLOCKS - 1) * LANE - k0) // sw, 0, nsub)

        def far_body(c, carry):
            sub_tile(c, False)
            return carry

        def near_body(c, carry):
            sub_tile(c, True)
            return carry

        lax.fori_loop(0, nfar, far_body, 0)
        lax.fori_loop(nfar, nsub, near_body, 0)

    @pl.when(k0 + tk >= vis_end)
    def _():
        for h in range(N_HEADS):
            a = acc_ref[h]
            o_ref[0, :, h * HEAD_DIM:(h + 1) * HEAD_DIM] = a[:, :HEAD_DIM] / a[:, HEAD_DIM:HEAD_DIM + 1]


def _rel_tiles(rel_table, rel):
    onehot = jax.nn.one_hot(rel_bucket(rel), NUM_BUCKETS, dtype=F32)
    return jnp.einsum("nrsk,kh->nhrs", onehot, rel_table.astype(F32), precision=lax.Precision.HIGHEST)


def _dsa_bias_tiles(rel_table):
    d = jnp.arange(NEAR_BLOCKS + 1, dtype=jnp.int32)[:, None, None] - NEAR_BLOCKS
    rel = (d * LANE + jnp.arange(LANE, dtype=jnp.int32)[None, None, :]
           - jnp.arange(LANE, dtype=jnp.int32)[None, :, None])
    tiles = (_rel_tiles(rel_table, rel) - rel_table[FAR_BUCKET][None, :, None, None]) * LOG2E
    return tiles.at[0].set(0.0)


def _dsa_attention(q16, ktr, vx, qi16, kit, w, bias, *, p0, l_true, n_sel, tq, tk, sw, cw, tk_idx):
    bsz, t = q16.shape[0], q16.shape[2]
    l_pad = ktr.shape[3]
    nqb = t // tq
    assert t % tq == 0 and p0 % LANE == 0 and (tq == LANE or nqb == 1)
    assert l_pad % tk == 0 and tk % sw == 0 and tk % cw == 0 and tk % (2 * LANE) == 0
    assert l_pad % tk_idx == 0 and tk_idx % tk == 0
    qbt, ktt = _tri_steps(nqb, tq, tk, p0)
    nsteps = int(qbt.shape[0])
    qbt_idx, ktt_idx = _tri_steps(nqb, tq, tk_idx, p0)

    keys = pl.pallas_call(
        functools.partial(_idx_kernel, tq=tq, tk=tk_idx, sw=sw, p0=p0, l_true=l_true),
        grid_spec=pltpu.PrefetchScalarGridSpec(
            num_scalar_prefetch=2, grid=(bsz, int(qbt_idx.shape[0])),
            in_specs=[pl.BlockSpec((1, IDX_HEADS, tq, IDX_DIM), lambda b, s, qt, kt: (b, 0, qt[s], 0)),
                      pl.BlockSpec((1, tq, IDX_HEADS), lambda b, s, qt, kt: (b, qt[s], 0)),
                      pl.BlockSpec((1, IDX_DIM, tk_idx), lambda b, s, qt, kt: (b, 0, kt[s]))],
            out_specs=pl.BlockSpec((1, tq, tk_idx), lambda b, s, qt, kt: (b, qt[s], kt[s]))),
        out_shape=jax.ShapeDtypeStruct((bsz, t, l_pad), jnp.int32),
        compiler_params=_params(ARB2, vmem=False),
        name="dsa_index_keys",
    )(qbt_idx, ktt_idx, qi16, w, kit)

    tau, quota, flags = pl.pallas_call(
        functools.partial(_thr_kernel, tq=tq, cw=cw, p0=p0, n_sel=n_sel),
        grid=(bsz, nqb),
        in_specs=[pl.BlockSpec((1, tq, l_pad), lambda b, i: (b, i, 0))],
        out_specs=[pl.BlockSpec((1, tq, LANE), lambda b, i: (b, i, 0)),
                   pl.BlockSpec((1, tq, LANE), lambda b, i: (b, i, 0)),
                   pl.BlockSpec((1, 1, 8, LANE), lambda b, i: (b, i, 0, 0))],
        out_shape=[jax.ShapeDtypeStruct((bsz, t, LANE), jnp.int32),
                   jax.ShapeDtypeStruct((bsz, t, LANE), F32),
                   jax.ShapeDtypeStruct((bsz, nqb, 8, LANE), jnp.int32)],
        compiler_params=_params(ARB2),
        name="dsa_threshold",
    )(keys)
    flags = flags[:, :, 0, 0].reshape(bsz * nqb)

    return pl.pallas_call(
        functools.partial(_attn_kernel, tq=tq, tk=tk, sw=sw, p0=p0, nqb=nqb),
        grid_spec=pltpu.PrefetchScalarGridSpec(
            num_scalar_prefetch=3, grid=(bsz, nsteps),
            in_specs=[pl.BlockSpec((1, N_HEADS, tq, HEAD_DIM), lambda b, s, qt, kt, fl: (b, 0, qt[s], 0)),
                      pl.BlockSpec((1, KV_HEADS_A, HEAD_DIM, tk), lambda b, s, qt, kt, fl: (b, 0, 0, kt[s])),
                      pl.BlockSpec((1, KV_HEADS_A, tk, LANE), lambda b, s, qt, kt, fl: (b, 0, kt[s], 0)),
                      pl.BlockSpec((1, tq, tk), lambda b, s, qt, kt, fl: (b, qt[s], kt[s])),
                      pl.BlockSpec((1, tq, LANE), lambda b, s, qt, kt, fl: (b, qt[s], 0)),
                      pl.BlockSpec((1, tq, LANE), lambda b, s, qt, kt, fl: (b, qt[s], 0)),
                      pl.BlockSpec((NEAR_BLOCKS + 1, N_HEADS, tq, LANE), lambda b, s, qt, kt, fl: (0, 0, 0, 0))],
            out_specs=pl.BlockSpec((1, tq, ATTN_WIDTH), lambda b, s, qt, kt, fl: (b, qt[s], 0)),
            scratch_shapes=[pltpu.VMEM((N_HEADS, tq, LANE), F32), pltpu.VMEM((N_HEADS, tq, LANE), F32),
                            pltpu.VMEM((tq, tk), F32), pltpu.VMEM((tq, LANE), F32)]),
        out_shape=jax.ShapeDtypeStruct((bsz, t, ATTN_WIDTH), F32),
        compiler_params=_params(ARB2),
        name="dsa_masked_attention",
    )(qbt, ktt, flags, q16, ktr, vx, keys, tau, quota, bias[:, :, :tq, :])


def _band_kernel(q_ref, kp_ref, kc_ref, vp_ref, vc_ref, bias_ref, sink_ref, o_ref, *, tq, first_has_no_prev):
    group = N_HEADS // KV_HEADS_B
    extra = None
    if first_has_no_prev:
        col = lax.broadcasted_iota(jnp.int32, (tq, 2 * LANE), 1)
        extra = jnp.where((pl.program_id(1) == 0) & (col < LANE), NEG_BIG, 0.0)
    ones = jnp.ones((2 * LANE, LANE), BF16)
    low_half = lax.broadcasted_iota(jnp.int32, (tq, LANE), 1) < HEAD_DIM
    for j in range(KV_HEADS_B):
        heads = slice(j * group, (j + 1) * group)
        kw = jnp.concatenate([kp_ref[0, j], kc_ref[0, j]], axis=1)
        v_low = jnp.concatenate([vp_ref[0, j], vc_ref[0, j]], axis=0)
        v_high = jnp.concatenate([v_low[:, HEAD_DIM:], v_low[:, :HEAD_DIM]], axis=1)
        v_sum = [jnp.concatenate([v, ones], axis=1) for v in (v_low, v_high)]
        s_all = jnp.dot(q_ref[0, heads].reshape(group * tq, HEAD_DIM), kw,
                        preferred_element_type=F32).reshape(group, tq, 2 * LANE)
        for g0 in range(0, group, 2):
            halves = []
            for par in range(2):
                h = j * group + g0 + par
                s = s_all[g0 + par] + bias_ref[h]
                if extra is not None:
                    s = s + extra
                sink = sink_ref[h]
                m = jnp.maximum(jnp.max(s, axis=1, keepdims=True), sink)
                p = jnp.exp2(s - jnp.concatenate([m, m], axis=1)).astype(BF16)
                r = jnp.dot(p, v_sum[par], preferred_element_type=F32)
                halves.append(r[:, :LANE] / (r[:, LANE:] + jnp.exp2(sink - m)))
            h0 = j * group + g0
            o_ref[0, :, h0 * HEAD_DIM:(h0 + 2) * HEAD_DIM] = jnp.where(low_half, halves[0], halves[1])


def _band_bias(rel_table):
    r = jnp.arange(LANE, dtype=jnp.int32)[:, None]
    s = jnp.arange(2 * LANE, dtype=jnp.int32)[None, :]
    wc, qh = s // CHUNK, r // CHUNK
    band = (wc >= qh) & (wc <= qh + 2)
    tiles = _rel_tiles(rel_table, (s - LANE - r)[None])[0]
    return jnp.where(band[None], tiles * LOG2E, NEG_BIG)


def _band_attention(q16, kp, kc, vp, vc, bias, sinks, *, tq, prev_map, cur_map, first_has_no_prev):
    bsz, t = q16.shape[0], q16.shape[2]
    kblk = lambda m: pl.BlockSpec((1, KV_HEADS_B, HEAD_DIM, LANE), lambda b, i: (b, 0, 0, m(i)))
    vblk = lambda m: pl.BlockSpec((1, KV_HEADS_B, LANE, LANE), lambda b, i: (b, 0, m(i), 0))
    sink_rows = jnp.broadcast_to((sinks.astype(F32) * LOG2E)[:, None, None], (N_HEADS, 1, LANE))
    return pl.pallas_call(
        functools.partial(_band_kernel, tq=tq, first_has_no_prev=first_has_no_prev),
        grid=(bsz, t // tq),
        in_specs=[pl.BlockSpec((1, N_HEADS, tq, HEAD_DIM), lambda b, i: (b, 0, i, 0)),
                  kblk(prev_map), kblk(cur_map), vblk(prev_map), vblk(cur_map),
                  pl.BlockSpec((N_HEADS, tq, 2 * LANE), lambda b, i: (0, 0, 0)),
                  pl.BlockSpec((N_HEADS, 1, LANE), lambda b, i: (0, 0, 0))],
        out_specs=pl.BlockSpec((1, tq, ATTN_WIDTH), lambda b, i: (b, i, 0)),
        out_shape=jax.ShapeDtypeStruct((bsz, t, ATTN_WIDTH), F32),
        compiler_params=_params(ARB2, vmem=False),
        name="band_attention",
    )(q16, kp, kc, vp, vc, bias[:, :tq, :], sink_rows)


def _mixer_residual(x_ref, o_ref, wo_ref, gt1_ref):
    return x_ref[0] + gt1_ref[0] * jnp.dot(o_ref[0].astype(BF16), wo_ref[...], preferred_element_type=F32)


def _swiglu_chunk(h, wg, wu, wd):
    a = jax.nn.silu(jnp.dot(h, wg, preferred_element_type=F32)) * jnp.dot(h, wu, preferred_element_type=F32)
    return jnp.dot(a.astype(BF16), wd, preferred_element_type=F32)


def _ffn_kernel(x_ref, o_ref, wo_ref, gt1_ref, g_ref, sh_ref, sc_ref, gt2_ref, wg_ref, wu_ref, wd_ref,
                out_ref, xn_scr, h_scr, acc_scr):
    c = pl.program_id(2)

    @pl.when(c == 0)
    def _():
        xn = _mixer_residual(x_ref, o_ref, wo_ref, gt1_ref)
        xn_scr[...] = xn
        h_scr[...] = _modulate(xn, g_ref[...], sh_ref[0], sc_ref[0]).astype(BF16)
        acc_scr[...] = jnp.zeros(acc_scr.shape, F32)

    acc_scr[...] += _swiglu_chunk(h_scr[...], wg_ref[...], wu_ref[...], wd_ref[...])

    @pl.when(c == pl.num_programs(2) - 1)
    def _():
        out_ref[0] = xn_scr[...] + gt2_ref[0] * acc_scr[...]


def _moe_pre_kernel(x_ref, o_ref, wo_ref, gt1_ref, g_ref, sh_ref, sc_ref, wr_ref, xn_ref, h_ref, gate_ref, sel_ref):
    xn = _mixer_residual(x_ref, o_ref, wo_ref, gt1_ref)
    xn_ref[0] = xn
    h = _modulate(xn, g_ref[...], sh_ref[0], sc_ref[0])
    h_ref[0] = h
    lane = lax.broadcasted_iota(jnp.int32, (h.shape[0], LANE), 1)
    lanef = lane.astype(F32)
    logits = jnp.dot(h.astype(BF16), wr_ref[...], preferred_element_type=F32)
    logits = jnp.where(lane < N_EXPERTS, logits, -jnp.inf)
    m1 = jnp.max(logits, axis=1, keepdims=True)
    i1 = jnp.min(jnp.where(logits == m1, lanef, float(LANE)), axis=1, keepdims=True)
    rest = jnp.where(lanef == i1, -jnp.inf, logits)
    m2 = jnp.max(rest, axis=1, keepdims=True)
    i2 = jnp.min(jnp.where(rest == m2, lanef, float(LANE)), axis=1, keepdims=True)
    e2 = jnp.exp(m2 - m1)
    den = 1.0 + e2
    gates = jnp.where(lanef == i1, 1.0 / den, 0.0) + jnp.where(lanef == i2, e2 / den, 0.0)
    sel = jnp.where(lanef == i1, 1.0, 0.0) + jnp.where(lanef == i2, 1.0, 0.0)
    gate_ref[0] = gates[:, :N_EXPERTS]
    sel_ref[0] = sel[:, :N_EXPERTS]


def _moe_expert_kernel(te_tab, tv_tab, src_hbm, dst_hbm, h_hbm, wg_ref, wu_ref, wd_ref, y_hbm,
                       src_smem, dst_smem, hbuf, hb, acc, sem_idx, sem_in, sem_out, *, rows):
    i, c = pl.program_id(0), pl.program_id(1)
    valid = tv_tab[i] != 0
    all_rows_in = pltpu.make_async_copy(h_hbm.at[pl.ds(0, rows), :], hbuf, sem_in)
    all_rows_out = pltpu.make_async_copy(acc, y_hbm.at[pl.ds(0, rows), :], sem_out)

    @pl.when(valid & (c == 0))
    def _():
        idx_copies = [pltpu.make_async_copy(src_hbm.at[i], src_smem, sem_idx.at[0]),
                      pltpu.make_async_copy(dst_hbm.at[i], dst_smem, sem_idx.at[1])]
        for cp in idx_copies:
            cp.start()
        for cp in idx_copies:
            cp.wait()

        def gather_rows(r0, carry):
            for u in range(DMA_BURST):
                r = r0 * DMA_BURST + u
                pltpu.make_async_copy(h_hbm.at[pl.ds(src_smem[r], 1), :], hbuf.at[pl.ds(r, 1), :],
                                      sem_in).start(priority=u % 2)
            return carry
        lax.fori_loop(0, rows // DMA_BURST, gather_rows, 0)
        all_rows_in.wait()
        hb[...] = hbuf[...].astype(BF16)
        acc[...] = jnp.zeros(acc.shape, F32)

    @pl.when(valid)
    def _():
        acc[...] += _swiglu_chunk(hb[...], wg_ref[0], wu_ref[0], wd_ref[0])

    @pl.when(valid & (c == pl.num_programs(1) - 1))
    def _():
        def scatter_rows(r0, carry):
            for u in range(DMA_BURST):
                r = r0 * DMA_BURST + u
                pltpu.make_async_copy(acc.at[pl.ds(r, 1), :], y_hbm.at[pl.ds(dst_smem[r], 1), :],
                                      sem_out).start(priority=u % 2)
            return carry
        lax.fori_loop(0, rows // DMA_BURST, scatter_rows, 0)
        all_rows_out.wait()


def _moe_combine_kernel(xn_ref, gt2_ref, g2_ref, ya_ref, yb_ref, out_ref):
    g2 = g2_ref[0]
    out_ref[0] = xn_ref[0] + gt2_ref[0] * (g2[:, 0:1] * ya_ref[...] + g2[:, 1:2] * yb_ref[...])


def _token_specs(tm, nd):
    pick = (lambda f: lambda b, i: f(b, i)) if nd == 2 else (lambda f: lambda b, i, c: f(b, i))
    tok = pl.BlockSpec((1, tm, D_MODEL), pick(lambda b, i: (b, i, 0)))

    def mod_spec(a):
        return pl.BlockSpec((1, 1, D_MODEL), pick(lambda b, i: (b, 0, 0))) if a.shape[1] == 1 else tok

    cst = lambda shape: pl.BlockSpec(shape, pick(lambda b, i: (0, 0)))
    return tok, mod_spec, cst


def _swiglu_mixer(x, o, w_out, gt1, g, shift, scale, gt2, w_gate, w_up, w_down, *, tm):
    gsz, t, _ = x.shape
    d_ff = w_gate.shape[-1]
    assert t % tm == 0 and d_ff % FF_TILE == 0
    tok, mod_spec, cst = _token_specs(tm, 3)
    return pl.pallas_call(
        _ffn_kernel, grid=(gsz, t // tm, d_ff // FF_TILE),
        in_specs=[tok, tok, cst((ATTN_WIDTH, D_MODEL)), mod_spec(gt1), cst((1, D_MODEL)),
                  mod_spec(shift), mod_spec(scale), mod_spec(gt2),
                  pl.BlockSpec((D_MODEL, FF_TILE), lambda b, i, c: (0, c)),
                  pl.BlockSpec((D_MODEL, FF_TILE), lambda b, i, c: (0, c)),
                  pl.BlockSpec((FF_TILE, D_MODEL), lambda b, i, c: (c, 0))],
        out_specs=tok,
        out_shape=jax.ShapeDtypeStruct((gsz, t, D_MODEL), F32),
        scratch_shapes=[pltpu.VMEM((tm, D_MODEL), F32), pltpu.VMEM((tm, D_MODEL), BF16),
                        pltpu.VMEM((tm, D_MODEL), F32)],
        compiler_params=_params(("arbitrary",) * 3),
        name="swiglu_channel_mixer",
    )(x, o, w_out.astype(BF16), gt1, g[None, :], shift, scale, gt2,
      w_gate.astype(BF16), w_up.astype(BF16), w_down.astype(BF16))


def _moe_pre(x, o, w_out, gt1, g, shift, scale, w_router, *, tm):
    gsz, t, _ = x.shape
    assert t % tm == 0
    tok, mod_spec, cst = _token_specs(tm, 2)
    small = pl.BlockSpec((1, tm, N_EXPERTS), lambda b, i: (b, i, 0))
    wr = jnp.pad(w_router.astype(BF16), ((0, 0), (0, LANE - N_EXPERTS)))
    return pl.pallas_call(
        _moe_pre_kernel, grid=(gsz, t // tm),
        in_specs=[tok, tok, cst((ATTN_WIDTH, D_MODEL)), mod_spec(gt1), cst((1, D_MODEL)),
                  mod_spec(shift), mod_spec(scale), cst((D_MODEL, LANE))],
        out_specs=[tok, tok, small, small],
        out_shape=[jax.ShapeDtypeStruct((gsz, t, D_MODEL), F32), jax.ShapeDtypeStruct((gsz, t, D_MODEL), F32),
                   jax.ShapeDtypeStruct((gsz, t, N_EXPERTS), F32), jax.ShapeDtypeStruct((gsz, t, N_EXPERTS), F32)],
        compiler_params=_params(ARB2),
        name="moe_pre_router",
    )(x, o, w_out.astype(BF16), gt1, g[None, :], shift, scale, wr)


def _moe_routing(gates, sel, rows):
    n = gates.shape[0]
    member = sel > 0
    mi = member.astype(jnp.int32)
    counts = mi.sum(0)
    padded = (counts + rows - 1) // rows * rows
    ends = jnp.cumsum(padded)
    slot = jnp.cumsum(mi, axis=1) - mi
    expert_of = [jnp.sum(jnp.where(member & (slot == k), jnp.arange(N_EXPERTS)[None, :], 0), axis=1)
                 for k in range(TOP_K)]
    g2 = jnp.stack([jnp.sum(jnp.where(member & (slot == k), gates, 0.0), axis=1) for k in range(TOP_K)], axis=1)
    n_rows = (2 * n + N_EXPERTS * rows) // rows * rows
    n_pad = n_rows - 2 * n
    pad_ends = jnp.cumsum(padded - counts)
    pad_expert = (jnp.arange(n_pad, dtype=jnp.int32)[:, None] >= pad_ends[None, :]).sum(1)
    tok = jnp.arange(n, dtype=jnp.int32)
    sort_key = jnp.concatenate([expert_of[0], expert_of[1], pad_expert]).astype(jnp.int32)
    pair = jnp.concatenate([tok, n + tok, jnp.full((n_pad,), -1, jnp.int32)])
    _, pair = lax.sort((sort_key, pair), num_keys=1, is_stable=True)
    spare = 2 * n + jnp.arange(n_rows, dtype=jnp.int32) % rows
    src = jnp.where(pair >= 0, pair % n, 0)
    dst = jnp.where(pair >= 0, pair, spare)
    n_tiles = n_rows // rows
    tile_start = jnp.arange(n_tiles, dtype=jnp.int32) * rows
    tile_expert = jnp.minimum((tile_start[:, None] >= ends[None, :]).sum(1), N_EXPERTS - 1).astype(jnp.int32)
    tile_valid = (tile_start < ends[-1]).astype(jnp.int32)
    return src.reshape(n_tiles, rows), dst.reshape(n_tiles, rows), tile_expert, tile_valid, g2


def _moe_experts(h_all, src, dst, tile_expert, tile_valid, w_gate, w_up, w_down, *, rows):
    n = h_all.shape[0]
    n_tiles = src.shape[0]
    d_ff = w_gate.shape[-1]
    any_spec = pl.BlockSpec(memory_space=pl.ANY)
    return pl.pallas_call(
        functools.partial(_moe_expert_kernel, rows=rows),
        grid_spec=pltpu.PrefetchScalarGridSpec(
            num_scalar_prefetch=2, grid=(n_tiles, d_ff // FF_TILE),
            in_specs=[any_spec, any_spec, any_spec,
                      pl.BlockSpec((1, D_MODEL, FF_TILE), lambda i, c, te, tv: (te[i], 0, c)),
                      pl.BlockSpec((1, D_MODEL, FF_TILE), lambda i, c, te, tv: (te[i], 0, c)),
                      pl.BlockSpec((1, FF_TILE, D_MODEL), lambda i, c, te, tv: (te[i], c, 0))],
            out_specs=any_spec,
            scratch_shapes=[pltpu.SMEM((rows,), jnp.int32), pltpu.SMEM((rows,), jnp.int32),
                            pltpu.VMEM((rows, D_MODEL), F32), pltpu.VMEM((rows, D_MODEL), BF16),
                            pltpu.VMEM((rows, D_MODEL), F32), pltpu.SemaphoreType.DMA((2,)),
                            pltpu.SemaphoreType.DMA(()), pltpu.SemaphoreType.DMA(())]),
        out_shape=jax.ShapeDtypeStruct((2 * n + rows, D_MODEL), F32),
        compiler_params=_params(ARB2),
        name="moe_routed_experts",
    )(tile_expert, tile_valid, src, dst, h_all, w_gate.astype(BF16), w_up.astype(BF16), w_down.astype(BF16))


def _moe_combine(xn, gt2, g2, y, *, tm, n_all, tok0):
    gsz, t, _ = xn.shape
    assert t % tm == 0 and n_all % tm == 0 and tok0 % tm == 0
    tok, mod_spec, _ = _token_specs(tm, 2)
    nt = t // tm
    yspec = lambda slot: pl.BlockSpec((tm, D_MODEL), lambda b, i: ((slot * n_all + tok0) // tm + b * nt + i, 0))
    return pl.pallas_call(
        _moe_combine_kernel, grid=(gsz, nt),
        in_specs=[tok, mod_spec(gt2), pl.BlockSpec((1, tm, TOP_K), lambda b, i: (b, i, 0)), yspec(0), yspec(1)],
        out_specs=tok,
        out_shape=jax.ShapeDtypeStruct((gsz, t, D_MODEL), F32),
        compiler_params=_params(ARB2),
        name="moe_combine",
    )(xn, gt2, g2, y, y)


def _per_token(a, n):
    return jnp.broadcast_to(a, (a.shape[0], n, a.shape[2])).reshape(1, a.shape[0] * n, a.shape[2])


def _with_ones_column(v):
    return jnp.concatenate([v, jnp.ones(v.shape[:-1] + (1,), BF16),
                            jnp.zeros(v.shape[:-1] + (LANE - HEAD_DIM - 1,), BF16)], axis=-1)


def _pad_keys(a, axis, l_pad):
    pad = [(0, 0)] * a.ndim
    pad[axis] = (0, l_pad - a.shape[axis])
    return jnp.pad(a, pad)


def _layer_a(xp, xs, cp, cs, ck, cv, cki, rel_bias, norm_g, w_ada, b_ada, w_in, w_out, g_q, g_k, g_kidx,
             ffn_wg, ffn_wu, ffn_wd):
    bias = _dsa_bias_tiles(rel_bias)
    nb, n = xs.shape[0], xs.shape[1]
    past = ck.shape[1]
    outs = []
    for x, c, sample in ((xp, cp, False), (xs, cs, True)):
        sh1, sc1, gt1, sh2, sc2, gt2 = adaln(c, w_ada, b_ada)
        q16, kf, vf, ktr, vx, kif, qi16, kit, wsc = _project(
            x, norm_g[0], sh1, sc1, w_in, g_q, g_k, g_kidx, n_kv=KV_HEADS_A,
            tm=n if sample else PROJ_TM)
        if sample:
            l_true = past + n
            l_pad = -(-l_true // DSA_SUB) * DSA_SUB
            kfull = jnp.concatenate([ck, kf.reshape(nb, n, KV_HEADS_A, HEAD_DIM)], axis=1).astype(BF16)
            vfull = jnp.concatenate([cv, vf.reshape(nb, n, KV_HEADS_A, HEAD_DIM)], axis=1).astype(BF16)
            kifull = jnp.concatenate([cki, kif], axis=1).astype(BF16)
            ktr = _pad_keys(kfull, 1, l_pad).transpose(0, 2, 3, 1)
            vx = _pad_keys(_with_ones_column(vfull), 1, l_pad).transpose(0, 2, 1, 3)
            kit = _pad_keys(kifull, 1, l_pad).transpose(0, 2, 1)
            o = _dsa_attention(q16, ktr, vx, qi16, kit, wsc, bias, p0=past, l_true=l_true,
                               n_sel=min(TOPK_MAX, l_true // 4), tq=n, tk=l_pad, sw=DSA_SUB, cw=DSA_SUB,
                               tk_idx=l_pad)
            flat = lambda a: a.reshape(1, nb * n, a.shape[-1])
            y = _swiglu_mixer(flat(x), flat(o), w_out, _per_token(gt1, n), norm_g[1], _per_token(sh2, n),
                              _per_token(sc2, n), _per_token(gt2, n), ffn_wg, ffn_wu, ffn_wd, tm=MIX_TM_SAMPLE)
            y = y.reshape(x.shape)
        else:
            t = x.shape[1]
            o = _dsa_attention(q16, ktr, vx, qi16, kit, wsc, bias, p0=0, l_true=t,
                               n_sel=min(TOPK_MAX, t // 4), tq=LANE, tk=DSA_KEY_TILE, sw=DSA_SUB,
                               cw=DSA_COUNT_CHUNK, tk_idx=DSA_INDEX_TILE)
            y = _swiglu_mixer(x, o, w_out, gt1, norm_g[1], sh2, sc2, gt2, ffn_wg, ffn_wu, ffn_wd, tm=MIX_TM_PROMPT)
        bsz, t = x.shape[0], x.shape[1]
        outs.append((y, kf.reshape(bsz, t, KV_HEADS_A, HEAD_DIM), vf.reshape(bsz, t, KV_HEADS_A, HEAD_DIM), kif))
    return outs


def _layer_b(xp, xs, cp, cs, ck, cv, rel_bias, norm_g, w_ada, b_ada, w_in, w_out, g_q, g_k, sinks,
             w_router, moe_wg, moe_wu, moe_wd):
    bias = _band_bias(rel_bias)
    nb, n = xs.shape[0], xs.shape[1]
    streams = []
    for x, c, sample in ((xp, cp, False), (xs, cs, True)):
        sh1, sc1, gt1, sh2, sc2, gt2 = adaln(c, w_ada, b_ada)
        q16, kf, vf, ktr, vx = _project(x, norm_g[0], sh1, sc1, w_in, g_q, g_k, n_kv=KV_HEADS_B,
                                         tm=n if sample else PROJ_TM)
        bsz, t = x.shape[0], x.shape[1]
        k4 = kf.reshape(bsz, t, KV_HEADS_B, HEAD_DIM)
        v4 = vf.reshape(bsz, t, KV_HEADS_B, HEAD_DIM)
        zero = lambda i: 0
        if sample:
            kp = ck.astype(BF16).transpose(0, 2, 3, 1)
            vp = _with_ones_column(cv.astype(BF16)).transpose(0, 2, 1, 3)
            o = _band_attention(q16, kp, _pad_keys(ktr, 3, LANE), vp, _pad_keys(vx, 2, LANE), bias, sinks,
                                tq=n, prev_map=zero, cur_map=zero, first_has_no_prev=False)
            flat = lambda a: a.reshape(1, nb * n, a.shape[-1])
            xn, h, gates, sel = _moe_pre(flat(x), flat(o), w_out, _per_token(gt1, n), norm_g[1],
                                         _per_token(sh2, n), _per_token(sc2, n), w_router, tm=MIX_TM_SAMPLE)
            gt2 = _per_token(gt2, n)
            k_new = jnp.concatenate([ck, k4], axis=1)[:, -WINDOW:]
            v_new = jnp.concatenate([cv, v4], axis=1)[:, -WINDOW:]
        else:
            o = _band_attention(q16, ktr, ktr, vx, vx, bias, sinks, tq=LANE,
                                prev_map=lambda i: jnp.maximum(i - 1, 0), cur_map=lambda i: i,
                                first_has_no_prev=True)
            xn, h, gates, sel = _moe_pre(x, o, w_out, gt1, norm_g[1], sh2, sc2, w_router, tm=MIX_TM_PROMPT)
            k_new, v_new = k4[:, -WINDOW:], v4[:, -WINDOW:]
        streams.append((x.shape, xn, h, gates, sel, gt2, k_new, v_new))

    rows2d = lambda a: a.reshape(-1, a.shape[-1])
    h_all = jnp.concatenate([rows2d(st[2]) for st in streams])
    n_all = h_all.shape[0]
    src, dst, tile_expert, tile_valid, g2 = _moe_routing(
        jnp.concatenate([rows2d(st[3]) for st in streams]), jnp.concatenate([rows2d(st[4]) for st in streams]),
        MOE_ROW_TILE)
    y = _moe_experts(h_all, src, dst, tile_expert, tile_valid, moe_wg, moe_wu, moe_wd, rows=MOE_ROW_TILE)
    outs, tok0 = [], 0
    for shape, xn, _, _, _, gt2, k_new, v_new in streams:
        cnt = xn.shape[0] * xn.shape[1]
        out = _moe_combine(xn, gt2, g2[tok0:tok0 + cnt].reshape(xn.shape[0], xn.shape[1], TOP_K), y,
                           tm=MOE_COMBINE_TM, n_all=n_all, tok0=tok0)
        outs.append((out.reshape(shape), k_new, v_new))
        tok0 += cnt
    return outs


def kernel(x_prompt, x_sample, c_prompt, c_sample, cache_a_k, cache_a_v, cache_a_kidx,
           cache_b_k, cache_b_v, rel_bias, norm_g, w_ada, b_ada,
           a_w_in, a_w_out, a_g_q, a_g_k, a_g_kidx,
           b_w_in, b_w_out, b_g_q, b_g_k, b_sinks,
           ffn_w_gate, ffn_w_up, ffn_w_down,
           moe_w_router, moe_w_gate, moe_w_up, moe_w_down):
    xp, xs = x_prompt, x_sample
    a_out = [[] for _ in range(6)]
    b_out = [[] for _ in range(4)]
    for i in range(DEPTH):
        j = i // 2
        if i % 2 == 0:
            (xp, k1, v1, i1), (xs, k2, v2, i2) = _layer_a(
                xp, xs, c_prompt, c_sample, cache_a_k[j], cache_a_v[j], cache_a_kidx[j], rel_bias, norm_g[i],
                w_ada[i], b_ada[i], a_w_in[j], a_w_out[j], a_g_q[j], a_g_k[j], a_g_kidx[j],
                ffn_w_gate[j], ffn_w_up[j], ffn_w_down[j])
            for lst, val in zip(a_out, (k1, v1, i1, k2, v2, i2)):
                lst.append(val)
        else:
            (xp, k1, v1), (xs, k2, v2) = _layer_b(
                xp, xs, c_prompt, c_sample, cache_b_k[j], cache_b_v[j], rel_bias, norm_g[i],
                w_ada[i], b_ada[i], b_w_in[j], b_w_out[j], b_g_q[j], b_g_k[j], b_sinks[j],
                moe_w_router[j], moe_w_gate[j], moe_w_up[j], moe_w_down[j])
            for lst, val in zip(b_out, (k1, v1, k2, v2)):
                lst.append(val)
    return (xp, xs, *(jnp.stack(l) for l in a_out), *(jnp.stack(l) for l in b_out))
```

```python
import math, functools
import jax, jax.numpy as jnp
from jax import lax
import numpy as np
from jax.experimental import pallas as pl
from jax.experimental.pallas import tpu as pltpu

D_MODEL = 1024
DEPTH = 2
CHUNK = 64
N_HEADS = 16
HEAD_DIM = 64
ATTN_WIDTH = N_HEADS * HEAD_DIM
KV_HEADS_A = 4
IDX_HEADS = 8
IDX_DIM = 64
TOPK_MAX = 256
KV_HEADS_B = 2
WINDOW = 128
NUM_BUCKETS = 32
MAX_DISTANCE = 1024
N_EXPERTS = 8
TOP_K = 2
EPS = 1e-6

F32 = jnp.float32
BF16 = jnp.bfloat16
LANE = 128
VMEM_LIMIT_BYTES = 48 * 1024 * 1024
LOG2E = math.log2(math.e)
NEG_BIG = -1e30
KEY_MIN = -2 ** 31
FAR_BUCKET = NUM_BUCKETS // 2 - 1
NEAR_BLOCKS = 6
FF_TILE = 512
MIX_TM_PROMPT = 1024
MIX_TM_SAMPLE = 256
MOE_ROW_TILE = 1024
MOE_COMBINE_TM = 512
DMA_BURST = 8
PROJ_TM = 512
DSA_SUB = 2 * LANE
DSA_KEY_TILE = 2048
DSA_INDEX_TILE = 4096
DSA_COUNT_CHUNK = 1024
CHUNK_SHIFT = CHUNK.bit_length() - 1
LANE_SHIFT = LANE.bit_length() - 1
COUNT_ALL = float(2 ** 24)
ARB2 = ("arbitrary", "arbitrary")


def _params(sem, vmem=True):
    return pltpu.CompilerParams(dimension_semantics=sem, vmem_limit_bytes=VMEM_LIMIT_BYTES if vmem else None)


def rel_bucket(rel):
    half = NUM_BUCKETS // 2
    max_exact = half // 2
    base = jnp.where(rel > 0, half, 0)
    n = jnp.abs(rel)
    nf = jnp.maximum(n, 1).astype(F32)
    large = max_exact + (jnp.log(nf / max_exact) / math.log(MAX_DISTANCE / max_exact)
                         * (half - max_exact)).astype(jnp.int32)
    large = jnp.minimum(large, half - 1)
    return base + jnp.where(n < max_exact, n, large)


def _adaln_kernel(c_ref, w_ref, b_ref, o_ref):
    c = c_ref[...]
    o_ref[...] = jnp.dot(c * jax.nn.sigmoid(c), w_ref[...], preferred_element_type=F32) + b_ref[...]


def adaln(c, w, b):
    n = c.shape[0]
    mod = pl.pallas_call(
        _adaln_kernel,
        grid=(6,),
        in_specs=[pl.BlockSpec((n, D_MODEL), lambda j: (0, 0)),
                  pl.BlockSpec((D_MODEL, D_MODEL), lambda j: (0, j)),
                  pl.BlockSpec((1, D_MODEL), lambda j: (0, j))],
        out_specs=pl.BlockSpec((n, D_MODEL), lambda j: (0, j)),
        out_shape=jax.ShapeDtypeStruct((n, 6 * D_MODEL), F32),
        name="adaln",
    )(c, w, b[None, :])
    return jnp.split(mod[:, None, :], 6, axis=-1)


def _modulate(x, g, shift, scale):
    y = x * lax.rsqrt(jnp.mean(x * x, axis=-1, keepdims=True) + EPS)
    return (y * g) * (1.0 + scale) + shift


def _seg_norm(seg, g):
    return seg * lax.rsqrt(jnp.mean(seg * seg, axis=-1, keepdims=True) + EPS) * g


def _proj_kernel(x_ref, g_ref, sh_ref, sc_ref, w_ref, gq_ref, gk_ref, *rest, n_kv, has_idx):
    h = _modulate(x_ref[0], g_ref[...], sh_ref[0], sc_ref[0]).astype(BF16)
    y = jnp.dot(h, w_ref[...], preferred_element_type=F32)
    tm = y.shape[0]
    kv_w = n_kv * HEAD_DIM
    if has_idx:
        gki_ref, q16, kf, vf, ktr, vx, kif, qi16, kit, wsc = rest
    else:
        q16, kf, vf, ktr, vx = rest
    for hd in range(N_HEADS):
        seg = y[:, hd * HEAD_DIM:(hd + 1) * HEAD_DIM]
        q16[0, hd] = (_seg_norm(seg, gq_ref[...]) * (HEAD_DIM ** -0.5 * LOG2E)).astype(BF16)
    lane = lax.broadcasted_iota(jnp.int32, (tm, LANE - HEAD_DIM), 1)
    ones_col = jnp.where(lane == 0, 1.0, 0.0).astype(BF16)
    for j in range(n_kv):
        kseg = _seg_norm(y[:, ATTN_WIDTH + j * HEAD_DIM:ATTN_WIDTH + (j + 1) * HEAD_DIM], gk_ref[...])
        vseg = y[:, ATTN_WIDTH + kv_w + j * HEAD_DIM:ATTN_WIDTH + kv_w + (j + 1) * HEAD_DIM]
        kf[0, :, j * HEAD_DIM:(j + 1) * HEAD_DIM] = kseg
        vf[0, :, j * HEAD_DIM:(j + 1) * HEAD_DIM] = vseg
        ktr[0, j] = kseg.T.astype(BF16)
        vx[0, j] = jnp.concatenate([vseg.astype(BF16), ones_col], axis=1)
    if has_idx:
        base = ATTN_WIDTH + 2 * kv_w
        for hd in range(IDX_HEADS):
            qi16[0, hd] = y[:, base + hd * IDX_DIM:base + (hd + 1) * IDX_DIM].astype(BF16)
        base += IDX_HEADS * IDX_DIM
        kiseg = _seg_norm(y[:, base:base + IDX_DIM], gki_ref[...])
        kif[0] = kiseg
        kit[0] = kiseg.T.astype(BF16)
        wsc[0] = y[:, base + IDX_DIM:base + IDX_DIM + IDX_HEADS] * IDX_HEADS ** -0.5 * IDX_DIM ** -0.5


def _project(x, g, shift, scale, w_in, g_q, g_k, g_kidx=None, *, n_kv, tm):
    has_idx = g_kidx is not None
    gsz, t, _ = x.shape
    n_in = w_in.shape[1]
    kv_w = n_kv * HEAD_DIM
    tok = lambda b, i: (b, i, 0)
    head = lambda b, i: (b, 0, i, 0)
    vec = lambda b, i: (b, 0, 0)
    cst = lambda b, i: (0, 0)
    shapes = [((gsz, N_HEADS, t, HEAD_DIM), BF16, (1, N_HEADS, tm, HEAD_DIM), head),
              ((gsz, t, kv_w), F32, (1, tm, kv_w), tok),
              ((gsz, t, kv_w), F32, (1, tm, kv_w), tok),
              ((gsz, n_kv, HEAD_DIM, t), BF16, (1, n_kv, HEAD_DIM, tm), lambda b, i: (b, 0, 0, i)),
              ((gsz, n_kv, t, LANE), BF16, (1, n_kv, tm, LANE), head)]
    if has_idx:
        shapes += [((gsz, t, IDX_DIM), F32, (1, tm, IDX_DIM), tok),
                   ((gsz, IDX_HEADS, t, IDX_DIM), BF16, (1, IDX_HEADS, tm, IDX_DIM), head),
                   ((gsz, IDX_DIM, t), BF16, (1, IDX_DIM, tm), lambda b, i: (b, 0, i)),
                   ((gsz, t, IDX_HEADS), F32, (1, tm, IDX_HEADS), tok)]
    return pl.pallas_call(
        functools.partial(_proj_kernel, n_kv=n_kv, has_idx=has_idx),
        grid=(gsz, t // tm),
        in_specs=[pl.BlockSpec((1, tm, D_MODEL), tok), pl.BlockSpec((1, D_MODEL), cst),
                  pl.BlockSpec((1, 1, D_MODEL), vec), pl.BlockSpec((1, 1, D_MODEL), vec),
                  pl.BlockSpec((D_MODEL, n_in), cst), pl.BlockSpec((1, HEAD_DIM), cst),
                  pl.BlockSpec((1, HEAD_DIM), cst)] + ([pl.BlockSpec((1, IDX_DIM), cst)] if has_idx else []),
        out_specs=[pl.BlockSpec(blk, im) for _, _, blk, im in shapes],
        out_shape=[jax.ShapeDtypeStruct(s, d) for s, d, _, _ in shapes],
        compiler_params=_params(ARB2),
        name="mixer_in_proj",
    )(x, g[None, :], shift, scale, w_in.astype(BF16), g_q[None, :], g_k[None, :],
      *([g_kidx[None, :]] if has_idx else []))


def _tri_steps(nqb, tq, tk, p0):
    qbs, kts = [], []
    for qb in range(nqb):
        vis_end = ((p0 + qb * tq + tq - 1) // CHUNK + 1) * CHUNK
        for kt in range(-(-vis_end // tk)):
            qbs.append(qb)
            kts.append(kt)
    return jnp.asarray(np.array(qbs, np.int32)), jnp.asarray(np.array(kts, np.int32))


def _idx_kernel(qb_tab, kt_tab, qi_ref, w_ref, kit_ref, out_ref, *, tq, tk, sw, p0, l_true):
    step = pl.program_id(1)
    q0 = p0 + qb_tab[step] * tq
    k0 = kt_tab[step] * tk
    row = lax.broadcasted_iota(jnp.int32, (tq, 1), 0) + q0
    vis_end = jnp.minimum(((row >> CHUNK_SHIFT) + 1) << CHUNK_SHIFT, l_true)
    wv = w_ref[0]
    for c in range(tk // sw):
        kit = kit_ref[0, :, c * sw:(c + 1) * sw]
        acc = jnp.zeros((tq, sw), F32)
        for h in range(IDX_HEADS):
            sh = jnp.dot(qi_ref[0, h], kit, preferred_element_type=F32)
            acc = acc + wv[:, h:h + 1] * jnp.maximum(sh, 0.0)
        bits = lax.bitcast_convert_type(acc, jnp.int32)
        key = bits ^ ((bits >> 31) & 0x7FFFFFFF)
        kpos = k0 + c * sw + lax.broadcasted_iota(jnp.int32, (tq, sw), 1)
        out_ref[0, :, c * sw:(c + 1) * sw] = jnp.where(kpos < vis_end, key, KEY_MIN)


def _thr_kernel(keys_ref, tau_ref, quota_ref, flag_ref, *, tq, cw, p0, n_sel):
    qb = pl.program_id(1)
    vis_end = (((p0 + qb * tq + tq - 1) >> CHUNK_SHIFT) + 1) << CHUNK_SHIFT
    nch = (vis_end + cw - 1) // cw
    key_max = 2 ** 31 - 1

    def count_ge(cand):
        def body(j, acc):
            x = keys_ref[0, :, pl.ds(pl.multiple_of(j * cw, cw), cw)]
            for u in range(cw // LANE):
                acc = acc + jnp.where(x[:, u * LANE:(u + 1) * LANE] >= cand, 1.0, 0.0)
            return acc
        acc = lax.fori_loop(0, nch, body, jnp.zeros((tq, LANE), F32))
        return jnp.sum(acc, axis=1, keepdims=True)

    zero = jnp.zeros((tq, LANE), jnp.int32)
    c0 = count_ge(zero)
    tau = jnp.where(c0 >= n_sel, zero, KEY_MIN)
    n_ge = jnp.where(c0 >= n_sel, c0, COUNT_ALL)

    def bit_body(carry):
        i, tau, n_ge = carry
        cand = tau + jnp.left_shift(jnp.int32(1), 30 - i)
        c = count_ge(cand)
        return i + 1, jnp.where(c >= n_sel, cand, tau), jnp.where(c >= n_sel, c, n_ge)

    def unresolved(carry):
        i, _, n_ge = carry
        return (i < 31) & (jnp.max(jnp.abs(n_ge - n_sel)) > 0.0)

    _, tau, n_ge = lax.while_loop(unresolved, bit_body, (jnp.int32(0), tau, n_ge))
    n_gt = jnp.where(tau[:, :1] < key_max, count_ge(jnp.minimum(tau, key_max - 1) + 1), 0.0)
    tau_ref[0] = jnp.maximum(tau, KEY_MIN + 1)
    quota_ref[0] = jnp.broadcast_to(n_sel - n_gt, (tq, LANE))
    over = jnp.max(jnp.where(n_ge > n_sel, 1, 0), axis=0, keepdims=True)
    flag_ref[0, 0] = jnp.broadcast_to(over, flag_ref.shape[2:])


def _attn_kernel(qb_tab, kt_tab, flag_tab, q_ref, kt_ref, vx_ref, keys_ref, tau_ref, quota_ref, bias_ref, o_ref,
                 m_ref, acc_ref, negm_ref, seen_ref, *, tq, tk, sw, p0, nqb):
    step = pl.program_id(1)
    qb, kt = qb_tab[step], kt_tab[step]
    q0 = p0 + qb * tq
    k0 = kt * tk
    vis_end = (((q0 + tq - 1) >> CHUNK_SHIFT) + 1) << CHUNK_SHIFT
    group = N_HEADS // KV_HEADS_A
    nblk = sw // LANE
    tie_w = 2 * LANE

    @pl.when(kt == 0)
    def _():
        m_ref[...] = jnp.full(m_ref.shape, NEG_BIG, F32)
        acc_ref[...] = jnp.zeros(acc_ref.shape, F32)
        seen_ref[...] = jnp.zeros(seen_ref.shape, F32)

    has_ties = flag_tab[pl.program_id(0) * nqb + qb] != 0
    tau = tau_ref[0]

    @pl.when(jnp.logical_not(has_ties))
    def _():
        for u in range(tk // LANE):
            cols = slice(u * LANE, (u + 1) * LANE)
            negm_ref[:, cols] = jnp.where(keys_ref[0, :, cols] >= tau, 0.0, NEG_BIG)

    @pl.when(has_ties)
    def _():
        tau2 = jnp.concatenate([tau] * (tie_w // LANE), axis=1)
        quota = jnp.concatenate([quota_ref[0]] * (tie_w // LANE), axis=1)
        before = (lax.broadcasted_iota(jnp.int32, (tie_w, tie_w), 0)
                  < lax.broadcasted_iota(jnp.int32, (tie_w, tie_w), 1))
        before = jnp.where(before, 1.0, 0.0).astype(BF16)
        seen = seen_ref[...]
        for u in range(tk // tie_w):
            cols = slice(u * tie_w, (u + 1) * tie_w)
            k = keys_ref[0, :, cols]
            tie = jnp.where(k == tau2, 1.0, 0.0)
            rank = (jnp.dot(tie.astype(BF16), before, preferred_element_type=F32)
                    + jnp.concatenate([seen] * (tie_w // LANE), axis=1))
            keep = jnp.where(k > tau2, 1.0, jnp.where(rank < quota, tie, 0.0))
            negm_ref[:, cols] = jnp.where(keep > 0.0, 0.0, NEG_BIG)
            seen = seen + jnp.sum(tie, axis=1, keepdims=True)
        seen_ref[...] = seen

    def sub_tile(c, with_bias):
        off = c * sw if isinstance(c, int) else pl.multiple_of(c * sw, sw)
        negm = negm_ref[:, pl.ds(off, sw)]
        if with_bias:
            d0 = ((k0 + off - q0) >> LANE_SHIFT) + NEAR_BLOCKS
            bidx = [jnp.clip(d0 + u, 0, NEAR_BLOCKS) for u in range(nblk)]
        for h in range(N_HEADS):
            j = h // group
            s = jnp.dot(q_ref[0, h], kt_ref[0, j, :, pl.ds(off, sw)], preferred_element_type=F32) + negm
            if with_bias:
                s = s + jnp.concatenate([bias_ref[bidx[u], h] for u in range(nblk)], axis=1)
            m_prev = m_ref[h]
            m_cur = jnp.maximum(m_prev, jnp.max(s, axis=1, keepdims=True))
            alpha = jnp.exp2(m_prev - m_cur)
            p = jnp.exp2(s - jnp.concatenate([m_cur] * nblk, axis=1))
            pv = jnp.dot(p.astype(BF16), vx_ref[0, j, pl.ds(off, sw), :], preferred_element_type=F32)
            acc_ref[h] = alpha * acc_ref[h] + pv
            m_ref[h] = m_cur

    tile_far = k0 + tk <= q0 - (NEAR_BLOCKS - 1) * LANE

    @pl.when(tile_far)
    def _():
        for c in range(tk // sw):
            sub_tile(c, False)

    @pl.when(jnp.logical_not(tile_far))
    def _():
        nsub = jnp.minimum(tk // sw, (vis_end - k0 + sw - 1) // sw)
        nfar = jnp.clip((q0 - (NEAR_BLOCKS - 1) * LANE - k0) // sw, 0, nsub)

        def run(lo, hi, with_bias):
            n = hi - lo

            def pair(c, carry):
                sub_tile(lo + 2 * c, with_bias)
                sub_tile(lo + 2 * c + 1, with_bias)
                return carry
            lax.fori_loop(0, n >> 1, pair, 0)

            @pl.when((n & 1) == 1)
            def _():
                sub_tile(hi - 1, with_bias)

        run(0, nfar, False)
        run(nfar, nsub, True)

    @pl.when(k0 + tk >= vis_end)
    def _():
        for h in range(N_HEADS):
            a = acc_ref[h]
            o_ref[0, :, h * HEAD_DIM:(h + 1) * HEAD_DIM] = a[:, :HEAD_DIM] / a[:, HEAD_DIM:HEAD_DIM + 1]


def _rel_tiles(rel_table, rel):
    onehot = jax.nn.one_hot(rel_bucket(rel), NUM_BUCKETS, dtype=F32)
    return jnp.einsum("nrsk,kh->nhrs", onehot, rel_table.astype(F32), precision=lax.Precision.HIGHEST)


def _dsa_bias_tiles(rel_table):
    d = jnp.arange(NEAR_BLOCKS + 1, dtype=jnp.int32)[:, None, None] - NEAR_BLOCKS
    rel = (d * LANE + jnp.arange(LANE, dtype=jnp.int32)[None, None, :]
           - jnp.arange(LANE, dtype=jnp.int32)[None, :, None])
    tiles = (_rel_tiles(rel_table, rel) - rel_table[FAR_BUCKET][None, :, None, None]) * LOG2E
    return tiles.at[0].set(0.0)


def _dsa_attention(q16, ktr, vx, qi16, kit, w, bias, *, p0, l_true, n_sel, tq, tk, sw, cw, tk_idx):
    bsz, t = q16.shape[0], q16.shape[2]
    l_pad = ktr.shape[3]
    nqb = t // tq
    assert t % tq == 0 and p0 % LANE == 0 and (tq == LANE or nqb == 1)
    assert l_pad % tk == 0 and tk % sw == 0 and tk % cw == 0 and tk % (2 * LANE) == 0
    assert l_pad % tk_idx == 0 and tk_idx % tk == 0
    qbt, ktt = _tri_steps(nqb, tq, tk, p0)
    nsteps = int(qbt.shape[0])
    qbt_idx, ktt_idx = _tri_steps(nqb, tq, tk_idx, p0)

    keys = pl.pallas_call(
        functools.partial(_idx_kernel, tq=tq, tk=tk_idx, sw=sw, p0=p0, l_true=l_true),
        grid_spec=pltpu.PrefetchScalarGridSpec(
            num_scalar_prefetch=2, grid=(bsz, int(qbt_idx.shape[0])),
            in_specs=[pl.BlockSpec((1, IDX_HEADS, tq, IDX_DIM), lambda b, s, qt, kt: (b, 0, qt[s], 0)),
                      pl.BlockSpec((1, tq, IDX_HEADS), lambda b, s, qt, kt: (b, qt[s], 0)),
                      pl.BlockSpec((1, IDX_DIM, tk_idx), lambda b, s, qt, kt: (b, 0, kt[s]))],
            out_specs=pl.BlockSpec((1, tq, tk_idx), lambda b, s, qt, kt: (b, qt[s], kt[s]))),
        out_shape=jax.ShapeDtypeStruct((bsz, t, l_pad), jnp.int32),
        compiler_params=_params(ARB2, vmem=False),
        name="dsa_index_keys",
    )(qbt_idx, ktt_idx, qi16, w, kit)

    tau, quota, flags = pl.pallas_call(
        functools.partial(_thr_kernel, tq=tq, cw=cw, p0=p0, n_sel=n_sel),
        grid=(bsz, nqb),
        in_specs=[pl.BlockSpec((1, tq, l_pad), lambda b, i: (b, i, 0))],
        out_specs=[pl.BlockSpec((1, tq, LANE), lambda b, i: (b, i, 0)),
                   pl.BlockSpec((1, tq, LANE), lambda b, i: (b, i, 0)),
                   pl.BlockSpec((1, 1, 8, LANE), lambda b, i: (b, i, 0, 0))],
        out_shape=[jax.ShapeDtypeStruct((bsz, t, LANE), jnp.int32),
                   jax.ShapeDtypeStruct((bsz, t, LANE), F32),
                   jax.ShapeDtypeStruct((bsz, nqb, 8, LANE), jnp.int32)],
        compiler_params=_params(ARB2),
        name="dsa_threshold",
    )(keys)
    flags = flags[:, :, 0, 0].reshape(bsz * nqb)

    return pl.pallas_call(
        functools.partial(_attn_kernel, tq=tq, tk=tk, sw=sw, p0=p0, nqb=nqb),
        grid_spec=pltpu.PrefetchScalarGridSpec(
            num_scalar_prefetch=3, grid=(bsz, nsteps),
            in_specs=[pl.BlockSpec((1, N_HEADS, tq, HEAD_DIM), lambda b, s, qt, kt, fl: (b, 0, qt[s], 0)),
                      pl.BlockSpec((1, KV_HEADS_A, HEAD_DIM, tk), lambda b, s, qt, kt, fl: (b, 0, 0, kt[s])),
                      pl.BlockSpec((1, KV_HEADS_A, tk, LANE), lambda b, s, qt, kt, fl: (b, 0, kt[s], 0)),
                      pl.BlockSpec((1, tq, tk), lambda b, s, qt, kt, fl: (b, qt[s], kt[s])),
                      pl.BlockSpec((1, tq, LANE), lambda b, s, qt, kt, fl: (b, qt[s], 0)),
                      pl.BlockSpec((1, tq, LANE), lambda b, s, qt, kt, fl: (b, qt[s], 0)),
                      pl.BlockSpec((NEAR_BLOCKS + 1, N_HEADS, tq, LANE), lambda b, s, qt, kt, fl: (0, 0, 0, 0))],
            out_specs=pl.BlockSpec((1, tq, ATTN_WIDTH), lambda b, s, qt, kt, fl: (b, qt[s], 0)),
            scratch_shapes=[pltpu.VMEM((N_HEADS, tq, LANE), F32), pltpu.VMEM((N_HEADS, tq, LANE), F32),
                            pltpu.VMEM((tq, tk), F32), pltpu.VMEM((tq, LANE), F32)]),
        out_shape=jax.ShapeDtypeStruct((bsz, t, ATTN_WIDTH), F32),
        compiler_params=_params(ARB2),
        name="dsa_masked_attention",
    )(qbt, ktt, flags, q16, ktr, vx, keys, tau, quota, bias[:, :, :tq, :])


def _band_kernel(q_ref, kp_ref, kc_ref, vp_ref, vc_ref, bias_ref, sink_ref, o_ref, *, tq, first_has_no_prev):
    group = N_HEADS // KV_HEADS_B
    extra = None
    if first_has_no_prev:
        col = lax.broadcasted_iota(jnp.int32, (tq, 2 * LANE), 1)
        extra = jnp.where((pl.program_id(1) == 0) & (col < LANE), NEG_BIG, 0.0)
    ones = jnp.ones((2 * LANE, LANE), BF16)
    low_half = lax.broadcasted_iota(jnp.int32, (tq, LANE), 1) < HEAD_DIM
    for j in range(KV_HEADS_B):
        heads = slice(j * group, (j + 1) * group)
        kw = jnp.concatenate([kp_ref[0, j], kc_ref[0, j]], axis=1)
        v_low = jnp.concatenate([vp_ref[0, j], vc_ref[0, j]], axis=0)
        v_high = jnp.concatenate([v_low[:, HEAD_DIM:], v_low[:, :HEAD_DIM]], axis=1)
        v_sum = [jnp.concatenate([v, ones], axis=1) for v in (v_low, v_high)]
        s_all = jnp.dot(q_ref[0, heads].reshape(group * tq, HEAD_DIM), kw,
                        preferred_element_type=F32).reshape(group, tq, 2 * LANE)
        for g0 in range(0, group, 2):
            halves = []
            for par in range(2):
                h = j * group + g0 + par
                s = s_all[g0 + par] + bias_ref[h]
                if extra is not None:
                    s = s + extra
                sink = sink_ref[h]
                m = jnp.maximum(jnp.max(s, axis=1, keepdims=True), sink)
                p = jnp.exp2(s - jnp.concatenate([m, m], axis=1)).astype(BF16)
                r = jnp.dot(p, v_sum[par], preferred_element_type=F32)
                halves.append(r[:, :LANE] / (r[:, LANE:] + jnp.exp2(sink - m)))
            h0 = j * group + g0
            o_ref[0, :, h0 * HEAD_DIM:(h0 + 2) * HEAD_DIM] = jnp.where(low_half, halves[0], halves[1])


def _band_bias(rel_table):
    r = jnp.arange(LANE, dtype=jnp.int32)[:, None]
    s = jnp.arange(2 * LANE, dtype=jnp.int32)[None, :]
    wc, qh = s // CHUNK, r // CHUNK
    band = (wc >= qh) & (wc <= qh + 2)
    tiles = _rel_tiles(rel_table, (s - LANE - r)[None])[0]
    return jnp.where(band[None], tiles * LOG2E, NEG_BIG)


def _band_attention(q16, kp, kc, vp, vc, bias, sinks, *, tq, prev_map, cur_map, first_has_no_prev):
    bsz, t = q16.shape[0], q16.shape[2]
    kblk = lambda m: pl.BlockSpec((1, KV_HEADS_B, HEAD_DIM, LANE), lambda b, i: (b, 0, 0, m(i)))
    vblk = lambda m: pl.BlockSpec((1, KV_HEADS_B, LANE, LANE), lambda b, i: (b, 0, m(i), 0))
    sink_rows = jnp.broadcast_to((sinks.astype(F32) * LOG2E)[:, None, None], (N_HEADS, 1, LANE))
    return pl.pallas_call(
        functools.partial(_band_kernel, tq=tq, first_has_no_prev=first_has_no_prev),
        grid=(bsz, t // tq),
        in_specs=[pl.BlockSpec((1, N_HEADS, tq, HEAD_DIM), lambda b, i: (b, 0, i, 0)),
                  kblk(prev_map), kblk(cur_map), vblk(prev_map), vblk(cur_map),
                  pl.BlockSpec((N_HEADS, tq, 2 * LANE), lambda b, i: (0, 0, 0)),
                  pl.BlockSpec((N_HEADS, 1, LANE), lambda b, i: (0, 0, 0))],
        out_specs=pl.BlockSpec((1, tq, ATTN_WIDTH), lambda b, i: (b, i, 0)),
        out_shape=jax.ShapeDtypeStruct((bsz, t, ATTN_WIDTH), F32),
        compiler_params=_params(ARB2, vmem=False),
        name="band_attention",
    )(q16, kp, kc, vp, vc, bias[:, :tq, :], sink_rows)


def _mixer_residual(x_ref, o_ref, wo_ref, gt1_ref):
    return x_ref[0] + gt1_ref[0] * jnp.dot(o_ref[0].astype(BF16), wo_ref[...], preferred_element_type=F32)


def _swiglu_chunk(h, wg, wu, wd):
    a = jax.nn.silu(jnp.dot(h, wg, preferred_element_type=F32)) * jnp.dot(h, wu, preferred_element_type=F32)
    return jnp.dot(a.astype(BF16), wd, preferred_element_type=F32)


def _ffn_kernel(x_ref, o_ref, wo_ref, gt1_ref, g_ref, sh_ref, sc_ref, gt2_ref, wg_ref, wu_ref, wd_ref,
                out_ref, xn_scr, h_scr, acc_scr):
    c = pl.program_id(2)

    @pl.when(c == 0)
    def _():
        xn = _mixer_residual(x_ref, o_ref, wo_ref, gt1_ref)
        xn_scr[...] = xn
        h_scr[...] = _modulate(xn, g_ref[...], sh_ref[0], sc_ref[0]).astype(BF16)
        acc_scr[...] = jnp.zeros(acc_scr.shape, F32)

    acc_scr[...] += _swiglu_chunk(h_scr[...], wg_ref[...], wu_ref[...], wd_ref[...])

    @pl.when(c == pl.num_programs(2) - 1)
    def _():
        out_ref[0] = xn_scr[...] + gt2_ref[0] * acc_scr[...]


def _moe_pre_kernel(x_ref, o_ref, wo_ref, gt1_ref, g_ref, sh_ref, sc_ref, wr_ref, xn_ref, h_ref, gate_ref, sel_ref):
    xn = _mixer_residual(x_ref, o_ref, wo_ref, gt1_ref)
    xn_ref[0] = xn
    h = _modulate(xn, g_ref[...], sh_ref[0], sc_ref[0])
    h_ref[0] = h
    lane = lax.broadcasted_iota(jnp.int32, (h.shape[0], LANE), 1)
    lanef = lane.astype(F32)
    logits = jnp.dot(h.astype(BF16), wr_ref[...], preferred_element_type=F32)
    logits = jnp.where(lane < N_EXPERTS, logits, -jnp.inf)
    m1 = jnp.max(logits, axis=1, keepdims=True)
    i1 = jnp.min(jnp.where(logits == m1, lanef, float(LANE)), axis=1, keepdims=True)
    rest = jnp.where(lanef == i1, -jnp.inf, logits)
    m2 = jnp.max(rest, axis=1, keepdims=True)
    i2 = jnp.min(jnp.where(rest == m2, lanef, float(LANE)), axis=1, keepdims=True)
    e2 = jnp.exp(m2 - m1)
    den = 1.0 + e2
    gates = jnp.where(lanef == i1, 1.0 / den, 0.0) + jnp.where(lanef == i2, e2 / den, 0.0)
    sel = jnp.where(lanef == i1, 1.0, 0.0) + jnp.where(lanef == i2, 1.0, 0.0)
    gate_ref[0] = gates[:, :N_EXPERTS]
    sel_ref[0] = sel[:, :N_EXPERTS]


def _moe_expert_kernel(te_tab, tv_tab, src_hbm, dst_hbm, h_hbm, wg_ref, wu_ref, wd_ref, y_hbm,
                       src_smem, dst_smem, hbuf, hb, acc, sem_idx, sem_in, sem_out, *, rows):
    i, c = pl.program_id(0), pl.program_id(1)
    valid = tv_tab[i] != 0
    all_rows_in = pltpu.make_async_copy(h_hbm.at[pl.ds(0, rows), :], hbuf, sem_in)
    all_rows_out = pltpu.make_async_copy(acc, y_hbm.at[pl.ds(0, rows), :], sem_out)

    @pl.when(valid & (c == 0))
    def _():
        idx_copies = [pltpu.make_async_copy(src_hbm.at[i], src_smem, sem_idx.at[0]),
                      pltpu.make_async_copy(dst_hbm.at[i], dst_smem, sem_idx.at[1])]
        for cp in idx_copies:
            cp.start()
        for cp in idx_copies:
            cp.wait()

        def gather_rows(r0, carry):
            for u in range(DMA_BURST):
                r = r0 * DMA_BURST + u
                pltpu.make_async_copy(h_hbm.at[pl.ds(src_smem[r], 1), :], hbuf.at[pl.ds(r, 1), :],
                                      sem_in).start(priority=u % 2)
            return carry
        lax.fori_loop(0, rows // DMA_BURST, gather_rows, 0)
        all_rows_in.wait()
        hb[...] = hbuf[...].astype(BF16)
        acc[...] = jnp.zeros(acc.shape, F32)

    @pl.when(valid)
    def _():
        acc[...] += _swiglu_chunk(hb[...], wg_ref[0], wu_ref[0], wd_ref[0])

    @pl.when(valid & (c == pl.num_programs(1) - 1))
    def _():
        def scatter_rows(r0, carry):
            for u in range(DMA_BURST):
                r = r0 * DMA_BURST + u
                pltpu.make_async_copy(acc.at[pl.ds(r, 1), :], y_hbm.at[pl.ds(dst_smem[r], 1), :],
                                      sem_out).start(priority=u % 2)
            return carry
        lax.fori_loop(0, rows // DMA_BURST, scatter_rows, 0)
        all_rows_out.wait()


def _moe_combine_kernel(xn_ref, gt2_ref, g2_ref, ya_ref, yb_ref, out_ref):
    g2 = g2_ref[0]
    out_ref[0] = xn_ref[0] + gt2_ref[0] * (g2[:, 0:1] * ya_ref[...] + g2[:, 1:2] * yb_ref[...])


def _token_specs(tm, nd):
    pick = (lambda f: lambda b, i: f(b, i)) if nd == 2 else (lambda f: lambda b, i, c: f(b, i))
    tok = pl.BlockSpec((1, tm, D_MODEL), pick(lambda b, i: (b, i, 0)))

    def mod_spec(a):
        return pl.BlockSpec((1, 1, D_MODEL), pick(lambda b, i: (b, 0, 0))) if a.shape[1] == 1 else tok

    cst = lambda shape: pl.BlockSpec(shape, pick(lambda b, i: (0, 0)))
    return tok, mod_spec, cst


def _swiglu_mixer(x, o, w_out, gt1, g, shift, scale, gt2, w_gate, w_up, w_down, *, tm):
    gsz, t, _ = x.shape
    d_ff = w_gate.shape[-1]
    assert t % tm == 0 and d_ff % FF_TILE == 0
    tok, mod_spec, cst = _token_specs(tm, 3)
    return pl.pallas_call(
        _ffn_kernel, grid=(gsz, t // tm, d_ff // FF_TILE),
        in_specs=[tok, tok, cst((ATTN_WIDTH, D_MODEL)), mod_spec(gt1), cst((1, D_MODEL)),
                  mod_spec(shift), mod_spec(scale), mod_spec(gt2),
                  pl.BlockSpec((D_MODEL, FF_TILE), lambda b, i, c: (0, c)),
                  pl.BlockSpec((D_MODEL, FF_TILE), lambda b, i, c: (0, c)),
                  pl.BlockSpec((FF_TILE, D_MODEL), lambda b, i, c: (c, 0))],
        out_specs=tok,
        out_shape=jax.ShapeDtypeStruct((gsz, t, D_MODEL), F32),
        scratch_shapes=[pltpu.VMEM((tm, D_MODEL), F32), pltpu.VMEM((tm, D_MODEL), BF16),
                        pltpu.VMEM((tm, D_MODEL), F32)],
        compiler_params=_params(("arbitrary",) * 3),
        name="swiglu_channel_mixer",
    )(x, o, w_out.astype(BF16), gt1, g[None, :], shift, scale, gt2,
      w_gate.astype(BF16), w_up.astype(BF16), w_down.astype(BF16))


def _moe_pre(x, o, w_out, gt1, g, shift, scale, w_router, *, tm):
    gsz, t, _ = x.shape
    assert t % tm == 0
    tok, mod_spec, cst = _token_specs(tm, 2)
    small = pl.BlockSpec((1, tm, N_EXPERTS), lambda b, i: (b, i, 0))
    wr = jnp.pad(w_router.astype(BF16), ((0, 0), (0, LANE - N_EXPERTS)))
    return pl.pallas_call(
        _moe_pre_kernel, grid=(gsz, t // tm),
        in_specs=[tok, tok, cst((ATTN_WIDTH, D_MODEL)), mod_spec(gt1), cst((1, D_MODEL)),
                  mod_spec(shift), mod_spec(scale), cst((D_MODEL, LANE))],
        out_specs=[tok, tok, small, small],
        out_shape=[jax.ShapeDtypeStruct((gsz, t, D_MODEL), F32), jax.ShapeDtypeStruct((gsz, t, D_MODEL), F32),
                   jax.ShapeDtypeStruct((gsz, t, N_EXPERTS), F32), jax.ShapeDtypeStruct((gsz, t, N_EXPERTS), F32)],
        compiler_params=_params(ARB2),
        name="moe_pre_router",
    )(x, o, w_out.astype(BF16), gt1, g[None, :], shift, scale, wr)


def _moe_routing(gates, sel, rows):
    n = gates.shape[0]
    member = sel > 0
    mi = member.astype(jnp.int32)
    counts = mi.sum(0)
    padded = (counts + rows - 1) // rows * rows
    ends = jnp.cumsum(padded)
    slot = jnp.cumsum(mi, axis=1) - mi
    expert_of = [jnp.sum(jnp.where(member & (slot == k), jnp.arange(N_EXPERTS)[None, :], 0), axis=1)
                 for k in range(TOP_K)]
    g2 = jnp.stack([jnp.sum(jnp.where(member & (slot == k), gates, 0.0), axis=1) for k in range(TOP_K)], axis=1)
    n_rows = (2 * n + N_EXPERTS * rows) // rows * rows
    n_pad = n_rows - 2 * n
    pad_ends = jnp.cumsum(padded - counts)
    pad_expert = (jnp.arange(n_pad, dtype=jnp.int32)[:, None] >= pad_ends[None, :]).sum(1)
    tok = jnp.arange(n, dtype=jnp.int32)
    sort_key = jnp.concatenate([expert_of[0], expert_of[1], pad_expert]).astype(jnp.int32)
    pair = jnp.concatenate([tok, n + tok, jnp.full((n_pad,), -1, jnp.int32)])
    _, pair = lax.sort((sort_key, pair), num_keys=1, is_stable=True)
    spare = 2 * n + jnp.arange(n_rows, dtype=jnp.int32) % rows
    src = jnp.where(pair >= 0, pair % n, 0)
    dst = jnp.where(pair >= 0, pair, spare)
    n_tiles = n_rows // rows
    tile_start = jnp.arange(n_tiles, dtype=jnp.int32) * rows
    tile_expert = jnp.minimum((tile_start[:, None] >= ends[None, :]).sum(1), N_EXPERTS - 1).astype(jnp.int32)
    tile_valid = (tile_start < ends[-1]).astype(jnp.int32)
    return src.reshape(n_tiles, rows), dst.reshape(n_tiles, rows), tile_expert, tile_valid, g2


def _moe_experts(h_all, src, dst, tile_expert, tile_valid, w_gate, w_up, w_down, *, rows):
    n = h_all.shape[0]
    n_tiles = src.shape[0]
    d_ff = w_gate.shape[-1]
    any_spec = pl.BlockSpec(memory_space=pl.ANY)
    return pl.pallas_call(
        functools.partial(_moe_expert_kernel, rows=rows),
        grid_spec=pltpu.PrefetchScalarGridSpec(
            num_scalar_prefetch=2, grid=(n_tiles, d_ff // FF_TILE),
            in_specs=[any_spec, any_spec, any_spec,
                      pl.BlockSpec((1, D_MODEL, FF_TILE), lambda i, c, te, tv: (te[i], 0, c)),
                      pl.BlockSpec((1, D_MODEL, FF_TILE), lambda i, c, te, tv: (te[i], 0, c)),
                      pl.BlockSpec((1, FF_TILE, D_MODEL), lambda i, c, te, tv: (te[i], c, 0))],
            out_specs=any_spec,
            scratch_shapes=[pltpu.SMEM((rows,), jnp.int32), pltpu.SMEM((rows,), jnp.int32),
                            pltpu.VMEM((rows, D_MODEL), F32), pltpu.VMEM((rows, D_MODEL), BF16),
                            pltpu.VMEM((rows, D_MODEL), F32), pltpu.SemaphoreType.DMA((2,)),
                            pltpu.SemaphoreType.DMA(()), pltpu.SemaphoreType.DMA(())]),
        out_shape=jax.ShapeDtypeStruct((2 * n + rows, D_MODEL), F32),
        compiler_params=_params(ARB2),
        name="moe_routed_experts",
    )(tile_expert, tile_valid, src, dst, h_all, w_gate.astype(BF16), w_up.astype(BF16), w_down.astype(BF16))


def _moe_combine(xn, gt2, g2, y, *, tm, n_all, tok0):
    gsz, t, _ = xn.shape
    assert t % tm == 0 and n_all % tm == 0 and tok0 % tm == 0
    tok, mod_spec, _ = _token_specs(tm, 2)
    nt = t // tm
    yspec = lambda slot: pl.BlockSpec((tm, D_MODEL), lambda b, i: ((slot * n_all + tok0) // tm + b * nt + i, 0))
    return pl.pallas_call(
        _moe_combine_kernel, grid=(gsz, nt),
        in_specs=[tok, mod_spec(gt2), pl.BlockSpec((1, tm, TOP_K), lambda b, i: (b, i, 0)), yspec(0), yspec(1)],
        out_specs=tok,
        out_shape=jax.ShapeDtypeStruct((gsz, t, D_MODEL), F32),
        compiler_params=_params(ARB2),
        name="moe_combine",
    )(xn, gt2, g2, y, y)


def _per_token(a, n):
    return jnp.broadcast_to(a, (a.shape[0], n, a.shape[2])).reshape(1, a.shape[0] * n, a.shape[2])


def _with_ones_column(v):
    return jnp.concatenate([v, jnp.ones(v.shape[:-1] + (1,), BF16),
                            jnp.zeros(v.shape[:-1] + (LANE - HEAD_DIM - 1,), BF16)], axis=-1)


def _pad_keys(a, axis, l_pad):
    pad = [(0, 0)] * a.ndim
    pad[axis] = (0, l_pad - a.shape[axis])
    return jnp.pad(a, pad)


def _layer_a(xp, xs, cp, cs, ck, cv, cki, rel_bias, norm_g, w_ada, b_ada, w_in, w_out, g_q, g_k, g_kidx,
             ffn_wg, ffn_wu, ffn_wd):
    bias = _dsa_bias_tiles(rel_bias)
    nb, n = xs.shape[0], xs.shape[1]
    past = ck.shape[1]
    outs = []
    for x, c, sample in ((xp, cp, False), (xs, cs, True)):
        sh1, sc1, gt1, sh2, sc2, gt2 = adaln(c, w_ada, b_ada)
        q16, kf, vf, ktr, vx, kif, qi16, kit, wsc = _project(
            x, norm_g[0], sh1, sc1, w_in, g_q, g_k, g_kidx, n_kv=KV_HEADS_A,
            tm=n if sample else PROJ_TM)
        if sample:
            l_true = past + n
            l_pad = -(-l_true // DSA_SUB) * DSA_SUB
            kfull = jnp.concatenate([ck, kf.reshape(nb, n, KV_HEADS_A, HEAD_DIM)], axis=1).astype(BF16)
            vfull = jnp.concatenate([cv, vf.reshape(nb, n, KV_HEADS_A, HEAD_DIM)], axis=1).astype(BF16)
            kifull = jnp.concatenate([cki, kif], axis=1).astype(BF16)
            ktr = _pad_keys(kfull, 1, l_pad).transpose(0, 2, 3, 1)
            vx = _pad_keys(_with_ones_column(vfull), 1, l_pad).transpose(0, 2, 1, 3)
            kit = _pad_keys(kifull, 1, l_pad).transpose(0, 2, 1)
            o = _dsa_attention(q16, ktr, vx, qi16, kit, wsc, bias, p0=past, l_true=l_true,
                               n_sel=min(TOPK_MAX, l_true // 4), tq=n, tk=l_pad, sw=DSA_SUB, cw=DSA_SUB,
                               tk_idx=l_pad)
            flat = lambda a: a.reshape(1, nb * n, a.shape[-1])
            y = _swiglu_mixer(flat(x), flat(o), w_out, _per_token(gt1, n), norm_g[1], _per_token(sh2, n),
                              _per_token(sc2, n), _per_token(gt2, n), ffn_wg, ffn_wu, ffn_wd, tm=MIX_TM_SAMPLE)
            y = y.reshape(x.shape)
        else:
            t = x.shape[1]
            o = _dsa_attention(q16, ktr, vx, qi16, kit, wsc, bias, p0=0, l_true=t,
                               n_sel=min(TOPK_MAX, t // 4), tq=LANE, tk=DSA_KEY_TILE, sw=DSA_SUB,
                               cw=DSA_COUNT_CHUNK, tk_idx=DSA_INDEX_TILE)
            y = _swiglu_mixer(x, o, w_out, gt1, norm_g[1], sh2, sc2, gt2, ffn_wg, ffn_wu, ffn_wd, tm=MIX_TM_PROMPT)
        bsz, t = x.shape[0], x.shape[1]
        outs.append((y, kf.reshape(bsz, t, KV_HEADS_A, HEAD_DIM), vf.reshape(bsz, t, KV_HEADS_A, HEAD_DIM), kif))
    return outs


def _layer_b(xp, xs, cp, cs, ck, cv, rel_bias, norm_g, w_ada, b_ada, w_in, w_out, g_q, g_k, sinks,
             w_router, moe_wg, moe_wu, moe_wd):
    bias = _band_bias(rel_bias)
    nb, n = xs.shape[0], xs.shape[1]
    streams = []
    for x, c, sample in ((xp, cp, False), (xs, cs, True)):
        sh1, sc1, gt1, sh2, sc2, gt2 = adaln(c, w_ada, b_ada)
        q16, kf, vf, ktr, vx = _project(x, norm_g[0], sh1, sc1, w_in, g_q, g_k, n_kv=KV_HEADS_B,
                                         tm=n if sample else PROJ_TM)
        bsz, t = x.shape[0], x.shape[1]
        k4 = kf.reshape(bsz, t, KV_HEADS_B, HEAD_DIM)
        v4 = vf.reshape(bsz, t, KV_HEADS_B, HEAD_DIM)
        zero = lambda i: 0
        if sample:
            kp = ck.astype(BF16).transpose(0, 2, 3, 1)
            vp = _with_ones_column(cv.astype(BF16)).transpose(0, 2, 1, 3)
            o = _band_attention(q16, kp, _pad_keys(ktr, 3, LANE), vp, _pad_keys(vx, 2, LANE), bias, sinks,
                                tq=n, prev_map=zero, cur_map=zero, first_has_no_prev=False)
            flat = lambda a: a.reshape(1, nb * n, a.shape[-1])
            xn, h, gates, sel = _moe_pre(flat(x), flat(o), w_out, _per_token(gt1, n), norm_g[1],
                                         _per_token(sh2, n), _per_token(sc2, n), w_router, tm=MIX_TM_SAMPLE)
            gt2 = _per_token(gt2, n)
            k_new = jnp.concatenate([ck, k4], axis=1)[:, -WINDOW:]
            v_new = jnp.concatenate([cv, v4], axis=1)[:, -WINDOW:]
        else:
            o = _band_attention(q16, ktr, ktr, vx, vx, bias, sinks, tq=LANE,
                                prev_map=lambda i: jnp.maximum(i - 1, 0), cur_map=lambda i: i,
                                first_has_no_prev=True)
            xn, h, gates, sel = _moe_pre(x, o, w_out, gt1, norm_g[1], sh2, sc2, w_router, tm=MIX_TM_PROMPT)
            k_new, v_new = k4[:, -WINDOW:], v4[:, -WINDOW:]
        streams.append((x.shape, xn, h, gates, sel, gt2, k_new, v_new))

    rows2d = lambda a: a.reshape(-1, a.shape[-1])
    h_all = jnp.concatenate([rows2d(st[2]) for st in streams])
    n_all = h_all.shape[0]
    src, dst, tile_expert, tile_valid, g2 = _moe_routing(
        jnp.concatenate([rows2d(st[3]) for st in streams]), jnp.concatenate([rows2d(st[4]) for st in streams]),
        MOE_ROW_TILE)
    y = _moe_experts(h_all, src, dst, tile_expert, tile_valid, moe_wg, moe_wu, moe_wd, rows=MOE_ROW_TILE)
    outs, tok0 = [], 0
    for shape, xn, _, _, _, gt2, k_new, v_new in streams:
        cnt = xn.shape[0] * xn.shape[1]
        out = _moe_combine(xn, gt2, g2[tok0:tok0 + cnt].reshape(xn.shape[0], xn.shape[1], TOP_K), y,
                           tm=MOE_COMBINE_TM, n_all=n_all, tok0=tok0)
        outs.append((out.reshape(shape), k_new, v_new))
        tok0 += cnt
    return outs


def kernel(x_prompt, x_sample, c_prompt, c_sample, cache_a_k, cache_a_v, cache_a_kidx,
           cache_b_k, cache_b_v, rel_bias, norm_g, w_ada, b_ada,
           a_w_in, a_w_out, a_g_q, a_g_k, a_g_kidx,
           b_w_in, b_w_out, b_g_q, b_g_k, b_sinks,
           ffn_w_gate, ffn_w_up, ffn_w_down,
           moe_w_router, moe_w_gate, moe_w_up, moe_w_down):
    xp, xs = x_prompt, x_sample
    a_out = [[] for _ in range(6)]
    b_out = [[] for _ in range(4)]
    for i in range(DEPTH):
        j = i // 2
        if i % 2 == 0:
            (xp, k1, v1, i1), (xs, k2, v2, i2) = _layer_a(
                xp, xs, c_prompt, c_sample, cache_a_k[j], cache_a_v[j], cache_a_kidx[j], rel_bias, norm_g[i],
                w_ada[i], b_ada[i], a_w_in[j], a_w_out[j], a_g_q[j], a_g_k[j], a_g_kidx[j],
                ffn_w_gate[j], ffn_w_up[j], ffn_w_down[j])
            for lst, val in zip(a_out, (k1, v1, i1, k2, v2, i2)):
                lst.append(val)
        else:
            (xp, k1, v1), (xs, k2, v2) = _layer_b(
                xp, xs, c_prompt, c_sample, cache_b_k[j], cache_b_v[j], rel_bias, norm_g[i],
                w_ada[i], b_ada[i], b_w_in[j], b_w_out[j], b_g_q[j], b_g_k[j], b_sinks[j],
                moe_w_router[j], moe_w_gate[j], moe_w_up[j], moe_w_down[j])
            for lst, val in zip(b_out, (k1, v1, k2, v2)):
                lst.append(val)
    return (xp, xs, *(jnp.stack(l) for l in a_out), *(jnp.stack(l) for l in b_out))
```

```python
import math, functools
import jax, jax.numpy as jnp
from jax import lax
import numpy as np
from jax.experimental import pallas as pl
from jax.experimental.pallas import tpu as pltpu

D_MODEL = 1024
DEPTH = 2
CHUNK = 64
N_HEADS = 16
HEAD_DIM = 64
ATTN_WIDTH = N_HEADS * HEAD_DIM
KV_HEADS_A = 4
IDX_HEADS = 8
IDX_DIM = 64
TOPK_MAX = 256
KV_HEADS_B = 2
WINDOW = 128
NUM_BUCKETS = 32
MAX_DISTANCE = 1024
N_EXPERTS = 8
TOP_K = 2
EPS = 1e-6

F32 = jnp.float32
BF16 = jnp.bfloat16
LANE = 128
VMEM_LIMIT_BYTES = 48 * 1024 * 1024
LOG2E = math.log2(math.e)
NEG_BIG = -1e30
KEY_MIN = -2 ** 31
FAR_BUCKET = NUM_BUCKETS // 2 - 1
NEAR_BLOCKS = 6
FF_TILE = 512
MIX_TM_PROMPT = 1024
MIX_TM_SAMPLE = 256
MOE_ROW_TILE = 1024
MOE_FF_TILE = 1792
MOE_COMBINE_TM = 512
DMA_BURST = 8
PROJ_TM = 512
DSA_SUB = 2 * LANE
DSA_KEY_TILE = 2048
DSA_INDEX_TILE = 4096
DSA_COUNT_CHUNK = 1024
CHUNK_SHIFT = CHUNK.bit_length() - 1
LANE_SHIFT = LANE.bit_length() - 1
COUNT_ALL = float(2 ** 24)
ARB2 = ("arbitrary", "arbitrary")


def _params(sem, vmem=True):
    return pltpu.CompilerParams(dimension_semantics=sem, vmem_limit_bytes=VMEM_LIMIT_BYTES if vmem else None)


def rel_bucket(rel):
    half = NUM_BUCKETS // 2
    max_exact = half // 2
    base = jnp.where(rel > 0, half, 0)
    n = jnp.abs(rel)
    nf = jnp.maximum(n, 1).astype(F32)
    large = max_exact + (jnp.log(nf / max_exact) / math.log(MAX_DISTANCE / max_exact)
                         * (half - max_exact)).astype(jnp.int32)
    large = jnp.minimum(large, half - 1)
    return base + jnp.where(n < max_exact, n, large)


def _adaln_kernel(c_ref, w_ref, b_ref, o_ref):
    c = c_ref[...]
    o_ref[...] = jnp.dot(c * jax.nn.sigmoid(c), w_ref[...], preferred_element_type=F32) + b_ref[...]


def adaln(c, w, b):
    n = c.shape[0]
    mod = pl.pallas_call(
        _adaln_kernel,
        grid=(6,),
        in_specs=[pl.BlockSpec((n, D_MODEL), lambda j: (0, 0)),
                  pl.BlockSpec((D_MODEL, D_MODEL), lambda j: (0, j)),
                  pl.BlockSpec((1, D_MODEL), lambda j: (0, j))],
        out_specs=pl.BlockSpec((n, D_MODEL), lambda j: (0, j)),
        out_shape=jax.ShapeDtypeStruct((n, 6 * D_MODEL), F32),
        name="adaln",
    )(c, w, b[None, :])
    return jnp.split(mod[:, None, :], 6, axis=-1)


def _modulate(x, g, shift, scale):
    y = x * lax.rsqrt(jnp.mean(x * x, axis=-1, keepdims=True) + EPS)
    return (y * g) * (1.0 + scale) + shift


def _seg_norm(seg, g):
    return seg * lax.rsqrt(jnp.mean(seg * seg, axis=-1, keepdims=True) + EPS) * g


def _proj_kernel(x_ref, g_ref, sh_ref, sc_ref, w_ref, gq_ref, gk_ref, *rest, n_kv, has_idx):
    h = _modulate(x_ref[0], g_ref[...], sh_ref[0], sc_ref[0]).astype(BF16)
    y = jnp.dot(h, w_ref[...], preferred_element_type=F32)
    tm = y.shape[0]
    kv_w = n_kv * HEAD_DIM
    if has_idx:
        gki_ref, q16, kf, vf, ktr, vx, kif, qi16, kit, wsc = rest
    else:
        q16, kf, vf, ktr, vx = rest
    for hd in range(N_HEADS):
        seg = y[:, hd * HEAD_DIM:(hd + 1) * HEAD_DIM]
        q16[0, hd] = (_seg_norm(seg, gq_ref[...]) * (HEAD_DIM ** -0.5 * LOG2E)).astype(BF16)
    lane = lax.broadcasted_iota(jnp.int32, (tm, LANE - HEAD_DIM), 1)
    ones_col = jnp.where(lane == 0, 1.0, 0.0).astype(BF16)
    for j in range(n_kv):
        kseg = _seg_norm(y[:, ATTN_WIDTH + j * HEAD_DIM:ATTN_WIDTH + (j + 1) * HEAD_DIM], gk_ref[...])
        vseg = y[:, ATTN_WIDTH + kv_w + j * HEAD_DIM:ATTN_WIDTH + kv_w + (j + 1) * HEAD_DIM]
        kf[0, :, j * HEAD_DIM:(j + 1) * HEAD_DIM] = kseg
        vf[0, :, j * HEAD_DIM:(j + 1) * HEAD_DIM] = vseg
        ktr[0, j] = kseg.T.astype(BF16)
        vx[0, j] = jnp.concatenate([vseg.astype(BF16), ones_col], axis=1)
    if has_idx:
        base = ATTN_WIDTH + 2 * kv_w
        for hd in range(IDX_HEADS):
            qi16[0, hd] = y[:, base + hd * IDX_DIM:base + (hd + 1) * IDX_DIM].astype(BF16)
        base += IDX_HEADS * IDX_DIM
        kiseg = _seg_norm(y[:, base:base + IDX_DIM], gki_ref[...])
        kif[0] = kiseg
        kit[0] = kiseg.T.astype(BF16)
        wsc[0] = y[:, base + IDX_DIM:base + IDX_DIM + IDX_HEADS] * IDX_HEADS ** -0.5 * IDX_DIM ** -0.5


def _project(x, g, shift, scale, w_in, g_q, g_k, g_kidx=None, *, n_kv, tm):
    has_idx = g_kidx is not None
    gsz, t, _ = x.shape
    n_in = w_in.shape[1]
    kv_w = n_kv * HEAD_DIM
    tok = lambda b, i: (b, i, 0)
    head = lambda b, i: (b, 0, i, 0)
    vec = lambda b, i: (b, 0, 0)
    cst = lambda b, i: (0, 0)
    shapes = [((gsz, N_HEADS, t, HEAD_DIM), BF16, (1, N_HEADS, tm, HEAD_DIM), head),
              ((gsz, t, kv_w), F32, (1, tm, kv_w), tok),
              ((gsz, t, kv_w), F32, (1, tm, kv_w), tok),
              ((gsz, n_kv, HEAD_DIM, t), BF16, (1, n_kv, HEAD_DIM, tm), lambda b, i: (b, 0, 0, i)),
              ((gsz, n_kv, t, LANE), BF16, (1, n_kv, tm, LANE), head)]
    if has_idx:
        shapes += [((gsz, t, IDX_DIM), F32, (1, tm, IDX_DIM), tok),
                   ((gsz, IDX_HEADS, t, IDX_DIM), BF16, (1, IDX_HEADS, tm, IDX_DIM), head),
                   ((gsz, IDX_DIM, t), BF16, (1, IDX_DIM, tm), lambda b, i: (b, 0, i)),
                   ((gsz, t, IDX_HEADS), F32, (1, tm, IDX_HEADS), tok)]
    return pl.pallas_call(
        functools.partial(_proj_kernel, n_kv=n_kv, has_idx=has_idx),
        grid=(gsz, t // tm),
        in_specs=[pl.BlockSpec((1, tm, D_MODEL), tok), pl.BlockSpec((1, D_MODEL), cst),
                  pl.BlockSpec((1, 1, D_MODEL), vec), pl.BlockSpec((1, 1, D_MODEL), vec),
                  pl.BlockSpec((D_MODEL, n_in), cst), pl.BlockSpec((1, HEAD_DIM), cst),
                  pl.BlockSpec((1, HEAD_DIM), cst)] + ([pl.BlockSpec((1, IDX_DIM), cst)] if has_idx else []),
        out_specs=[pl.BlockSpec(blk, im) for _, _, blk, im in shapes],
        out_shape=[jax.ShapeDtypeStruct(s, d) for s, d, _, _ in shapes],
        compiler_params=_params(ARB2),
        name="mixer_in_proj",
    )(x, g[None, :], shift, scale, w_in.astype(BF16), g_q[None, :], g_k[None, :],
      *([g_kidx[None, :]] if has_idx else []))


def _tri_steps(nqb, tq, tk, p0):
    qbs, kts = [], []
    for qb in range(nqb):
        vis_end = ((p0 + qb * tq + tq - 1) // CHUNK + 1) * CHUNK
        for kt in range(-(-vis_end // tk)):
            qbs.append(qb)
            kts.append(kt)
    return jnp.asarray(np.array(qbs, np.int32)), jnp.asarray(np.array(kts, np.int32))


def _idx_kernel(qb_tab, kt_tab, qi_ref, w_ref, kit_ref, out_ref, *, tq, tk, sw, p0, l_true):
    step = pl.program_id(1)
    q0 = p0 + qb_tab[step] * tq
    k0 = kt_tab[step] * tk
    row = lax.broadcasted_iota(jnp.int32, (tq, 1), 0) + q0
    vis_end = jnp.minimum(((row >> CHUNK_SHIFT) + 1) << CHUNK_SHIFT, l_true)
    wv = w_ref[0]
    for c in range(tk // sw):
        kit = kit_ref[0, :, c * sw:(c + 1) * sw]
        acc = jnp.zeros((tq, sw), F32)
        for h in range(IDX_HEADS):
            sh = jnp.dot(qi_ref[0, h], kit, preferred_element_type=F32)
            acc = acc + wv[:, h:h + 1] * jnp.maximum(sh, 0.0)
        bits = lax.bitcast_convert_type(acc, jnp.int32)
        key = bits ^ ((bits >> 31) & 0x7FFFFFFF)
        kpos = k0 + c * sw + lax.broadcasted_iota(jnp.int32, (tq, sw), 1)
        out_ref[0, :, c * sw:(c + 1) * sw] = jnp.where(kpos < vis_end, key, KEY_MIN)


def _thr_kernel(keys_ref, tau_ref, quota_ref, flag_ref, *, tq, cw, p0, n_sel):
    qb = pl.program_id(1)
    vis_end = (((p0 + qb * tq + tq - 1) >> CHUNK_SHIFT) + 1) << CHUNK_SHIFT
    nch = (vis_end + cw - 1) // cw
    key_max = 2 ** 31 - 1

    def count_ge(cand):
        def body(j, acc):
            x = keys_ref[0, :, pl.ds(pl.multiple_of(j * cw, cw), cw)]
            for u in range(cw // LANE):
                acc = acc + jnp.where(x[:, u * LANE:(u + 1) * LANE] >= cand, 1.0, 0.0)
            return acc
        acc = lax.fori_loop(0, nch, body, jnp.zeros((tq, LANE), F32))
        return jnp.sum(acc, axis=1, keepdims=True)

    zero = jnp.zeros((tq, LANE), jnp.int32)
    c0 = count_ge(zero)
    tau = jnp.where(c0 >= n_sel, zero, KEY_MIN)
    n_ge = jnp.where(c0 >= n_sel, c0, COUNT_ALL)

    def bit_body(carry):
        i, tau, n_ge = carry
        cand = tau + jnp.left_shift(jnp.int32(1), 30 - i)
        c = count_ge(cand)
        return i + 1, jnp.where(c >= n_sel, cand, tau), jnp.where(c >= n_sel, c, n_ge)

    def unresolved(carry):
        i, _, n_ge = carry
        return (i < 31) & (jnp.max(jnp.abs(n_ge - n_sel)) > 0.0)

    _, tau, n_ge = lax.while_loop(unresolved, bit_body, (jnp.int32(0), tau, n_ge))
    n_gt = jnp.where(tau[:, :1] < key_max, count_ge(jnp.minimum(tau, key_max - 1) + 1), 0.0)
    tau_ref[0] = jnp.maximum(tau, KEY_MIN + 1)
    quota_ref[0] = jnp.broadcast_to(n_sel - n_gt, (tq, LANE))
    over = jnp.max(jnp.where(n_ge > n_sel, 1, 0), axis=0, keepdims=True)
    flag_ref[0, 0] = jnp.broadcast_to(over, flag_ref.shape[2:])


def _attn_kernel(qb_tab, kt_tab, flag_tab, q_ref, kt_ref, vx_ref, keys_ref, tau_ref, quota_ref, bias_ref, o_ref,
                 m_ref, acc_ref, negm_ref, seen_ref, *, tq, tk, sw, p0, nqb):
    step = pl.program_id(1)
    qb, kt = qb_tab[step], kt_tab[step]
    q0 = p0 + qb * tq
    k0 = kt * tk
    vis_end = (((q0 + tq - 1) >> CHUNK_SHIFT) + 1) << CHUNK_SHIFT
    group = N_HEADS // KV_HEADS_A
    nblk = sw // LANE
    tie_w = 2 * LANE

    @pl.when(kt == 0)
    def _():
        m_ref[...] = jnp.full(m_ref.shape, NEG_BIG, F32)
        acc_ref[...] = jnp.zeros(acc_ref.shape, F32)
        seen_ref[...] = jnp.zeros(seen_ref.shape, F32)

    has_ties = flag_tab[pl.program_id(0) * nqb + qb] != 0
    tau = tau_ref[0]

    @pl.when(jnp.logical_not(has_ties))
    def _():
        for u in range(tk // LANE):
            cols = slice(u * LANE, (u + 1) * LANE)
            negm_ref[:, cols] = jnp.where(keys_ref[0, :, cols] >= tau, 0.0, NEG_BIG)

    @pl.when(has_ties)
    def _():
        tau2 = jnp.concatenate([tau] * (tie_w // LANE), axis=1)
        quota = jnp.concatenate([quota_ref[0]] * (tie_w // LANE), axis=1)
        before = (lax.broadcasted_iota(jnp.int32, (tie_w, tie_w), 0)
                  < lax.broadcasted_iota(jnp.int32, (tie_w, tie_w), 1))
        before = jnp.where(before, 1.0, 0.0).astype(BF16)
        seen = seen_ref[...]
        for u in range(tk // tie_w):
            cols = slice(u * tie_w, (u + 1) * tie_w)
            k = keys_ref[0, :, cols]
            tie = jnp.where(k == tau2, 1.0, 0.0)
            rank = (jnp.dot(tie.astype(BF16), before, preferred_element_type=F32)
                    + jnp.concatenate([seen] * (tie_w // LANE), axis=1))
            keep = jnp.where(k > tau2, 1.0, jnp.where(rank < quota, tie, 0.0))
            negm_ref[:, cols] = jnp.where(keep > 0.0, 0.0, NEG_BIG)
            seen = seen + jnp.sum(tie, axis=1, keepdims=True)
        seen_ref[...] = seen

    def sub_tile(c, with_bias):
        off = c * sw if isinstance(c, int) else pl.multiple_of(c * sw, sw)
        negm = negm_ref[:, pl.ds(off, sw)]
        if with_bias:
            d0 = ((k0 + off - q0) >> LANE_SHIFT) + NEAR_BLOCKS
            bidx = [jnp.clip(d0 + u, 0, NEAR_BLOCKS) for u in range(nblk)]
        for h in range(N_HEADS):
            j = h // group
            s = jnp.dot(q_ref[0, h], kt_ref[0, j, :, pl.ds(off, sw)], preferred_element_type=F32) + negm
            if with_bias:
                s = s + jnp.concatenate([bias_ref[bidx[u], h] for u in range(nblk)], axis=1)
            m_prev = m_ref[h]
            m_cur = jnp.maximum(m_prev, jnp.max(s, axis=1, keepdims=True))
            alpha = jnp.exp2(m_prev - m_cur)
            p = jnp.exp2(s - jnp.concatenate([m_cur] * nblk, axis=1))
            pv = jnp.dot(p.astype(BF16), vx_ref[0, j, pl.ds(off, sw), :], preferred_element_type=F32)
            acc_ref[h] = alpha * acc_ref[h] + pv
            m_ref[h] = m_cur

    tile_far = k0 + tk <= q0 - (NEAR_BLOCKS - 1) * LANE

    @pl.when(tile_far)
    def _():
        for c in range(tk // sw):
            sub_tile(c, False)

    @pl.when(jnp.logical_not(tile_far))
    def _():
        nsub = jnp.minimum(tk // sw, (vis_end - k0 + sw - 1) // sw)
        nfar = jnp.clip((q0 - (NEAR_BLOCKS - 1) * LANE - k0) // sw, 0, nsub)

        def run(lo, hi, with_bias):
            n = hi - lo

            def pair(c, carry):
                sub_tile(lo + 2 * c, with_bias)
                sub_tile(lo + 2 * c + 1, with_bias)
                return carry
            lax.fori_loop(0, n >> 1, pair, 0)

            @pl.when((n & 1) == 1)
            def _():
                sub_tile(hi - 1, with_bias)

        run(0, nfar, False)
        run(nfar, nsub, True)

    @pl.when(k0 + tk >= vis_end)
    def _():
        for h in range(N_HEADS):
            a = acc_ref[h]
            o_ref[0, :, h * HEAD_DIM:(h + 1) * HEAD_DIM] = a[:, :HEAD_DIM] / a[:, HEAD_DIM:HEAD_DIM + 1]


def _rel_tiles(rel_table, rel):
    onehot = jax.nn.one_hot(rel_bucket(rel), NUM_BUCKETS, dtype=F32)
    return jnp.einsum("nrsk,kh->nhrs", onehot, rel_table.astype(F32), precision=lax.Precision.HIGHEST)


def _dsa_bias_tiles(rel_table):
    d = jnp.arange(NEAR_BLOCKS + 1, dtype=jnp.int32)[:, None, None] - NEAR_BLOCKS
    rel = (d * LANE + jnp.arange(LANE, dtype=jnp.int32)[None, None, :]
           - jnp.arange(LANE, dtype=jnp.int32)[None, :, None])
    tiles = (_rel_tiles(rel_table, rel) - rel_table[FAR_BUCKET][None, :, None, None]) * LOG2E
    return tiles.at[0].set(0.0)


def _dsa_attention(q16, ktr, vx, qi16, kit, w, bias, *, p0, l_true, n_sel, tq, tk, sw, cw, tk_idx):
    bsz, t = q16.shape[0], q16.shape[2]
    l_pad = ktr.shape[3]
    nqb = t // tq
    assert t % tq == 0 and p0 % LANE == 0 and (tq == LANE or nqb == 1)
    assert l_pad % tk == 0 and tk % sw == 0 and tk % cw == 0 and tk % (2 * LANE) == 0
    assert l_pad % tk_idx == 0 and tk_idx % tk == 0
    qbt, ktt = _tri_steps(nqb, tq, tk, p0)
    nsteps = int(qbt.shape[0])
    qbt_idx, ktt_idx = _tri_steps(nqb, tq, tk_idx, p0)

    keys = pl.pallas_call(
        functools.partial(_idx_kernel, tq=tq, tk=tk_idx, sw=sw, p0=p0, l_true=l_true),
        grid_spec=pltpu.PrefetchScalarGridSpec(
            num_scalar_prefetch=2, grid=(bsz, int(qbt_idx.shape[0])),
            in_specs=[pl.BlockSpec((1, IDX_HEADS, tq, IDX_DIM), lambda b, s, qt, kt: (b, 0, qt[s], 0)),
                      pl.BlockSpec((1, tq, IDX_HEADS), lambda b, s, qt, kt: (b, qt[s], 0)),
                      pl.BlockSpec((1, IDX_DIM, tk_idx), lambda b, s, qt, kt: (b, 0, kt[s]))],
            out_specs=pl.BlockSpec((1, tq, tk_idx), lambda b, s, qt, kt: (b, qt[s], kt[s]))),
        out_shape=jax.ShapeDtypeStruct((bsz, t, l_pad), jnp.int32),
        compiler_params=_params(ARB2, vmem=False),
        name="dsa_index_keys",
    )(qbt_idx, ktt_idx, qi16, w, kit)

    tau, quota, flags = pl.pallas_call(
        functools.partial(_thr_kernel, tq=tq, cw=cw, p0=p0, n_sel=n_sel),
        grid=(bsz, nqb),
        in_specs=[pl.BlockSpec((1, tq, l_pad), lambda b, i: (b, i, 0))],
        out_specs=[pl.BlockSpec((1, tq, LANE), lambda b, i: (b, i, 0)),
                   pl.BlockSpec((1, tq, LANE), lambda b, i: (b, i, 0)),
                   pl.BlockSpec((1, 1, 8, LANE), lambda b, i: (b, i, 0, 0))],
        out_shape=[jax.ShapeDtypeStruct((bsz, t, LANE), jnp.int32),
                   jax.ShapeDtypeStruct((bsz, t, LANE), F32),
                   jax.ShapeDtypeStruct((bsz, nqb, 8, LANE), jnp.int32)],
        compiler_params=_params(ARB2),
        name="dsa_threshold",
    )(keys)
    flags = flags[:, :, 0, 0].reshape(bsz * nqb)

    return pl.pallas_call(
        functools.partial(_attn_kernel, tq=tq, tk=tk, sw=sw, p0=p0, nqb=nqb),
        grid_spec=pltpu.PrefetchScalarGridSpec(
            num_scalar_prefetch=3, grid=(bsz, nsteps),
            in_specs=[pl.BlockSpec((1, N_HEADS, tq, HEAD_DIM), lambda b, s, qt, kt, fl: (b, 0, qt[s], 0)),
                      pl.BlockSpec((1, KV_HEADS_A, HEAD_DIM, tk), lambda b, s, qt, kt, fl: (b, 0, 0, kt[s])),
                      pl.BlockSpec((1, KV_HEADS_A, tk, LANE), lambda b, s, qt, kt, fl: (b, 0, kt[s], 0)),
                      pl.BlockSpec((1, tq, tk), lambda b, s, qt, kt, fl: (b, qt[s], kt[s])),
                      pl.BlockSpec((1, tq, LANE), lambda b, s, qt, kt, fl: (b, qt[s], 0)),
                      pl.BlockSpec((1, tq, LANE), lambda b, s, qt, kt, fl: (b, qt[s], 0)),
                      pl.BlockSpec((NEAR_BLOCKS + 1, N_HEADS, tq, LANE), lambda b, s, qt, kt, fl: (0, 0, 0, 0))],
            out_specs=pl.BlockSpec((1, tq, ATTN_WIDTH), lambda b, s, qt, kt, fl: (b, qt[s], 0)),
            scratch_shapes=[pltpu.VMEM((N_HEADS, tq, LANE), F32), pltpu.VMEM((N_HEADS, tq, LANE), F32),
                            pltpu.VMEM((tq, tk), F32), pltpu.VMEM((tq, LANE), F32)]),
        out_shape=jax.ShapeDtypeStruct((bsz, t, ATTN_WIDTH), F32),
        compiler_params=_params(ARB2),
        name="dsa_masked_attention",
    )(qbt, ktt, flags, q16, ktr, vx, keys, tau, quota, bias[:, :, :tq, :])


def _band_kernel(q_ref, kp_ref, kc_ref, vp_ref, vc_ref, bias_ref, sink_ref, o_ref, *, tq, first_has_no_prev):
    group = N_HEADS // KV_HEADS_B
    extra = None
    if first_has_no_prev:
        col = lax.broadcasted_iota(jnp.int32, (tq, 2 * LANE), 1)
        extra = jnp.where((pl.program_id(1) == 0) & (col < LANE), NEG_BIG, 0.0)
    ones = jnp.ones((2 * LANE, LANE), BF16)
    low_half = lax.broadcasted_iota(jnp.int32, (tq, LANE), 1) < HEAD_DIM
    for j in range(KV_HEADS_B):
        heads = slice(j * group, (j + 1) * group)
        kw = jnp.concatenate([kp_ref[0, j], kc_ref[0, j]], axis=1)
        v_low = jnp.concatenate([vp_ref[0, j], vc_ref[0, j]], axis=0)
        v_high = jnp.concatenate([v_low[:, HEAD_DIM:], v_low[:, :HEAD_DIM]], axis=1)
        v_sum = [jnp.concatenate([v, ones], axis=1) for v in (v_low, v_high)]
        s_all = jnp.dot(q_ref[0, heads].reshape(group * tq, HEAD_DIM), kw,
                        preferred_element_type=F32).reshape(group, tq, 2 * LANE)
        for g0 in range(0, group, 2):
            halves = []
            for par in range(2):
                h = j * group + g0 + par
                s = s_all[g0 + par] + bias_ref[h]
                if extra is not None:
                    s = s + extra
                sink = sink_ref[h]
                m = jnp.maximum(jnp.max(s, axis=1, keepdims=True), sink)
                p = jnp.exp2(s - jnp.concatenate([m, m], axis=1)).astype(BF16)
                r = jnp.dot(p, v_sum[par], preferred_element_type=F32)
                halves.append(r[:, :LANE] / (r[:, LANE:] + jnp.exp2(sink - m)))
            h0 = j * group + g0
            o_ref[0, :, h0 * HEAD_DIM:(h0 + 2) * HEAD_DIM] = jnp.where(low_half, halves[0], halves[1])


def _band_bias(rel_table):
    r = jnp.arange(LANE, dtype=jnp.int32)[:, None]
    s = jnp.arange(2 * LANE, dtype=jnp.int32)[None, :]
    wc, qh = s // CHUNK, r // CHUNK
    band = (wc >= qh) & (wc <= qh + 2)
    tiles = _rel_tiles(rel_table, (s - LANE - r)[None])[0]
    return jnp.where(band[None], tiles * LOG2E, NEG_BIG)


def _band_attention(q16, kp, kc, vp, vc, bias, sinks, *, tq, prev_map, cur_map, first_has_no_prev):
    bsz, t = q16.shape[0], q16.shape[2]
    kblk = lambda m: pl.BlockSpec((1, KV_HEADS_B, HEAD_DIM, LANE), lambda b, i: (b, 0, 0, m(i)))
    vblk = lambda m: pl.BlockSpec((1, KV_HEADS_B, LANE, LANE), lambda b, i: (b, 0, m(i), 0))
    sink_rows = jnp.broadcast_to((sinks.astype(F32) * LOG2E)[:, None, None], (N_HEADS, 1, LANE))
    return pl.pallas_call(
        functools.partial(_band_kernel, tq=tq, first_has_no_prev=first_has_no_prev),
        grid=(bsz, t // tq),
        in_specs=[pl.BlockSpec((1, N_HEADS, tq, HEAD_DIM), lambda b, i: (b, 0, i, 0)),
                  kblk(prev_map), kblk(cur_map), vblk(prev_map), vblk(cur_map),
                  pl.BlockSpec((N_HEADS, tq, 2 * LANE), lambda b, i: (0, 0, 0)),
                  pl.BlockSpec((N_HEADS, 1, LANE), lambda b, i: (0, 0, 0))],
        out_specs=pl.BlockSpec((1, tq, ATTN_WIDTH), lambda b, i: (b, i, 0)),
        out_shape=jax.ShapeDtypeStruct((bsz, t, ATTN_WIDTH), F32),
        compiler_params=_params(ARB2, vmem=False),
        name="band_attention",
    )(q16, kp, kc, vp, vc, bias[:, :tq, :], sink_rows)


def _mixer_residual(x_ref, o_ref, wo_ref, gt1_ref):
    return x_ref[0] + gt1_ref[0] * jnp.dot(o_ref[0].astype(BF16), wo_ref[...], preferred_element_type=F32)


def _swiglu_chunk(h, wg, wu, wd):
    a = jax.nn.silu(jnp.dot(h, wg, preferred_element_type=F32)) * jnp.dot(h, wu, preferred_element_type=F32)
    return jnp.dot(a.astype(BF16), wd, preferred_element_type=F32)


def _ffn_kernel(x_ref, o_ref, wo_ref, gt1_ref, g_ref, sh_ref, sc_ref, gt2_ref, wg_ref, wu_ref, wd_ref,
                out_ref, xn_scr, h_scr, acc_scr):
    c = pl.program_id(2)

    @pl.when(c == 0)
    def _():
        xn = _mixer_residual(x_ref, o_ref, wo_ref, gt1_ref)
        xn_scr[...] = xn
        h_scr[...] = _modulate(xn, g_ref[...], sh_ref[0], sc_ref[0]).astype(BF16)
        acc_scr[...] = jnp.zeros(acc_scr.shape, F32)

    acc_scr[...] += _swiglu_chunk(h_scr[...], wg_ref[...], wu_ref[...], wd_ref[...])

    @pl.when(c == pl.num_programs(2) - 1)
    def _():
        out_ref[0] = xn_scr[...] + gt2_ref[0] * acc_scr[...]


def _moe_pre_kernel(x_ref, o_ref, wo_ref, gt1_ref, g_ref, sh_ref, sc_ref, wr_ref, xn_ref, h_ref, gate_ref, sel_ref):
    xn = _mixer_residual(x_ref, o_ref, wo_ref, gt1_ref)
    xn_ref[0] = xn
    h = _modulate(xn, g_ref[...], sh_ref[0], sc_ref[0])
    h_ref[0] = h
    lane = lax.broadcasted_iota(jnp.int32, (h.shape[0], LANE), 1)
    lanef = lane.astype(F32)
    logits = jnp.dot(h.astype(BF16), wr_ref[...], preferred_element_type=F32)
    logits = jnp.where(lane < N_EXPERTS, logits, -jnp.inf)
    m1 = jnp.max(logits, axis=1, keepdims=True)
    i1 = jnp.min(jnp.where(logits == m1, lanef, float(LANE)), axis=1, keepdims=True)
    rest = jnp.where(lanef == i1, -jnp.inf, logits)
    m2 = jnp.max(rest, axis=1, keepdims=True)
    i2 = jnp.min(jnp.where(rest == m2, lanef, float(LANE)), axis=1, keepdims=True)
    e2 = jnp.exp(m2 - m1)
    den = 1.0 + e2
    gates = jnp.where(lanef == i1, 1.0 / den, 0.0) + jnp.where(lanef == i2, e2 / den, 0.0)
    sel = jnp.where(lanef == i1, 1.0, 0.0) + jnp.where(lanef == i2, 1.0, 0.0)
    gate_ref[0] = gates[:, :N_EXPERTS]
    sel_ref[0] = sel[:, :N_EXPERTS]


def _moe_expert_kernel(te_tab, tv_tab, src_hbm, dst_hbm, h_hbm, wg_ref, wu_ref, wd_ref, y_hbm,
                       src_smem, dst_smem, hbuf, hb, acc, sem_idx, sem_in, sem_out, *, rows):
    i, c = pl.program_id(0), pl.program_id(1)
    valid = tv_tab[i] != 0
    all_rows_in = pltpu.make_async_copy(h_hbm.at[pl.ds(0, rows), :], hbuf, sem_in)
    all_rows_out = pltpu.make_async_copy(acc, y_hbm.at[pl.ds(0, rows), :], sem_out)

    @pl.when(valid & (c == 0))
    def _():
        idx_copies = [pltpu.make_async_copy(src_hbm.at[i], src_smem, sem_idx.at[0]),
                      pltpu.make_async_copy(dst_hbm.at[i], dst_smem, sem_idx.at[1])]
        for cp in idx_copies:
            cp.start()
        for cp in idx_copies:
            cp.wait()

        def gather_rows(r0, carry):
            for u in range(DMA_BURST):
                r = r0 * DMA_BURST + u
                pltpu.make_async_copy(h_hbm.at[pl.ds(src_smem[r], 1), :], hbuf.at[pl.ds(r, 1), :],
                                      sem_in).start(priority=u % 2)
            return carry
        lax.fori_loop(0, rows // DMA_BURST, gather_rows, 0)
        all_rows_in.wait()
        hb[...] = hbuf[...].astype(BF16)
        acc[...] = jnp.zeros(acc.shape, F32)

    @pl.when(valid)
    def _():
        half = MOE_FF_TILE // 2
        for u in range(2):
            cols = slice(u * half, (u + 1) * half)
            acc[...] += _swiglu_chunk(hb[...], wg_ref[0, :, cols], wu_ref[0, :, cols], wd_ref[0, cols, :])

    @pl.when(valid & (c == pl.num_programs(1) - 1))
    def _():
        def scatter_rows(r0, carry):
            for u in range(DMA_BURST):
                r = r0 * DMA_BURST + u
                pltpu.make_async_copy(acc.at[pl.ds(r, 1), :], y_hbm.at[pl.ds(dst_smem[r], 1), :],
                                      sem_out).start(priority=u % 2)
            return carry
        lax.fori_loop(0, rows // DMA_BURST, scatter_rows, 0)
        all_rows_out.wait()


def _moe_combine_kernel(xn_ref, gt2_ref, g2_ref, ya_ref, yb_ref, out_ref):
    g2 = g2_ref[0]
    out_ref[0] = xn_ref[0] + gt2_ref[0] * (g2[:, 0:1] * ya_ref[...] + g2[:, 1:2] * yb_ref[...])


def _token_specs(tm, nd):
    pick = (lambda f: lambda b, i: f(b, i)) if nd == 2 else (lambda f: lambda b, i, c: f(b, i))
    tok = pl.BlockSpec((1, tm, D_MODEL), pick(lambda b, i: (b, i, 0)))

    def mod_spec(a):
        return pl.BlockSpec((1, 1, D_MODEL), pick(lambda b, i: (b, 0, 0))) if a.shape[1] == 1 else tok

    cst = lambda shape: pl.BlockSpec(shape, pick(lambda b, i: (0, 0)))
    return tok, mod_spec, cst


def _swiglu_mixer(x, o, w_out, gt1, g, shift, scale, gt2, w_gate, w_up, w_down, *, tm):
    gsz, t, _ = x.shape
    d_ff = w_gate.shape[-1]
    assert t % tm == 0 and d_ff % FF_TILE == 0
    tok, mod_spec, cst = _token_specs(tm, 3)
    return pl.pallas_call(
        _ffn_kernel, grid=(gsz, t // tm, d_ff // FF_TILE),
        in_specs=[tok, tok, cst((ATTN_WIDTH, D_MODEL)), mod_spec(gt1), cst((1, D_MODEL)),
                  mod_spec(shift), mod_spec(scale), mod_spec(gt2),
                  pl.BlockSpec((D_MODEL, FF_TILE), lambda b, i, c: (0, c)),
                  pl.BlockSpec((D_MODEL, FF_TILE), lambda b, i, c: (0, c)),
                  pl.BlockSpec((FF_TILE, D_MODEL), lambda b, i, c: (c, 0))],
        out_specs=tok,
        out_shape=jax.ShapeDtypeStruct((gsz, t, D_MODEL), F32),
        scratch_shapes=[pltpu.VMEM((tm, D_MODEL), F32), pltpu.VMEM((tm, D_MODEL), BF16),
                        pltpu.VMEM((tm, D_MODEL), F32)],
        compiler_params=_params(("arbitrary",) * 3),
        name="swiglu_channel_mixer",
    )(x, o, w_out.astype(BF16), gt1, g[None, :], shift, scale, gt2,
      w_gate.astype(BF16), w_up.astype(BF16), w_down.astype(BF16))


def _moe_pre(x, o, w_out, gt1, g, shift, scale, w_router, *, tm):
    gsz, t, _ = x.shape
    assert t % tm == 0
    tok, mod_spec, cst = _token_specs(tm, 2)
    small = pl.BlockSpec((1, tm, N_EXPERTS), lambda b, i: (b, i, 0))
    wr = jnp.pad(w_router.astype(BF16), ((0, 0), (0, LANE - N_EXPERTS)))
    return pl.pallas_call(
        _moe_pre_kernel, grid=(gsz, t // tm),
        in_specs=[tok, tok, cst((ATTN_WIDTH, D_MODEL)), mod_spec(gt1), cst((1, D_MODEL)),
                  mod_spec(shift), mod_spec(scale), cst((D_MODEL, LANE))],
        out_specs=[tok, tok, small, small],
        out_shape=[jax.ShapeDtypeStruct((gsz, t, D_MODEL), F32), jax.ShapeDtypeStruct((gsz, t, D_MODEL), F32),
                   jax.ShapeDtypeStruct((gsz, t, N_EXPERTS), F32), jax.ShapeDtypeStruct((gsz, t, N_EXPERTS), F32)],
        compiler_params=_params(ARB2),
        name="moe_pre_router",
    )(x, o, w_out.astype(BF16), gt1, g[None, :], shift, scale, wr)


def _moe_routing(gates, sel, rows):
    n = gates.shape[0]
    member = sel > 0
    mi = member.astype(jnp.int32)
    counts = mi.sum(0)
    padded = (counts + rows - 1) // rows * rows
    ends = jnp.cumsum(padded)
    slot = jnp.cumsum(mi, axis=1) - mi
    expert_of = [jnp.sum(jnp.where(member & (slot == k), jnp.arange(N_EXPERTS)[None, :], 0), axis=1)
                 for k in range(TOP_K)]
    g2 = jnp.stack([jnp.sum(jnp.where(member & (slot == k), gates, 0.0), axis=1) for k in range(TOP_K)], axis=1)
    n_rows = (2 * n + N_EXPERTS * rows) // rows * rows
    n_pad = n_rows - 2 * n
    pad_ends = jnp.cumsum(padded - counts)
    pad_expert = (jnp.arange(n_pad, dtype=jnp.int32)[:, None] >= pad_ends[None, :]).sum(1)
    tok = jnp.arange(n, dtype=jnp.int32)
    sort_key = jnp.concatenate([expert_of[0], expert_of[1], pad_expert]).astype(jnp.int32)
    pair = jnp.concatenate([tok, n + tok, jnp.full((n_pad,), -1, jnp.int32)])
    _, pair = lax.sort((sort_key, pair), num_keys=1, is_stable=True)
    spare = 2 * n + jnp.arange(n_rows, dtype=jnp.int32) % rows
    src = jnp.where(pair >= 0, pair % n, 0)
    dst = jnp.where(pair >= 0, pair, spare)
    n_tiles = n_rows // rows
    tile_start = jnp.arange(n_tiles, dtype=jnp.int32) * rows
    tile_expert = jnp.minimum((tile_start[:, None] >= ends[None, :]).sum(1), N_EXPERTS - 1).astype(jnp.int32)
    tile_valid = (tile_start < ends[-1]).astype(jnp.int32)
    return src.reshape(n_tiles, rows), dst.reshape(n_tiles, rows), tile_expert, tile_valid, g2


def _moe_experts(h_all, src, dst, tile_expert, tile_valid, w_gate, w_up, w_down, *, rows):
    n = h_all.shape[0]
    n_tiles = src.shape[0]
    d_ff = w_gate.shape[-1]
    assert d_ff % MOE_FF_TILE == 0 and (MOE_FF_TILE // 2) % LANE == 0
    any_spec = pl.BlockSpec(memory_space=pl.ANY)
    return pl.pallas_call(
        functools.partial(_moe_expert_kernel, rows=rows),
        grid_spec=pltpu.PrefetchScalarGridSpec(
            num_scalar_prefetch=2, grid=(n_tiles, d_ff // MOE_FF_TILE),
            in_specs=[any_spec, any_spec, any_spec,
                      pl.BlockSpec((1, D_MODEL, MOE_FF_TILE), lambda i, c, te, tv: (te[i], 0, c)),
                      pl.BlockSpec((1, D_MODEL, MOE_FF_TILE), lambda i, c, te, tv: (te[i], 0, c)),
                      pl.BlockSpec((1, MOE_FF_TILE, D_MODEL), lambda i, c, te, tv: (te[i], c, 0))],
            out_specs=any_spec,
            scratch_shapes=[pltpu.SMEM((rows,), jnp.int32), pltpu.SMEM((rows,), jnp.int32),
                            pltpu.VMEM((rows, D_MODEL), F32), pltpu.VMEM((rows, D_MODEL), BF16),
                            pltpu.VMEM((rows, D_MODEL), F32), pltpu.SemaphoreType.DMA((2,)),
                            pltpu.SemaphoreType.DMA(()), pltpu.SemaphoreType.DMA(())]),
        out_shape=jax.ShapeDtypeStruct((2 * n + rows, D_MODEL), F32),
        compiler_params=_params(ARB2),
        name="moe_routed_experts",
    )(tile_expert, tile_valid, src, dst, h_all, w_gate.astype(BF16), w_up.astype(BF16), w_down.astype(BF16))


def _moe_combine(xn, gt2, g2, y, *, tm, n_all, tok0):
    gsz, t, _ = xn.shape
    assert t % tm == 0 and n_all % tm == 0 and tok0 % tm == 0
    tok, mod_spec, _ = _token_specs(tm, 2)
    nt = t // tm
    yspec = lambda slot: pl.BlockSpec((tm, D_MODEL), lambda b, i: ((slot * n_all + tok0) // tm + b * nt + i, 0))
    return pl.pallas_call(
        _moe_combine_kernel, grid=(gsz, nt),
        in_specs=[tok, mod_spec(gt2), pl.BlockSpec((1, tm, TOP_K), lambda b, i: (b, i, 0)), yspec(0), yspec(1)],
        out_specs=tok,
        out_shape=jax.ShapeDtypeStruct((gsz, t, D_MODEL), F32),
        compiler_params=_params(ARB2),
        name="moe_combine",
    )(xn, gt2, g2, y, y)


def _per_token(a, n):
    return jnp.broadcast_to(a, (a.shape[0], n, a.shape[2])).reshape(1, a.shape[0] * n, a.shape[2])


def _with_ones_column(v):
    return jnp.concatenate([v, jnp.ones(v.shape[:-1] + (1,), BF16),
                            jnp.zeros(v.shape[:-1] + (LANE - HEAD_DIM - 1,), BF16)], axis=-1)


def _pad_keys(a, axis, l_pad):
    pad = [(0, 0)] * a.ndim
    pad[axis] = (0, l_pad - a.shape[axis])
    return jnp.pad(a, pad)


def _layer_a(xp, xs, cp, cs, ck, cv, cki, rel_bias, norm_g, w_ada, b_ada, w_in, w_out, g_q, g_k, g_kidx,
             ffn_wg, ffn_wu, ffn_wd):
    bias = _dsa_bias_tiles(rel_bias)
    nb, n = xs.shape[0], xs.shape[1]
    past = ck.shape[1]
    outs = []
    for x, c, sample in ((xp, cp, False), (xs, cs, True)):
        sh1, sc1, gt1, sh2, sc2, gt2 = adaln(c, w_ada, b_ada)
        q16, kf, vf, ktr, vx, kif, qi16, kit, wsc = _project(
            x, norm_g[0], sh1, sc1, w_in, g_q, g_k, g_kidx, n_kv=KV_HEADS_A,
            tm=n if sample else PROJ_TM)
        if sample:
            l_true = past + n
            l_pad = -(-l_true // DSA_SUB) * DSA_SUB
            kfull = jnp.concatenate([ck, kf.reshape(nb, n, KV_HEADS_A, HEAD_DIM)], axis=1).astype(BF16)
            vfull = jnp.concatenate([cv, vf.reshape(nb, n, KV_HEADS_A, HEAD_DIM)], axis=1).astype(BF16)
            kifull = jnp.concatenate([cki, kif], axis=1).astype(BF16)
            ktr = _pad_keys(kfull, 1, l_pad).transpose(0, 2, 3, 1)
            vx = _pad_keys(_with_ones_column(vfull), 1, l_pad).transpose(0, 2, 1, 3)
            kit = _pad_keys(kifull, 1, l_pad).transpose(0, 2, 1)
            o = _dsa_attention(q16, ktr, vx, qi16, kit, wsc, bias, p0=past, l_true=l_true,
                               n_sel=min(TOPK_MAX, l_true // 4), tq=n, tk=l_pad, sw=DSA_SUB, cw=DSA_SUB,
                               tk_idx=l_pad)
            flat = lambda a: a.reshape(1, nb * n, a.shape[-1])
            y = _swiglu_mixer(flat(x), flat(o), w_out, _per_token(gt1, n), norm_g[1], _per_token(sh2, n),
                              _per_token(sc2, n), _per_token(gt2, n), ffn_wg, ffn_wu, ffn_wd, tm=MIX_TM_SAMPLE)
            y = y.reshape(x.shape)
        else:
            t = x.shape[1]
            o = _dsa_attention(q16, ktr, vx, qi16, kit, wsc, bias, p0=0, l_true=t,
                               n_sel=min(TOPK_MAX, t // 4), tq=LANE, tk=DSA_KEY_TILE, sw=DSA_SUB,
                               cw=DSA_COUNT_CHUNK, tk_idx=DSA_INDEX_TILE)
            y = _swiglu_mixer(x, o, w_out, gt1, norm_g[1], sh2, sc2, gt2, ffn_wg, ffn_wu, ffn_wd, tm=MIX_TM_PROMPT)
        bsz, t = x.shape[0], x.shape[1]
        outs.append((y, kf.reshape(bsz, t, KV_HEADS_A, HEAD_DIM), vf.reshape(bsz, t, KV_HEADS_A, HEAD_DIM), kif))
    return outs


def _layer_b(xp, xs, cp, cs, ck, cv, rel_bias, norm_g, w_ada, b_ada, w_in, w_out, g_q, g_k, sinks,
             w_router, moe_wg, moe_wu, moe_wd):
    bias = _band_bias(rel_bias)
    nb, n = xs.shape[0], xs.shape[1]
    streams = []
    for x, c, sample in ((xp, cp, False), (xs, cs, True)):
        sh1, sc1, gt1, sh2, sc2, gt2 = adaln(c, w_ada, b_ada)
        q16, kf, vf, ktr, vx = _project(x, norm_g[0], sh1, sc1, w_in, g_q, g_k, n_kv=KV_HEADS_B,
                                         tm=n if sample else PROJ_TM)
        bsz, t = x.shape[0], x.shape[1]
        k4 = kf.reshape(bsz, t, KV_HEADS_B, HEAD_DIM)
        v4 = vf.reshape(bsz, t, KV_HEADS_B, HEAD_DIM)
        zero = lambda i: 0
        if sample:
            kp = ck.astype(BF16).transpose(0, 2, 3, 1)
            vp = _with_ones_column(cv.astype(BF16)).transpose(0, 2, 1, 3)
            o = _band_attention(q16, kp, _pad_keys(ktr, 3, LANE), vp, _pad_keys(vx, 2, LANE), bias, sinks,
                                tq=n, prev_map=zero, cur_map=zero, first_has_no_prev=False)
            flat = lambda a: a.reshape(1, nb * n, a.shape[-1])
            xn, h, gates, sel = _moe_pre(flat(x), flat(o), w_out, _per_token(gt1, n), norm_g[1],
                                         _per_token(sh2, n), _per_token(sc2, n), w_router, tm=MIX_TM_SAMPLE)
            gt2 = _per_token(gt2, n)
            k_new = jnp.concatenate([ck, k4], axis=1)[:, -WINDOW:]
            v_new = jnp.concatenate([cv, v4], axis=1)[:, -WINDOW:]
        else:
            o = _band_attention(q16, ktr, ktr, vx, vx, bias, sinks, tq=LANE,
                                prev_map=lambda i: jnp.maximum(i - 1, 0), cur_map=lambda i: i,
                                first_has_no_prev=True)
            xn, h, gates, sel = _moe_pre(x, o, w_out, gt1, norm_g[1], sh2, sc2, w_router, tm=MIX_TM_PROMPT)
            k_new, v_new = k4[:, -WINDOW:], v4[:, -WINDOW:]
        streams.append((x.shape, xn, h, gates, sel, gt2, k_new, v_new))

    rows2d = lambda a: a.reshape(-1, a.shape[-1])
    h_all = jnp.concatenate([rows2d(st[2]) for st in streams])
    n_all = h_all.shape[0]
    src, dst, tile_expert, tile_valid, g2 = _moe_routing(
        jnp.concatenate([rows2d(st[3]) for st in streams]), jnp.concatenate([rows2d(st[4]) for st in streams]),
        MOE_ROW_TILE)
    y = _moe_experts(h_all, src, dst, tile_expert, tile_valid, moe_wg, moe_wu, moe_wd, rows=MOE_ROW_TILE)
    outs, tok0 = [], 0
    for shape, xn, _, _, _, gt2, k_new, v_new in streams:
        cnt = xn.shape[0] * xn.shape[1]
        out = _moe_combine(xn, gt2, g2[tok0:tok0 + cnt].reshape(xn.shape[0], xn.shape[1], TOP_K), y,
                           tm=MOE_COMBINE_TM, n_all=n_all, tok0=tok0)
        outs.append((out.reshape(shape), k_new, v_new))
        tok0 += cnt
    return outs


def kernel(x_prompt, x_sample, c_prompt, c_sample, cache_a_k, cache_a_v, cache_a_kidx,
           cache_b_k, cache_b_v, rel_bias, norm_g, w_ada, b_ada,
           a_w_in, a_w_out, a_g_q, a_g_k, a_g_kidx,
           b_w_in, b_w_out, b_g_q, b_g_k, b_sinks,
           ffn_w_gate, ffn_w_up, ffn_w_down,
           moe_w_router, moe_w_gate, moe_w_up, moe_w_down):
    xp, xs = x_prompt, x_sample
    a_out = [[] for _ in range(6)]
    b_out = [[] for _ in range(4)]
    for i in range(DEPTH):
        j = i // 2
        if i % 2 == 0:
            (xp, k1, v1, i1), (xs, k2, v2, i2) = _layer_a(
                xp, xs, c_prompt, c_sample, cache_a_k[j], cache_a_v[j], cache_a_kidx[j], rel_bias, norm_g[i],
                w_ada[i], b_ada[i], a_w_in[j], a_w_out[j], a_g_q[j], a_g_k[j], a_g_kidx[j],
                ffn_w_gate[j], ffn_w_up[j], ffn_w_down[j])
            for lst, val in zip(a_out, (k1, v1, i1, k2, v2, i2)):
                lst.append(val)
        else:
            (xp, k1, v1), (xs, k2, v2) = _layer_b(
                xp, xs, c_prompt, c_sample, cache_b_k[j], cache_b_v[j], rel_bias, norm_g[i],
                w_ada[i], b_ada[i], b_w_in[j], b_w_out[j], b_g_q[j], b_g_k[j], b_sinks[j],
                moe_w_router[j], moe_w_gate[j], moe_w_up[j], moe_w_down[j])
            for lst, val in zip(b_out, (k1, v1, k2, v2)):
                lst.append(val)
    return (xp, xs, *(jnp.stack(l) for l in a_out), *(jnp.stack(l) for l in b_out))
```
